```python
import math
import jax, jax.numpy as jnp
from jax import lax
import numpy as np

D_MODEL = 1024
BATCH = 1
SEQ = 16384
DEPTH = 1

D_MIX = D_MODEL
D_MLSTM = D_MIX // 2
N_MLSTM_HEADS = 4
MLSTM_HEAD_DIM = D_MLSTM // N_MLSTM_HEADS
CHUNK = 64
CONV_WIDTH = 4
D_SSM = D_MIX - D_MLSTM
SSM_GROUP = 16
N_SSM_GROUPS = D_SSM // SSM_GROUP
SSM_STATE = 64
D_FF = 4 * D_MODEL
EPS = 1e-6
DT_MIN = 1e-3
DT_MAX = 1e-1
IN_COLS = 4 * D_MLSTM + 2 * N_MLSTM_HEADS + D_SSM

kernel_name = 'hymba_mlstm_s5_hybrid_block'


def rmsnorm(x, g):
    xf = x.astype(jnp.float32)
    r = lax.rsqrt(jnp.mean(xf * xf, axis=-1, keepdims=True) + EPS)
    return (xf * r * g.astype(jnp.float32)).astype(x.dtype)


def causal_depthwise_conv(x, w, b):
    S = x.shape[1]
    xp = jnp.pad(x, ((0, 0), (CONV_WIDTH - 1, 0), (0, 0)))
    out = b
    for j in range(CONV_WIDTH):
        out = out + xp[:, j:j + S, :] * w[j]
    return out


def headwise_layernorm(h, w):
    mu = jnp.mean(h, axis=-1, keepdims=True)
    var = jnp.mean(jnp.square(h - mu), axis=-1, keepdims=True)
    hn = (h - mu) * lax.rsqrt(var + EPS)
    Bsz, S, H, D = h.shape
    return hn.reshape(Bsz, S, H * D) * w.astype(jnp.float32)


def mlstm_chunkwise(q, k, v, i_pre, f_pre):
    Bsz, S, H, D = q.shape
    NC, L = S // CHUNK, CHUNK

    def to_chunks(t):
        return t.reshape(Bsz, NC, L, H, D).transpose(0, 3, 1, 2, 4)

    q = to_chunks(q) * (D ** -0.5)
    k = to_chunks(k)
    v = to_chunks(v)
    log_f = jax.nn.log_sigmoid(f_pre).reshape(Bsz, NC, L, H).transpose(0, 3, 1, 2)
    log_i = i_pre.reshape(Bsz, NC, L, H).transpose(0, 3, 1, 2)
    b = jnp.cumsum(log_f, axis=-1)
    b_last = b[..., -1]

    causal = jnp.tril(jnp.ones((L, L), dtype=bool))
    log_d = jnp.where(causal, b[..., :, None] - b[..., None, :] + log_i[..., None, :], -jnp.inf)

    g = b_last[..., None] - b + log_i
    a = jnp.max(g, axis=-1)
    w_g = jnp.exp(g - a[..., None])
    C_loc = jnp.einsum('bhcl,bhcld,bhcle->bhcde', w_g, v, k)
    n_loc = jnp.einsum('bhcl,bhcle->bhce', w_g, k)

    def step(carry, inp):
        C, n, m = carry
        bl, a_c, Cl, nl = inp
        m_new = jnp.maximum(bl + m, a_c)
        s_old = jnp.exp(bl + m - m_new)
        s_new = jnp.exp(a_c - m_new)
        C_new = s_old[..., None, None] * C + s_new[..., None, None] * Cl
        n_new = s_old[..., None] * n + s_new[..., None] * nl
        return (C_new, n_new, m_new), (C, n, m)

    init = (jnp.zeros((Bsz, H, D, D), jnp.float32),
            jnp.zeros((Bsz, H, D), jnp.float32),
            jnp.zeros((Bsz, H), jnp.float32))
    xs = (jnp.moveaxis(b_last, 2, 0), jnp.moveaxis(a, 2, 0),
          jnp.moveaxis(C_loc, 2, 0), jnp.moveaxis(n_loc, 2, 0))
    _, (C_prev, n_prev, m_prev) = lax.scan(step, init, xs)
    C_prev = jnp.moveaxis(C_prev, 0, 2)
    n_prev = jnp.moveaxis(n_prev, 0, 2)
    m_prev = jnp.moveaxis(m_prev, 0, 2)

    inter_log = b + m_prev[..., None]
    m_t = jnp.maximum(inter_log, jnp.max(log_d, axis=-1))
    inter_scale = jnp.exp(inter_log - m_t)
    d_mat = jnp.exp(log_d - m_t[..., None])
    s_qk = jnp.einsum('bhcld,bhcsd->bhcls', q, k) * d_mat
    num = (inter_scale[..., None] * jnp.einsum('bhcde,bhcle->bhcld', C_prev, q)
           + jnp.einsum('bhcls,bhcsd->bhcld', s_qk, v))
    den = inter_scale * jnp.einsum('bhce,bhcle->bhcl', n_prev, q) + jnp.sum(s_qk, axis=-1)
    h = num / jnp.maximum(jnp.abs(den), jnp.exp(-m_t))[..., None]
    return h.transpose(0, 2, 3, 1, 4).reshape(Bsz, S, H, D)


def s5_groups(u, lam_re, lam_im, log_dt, b_re, b_im, c_re, c_im, d):
    f32 = jnp.float32
    lam = lax.complex(lam_re.astype(f32), lam_im.astype(f32))
    dt = jnp.exp(log_dt.astype(f32))[:, None]
    lam_bar = jnp.exp(lam * dt)
    Bc = lax.complex(b_re.astype(f32), b_im.astype(f32))
    B_bar = ((lam_bar - 1.0) / lam)[..., None] * Bc
    Cc = lax.complex(c_re.astype(f32), c_im.astype(f32))
    Bu = jnp.einsum('gpc,bsgc->bsgp', B_bar, u.astype(jnp.complex64))
    A_elems = jnp.broadcast_to(lam_bar, Bu.shape)

    def combine(left, right):
        a_l, x_l = left
        a_r, x_r = right
        return a_r * a_l, a_r * x_l + x_r

    _, states = lax.associative_scan(combine, (A_elems, Bu), axis=1)
    y = jnp.einsum('gcp,bsgp->bsgc', Cc, states).real
    return y + d.astype(f32) * u


def hybrid_mixer(h, w_in, conv_w, conv_b, i_bias, f_bias, mlstm_norm_w,
                 lam_re, lam_im, log_dt, b_re, b_im, c_re, c_im, ssm_d,
                 glu_w, glu_b, w_out):
    Bsz, S, _ = h.shape
    f32 = jnp.float32
    proj = h @ w_in
    qk_pre = proj[..., :2 * D_MLSTM]
    v = proj[..., 2 * D_MLSTM:3 * D_MLSTM]
    o_pre = proj[..., 3 * D_MLSTM:4 * D_MLSTM]
    gates = proj[..., 4 * D_MLSTM:4 * D_MLSTM + 2 * N_MLSTM_HEADS]
    u = proj[..., 4 * D_MLSTM + 2 * N_MLSTM_HEADS:]

    qk = jax.nn.silu(causal_depthwise_conv(qk_pre, conv_w, conv_b)).astype(f32)
    heads = lambda t: t.reshape(Bsz, S, N_MLSTM_HEADS, MLSTM_HEAD_DIM)
    q = heads(qk[..., :D_MLSTM])
    k = heads(qk[..., D_MLSTM:])
    vh = heads(v.astype(f32))
    gates = gates.astype(f32)
    i_pre = gates[..., :N_MLSTM_HEADS] + i_bias.astype(f32)
    f_pre = gates[..., N_MLSTM_HEADS:] + f_bias.astype(f32)
    h_m = mlstm_chunkwise(q, k, vh, i_pre, f_pre)
    h_m = headwise_layernorm(h_m, mlstm_norm_w) * jax.nn.sigmoid(o_pre.astype(f32))

    ug = u.astype(f32).reshape(Bsz, S, N_SSM_GROUPS, SSM_GROUP)
    y = s5_groups(ug, lam_re, lam_im, log_dt, b_re, b_im, c_re, c_im,
                  ssm_d.reshape(N_SSM_GROUPS, SSM_GROUP)).reshape(Bsz, S, D_SSM)
    z = jax.nn.gelu(y)
    y = z * jax.nn.sigmoid(z @ glu_w.astype(f32) + glu_b.astype(f32))

    mixed = jnp.concatenate([h_m, y], axis=-1).astype(h.dtype)
    return mixed @ w_out


def squared_relu_mlp(h, w_ff1, w_ff2):
    a = jax.nn.relu(h @ w_ff1)
    return (a * a) @ w_ff2


def setup_inputs(seed: int = 0) -> dict:
    key = jax.random.key(seed)
    ks = jax.random.split(key, 24)
    f32 = jnp.float32
    nrm = lambda k, shape, s: jax.random.normal(k, shape, f32) * s
    x = jax.random.normal(ks[0], (BATCH, SEQ, D_MODEL), f32)
    mix_norm_w = 1.0 + nrm(ks[1], (DEPTH, D_MODEL), 0.02)
    w_in = nrm(ks[2], (DEPTH, D_MODEL, IN_COLS), D_MODEL ** -0.5)
    conv_w = nrm(ks[3], (DEPTH, CONV_WIDTH, 2 * D_MLSTM), CONV_WIDTH ** -0.5)
    conv_b = nrm(ks[4], (DEPTH, 2 * D_MLSTM), 0.02)
    i_bias = nrm(ks[5], (DEPTH, N_MLSTM_HEADS), 0.1)
    f_bias = jnp.linspace(3.0, 6.0, N_MLSTM_HEADS, dtype=f32)[None, :] + nrm(ks[6], (DEPTH, N_MLSTM_HEADS), 0.01)
    mlstm_norm_w = 1.0 + nrm(ks[7], (DEPTH, D_MLSTM), 0.02)
    ssm_lam_re = -0.5 + nrm(ks[8], (DEPTH, N_SSM_GROUPS, SSM_STATE), 0.01)
    ssm_lam_im = (jnp.pi * jnp.arange(SSM_STATE, dtype=f32))[None, None, :] + nrm(ks[9], (DEPTH, N_SSM_GROUPS, SSM_STATE), 0.01)
    ssm_log_dt = jax.random.uniform(ks[10], (DEPTH, N_SSM_GROUPS), f32, math.log(DT_MIN), math.log(DT_MAX))
    b_scale = (2.0 * SSM_GROUP) ** -0.5
    ssm_b_re = nrm(ks[11], (DEPTH, N_SSM_GROUPS, SSM_STATE, SSM_GROUP), b_scale)
    ssm_b_im = nrm(ks[12], (DEPTH, N_SSM_GROUPS, SSM_STATE, SSM_GROUP), b_scale)
    c_scale = SSM_STATE ** -0.5
    ssm_c_re = nrm(ks[13], (DEPTH, N_SSM_GROUPS, SSM_GROUP, SSM_STATE), c_scale)
    ssm_c_im = nrm(ks[14], (DEPTH, N_SSM_GROUPS, SSM_GROUP, SSM_STATE), c_scale)
    ssm_d = nrm(ks[15], (DEPTH, D_SSM), 1.0)
    glu_w = nrm(ks[16], (DEPTH, D_SSM, D_SSM), D_SSM ** -0.5)
    glu_b = nrm(ks[17], (DEPTH, D_SSM), 0.02)
    w_out = nrm(ks[18], (DEPTH, D_MIX, D_MODEL), D_MIX ** -0.5)
    mlp_norm_w = 1.0 + nrm(ks[19], (DEPTH, D_MODEL), 0.02)
    w_ff1 = nrm(ks[20], (DEPTH, D_MODEL, D_FF), D_MODEL ** -0.5)
    w_ff2 = nrm(ks[21], (DEPTH, D_FF, D_MODEL), D_FF ** -0.5)
    final_norm_w = 1.0 + nrm(ks[22], (D_MODEL,), 0.02)
    return {'x': x, 'mix_norm_w': mix_norm_w, 'w_in': w_in, 'conv_w': conv_w, 'conv_b': conv_b,
            'i_bias': i_bias, 'f_bias': f_bias, 'mlstm_norm_w': mlstm_norm_w,
            'ssm_lam_re': ssm_lam_re, 'ssm_lam_im': ssm_lam_im, 'ssm_log_dt': ssm_log_dt,
            'ssm_b_re': ssm_b_re, 'ssm_b_im': ssm_b_im, 'ssm_c_re': ssm_c_re, 'ssm_c_im': ssm_c_im,
            'ssm_d': ssm_d, 'glu_w': glu_w, 'glu_b': glu_b, 'w_out': w_out,
            'mlp_norm_w': mlp_norm_w, 'w_ff1': w_ff1, 'w_ff2': w_ff2, 'final_norm_w': final_norm_w}


def reference(x, mix_norm_w, w_in, conv_w, conv_b, i_bias, f_bias, mlstm_norm_w,
              ssm_lam_re, ssm_lam_im, ssm_log_dt, ssm_b_re, ssm_b_im, ssm_c_re, ssm_c_im,
              ssm_d, glu_w, glu_b, w_out, mlp_norm_w, w_ff1, w_ff2, final_norm_w):
    for l in range(DEPTH):
        h = rmsnorm(x, mix_norm_w[l])
        x = x + hybrid_mixer(h, w_in[l], conv_w[l], conv_b[l], i_bias[l], f_bias[l], mlstm_norm_w[l],
                             ssm_lam_re[l], ssm_lam_im[l], ssm_log_dt[l], ssm_b_re[l], ssm_b_im[l],
                             ssm_c_re[l], ssm_c_im[l], ssm_d[l], glu_w[l], glu_b[l], w_out[l])
        h = rmsnorm(x, mlp_norm_w[l])
        x = x + squared_relu_mlp(h, w_ff1[l], w_ff2[l])
    return rmsnorm(x, final_norm_w)
```

```python
import functools
import math

import jax
import jax.numpy as jnp
from jax import lax
from jax.experimental import pallas as pl
from jax.experimental.pallas import tpu as pltpu

F32 = jnp.float32
BF16 = jnp.bfloat16

D_MODEL = 1024
D_MLSTM = 512
N_HEADS = 4
HEAD_DIM = 128
CONV_WIDTH = 4
D_SSM = 512
SSM_GROUP = 16
N_GROUPS = 32
SSM_STATE = 64
D_FF = 4096
EPS = 1e-6

SUBLANES = 8
LANES = 128

PROJ_ROWS = 512
MLSTM_CHUNK = 256
S5_ROWS = 256
FFN_ROWS = 512
FF_CHUNK = 1024
WEIGHT_BUFFERS = 1
VMEM_LIMIT = 56 * 1024 * 1024

GROUPS_PER_SUBLANE = N_GROUPS // SUBLANES
STATE_HALF = GROUPS_PER_SUBLANE * SSM_STATE
STATE_LANES = 2 * STATE_HALF
STATE_BLOCKS = STATE_LANES // LANES


def _const_spec(shape):
    return pl.BlockSpec(shape, lambda i: (0,) * len(shape))


def _row_spec(rows, cols):
    return pl.BlockSpec((rows, cols), lambda i: (i, 0))


def _params():
    return pltpu.CompilerParams(dimension_semantics=("arbitrary",),
                                vmem_limit_bytes=VMEM_LIMIT)


def _rms(x, g):
    r = lax.rsqrt(jnp.mean(x * x, axis=-1, keepdims=True) + EPS)
    return x * r * g


def _proj_kernel(x_ref, g_ref, w_ref, wgc_ref, wgr_ref,
                 qk_ref, v_ref, o_ref, u_ref, gc_ref, gr_ref):
    h = _rms(x_ref[...], g_ref[...]).astype(BF16)
    qk_ref[...] = jnp.dot(h, w_ref[:, 0:1024], preferred_element_type=F32)
    v_ref[...] = jnp.dot(h, w_ref[:, 1024:1536], preferred_element_type=F32)
    o_ref[...] = jnp.dot(h, w_ref[:, 1536:2048], preferred_element_type=F32)
    u_ref[...] = jnp.dot(h, w_ref[:, 2048:2560], preferred_element_type=F32)
    gc_ref[...] = jnp.dot(h, wgc_ref[...], preferred_element_type=F32)
    gr_ref[...] = lax.dot_general(wgr_ref[...], h, (((1,), (1,)), ((), ())),
                                  preferred_element_type=F32)


def _proj_call(x, g, w, wgc, wgr):
    S = x.shape[0]
    T = PROJ_ROWS
    return pl.pallas_call(
        _proj_kernel,
        grid=(S // T,),
        in_specs=[_row_spec(T, D_MODEL), _const_spec((1, D_MODEL)),
                  _const_spec(w.shape), _const_spec(wgc.shape), _const_spec(wgr.shape)],
        out_specs=[_row_spec(T, 1024), _row_spec(T, 512), _row_spec(T, 512),
                   _row_spec(T, 512), _row_spec(T, LANES),
                   pl.BlockSpec((SUBLANES, T), lambda i: (0, i))],
        out_shape=[jax.ShapeDtypeStruct((S, 1024), F32),
                   jax.ShapeDtypeStruct((S, 512), F32),
                   jax.ShapeDtypeStruct((S, 512), F32),
                   jax.ShapeDtypeStruct((S, 512), F32),
                   jax.ShapeDtypeStruct((S, LANES), F32),
                   jax.ShapeDtypeStruct((SUBLANES, S), F32)],
        compiler_params=_params(),
        name="proj",
    )(x, g, w, wgc, wgr)


def _log_sigmoid(x):
    return jnp.minimum(x, 0.0) - jnp.log1p(jnp.exp(-jnp.abs(x)))


def _mlstm_kernel(qk_ref, v_ref, o_ref, gc_ref, gr_ref, cw_ref, cb_ref,
                  bc_ref, br_ref, nw_ref, out_ref, xpad_ref, ct_ref, m_ref):
    L = MLSTM_CHUNK
    D = HEAD_DIM

    @pl.when(pl.program_id(0) == 0)
    def _():
        xpad_ref[0:SUBLANES, :] = jnp.zeros((SUBLANES, 2 * D_MLSTM), F32)
        ct_ref[...] = jnp.zeros(ct_ref.shape, F32)
        m_ref[...] = jnp.zeros(m_ref.shape, F32)

    xpad_ref[SUBLANES:SUBLANES + L, :] = qk_ref[...]
    acc = cb_ref[...]
    for j in range(CONV_WIDTH):
        lo = SUBLANES - (CONV_WIDTH - 1) + j
        acc = acc + xpad_ref[lo:lo + L, :] * cw_ref[j:j + 1, :]
    xpad_ref[0:SUBLANES, :] = xpad_ref[L:L + SUBLANES, :]
    qk = acc * jax.nn.sigmoid(acc)

    gc = gc_ref[...] + bc_ref[...]
    gr = gr_ref[...] + br_ref[...]
    row_id = lax.broadcasted_iota(jnp.int32, (L, L), 0)
    col_id = lax.broadcasted_iota(jnp.int32, (L, L), 1)
    causal = row_id >= col_id
    tril = causal.astype(F32)
    triu = (row_id <= col_id).astype(F32)
    b_col = jnp.dot(tril, _log_sigmoid(gc), preferred_element_type=F32,
                    precision=lax.Precision.HIGHEST)
    b_row = jnp.dot(_log_sigmoid(gr), triu, preferred_element_type=F32,
                    precision=lax.Precision.HIGHEST)

    neg_inf = jnp.float32(-jnp.inf)
    for h in range(N_HEADS):
        q = (qk[:, h * D:(h + 1) * D] * (D ** -0.5)).astype(BF16)
        k = qk[:, D_MLSTM + h * D:D_MLSTM + (h + 1) * D]
        kb = k.astype(BF16)
        v = v_ref[:, h * D:(h + 1) * D]

        bc = b_col[:, N_HEADS + h:N_HEADS + h + 1]
        rc = gc[:, h:h + 1] - bc
        rr = gr[h:h + 1, :] - b_row[N_HEADS + h:N_HEADS + h + 1, :]
        m_prev = m_ref[h:h + 1, 0:1]

        rmat = jnp.where(causal, rr, neg_inf)
        m_intra = jnp.max(rmat, axis=1, keepdims=True)
        mu = jnp.maximum(m_intra, m_prev)
        dmat = jnp.exp(rmat - mu)
        inter_scale = jnp.exp(m_prev - mu)

        s_qk = lax.dot_general(q, kb, (((1,), (1,)), ((), ())),
                               preferred_element_type=F32) * dmat
        qc = jnp.dot(q, ct_ref[h].astype(BF16), preferred_element_type=F32)
        num = inter_scale * qc[:, 0:D] + jnp.dot(s_qk.astype(BF16), v.astype(BF16),
                                                 preferred_element_type=F32)
        den = inter_scale * qc[:, D:D + 1] + jnp.sum(s_qk, axis=1, keepdims=True)
        hh = num / jnp.maximum(jnp.abs(den), jnp.exp(-bc - mu))

        m_last = m_intra[L - 1:L, :]
        w = jnp.exp(rc - m_last)
        aug = jnp.concatenate([v * w, jnp.broadcast_to(w, (L, D))], axis=1).astype(BF16)
        ct_loc = jnp.dot(k.T.astype(BF16), aug, preferred_element_type=F32)
        mx = jnp.maximum(m_prev, m_last)
        ct_ref[h] = jnp.exp(m_prev - mx) * ct_ref[h] + jnp.exp(m_last - mx) * ct_loc
        m_ref[h:h + 1, :] = jnp.broadcast_to(bc[L - 1:L, :] + mx, (1, LANES))

        mean = jnp.mean(hh, axis=1, keepdims=True)
        cen = hh - mean
        var = jnp.mean(cen * cen, axis=1, keepdims=True)
        hn = cen * lax.rsqrt(var + EPS) * nw_ref[:, h * D:(h + 1) * D]
        gate = jax.nn.sigmoid(o_ref[:, h * D:(h + 1) * D])
        out_ref[:, h * D:(h + 1) * D] = (hn * gate).astype(BF16)


def _mlstm_call(qk, v, o, gc, gr, cw, cb, bias_c, bias_r, nw):
    S = qk.shape[0]
    L = MLSTM_CHUNK
    return pl.pallas_call(
        _mlstm_kernel,
        grid=(S // L,),
        in_specs=[_row_spec(L, 1024), _row_spec(L, 512), _row_spec(L, 512),
                  _row_spec(L, LANES), pl.BlockSpec((SUBLANES, L), lambda i: (0, i)),
                  _const_spec(cw.shape), _const_spec(cb.shape),
                  _const_spec(bias_c.shape), _const_spec(bias_r.shape),
                  _const_spec(nw.shape)],
        out_specs=_row_spec(L, D_MLSTM),
        out_shape=jax.ShapeDtypeStruct((S, D_MLSTM), BF16),
        scratch_shapes=[pltpu.VMEM((L + SUBLANES, 2 * D_MLSTM), F32),
                        pltpu.VMEM((N_HEADS, HEAD_DIM, 2 * HEAD_DIM), F32),
                        pltpu.VMEM((SUBLANES, LANES), F32)],
        compiler_params=_params(),
        name="mlstm",
    )(qk, v, o, gc, gr, cw, cb, bias_c, bias_r, nw)


def _gelu_tanh(x):
    c = math.sqrt(2.0 / math.pi)
    return x * (0.5 * (1.0 + jnp.tanh(c * (x + 0.044715 * (x * x * x)))))


def _s5_kernel(u_ref, a_ref, wb_ref, wc_ref, d_ref, gw_ref, gb_ref, out_ref,
               z_ref, x_ref):
    T = S5_ROWS

    @pl.when(pl.program_id(0) == 0)
    def _():
        x_ref[...] = jnp.zeros(x_ref.shape, F32)

    u = u_ref[...]
    ub = u.astype(BF16)
    for j in range(SUBLANES):
        blk = ub[:, (j // 2) * LANES:(j // 2 + 1) * LANES]
        bu = jnp.dot(blk, wb_ref[j], preferred_element_type=F32)
        for c in range(STATE_BLOCKS):
            z_ref[c, pl.ds(j, T, stride=SUBLANES), :] = bu[:, c * LANES:(c + 1) * LANES]

    half = STATE_BLOCKS // 2
    a_re = [a_ref[:, c * LANES:(c + 1) * LANES] for c in range(half)]
    a_im = [a_ref[:, (half + c) * LANES:(half + c + 1) * LANES] for c in range(half)]

    def step(t, carry):
        off = pl.multiple_of(t * SUBLANES, SUBLANES)
        new = []
        for c in range(half):
            x_re, x_im = carry[c], carry[half + c]
            n_re = a_re[c] * x_re - a_im[c] * x_im + z_ref[c, pl.ds(off, SUBLANES), :]
            n_im = a_re[c] * x_im + a_im[c] * x_re + z_ref[half + c, pl.ds(off, SUBLANES), :]
            z_ref[c, pl.ds(off, SUBLANES), :] = n_re
            z_ref[half + c, pl.ds(off, SUBLANES), :] = n_im
            new.append((n_re, n_im))
        return tuple(n[0] for n in new) + tuple(n[1] for n in new)

    init = tuple(x_ref[:, c * LANES:(c + 1) * LANES] for c in range(STATE_BLOCKS))
    final = lax.fori_loop(0, T, step, init, unroll=8)
    for c in range(STATE_BLOCKS):
        x_ref[:, c * LANES:(c + 1) * LANES] = final[c]

    blocks = []
    for b in range(SUBLANES // 2):
        acc = None
        for j in (2 * b, 2 * b + 1):
            xs = jnp.concatenate(
                [z_ref[c, pl.ds(j, T, stride=SUBLANES), :] for c in range(STATE_BLOCKS)],
                axis=1).astype(BF16)
            part = jnp.dot(xs, wc_ref[j], preferred_element_type=F32)
            acc = part if acc is None else acc + part
        blocks.append(acc)
    y = jnp.concatenate(blocks, axis=1) + d_ref[...] * u
    z = _gelu_tanh(y)
    gate = jax.nn.sigmoid(jnp.dot(z.astype(BF16), gw_ref[...],
                                  preferred_element_type=F32) + gb_ref[...])
    out_ref[...] = (z * gate).astype(BF16)


def _s5_call(u, a_tile, wb, wc, d, gw, gb):
    S = u.shape[0]
    T = S5_ROWS
    return pl.pallas_call(
        _s5_kernel,
        grid=(S // T,),
        in_specs=[_row_spec(T, D_SSM), _const_spec(a_tile.shape), _const_spec(wb.shape),
                  _const_spec(wc.shape), _const_spec(d.shape), _const_spec(gw.shape),
                  _const_spec(gb.shape)],
        out_specs=_row_spec(T, D_SSM),
        out_shape=jax.ShapeDtypeStruct((S, D_SSM), BF16),
        scratch_shapes=[pltpu.VMEM((STATE_BLOCKS, T * SUBLANES, LANES), F32),
                        pltpu.VMEM((SUBLANES, STATE_LANES), F32)],
        compiler_params=_params(),
        name="s5",
    )(u, a_tile, wb, wc, d, gw, gb)


def _s5_weights(lam_re, lam_im, log_dt, b_re, b_im, c_re, c_im):
    lam = lax.complex(lam_re.astype(F32), lam_im.astype(F32))
    dt = jnp.exp(log_dt.astype(F32))[:, None]
    lam_bar = jnp.exp(lam * dt)
    b_bar = ((lam_bar - 1.0) / lam)[..., None] * lax.complex(b_re.astype(F32),
                                                             b_im.astype(F32))
    J, Q, P, Hc = SUBLANES, GROUPS_PER_SUBLANE, SSM_STATE, SSM_GROUP
    a_tile = jnp.concatenate([jnp.real(lam_bar).reshape(J, Q * P),
                              jnp.imag(lam_bar).reshape(J, Q * P)], axis=1)
    eye = jnp.eye(Q, dtype=F32)
    half = (jnp.arange(J) % 2)[:, None, None, None] == jnp.arange(2)[None, :, None, None]

    def expand_in(b):
        return jnp.einsum('jqpc,qr->jqcrp', b.reshape(J, Q, P, Hc), eye).reshape(J, Q * Hc, Q * P)

    wb = jnp.concatenate([expand_in(jnp.real(b_bar)), expand_in(jnp.imag(b_bar))], axis=2)
    wb = jnp.where(half, wb[:, None], 0.0).reshape(J, 2 * Q * Hc, STATE_LANES)

    def expand_out(c):
        return jnp.einsum('jqcp,qr->jqprc', c.reshape(J, Q, Hc, P), eye).reshape(J, Q * P, Q * Hc)

    wc = jnp.concatenate([expand_out(c_re.astype(F32)), -expand_out(c_im.astype(F32))], axis=1)
    half_c = jnp.swapaxes(half, 1, 2)
    wc = jnp.where(half_c, wc[:, :, None, :], 0.0).reshape(J, STATE_LANES, 2 * Q * Hc)
    return a_tile, wb.astype(BF16), wc.astype(BF16)


def _ffn_kernel(x_ref, hm_ref, y_ref, wo_ref, g2_ref, w1_ref, w2_ref, g3_ref, out_ref):
    mixed = (jnp.dot(hm_ref[...], wo_ref[0:D_MLSTM, :], preferred_element_type=F32)
             + jnp.dot(y_ref[...], wo_ref[D_MLSTM:D_MODEL, :], preferred_element_type=F32))
    x1 = x_ref[...] + mixed
    h2 = _rms(x1, g2_ref[...]).astype(BF16)
    mlp = None
    for c in range(D_FF // FF_CHUNK):
        a = jnp.dot(h2, w1_ref[:, c * FF_CHUNK:(c + 1) * FF_CHUNK],
                    preferred_element_type=F32)
        a = jnp.maximum(a, 0.0)
        part = jnp.dot((a * a).astype(BF16), w2_ref[c * FF_CHUNK:(c + 1) * FF_CHUNK, :],
                       preferred_element_type=F32)
        mlp = part if mlp is None else mlp + part
    out_ref[...] = _rms(x1 + mlp, g3_ref[...])


def _ffn_call(x, hm, y, wo, g2, w1, w2, g3):
    S = x.shape[0]
    T = FFN_ROWS
    single = dict(pipeline_mode=pl.Buffered(WEIGHT_BUFFERS))
    return pl.pallas_call(
        _ffn_kernel,
        grid=(S // T,),
        in_specs=[_row_spec(T, D_MODEL), _row_spec(T, D_MLSTM), _row_spec(T, D_SSM),
                  pl.BlockSpec(wo.shape, lambda i: (0, 0), **single),
                  _const_spec(g2.shape),
                  pl.BlockSpec(w1.shape, lambda i: (0, 0), **single),
                  pl.BlockSpec(w2.shape, lambda i: (0, 0), **single),
                  _const_spec(g3.shape)],
        out_specs=_row_spec(T, D_MODEL),
        out_shape=jax.ShapeDtypeStruct((S, D_MODEL), F32),
        compiler_params=_params(),
        name="ffn",
    )(x, hm, y, wo, g2, w1, w2, g3)


def _layer(x, mix_norm_w, w_in, conv_w, conv_b, i_bias, f_bias, mlstm_norm_w,
           lam_re, lam_im, log_dt, b_re, b_im, c_re, c_im, ssm_d,
           glu_w, glu_b, w_out, mlp_norm_w, w_ff1, w_ff2, out_norm_w):
    n_main = 4 * D_MLSTM
    n_gate = 2 * N_HEADS
    w_main = jnp.concatenate([w_in[:, :n_main], w_in[:, n_main + n_gate:]], axis=1).astype(BF16)
    w_gate = w_in[:, n_main:n_main + n_gate]
    wgc = jnp.pad(w_gate, ((0, 0), (0, LANES - n_gate))).astype(BF16)
    wgr = w_gate.T.astype(BF16)
    qk, v, o, u, gc, gr = _proj_call(x, mix_norm_w[None, :], w_main, wgc, wgr)

    gate_bias = jnp.concatenate([i_bias, f_bias]).astype(F32)
    bias_c = jnp.pad(gate_bias, (0, LANES - n_gate))[None, :]
    bias_r = gate_bias[:, None]
    hm = _mlstm_call(qk, v, o, gc, gr, conv_w, conv_b[None, :], bias_c, bias_r,
                     mlstm_norm_w[None, :])

    a_tile, wb, wc = _s5_weights(lam_re, lam_im, log_dt, b_re, b_im, c_re, c_im)
    y = _s5_call(u, a_tile, wb, wc, ssm_d[None, :], glu_w.astype(BF16), glu_b[None, :])

    return _ffn_call(x, hm, y, w_out.astype(BF16), mlp_norm_w[None, :],
                     w_ff1.astype(BF16), w_ff2.astype(BF16), out_norm_w[None, :])


def kernel(x, mix_norm_w, w_in, conv_w, conv_b, i_bias, f_bias, mlstm_norm_w, ssm_lam_re, ssm_lam_im, ssm_log_dt, ssm_b_re, ssm_b_im, ssm_c_re, ssm_c_im, ssm_d, glu_w, glu_b, w_out, mlp_norm_w, w_ff1, w_ff2, final_norm_w):
    assert x.shape[0] == 1 and mix_norm_w.shape[0] == 1
    xs = x[0]
    out = _layer(xs, mix_norm_w[0], w_in[0], conv_w[0], conv_b[0], i_bias[0], f_bias[0],
                 mlstm_norm_w[0], ssm_lam_re[0], ssm_lam_im[0], ssm_log_dt[0],
                 ssm_b_re[0], ssm_b_im[0], ssm_c_re[0], ssm_c_im[0], ssm_d[0],
                 glu_w[0], glu_b[0], w_out[0], mlp_norm_w[0], w_ff1[0], w_ff2[0],
                 final_norm_w)
    return out[None]
```

```python
import math

import jax
import jax.numpy as jnp
from jax import lax
from jax.experimental import pallas as pl
from jax.experimental.pallas import tpu as pltpu

F32 = jnp.float32
BF16 = jnp.bfloat16

D_MODEL = 1024
D_MLSTM = 512
N_HEADS = 4
HEAD_DIM = 128
CONV_WIDTH = 4
D_SSM = 512
SSM_GROUP = 16
N_GROUPS = 32
SSM_STATE = 64
D_FF = 4096
EPS = 1e-6

SUBLANES = 8
LANES = 128

PROJ_ROWS = 512
MLSTM_ROWS = 512
MLSTM_CHUNK = 128
S5_ROWS = 256
FFN_ROWS = 512
FF_CHUNK = 1024
WEIGHT_BUFFERS = 1
VMEM_LIMIT = 56 * 1024 * 1024

GROUPS_PER_SUBLANE = N_GROUPS // SUBLANES
STATE_HALF = GROUPS_PER_SUBLANE * SSM_STATE
STATE_LANES = 2 * STATE_HALF
STATE_BLOCKS = STATE_LANES // LANES


def _const_spec(shape):
    return pl.BlockSpec(shape, lambda i: (0,) * len(shape))


def _row_spec(rows, cols):
    return pl.BlockSpec((rows, cols), lambda i: (i, 0))


def _lane_spec(rows, cols):
    return pl.BlockSpec((rows, cols), lambda i: (0, i))


def _params():
    return pltpu.CompilerParams(dimension_semantics=("arbitrary",),
                                vmem_limit_bytes=VMEM_LIMIT)


def _rms(x, g):
    r = lax.rsqrt(jnp.mean(x * x, axis=-1, keepdims=True) + EPS)
    return x * r * g


_NT_DIMS = (((1,), (1,)), ((), ()))


def _proj_kernel(x_ref, g_ref, w_ref, wt_ref, wg_ref, qk_ref, u_ref, vt_ref, ot_ref, gr_ref):
    h = _rms(x_ref[...], g_ref[...]).astype(BF16)
    qk_ref[...] = jnp.dot(h, w_ref[:, 0:1024], preferred_element_type=F32)
    u_ref[...] = jnp.dot(h, w_ref[:, 1024:1536], preferred_element_type=F32)
    vt_ref[...] = lax.dot_general(wt_ref[0:D_MLSTM, :], h, _NT_DIMS,
                                  preferred_element_type=F32)
    ot_ref[...] = lax.dot_general(wt_ref[D_MLSTM:2 * D_MLSTM, :], h, _NT_DIMS,
                                  preferred_element_type=F32)
    gr_ref[...] = lax.dot_general(wg_ref[...], h, _NT_DIMS, preferred_element_type=F32)


def _proj_call(x, g, w, wt, wg):
    S = x.shape[0]
    T = PROJ_ROWS
    return pl.pallas_call(
        _proj_kernel,
        grid=(S // T,),
        in_specs=[_row_spec(T, D_MODEL), _const_spec((1, D_MODEL)),
                  _const_spec(w.shape), _const_spec(wt.shape), _const_spec(wg.shape)],
        out_specs=[_row_spec(T, 1024), _row_spec(T, 512),
                   _lane_spec(D_MLSTM, T), _lane_spec(D_MLSTM, T), _lane_spec(SUBLANES, T)],
        out_shape=[jax.ShapeDtypeStruct((S, 1024), F32),
                   jax.ShapeDtypeStruct((S, 512), F32),
                   jax.ShapeDtypeStruct((D_MLSTM, S), F32),
                   jax.ShapeDtypeStruct((D_MLSTM, S), F32),
                   jax.ShapeDtypeStruct((SUBLANES, S), F32)],
        compiler_params=_params(),
        name="proj",
    )(x, g, w, wt, wg)


def _log_sigmoid(x):
    return jnp.minimum(x, 0.0) - jnp.log1p(jnp.exp(-jnp.abs(x)))


def _cummax_lanes(x, lane):
    d = 1
    while d < LANES:
        x = jnp.where(lane >= d, jnp.maximum(x, pltpu.roll(x, d, axis=1)), x)
        d *= 2
    return x


def _mlstm_kernel(qk_ref, vt_ref, ot_ref, gr_ref, cw_ref, cb_ref, br_ref, nw_ref,
                  out_ref, xpad_ref, c_ref, m_ref):
    T = MLSTM_ROWS
    L = MLSTM_CHUNK
    D = HEAD_DIM
    n_chunks = T // L

    @pl.when(pl.program_id(0) == 0)
    def _():
        xpad_ref[0:SUBLANES, :] = jnp.zeros((SUBLANES, 2 * D_MLSTM), F32)
        c_ref[...] = jnp.zeros(c_ref.shape, F32)
        m_ref[...] = jnp.zeros(m_ref.shape, F32)

    xpad_ref[SUBLANES:SUBLANES + T, :] = qk_ref[...]

    lane = lax.broadcasted_iota(jnp.int32, (SUBLANES, L), 1)
    row_id = lax.broadcasted_iota(jnp.int32, (L, L), 0)
    col_id = lax.broadcasted_iota(jnp.int32, (L, L), 1)
    causal = row_id <= col_id
    triu = causal.astype(F32)
    g = gr_ref[...] + br_ref[...]
    logf = _log_sigmoid(g)
    m = m_ref[...]
    rows = []
    for c in range(n_chunks):
        i_pre = g[:, c * L:(c + 1) * L]
        lf = pltpu.roll(logf[:, c * L:(c + 1) * L], N_HEADS, axis=0)
        b = jnp.dot(lf, triu, preferred_element_type=F32,
                    precision=lax.Precision.HIGHEST)
        r = i_pre - b
        m_run = _cummax_lanes(r, lane)
        m_last = jnp.broadcast_to(m_run[:, L - 1:L], (SUBLANES, L))
        b_last = jnp.broadcast_to(b[:, L - 1:L], (SUBLANES, L))
        mu = jnp.maximum(m_run, m)
        mx = jnp.maximum(m, m_last)
        rows.append(dict(
            r=r, mu=mu,
            inter_scale=jnp.exp(m - mu),
            eb=jnp.exp(-b - mu),
            w=jnp.exp(r - m_last),
            s_old=jnp.exp(m - mx), s_new=jnp.exp(m_last - mx)))
        m = b_last + mx
    m_ref[...] = m
    r_pad = [jnp.zeros((SUBLANES, L), F32)] * (LANES // SUBLANES - n_chunks)
    r_cols = jnp.concatenate([rw["r"] for rw in rows] + r_pad, axis=0).T

    neg_inf = jnp.float32(-jnp.inf)
    for c in range(n_chunks):
        acc = cb_ref[...]
        for j in range(CONV_WIDTH):
            lo = c * L + SUBLANES - (CONV_WIDTH - 1) + j
            acc = acc + xpad_ref[lo:lo + L, :] * cw_ref[j:j + 1, :]
        qk = acc * jax.nn.sigmoid(acc)

        for h in range(N_HEADS):
            row = lambda name: rows[c][name][h:h + 1, :]
            q = (qk[:, h * D:(h + 1) * D] * (D ** -0.5)).astype(BF16)
            k = qk[:, D_MLSTM + h * D:D_MLSTM + (h + 1) * D].astype(BF16)
            vt = vt_ref[h * D:(h + 1) * D, c * L:(c + 1) * L]
            r_col = r_cols[:, c * SUBLANES + h:c * SUBLANES + h + 1]

            dmat = jnp.exp(jnp.where(causal, r_col - row("mu"), neg_inf))
            s_kq = lax.dot_general(k, q, _NT_DIMS, preferred_element_type=F32) * dmat
            cq = lax.dot_general(c_ref[h].astype(BF16), q, _NT_DIMS,
                                 preferred_element_type=F32)
            isc = row("inter_scale")
            num = isc * cq[0:D, :] + jnp.dot(vt.astype(BF16), s_kq.astype(BF16),
                                             preferred_element_type=F32)
            den = isc * cq[D:D + 1, :] + jnp.sum(s_kq, axis=0, keepdims=True)
            hh = num * (1.0 / jnp.maximum(jnp.abs(den), row("eb")))

            w = row("w")
            vw = jnp.concatenate([vt * w, jnp.broadcast_to(w, (SUBLANES, L))],
                                 axis=0).astype(BF16)
            c_loc = jnp.dot(vw, k, preferred_element_type=F32)
            c_ref[h] = row("s_old") * c_ref[h] + row("s_new") * c_loc

            mean = jnp.mean(hh, axis=0, keepdims=True)
            cen = hh - mean
            var = jnp.mean(cen * cen, axis=0, keepdims=True)
            hn = cen * lax.rsqrt(var + EPS) * nw_ref[h * D:(h + 1) * D, :]
            gate = jax.nn.sigmoid(ot_ref[h * D:(h + 1) * D, c * L:(c + 1) * L])
            out_ref[c * L:(c + 1) * L, h * D:(h + 1) * D] = (hn * gate).T.astype(BF16)

    xpad_ref[0:SUBLANES, :] = xpad_ref[T:T + SUBLANES, :]


def _mlstm_call(qk, vt, ot, gr, cw, cb, bias_r, nw_cols):
    S = qk.shape[0]
    T = MLSTM_ROWS
    return pl.pallas_call(
        _mlstm_kernel,
        grid=(S // T,),
        in_specs=[_row_spec(T, 1024), _lane_spec(D_MLSTM, T), _lane_spec(D_MLSTM, T),
                  _lane_spec(SUBLANES, T),
                  _const_spec(cw.shape), _const_spec(cb.shape),
                  _const_spec(bias_r.shape), _const_spec(nw_cols.shape)],
        out_specs=_row_spec(T, D_MLSTM),
        out_shape=jax.ShapeDtypeStruct((S, D_MLSTM), BF16),
        scratch_shapes=[pltpu.VMEM((T + SUBLANES, 2 * D_MLSTM), F32),
                        pltpu.VMEM((N_HEADS, HEAD_DIM + SUBLANES, HEAD_DIM), F32),
                        pltpu.VMEM((SUBLANES, LANES), F32)],
        compiler_params=_params(),
        name="mlstm",
    )(qk, vt, ot, gr, cw, cb, bias_r, nw_cols)


def _gelu_tanh(x):
    c = math.sqrt(2.0 / math.pi)
    return x * (0.5 * (1.0 + jnp.tanh(c * (x + 0.044715 * (x * x * x)))))


def _s5_kernel(u_ref, a_ref, wb_ref, wc_ref, d_ref, gw_ref, gb_ref, out_ref,
               z_ref, x_ref):
    T = S5_ROWS

    @pl.when(pl.program_id(0) == 0)
    def _():
        x_ref[...] = jnp.zeros(x_ref.shape, F32)

    u = u_ref[...]
    ub = u.astype(BF16)
    for j in range(SUBLANES):
        blk = ub[:, (j // 2) * LANES:(j // 2 + 1) * LANES]
        bu = jnp.dot(blk, wb_ref[j], preferred_element_type=F32)
        for c in range(STATE_BLOCKS):
            z_ref[c, pl.ds(j, T, stride=SUBLANES), :] = bu[:, c * LANES:(c + 1) * LANES]

    half = STATE_BLOCKS // 2
    a_re = [a_ref[:, c * LANES:(c + 1) * LANES] for c in range(half)]
    a_im = [a_ref[:, (half + c) * LANES:(half + c + 1) * LANES] for c in range(half)]

    def step(t, carry):
        off = pl.multiple_of(t * SUBLANES, SUBLANES)
        new = []
        for c in range(half):
            x_re, x_im = carry[c], carry[half + c]
            n_re = a_re[c] * x_re - a_im[c] * x_im + z_ref[c, pl.ds(off, SUBLANES), :]
            n_im = a_re[c] * x_im + a_im[c] * x_re + z_ref[half + c, pl.ds(off, SUBLANES), :]
            z_ref[c, pl.ds(off, SUBLANES), :] = n_re
            z_ref[half + c, pl.ds(off, SUBLANES), :] = n_im
            new.append((n_re, n_im))
        return tuple(n[0] for n in new) + tuple(n[1] for n in new)

    init = tuple(x_ref[:, c * LANES:(c + 1) * LANES] for c in range(STATE_BLOCKS))
    final = lax.fori_loop(0, T, step, init, unroll=8)
    for c in range(STATE_BLOCKS):
        x_ref[:, c * LANES:(c + 1) * LANES] = final[c]

    blocks = []
    for b in range(SUBLANES // 2):
        acc = None
        for j in (2 * b, 2 * b + 1):
            xs = jnp.concatenate(
                [z_ref[c, pl.ds(j, T, stride=SUBLANES), :] for c in range(STATE_BLOCKS)],
                axis=1).astype(BF16)
            part = jnp.dot(xs, wc_ref[j], preferred_element_type=F32)
            acc = part if acc is None else acc + part
        blocks.append(acc)
    y = jnp.concatenate(blocks, axis=1) + d_ref[...] * u
    z = _gelu_tanh(y)
    gate = jax.nn.sigmoid(jnp.dot(z.astype(BF16), gw_ref[...],
                                  preferred_element_type=F32) + gb_ref[...])
    out_ref[...] = (z * gate).astype(BF16)


def _s5_call(u, a_tile, wb, wc, d, gw, gb):
    S = u.shape[0]
    T = S5_ROWS
    return pl.pallas_call(
        _s5_kernel,
        grid=(S // T,),
        in_specs=[_row_spec(T, D_SSM), _const_spec(a_tile.shape), _const_spec(wb.shape),
                  _const_spec(wc.shape), _const_spec(d.shape), _const_spec(gw.shape),
                  _const_spec(gb.shape)],
        out_specs=_row_spec(T, D_SSM),
        out_shape=jax.ShapeDtypeStruct((S, D_SSM), BF16),
        scratch_shapes=[pltpu.VMEM((STATE_BLOCKS, T * SUBLANES, LANES), F32),
                        pltpu.VMEM((SUBLANES, STATE_LANES), F32)],
        compiler_params=_params(),
        name="s5",
    )(u, a_tile, wb, wc, d, gw, gb)


def _s5_weights(lam_re, lam_im, log_dt, b_re, b_im, c_re, c_im):
    lr, li = lam_re.astype(F32), lam_im.astype(F32)
    dt = jnp.exp(log_dt.astype(F32))[:, None]
    mag = jnp.exp(lr * dt)
    abar_re, abar_im = mag * jnp.cos(li * dt), mag * jnp.sin(li * dt)
    den = lr * lr + li * li
    coef_re = ((abar_re - 1.0) * lr + abar_im * li) / den
    coef_im = (abar_im * lr - (abar_re - 1.0) * li) / den
    br, bi = b_re.astype(F32), b_im.astype(F32)
    bbar_re = coef_re[..., None] * br - coef_im[..., None] * bi
    bbar_im = coef_re[..., None] * bi + coef_im[..., None] * br
    J, Q, P, Hc = SUBLANES, GROUPS_PER_SUBLANE, SSM_STATE, SSM_GROUP
    a_tile = jnp.concatenate([abar_re.reshape(J, Q * P),
                              abar_im.reshape(J, Q * P)], axis=1)
    eye = jnp.eye(Q, dtype=F32)
    half = (jnp.arange(J) % 2)[:, None, None, None] == jnp.arange(2)[None, :, None, None]

    def expand_in(b):
        return jnp.einsum('jqpc,qr->jqcrp', b.reshape(J, Q, P, Hc), eye).reshape(J, Q * Hc, Q * P)

    wb = jnp.concatenate([expand_in(bbar_re), expand_in(bbar_im)], axis=2)
    wb = jnp.where(half, wb[:, None], 0.0).reshape(J, 2 * Q * Hc, STATE_LANES)

    def expand_out(c):
        return jnp.einsum('jqcp,qr->jqprc', c.reshape(J, Q, Hc, P), eye).reshape(J, Q * P, Q * Hc)

    wc = jnp.concatenate([expand_out(c_re.astype(F32)), -expand_out(c_im.astype(F32))], axis=1)
    half_c = jnp.swapaxes(half, 1, 2)
    wc = jnp.where(half_c, wc[:, :, None, :], 0.0).reshape(J, STATE_LANES, 2 * Q * Hc)
    return a_tile, wb.astype(BF16), wc.astype(BF16)


def _ffn_kernel(x_ref, hm_ref, y_ref, wo_ref, g2_ref, w1_ref, w2_ref, g3_ref, out_ref):
    mixed = (jnp.dot(hm_ref[...], wo_ref[0:D_MLSTM, :], preferred_element_type=F32)
             + jnp.dot(y_ref[...], wo_ref[D_MLSTM:D_MODEL, :], preferred_element_type=F32))
    x1 = x_ref[...] + mixed
    h2 = _rms(x1, g2_ref[...]).astype(BF16)
    mlp = None
    for c in range(D_FF // FF_CHUNK):
        a = jnp.dot(h2, w1_ref[:, c * FF_CHUNK:(c + 1) * FF_CHUNK],
                    preferred_element_type=F32)
        a = jnp.maximum(a, 0.0)
        part = jnp.dot((a * a).astype(BF16), w2_ref[c * FF_CHUNK:(c + 1) * FF_CHUNK, :],
                       preferred_element_type=F32)
        mlp = part if mlp is None else mlp + part
    out_ref[...] = _rms(x1 + mlp, g3_ref[...])


def _ffn_call(x, hm, y, wo, g2, w1, w2, g3):
    S = x.shape[0]
    T = FFN_ROWS
    single = dict(pipeline_mode=pl.Buffered(WEIGHT_BUFFERS))
    return pl.pallas_call(
        _ffn_kernel,
        grid=(S // T,),
        in_specs=[_row_spec(T, D_MODEL), _row_spec(T, D_MLSTM), _row_spec(T, D_SSM),
                  pl.BlockSpec(wo.shape, lambda i: (0, 0), **single),
                  _const_spec(g2.shape),
                  pl.BlockSpec(w1.shape, lambda i: (0, 0), **single),
                  pl.BlockSpec(w2.shape, lambda i: (0, 0), **single),
                  _const_spec(g3.shape)],
        out_specs=_row_spec(T, D_MODEL),
        out_shape=jax.ShapeDtypeStruct((S, D_MODEL), F32),
        compiler_params=_params(),
        name="ffn",
    )(x, hm, y, wo, g2, w1, w2, g3)


def _layer(x, mix_norm_w, w_in, conv_w, conv_b, i_bias, f_bias, mlstm_norm_w,
           lam_re, lam_im, log_dt, b_re, b_im, c_re, c_im, ssm_d,
           glu_w, glu_b, w_out, mlp_norm_w, w_ff1, w_ff2, out_norm_w):
    n_qk, n_vo, n_gate = 2 * D_MLSTM, 2 * D_MLSTM, 2 * N_HEADS
    w_rows = jnp.concatenate([w_in[:, :n_qk], w_in[:, n_qk + n_vo + n_gate:]],
                             axis=1).astype(BF16)
    w_cols = w_in[:, n_qk:n_qk + n_vo].T.astype(BF16)
    wg = w_in[:, n_qk + n_vo:n_qk + n_vo + n_gate].T.astype(BF16)
    qk, u, vt, ot, gr = _proj_call(x, mix_norm_w[None, :], w_rows, w_cols, wg)

    bias_r = jnp.concatenate([i_bias, f_bias]).astype(F32)[:, None]
    nw_cols = jnp.broadcast_to(mlstm_norm_w.astype(F32)[:, None], (D_MLSTM, LANES))
    hm = _mlstm_call(qk, vt, ot, gr, conv_w, conv_b[None, :], bias_r, nw_cols)

    a_tile, wb, wc = _s5_weights(lam_re, lam_im, log_dt, b_re, b_im, c_re, c_im)
    y = _s5_call(u, a_tile, wb, wc, ssm_d[None, :], glu_w.astype(BF16), glu_b[None, :])

    return _ffn_call(x, hm, y, w_out.astype(BF16), mlp_norm_w[None, :],
                     w_ff1.astype(BF16), w_ff2.astype(BF16), out_norm_w[None, :])


def kernel(x, mix_norm_w, w_in, conv_w, conv_b, i_bias, f_bias, mlstm_norm_w, ssm_lam_re, ssm_lam_im, ssm_log_dt, ssm_b_re, ssm_b_im, ssm_c_re, ssm_c_im, ssm_d, glu_w, glu_b, w_out, mlp_norm_w, w_ff1, w_ff2, final_norm_w):
    assert x.shape[0] == 1 and mix_norm_w.shape[0] == 1
    xs = x[0]
    out = _layer(xs, mix_norm_w[0], w_in[0], conv_w[0], conv_b[0], i_bias[0], f_bias[0],
                 mlstm_norm_w[0], ssm_lam_re[0], ssm_lam_im[0], ssm_log_dt[0],
                 ssm_b_re[0], ssm_b_im[0], ssm_c_re[0], ssm_c_im[0], ssm_d[0],
                 glu_w[0], glu_b[0], w_out[0], mlp_norm_w[0], w_ff1[0], w_ff2[0],
                 final_norm_w)
    return out[None]
```

```python
import math

import jax
import jax.numpy as jnp
from jax import lax
from jax.experimental import pallas as pl
from jax.experimental.pallas import tpu as pltpu

F32 = jnp.float32
BF16 = jnp.bfloat16

D_MODEL = 1024
D_MLSTM = 512
N_HEADS = 4
HEAD_DIM = 128
CONV_WIDTH = 4
D_SSM = 512
SSM_GROUP = 16
N_GROUPS = 32
SSM_STATE = 64
D_FF = 4096
EPS = 1e-6

SUBLANES = 8
LANES = 128

PROJ_ROWS = 512
MLSTM_ROWS = 512
MLSTM_CHUNK = 128
S5_ROWS = 512
S5_SUB = 256
FFN_ROWS = 512
FF_CHUNK = 1024
WEIGHT_BUFFERS = 1
VMEM_LIMIT = 56 * 1024 * 1024

GROUPS_PER_SUBLANE = N_GROUPS // SUBLANES
STATE_HALF = GROUPS_PER_SUBLANE * SSM_STATE
STATE_LANES = 2 * STATE_HALF
STATE_BLOCKS = STATE_LANES // LANES


def _const_spec(shape):
    return pl.BlockSpec(shape, lambda i: (0,) * len(shape))


def _row_spec(rows, cols):
    return pl.BlockSpec((rows, cols), lambda i: (i, 0))


def _lane_spec(rows, cols):
    return pl.BlockSpec((rows, cols), lambda i: (0, i))


def _params():
    return pltpu.CompilerParams(dimension_semantics=("arbitrary",),
                                vmem_limit_bytes=VMEM_LIMIT)


def _rms(x, g):
    r = lax.rsqrt(jnp.mean(x * x, axis=-1, keepdims=True) + EPS)
    return x * r * g


_NT_DIMS = (((1,), (1,)), ((), ()))


def _proj_kernel(x_ref, g_ref, w_ref, wt_ref, wg_ref, qk_ref, u_ref, vt_ref, ot_ref, gr_ref):
    h = _rms(x_ref[...], g_ref[...]).astype(BF16)
    qk_ref[...] = jnp.dot(h, w_ref[:, 0:1024], preferred_element_type=F32)
    u_ref[...] = jnp.dot(h, w_ref[:, 1024:1536], preferred_element_type=F32)
    vt_ref[...] = lax.dot_general(wt_ref[0:D_MLSTM, :], h, _NT_DIMS,
                                  preferred_element_type=F32)
    ot_ref[...] = lax.dot_general(wt_ref[D_MLSTM:2 * D_MLSTM, :], h, _NT_DIMS,
                                  preferred_element_type=F32)
    gr_ref[...] = lax.dot_general(wg_ref[...], h, _NT_DIMS, preferred_element_type=F32)


def _proj_call(x, g, w, wt, wg):
    S = x.shape[0]
    T = PROJ_ROWS
    return pl.pallas_call(
        _proj_kernel,
        grid=(S // T,),
        in_specs=[_row_spec(T, D_MODEL), _const_spec((1, D_MODEL)),
                  _const_spec(w.shape), _const_spec(wt.shape), _const_spec(wg.shape)],
        out_specs=[_row_spec(T, 1024), _row_spec(T, 512),
                   _lane_spec(D_MLSTM, T), _lane_spec(D_MLSTM, T), _lane_spec(SUBLANES, T)],
        out_shape=[jax.ShapeDtypeStruct((S, 1024), F32),
                   jax.ShapeDtypeStruct((S, 512), F32),
                   jax.ShapeDtypeStruct((D_MLSTM, S), F32),
                   jax.ShapeDtypeStruct((D_MLSTM, S), F32),
                   jax.ShapeDtypeStruct((SUBLANES, S), F32)],
        compiler_params=_params(),
        name="proj",
    )(x, g, w, wt, wg)


def _log_sigmoid(x):
    return jnp.minimum(x, 0.0) - jnp.log1p(jnp.exp(-jnp.abs(x)))


def _cummax_lanes(x, lane):
    d = 1
    while d < LANES:
        x = jnp.where(lane >= d, jnp.maximum(x, pltpu.roll(x, d, axis=1)), x)
        d *= 2
    return x


def _mlstm_kernel(qk_ref, vt_ref, ot_ref, gr_ref, cw_ref, cb_ref, br_ref, nw_ref,
                  out_ref, xpad_ref, c_ref, m_ref):
    T = MLSTM_ROWS
    L = MLSTM_CHUNK
    D = HEAD_DIM
    n_chunks = T // L

    @pl.when(pl.program_id(0) == 0)
    def _():
        xpad_ref[0:SUBLANES, :] = jnp.zeros((SUBLANES, 2 * D_MLSTM), F32)
        c_ref[...] = jnp.zeros(c_ref.shape, F32)
        m_ref[...] = jnp.zeros(m_ref.shape, F32)

    xpad_ref[SUBLANES:SUBLANES + T, :] = qk_ref[...]

    lane = lax.broadcasted_iota(jnp.int32, (SUBLANES, L), 1)
    row_id = lax.broadcasted_iota(jnp.int32, (L, L), 0)
    col_id = lax.broadcasted_iota(jnp.int32, (L, L), 1)
    causal = row_id <= col_id
    triu = causal.astype(F32)
    g = gr_ref[...] + br_ref[...]
    logf = _log_sigmoid(g)
    m = m_ref[...]
    rows = []
    for c in range(n_chunks):
        i_pre = g[:, c * L:(c + 1) * L]
        lf = pltpu.roll(logf[:, c * L:(c + 1) * L], N_HEADS, axis=0)
        b = jnp.dot(lf, triu, preferred_element_type=F32,
                    precision=lax.Precision.HIGHEST)
        r = i_pre - b
        m_run = _cummax_lanes(r, lane)
        m_last = jnp.broadcast_to(m_run[:, L - 1:L], (SUBLANES, L))
        b_last = jnp.broadcast_to(b[:, L - 1:L], (SUBLANES, L))
        mu = jnp.maximum(m_run, m)
        mx = jnp.maximum(m, m_last)
        rows.append(dict(
            r=r, mu=mu,
            inter_scale=jnp.exp(m - mu),
            eb=jnp.exp(-b - mu),
            w=jnp.exp(r - m_last),
            s_old=jnp.exp(m - mx), s_new=jnp.exp(m_last - mx)))
        m = b_last + mx
    m_ref[...] = m
    r_pad = [jnp.zeros((SUBLANES, L), F32)] * (LANES // SUBLANES - n_chunks)
    r_cols = jnp.concatenate([rw["r"] for rw in rows] + r_pad, axis=0).T

    neg_inf = jnp.float32(-jnp.inf)
    for c in range(n_chunks):
        acc = cb_ref[...]
        for j in range(CONV_WIDTH):
            lo = c * L + SUBLANES - (CONV_WIDTH - 1) + j
            acc = acc + xpad_ref[lo:lo + L, :] * cw_ref[j:j + 1, :]
        qk = acc * jax.nn.sigmoid(acc)

        for h in range(N_HEADS):
            row = lambda name: rows[c][name][h:h + 1, :]
            q = (qk[:, h * D:(h + 1) * D] * (D ** -0.5)).astype(BF16)
            k = qk[:, D_MLSTM + h * D:D_MLSTM + (h + 1) * D].astype(BF16)
            vt = vt_ref[h * D:(h + 1) * D, c * L:(c + 1) * L]
            r_col = r_cols[:, c * SUBLANES + h:c * SUBLANES + h + 1]

            dmat = jnp.exp(jnp.where(causal, r_col - row("mu"), neg_inf))
            s_kq = lax.dot_general(k, q, _NT_DIMS, preferred_element_type=F32) * dmat
            cq = lax.dot_general(c_ref[h].astype(BF16), q, _NT_DIMS,
                                 preferred_element_type=F32)
            isc = row("inter_scale")
            num = isc * cq[0:D, :] + jnp.dot(vt.astype(BF16), s_kq.astype(BF16),
                                             preferred_element_type=F32)
            den = isc * cq[D:D + 1, :] + jnp.sum(s_kq, axis=0, keepdims=True)
            hh = num * (1.0 / jnp.maximum(jnp.abs(den), row("eb")))

            w = row("w")
            vw = jnp.concatenate([vt * w, jnp.broadcast_to(w, (SUBLANES, L))],
                                 axis=0).astype(BF16)
            c_loc = jnp.dot(vw, k, preferred_element_type=F32)
            c_ref[h] = row("s_old") * c_ref[h] + row("s_new") * c_loc

            mean = jnp.mean(hh, axis=0, keepdims=True)
            cen = hh - mean
            var = jnp.mean(cen * cen, axis=0, keepdims=True)
            hn = cen * lax.rsqrt(var + EPS) * nw_ref[h * D:(h + 1) * D, :]
            gate = jax.nn.sigmoid(ot_ref[h * D:(h + 1) * D, c * L:(c + 1) * L])
            out_ref[c * L:(c + 1) * L, h * D:(h + 1) * D] = (hn * gate).T.astype(BF16)

    xpad_ref[0:SUBLANES, :] = xpad_ref[T:T + SUBLANES, :]


def _mlstm_call(qk, vt, ot, gr, cw, cb, bias_r, nw_cols):
    S = qk.shape[0]
    T = MLSTM_ROWS
    return pl.pallas_call(
        _mlstm_kernel,
        grid=(S // T,),
        in_specs=[_row_spec(T, 1024), _lane_spec(D_MLSTM, T), _lane_spec(D_MLSTM, T),
                  _lane_spec(SUBLANES, T),
                  _const_spec(cw.shape), _const_spec(cb.shape),
                  _const_spec(bias_r.shape), _const_spec(nw_cols.shape)],
        out_specs=_row_spec(T, D_MLSTM),
        out_shape=jax.ShapeDtypeStruct((S, D_MLSTM), BF16),
        scratch_shapes=[pltpu.VMEM((T + SUBLANES, 2 * D_MLSTM), F32),
                        pltpu.VMEM((N_HEADS, HEAD_DIM + SUBLANES, HEAD_DIM), F32),
                        pltpu.VMEM((SUBLANES, LANES), F32)],
        compiler_params=_params(),
        name="mlstm",
    )(qk, vt, ot, gr, cw, cb, bias_r, nw_cols)


def _gelu_tanh(x):
    c = math.sqrt(2.0 / math.pi)
    return x * (0.5 * (1.0 + jnp.tanh(c * (x + 0.044715 * (x * x * x)))))


def _s5_kernel(u_ref, a_ref, wb_ref, wc_ref, d_ref, gw_ref, gb_ref, out_ref,
               x_ref, *z_refs):
    T = S5_ROWS

    @pl.when(pl.program_id(0) == 0)
    def _():
        x_ref[...] = jnp.zeros(x_ref.shape, F32)

    R = S5_SUB
    n_sub = T // R
    half = STATE_BLOCKS // 2
    a_re = [a_ref[:, c * LANES:(c + 1) * LANES] for c in range(half)]
    a_im = [a_ref[:, (half + c) * LANES:(half + c + 1) * LANES] for c in range(half)]

    def expand_piece(sb, j):
        blk = u_ref[sb * R:(sb + 1) * R, (j // 2) * LANES:(j // 2 + 1) * LANES].astype(BF16)
        bu = jnp.dot(blk, wb_ref[j], preferred_element_type=F32)
        for c in range(STATE_BLOCKS):
            z_refs[sb][c, pl.ds(j, R, stride=SUBLANES), :] = bu[:, c * LANES:(c + 1) * LANES]

    def scan_piece(sb, x, t0, t1):
        z_ref = z_refs[sb]
        for t in range(t0, t1):
            rows = slice(t * SUBLANES, (t + 1) * SUBLANES)
            new = list(x)
            for c in range(half):
                x_re, x_im = x[c], x[half + c]
                new[c] = a_re[c] * x_re - a_im[c] * x_im + z_ref[c, rows, :]
                new[half + c] = a_re[c] * x_im + a_im[c] * x_re + z_ref[half + c, rows, :]
                z_ref[c, rows, :] = new[c]
                z_ref[half + c, rows, :] = new[half + c]
            x = new
        return x

    def contract_piece(sb, j):
        xs = jnp.concatenate([z_refs[sb][c, pl.ds(j, R, stride=SUBLANES), :]
                              for c in range(STATE_BLOCKS)], axis=1).astype(BF16)
        return jnp.dot(xs, wc_ref[j], preferred_element_type=F32)

    def finish(sb, parts):
        blocks = [parts[2 * b] + parts[2 * b + 1] for b in range(SUBLANES // 2)]
        y = jnp.concatenate(blocks, axis=1) + d_ref[...] * u_ref[sb * R:(sb + 1) * R, :]
        z = _gelu_tanh(y)
        gate = jax.nn.sigmoid(jnp.dot(z.astype(BF16), gw_ref[...],
                                      preferred_element_type=F32) + gb_ref[...])
        out_ref[sb * R:(sb + 1) * R, :] = (z * gate).astype(BF16)

    x = [x_ref[:, c * LANES:(c + 1) * LANES] for c in range(STATE_BLOCKS)]
    steps = R // SUBLANES
    for j in range(SUBLANES):
        expand_piece(0, j)
    for sb in range(n_sub + 1):
        parts = []
        for j in range(SUBLANES):
            if sb < n_sub:
                x = scan_piece(sb, x, j * steps, (j + 1) * steps)
            if sb + 1 < n_sub:
                expand_piece(sb + 1, j)
            if sb >= 1:
                parts.append(contract_piece(sb - 1, j))
        if sb >= 1:
            finish(sb - 1, parts)
    for c in range(STATE_BLOCKS):
        x_ref[:, c * LANES:(c + 1) * LANES] = x[c]


def _s5_call(u, a_tile, wb, wc, d, gw, gb):
    S = u.shape[0]
    T = S5_ROWS
    return pl.pallas_call(
        _s5_kernel,
        grid=(S // T,),
        in_specs=[_row_spec(T, D_SSM), _const_spec(a_tile.shape), _const_spec(wb.shape),
                  _const_spec(wc.shape), _const_spec(d.shape), _const_spec(gw.shape),
                  _const_spec(gb.shape)],
        out_specs=_row_spec(T, D_SSM),
        out_shape=jax.ShapeDtypeStruct((S, D_SSM), BF16),
        scratch_shapes=([pltpu.VMEM((SUBLANES, STATE_LANES), F32)]
                        + [pltpu.VMEM((STATE_BLOCKS, S5_SUB * SUBLANES, LANES), F32)]
                        * (T // S5_SUB)),
        compiler_params=_params(),
        name="s5",
    )(u, a_tile, wb, wc, d, gw, gb)


def _s5_weights(lam_re, lam_im, log_dt, b_re, b_im, c_re, c_im):
    lr, li = lam_re.astype(F32), lam_im.astype(F32)
    dt = jnp.exp(log_dt.astype(F32))[:, None]
    mag = jnp.exp(lr * dt)
    abar_re, abar_im = mag * jnp.cos(li * dt), mag * jnp.sin(li * dt)
    den = lr * lr + li * li
    coef_re = ((abar_re - 1.0) * lr + abar_im * li) / den
    coef_im = (abar_im * lr - (abar_re - 1.0) * li) / den
    br, bi = b_re.astype(F32), b_im.astype(F32)
    bbar_re = coef_re[..., None] * br - coef_im[..., None] * bi
    bbar_im = coef_re[..., None] * bi + coef_im[..., None] * br
    J, Q, P, Hc = SUBLANES, GROUPS_PER_SUBLANE, SSM_STATE, SSM_GROUP
    a_tile = jnp.concatenate([abar_re.reshape(J, Q * P),
                              abar_im.reshape(J, Q * P)], axis=1)
    eye = jnp.eye(Q, dtype=F32)
    half = (jnp.arange(J) % 2)[:, None, None, None] == jnp.arange(2)[None, :, None, None]

    def expand_in(b):
        return jnp.einsum('jqpc,qr->jqcrp', b.reshape(J, Q, P, Hc), eye).reshape(J, Q * Hc, Q * P)

    wb = jnp.concatenate([expand_in(bbar_re), expand_in(bbar_im)], axis=2)
    wb = jnp.where(half, wb[:, None], 0.0).reshape(J, 2 * Q * Hc, STATE_LANES)

    def expand_out(c):
        return jnp.einsum('jqcp,qr->jqprc', c.reshape(J, Q, Hc, P), eye).reshape(J, Q * P, Q * Hc)

    wc = jnp.concatenate([expand_out(c_re.astype(F32)), -expand_out(c_im.astype(F32))], axis=1)
    half_c = jnp.swapaxes(half, 1, 2)
    wc = jnp.where(half_c, wc[:, :, None, :], 0.0).reshape(J, STATE_LANES, 2 * Q * Hc)
    return a_tile, wb.astype(BF16), wc.astype(BF16)


def _ffn_kernel(x_ref, hm_ref, y_ref, wo_ref, g2_ref, w1_ref, w2_ref, g3_ref, out_ref):
    mixed = (jnp.dot(hm_ref[...], wo_ref[0:D_MLSTM, :], preferred_element_type=F32)
             + jnp.dot(y_ref[...], wo_ref[D_MLSTM:D_MODEL, :], preferred_element_type=F32))
    x1 = x_ref[...] + mixed
    h2 = _rms(x1, g2_ref[...]).astype(BF16)
    mlp = None
    for c in range(D_FF // FF_CHUNK):
        a = jnp.dot(h2, w1_ref[:, c * FF_CHUNK:(c + 1) * FF_CHUNK],
                    preferred_element_type=F32)
        a = jnp.maximum(a, 0.0)
        part = jnp.dot((a * a).astype(BF16), w2_ref[c * FF_CHUNK:(c + 1) * FF_CHUNK, :],
                       preferred_element_type=F32)
        mlp = part if mlp is None else mlp + part
    out_ref[...] = _rms(x1 + mlp, g3_ref[...])


def _ffn_call(x, hm, y, wo, g2, w1, w2, g3):
    S = x.shape[0]
    T = FFN_ROWS
    single = dict(pipeline_mode=pl.Buffered(WEIGHT_BUFFERS))
    return pl.pallas_call(
        _ffn_kernel,
        grid=(S // T,),
        in_specs=[_row_spec(T, D_MODEL), _row_spec(T, D_MLSTM), _row_spec(T, D_SSM),
                  pl.BlockSpec(wo.shape, lambda i: (0, 0), **single),
                  _const_spec(g2.shape),
                  pl.BlockSpec(w1.shape, lambda i: (0, 0), **single),
                  pl.BlockSpec(w2.shape, lambda i: (0, 0), **single),
                  _const_spec(g3.shape)],
        out_specs=_row_spec(T, D_MODEL),
        out_shape=jax.ShapeDtypeStruct((S, D_MODEL), F32),
        compiler_params=_params(),
        name="ffn",
    )(x, hm, y, wo, g2, w1, w2, g3)


def _layer(x, mix_norm_w, w_in, conv_w, conv_b, i_bias, f_bias, mlstm_norm_w,
           lam_re, lam_im, log_dt, b_re, b_im, c_re, c_im, ssm_d,
           glu_w, glu_b, w_out, mlp_norm_w, w_ff1, w_ff2, out_norm_w):
    n_qk, n_vo, n_gate = 2 * D_MLSTM, 2 * D_MLSTM, 2 * N_HEADS
    w_rows = jnp.concatenate([w_in[:, :n_qk], w_in[:, n_qk + n_vo + n_gate:]],
                             axis=1).astype(BF16)
    w_cols = w_in[:, n_qk:n_qk + n_vo].T.astype(BF16)
    wg = w_in[:, n_qk + n_vo:n_qk + n_vo + n_gate].T.astype(BF16)
    qk, u, vt, ot, gr = _proj_call(x, mix_norm_w[None, :], w_rows, w_cols, wg)

    bias_r = jnp.concatenate([i_bias, f_bias]).astype(F32)[:, None]
    nw_cols = jnp.broadcast_to(mlstm_norm_w.astype(F32)[:, None], (D_MLSTM, LANES))
    hm = _mlstm_call(qk, vt, ot, gr, conv_w, conv_b[None, :], bias_r, nw_cols)

    a_tile, wb, wc = _s5_weights(lam_re, lam_im, log_dt, b_re, b_im, c_re, c_im)
    y = _s5_call(u, a_tile, wb, wc, ssm_d[None, :], glu_w.astype(BF16), glu_b[None, :])

    return _ffn_call(x, hm, y, w_out.astype(BF16), mlp_norm_w[None, :],
                     w_ff1.astype(BF16), w_ff2.astype(BF16), out_norm_w[None, :])


def kernel(x, mix_norm_w, w_in, conv_w, conv_b, i_bias, f_bias, mlstm_norm_w, ssm_lam_re, ssm_lam_im, ssm_log_dt, ssm_b_re, ssm_b_im, ssm_c_re, ssm_c_im, ssm_d, glu_w, glu_b, w_out, mlp_norm_w, w_ff1, w_ff2, final_norm_w):
    assert x.shape[0] == 1 and mix_norm_w.shape[0] == 1
    xs = x[0]
    out = _layer(xs, mix_norm_w[0], w_in[0], conv_w[0], conv_b[0], i_bias[0], f_bias[0],
                 mlstm_norm_w[0], ssm_lam_re[0], ssm_lam_im[0], ssm_log_dt[0],
                 ssm_b_re[0], ssm_b_im[0], ssm_c_re[0], ssm_c_im[0], ssm_d[0],
                 glu_w[0], glu_b[0], w_out[0], mlp_norm_w[0], w_ff1[0], w_ff2[0],
                 final_norm_w)
    return out[None]
```

```python
import math

import jax
import jax.numpy as jnp
from jax import lax
from jax.experimental import pallas as pl
from jax.experimental.pallas import tpu as pltpu

F32 = jnp.float32
BF16 = jnp.bfloat16

D_MODEL = 1024
D_MLSTM = 512
N_HEADS = 4
HEAD_DIM = 128
CONV_WIDTH = 4
D_SSM = 512
SSM_GROUP = 16
N_GROUPS = 32
SSM_STATE = 64
D_FF = 4096
EPS = 1e-6

SUBLANES = 8
LANES = 128

PROJ_ROWS = 512
MLSTM_ROWS = 512
MLSTM_CHUNK = 128
S5_ROWS = 512
S5_SUB = 256
S5_PITCH = 2 * S5_SUB + 4
FFN_ROWS = 512
FF_CHUNK = 1024
WEIGHT_BUFFERS = 1
VMEM_LIMIT = 56 * 1024 * 1024

GROUPS_PER_SUBLANE = N_GROUPS // SUBLANES
STATE_HALF = GROUPS_PER_SUBLANE * SSM_STATE
STATE_LANES = 2 * STATE_HALF
STATE_BLOCKS = STATE_LANES // LANES


def _const_spec(shape):
    return pl.BlockSpec(shape, lambda i: (0,) * len(shape))


def _row_spec(rows, cols):
    return pl.BlockSpec((rows, cols), lambda i: (i, 0))


def _lane_spec(rows, cols):
    return pl.BlockSpec((rows, cols), lambda i: (0, i))


def _params():
    return pltpu.CompilerParams(dimension_semantics=("arbitrary",),
                                vmem_limit_bytes=VMEM_LIMIT)


def _rms(x, g):
    r = lax.rsqrt(jnp.mean(x * x, axis=-1, keepdims=True) + EPS)
    return x * r * g


_NT_DIMS = (((1,), (1,)), ((), ()))


def _proj_kernel(x_ref, g_ref, w_ref, wt_ref, wg_ref, qk_ref, u_ref, vt_ref, ot_ref, gr_ref):
    h = _rms(x_ref[...], g_ref[...]).astype(BF16)
    qk_ref[...] = jnp.dot(h, w_ref[:, 0:1024], preferred_element_type=F32)
    u_ref[...] = jnp.dot(h, w_ref[:, 1024:1536], preferred_element_type=F32)
    vt_ref[...] = lax.dot_general(wt_ref[0:D_MLSTM, :], h, _NT_DIMS,
                                  preferred_element_type=F32)
    ot_ref[...] = lax.dot_general(wt_ref[D_MLSTM:2 * D_MLSTM, :], h, _NT_DIMS,
                                  preferred_element_type=F32)
    gr_ref[...] = lax.dot_general(wg_ref[...], h, _NT_DIMS, preferred_element_type=F32)


def _proj_call(x, g, w, wt, wg):
    S = x.shape[0]
    T = PROJ_ROWS
    return pl.pallas_call(
        _proj_kernel,
        grid=(S // T,),
        in_specs=[_row_spec(T, D_MODEL), _const_spec((1, D_MODEL)),
                  _const_spec(w.shape), _const_spec(wt.shape), _const_spec(wg.shape)],
        out_specs=[_row_spec(T, 1024), _row_spec(T, 512),
                   _lane_spec(D_MLSTM, T), _lane_spec(D_MLSTM, T), _lane_spec(SUBLANES, T)],
        out_shape=[jax.ShapeDtypeStruct((S, 1024), F32),
                   jax.ShapeDtypeStruct((S, 512), F32),
                   jax.ShapeDtypeStruct((D_MLSTM, S), F32),
                   jax.ShapeDtypeStruct((D_MLSTM, S), F32),
                   jax.ShapeDtypeStruct((SUBLANES, S), F32)],
        compiler_params=_params(),
        name="proj",
    )(x, g, w, wt, wg)


def _log_sigmoid(x):
    return jnp.minimum(x, 0.0) - jnp.log1p(jnp.exp(-jnp.abs(x)))


def _cummax_lanes(x, lane):
    d = 1
    while d < LANES:
        x = jnp.where(lane >= d, jnp.maximum(x, pltpu.roll(x, d, axis=1)), x)
        d *= 2
    return x


def _mlstm_kernel(qk_ref, vt_ref, ot_ref, gr_ref, cw_ref, cb_ref, br_ref, nw_ref,
                  out_ref, xpad_ref, c_ref, m_ref):
    T = MLSTM_ROWS
    L = MLSTM_CHUNK
    D = HEAD_DIM
    n_chunks = T // L

    @pl.when(pl.program_id(0) == 0)
    def _():
        xpad_ref[0:SUBLANES, :] = jnp.zeros((SUBLANES, 2 * D_MLSTM), F32)
        c_ref[...] = jnp.zeros(c_ref.shape, F32)
        m_ref[...] = jnp.zeros(m_ref.shape, F32)

    xpad_ref[SUBLANES:SUBLANES + T, :] = qk_ref[...]

    lane = lax.broadcasted_iota(jnp.int32, (SUBLANES, L), 1)
    row_id = lax.broadcasted_iota(jnp.int32, (L, L), 0)
    col_id = lax.broadcasted_iota(jnp.int32, (L, L), 1)
    causal = row_id <= col_id
    triu = causal.astype(F32)
    g = gr_ref[...] + br_ref[...]
    logf = _log_sigmoid(g)
    m = m_ref[...]
    rows = []
    for c in range(n_chunks):
        i_pre = g[:, c * L:(c + 1) * L]
        lf = pltpu.roll(logf[:, c * L:(c + 1) * L], N_HEADS, axis=0)
        b = jnp.dot(lf, triu, preferred_element_type=F32,
                    precision=lax.Precision.HIGHEST)
        r = i_pre - b
        m_run = _cummax_lanes(r, lane)
        m_last = jnp.broadcast_to(m_run[:, L - 1:L], (SUBLANES, L))
        b_last = jnp.broadcast_to(b[:, L - 1:L], (SUBLANES, L))
        mu = jnp.maximum(m_run, m)
        mx = jnp.maximum(m, m_last)
        rows.append(dict(
            r=r, mu=mu,
            inter_scale=jnp.exp(m - mu),
            eb=jnp.exp(-b - mu),
            w=jnp.exp(r - m_last),
            s_old=jnp.exp(m - mx), s_new=jnp.exp(m_last - mx)))
        m = b_last + mx
    m_ref[...] = m
    r_pad = [jnp.zeros((SUBLANES, L), F32)] * (LANES // SUBLANES - n_chunks)
    r_cols = jnp.concatenate([rw["r"] for rw in rows] + r_pad, axis=0).T

    neg_inf = jnp.float32(-jnp.inf)
    for c in range(n_chunks):
        acc = cb_ref[...]
        for j in range(CONV_WIDTH):
            lo = c * L + SUBLANES - (CONV_WIDTH - 1) + j
            acc = acc + xpad_ref[lo:lo + L, :] * cw_ref[j:j + 1, :]
        qk = acc * jax.nn.sigmoid(acc)

        for h in range(N_HEADS):
            row = lambda name: rows[c][name][h:h + 1, :]
            q = (qk[:, h * D:(h + 1) * D] * (D ** -0.5)).astype(BF16)
            k = qk[:, D_MLSTM + h * D:D_MLSTM + (h + 1) * D].astype(BF16)
            vt = vt_ref[h * D:(h + 1) * D, c * L:(c + 1) * L]
            r_col = r_cols[:, c * SUBLANES + h:c * SUBLANES + h + 1]

            dmat = jnp.exp(jnp.where(causal, r_col - row("mu"), neg_inf))
            s_kq = lax.dot_general(k, q, _NT_DIMS, preferred_element_type=F32) * dmat
            cq = lax.dot_general(c_ref[h].astype(BF16), q, _NT_DIMS,
                                 preferred_element_type=F32)
            isc = row("inter_scale")
            num = isc * cq[0:D, :] + jnp.dot(vt.astype(BF16), s_kq.astype(BF16),
                                             preferred_element_type=F32)
            den = isc * cq[D:D + 1, :] + jnp.sum(s_kq, axis=0, keepdims=True)
            hh = num * (1.0 / jnp.maximum(jnp.abs(den), row("eb")))

            w = row("w")
            vw = jnp.concatenate([vt * w, jnp.broadcast_to(w, (SUBLANES, L))],
                                 axis=0).astype(BF16)
            c_loc = jnp.dot(vw, k, preferred_element_type=F32)
            c_ref[h] = row("s_old") * c_ref[h] + row("s_new") * c_loc

            mean = jnp.mean(hh, axis=0, keepdims=True)
            cen = hh - mean
            var = jnp.mean(cen * cen, axis=0, keepdims=True)
            hn = cen * lax.rsqrt(var + EPS) * nw_ref[h * D:(h + 1) * D, :]
            gate = jax.nn.sigmoid(ot_ref[h * D:(h + 1) * D, c * L:(c + 1) * L])
            out_ref[c * L:(c + 1) * L, h * D:(h + 1) * D] = (hn * gate).T.astype(BF16)

    xpad_ref[0:SUBLANES, :] = xpad_ref[T:T + SUBLANES, :]


def _mlstm_call(qk, vt, ot, gr, cw, cb, bias_r, nw_cols):
    S = qk.shape[0]
    T = MLSTM_ROWS
    return pl.pallas_call(
        _mlstm_kernel,
        grid=(S // T,),
        in_specs=[_row_spec(T, 1024), _lane_spec(D_MLSTM, T), _lane_spec(D_MLSTM, T),
                  _lane_spec(SUBLANES, T),
                  _const_spec(cw.shape), _const_spec(cb.shape),
                  _const_spec(bias_r.shape), _const_spec(nw_cols.shape)],
        out_specs=_row_spec(T, D_MLSTM),
        out_shape=jax.ShapeDtypeStruct((S, D_MLSTM), BF16),
        scratch_shapes=[pltpu.VMEM((T + SUBLANES, 2 * D_MLSTM), F32),
                        pltpu.VMEM((N_HEADS, HEAD_DIM + SUBLANES, HEAD_DIM), F32),
                        pltpu.VMEM((SUBLANES, LANES), F32)],
        compiler_params=_params(),
        name="mlstm",
    )(qk, vt, ot, gr, cw, cb, bias_r, nw_cols)


def _gelu_tanh(x):
    c = math.sqrt(2.0 / math.pi)
    return x * (0.5 * (1.0 + jnp.tanh(c * (x + 0.044715 * (x * x * x)))))


def _s5_kernel(u_ref, a_ref, wb_ref, wc_ref, d_ref, gw_ref, gb_ref, out_ref,
               x_ref, *z_refs):
    T = S5_ROWS

    @pl.when(pl.program_id(0) == 0)
    def _():
        x_ref[...] = jnp.zeros(x_ref.shape, F32)

    R = S5_SUB
    n_sub = T // R
    half = STATE_BLOCKS // 2
    a_re = [a_ref[:, c * LANES:(c + 1) * LANES] for c in range(half)]
    a_im = [a_ref[:, (half + c) * LANES:(half + c + 1) * LANES] for c in range(half)]

    def expand_piece(sb, j):
        blk = u_ref[sb * R:(sb + 1) * R, (j // 2) * LANES:(j // 2 + 1) * LANES].astype(BF16)
        bu = jnp.dot(blk, wb_ref[j], preferred_element_type=F32)
        for c in range(STATE_BLOCKS):
            z_refs[sb][c // 2, pl.ds(j * S5_PITCH + c % 2, R, stride=2), :] = (
                bu[:, c * LANES:(c + 1) * LANES])

    def scan_piece(sb, x, t0, t1):
        z_ref = z_refs[sb]
        for t in range(t0, t1):
            tile = lambda c: (c // 2, pl.ds(2 * t + c % 2, SUBLANES, stride=S5_PITCH),
                              slice(None))
            new = list(x)
            for c in range(half):
                x_re, x_im = x[c], x[half + c]
                new[c] = a_re[c] * x_re - a_im[c] * x_im + z_ref[tile(c)]
                new[half + c] = a_re[c] * x_im + a_im[c] * x_re + z_ref[tile(half + c)]
                z_ref[tile(c)] = new[c]
                z_ref[tile(half + c)] = new[half + c]
            x = new
        return x

    def contract_piece(sb, j):
        xs = jnp.concatenate(
            [z_refs[sb][c // 2, pl.ds(j * S5_PITCH + c % 2, R, stride=2), :]
             for c in range(STATE_BLOCKS)], axis=1).astype(BF16)
        return jnp.dot(xs, wc_ref[j], preferred_element_type=F32)

    def finish(sb, parts):
        blocks = [parts[2 * b] + parts[2 * b + 1] for b in range(SUBLANES // 2)]
        y = jnp.concatenate(blocks, axis=1) + d_ref[...] * u_ref[sb * R:(sb + 1) * R, :]
        z = _gelu_tanh(y)
        gate = jax.nn.sigmoid(jnp.dot(z.astype(BF16), gw_ref[...],
                                      preferred_element_type=F32) + gb_ref[...])
        out_ref[sb * R:(sb + 1) * R, :] = (z * gate).astype(BF16)

    x = [x_ref[:, c * LANES:(c + 1) * LANES] for c in range(STATE_BLOCKS)]
    steps = R // SUBLANES
    for j in range(SUBLANES):
        expand_piece(0, j)
    for sb in range(n_sub + 1):
        parts = []
        for j in range(SUBLANES):
            if sb < n_sub:
                x = scan_piece(sb, x, j * steps, (j + 1) * steps)
            if sb + 1 < n_sub:
                expand_piece(sb + 1, j)
            if sb >= 1:
                parts.append(contract_piece(sb - 1, j))
        if sb >= 1:
            finish(sb - 1, parts)
    for c in range(STATE_BLOCKS):
        x_ref[:, c * LANES:(c + 1) * LANES] = x[c]


def _s5_call(u, a_tile, wb, wc, d, gw, gb):
    S = u.shape[0]
    T = S5_ROWS
    return pl.pallas_call(
        _s5_kernel,
        grid=(S // T,),
        in_specs=[_row_spec(T, D_SSM), _const_spec(a_tile.shape), _const_spec(wb.shape),
                  _const_spec(wc.shape), _const_spec(d.shape), _const_spec(gw.shape),
                  _const_spec(gb.shape)],
        out_specs=_row_spec(T, D_SSM),
        out_shape=jax.ShapeDtypeStruct((S, D_SSM), BF16),
        scratch_shapes=([pltpu.VMEM((SUBLANES, STATE_LANES), F32)]
                        + [pltpu.VMEM((STATE_BLOCKS // 2, SUBLANES * S5_PITCH, LANES), F32)]
                        * (T // S5_SUB)),
        compiler_params=_params(),
        name="s5",
    )(u, a_tile, wb, wc, d, gw, gb)


def _s5_weights(lam_re, lam_im, log_dt, b_re, b_im, c_re, c_im):
    lr, li = lam_re.astype(F32), lam_im.astype(F32)
    dt = jnp.exp(log_dt.astype(F32))[:, None]
    mag = jnp.exp(lr * dt)
    abar_re, abar_im = mag * jnp.cos(li * dt), mag * jnp.sin(li * dt)
    den = lr * lr + li * li
    coef_re = ((abar_re - 1.0) * lr + abar_im * li) / den
    coef_im = (abar_im * lr - (abar_re - 1.0) * li) / den
    br, bi = b_re.astype(F32), b_im.astype(F32)
    bbar_re = coef_re[..., None] * br - coef_im[..., None] * bi
    bbar_im = coef_re[..., None] * bi + coef_im[..., None] * br
    J, Q, P, Hc = SUBLANES, GROUPS_PER_SUBLANE, SSM_STATE, SSM_GROUP
    a_tile = jnp.concatenate([abar_re.reshape(J, Q * P),
                              abar_im.reshape(J, Q * P)], axis=1)
    eye = jnp.eye(Q, dtype=F32)
    half = (jnp.arange(J) % 2)[:, None, None, None] == jnp.arange(2)[None, :, None, None]

    def expand_in(b):
        return jnp.einsum('jqpc,qr->jqcrp', b.reshape(J, Q, P, Hc), eye).reshape(J, Q * Hc, Q * P)

    wb = jnp.concatenate([expand_in(bbar_re), expand_in(bbar_im)], axis=2)
    wb = jnp.where(half, wb[:, None], 0.0).reshape(J, 2 * Q * Hc, STATE_LANES)

    def expand_out(c):
        return jnp.einsum('jqcp,qr->jqprc', c.reshape(J, Q, Hc, P), eye).reshape(J, Q * P, Q * Hc)

    wc = jnp.concatenate([expand_out(c_re.astype(F32)), -expand_out(c_im.astype(F32))], axis=1)
    half_c = jnp.swapaxes(half, 1, 2)
    wc = jnp.where(half_c, wc[:, :, None, :], 0.0).reshape(J, STATE_LANES, 2 * Q * Hc)
    return a_tile, wb.astype(BF16), wc.astype(BF16)


def _ffn_kernel(x_ref, hm_ref, y_ref, wo_ref, g2_ref, w1_ref, w2_ref, g3_ref, out_ref):
    mixed = (jnp.dot(hm_ref[...], wo_ref[0:D_MLSTM, :], preferred_element_type=F32)
             + jnp.dot(y_ref[...], wo_ref[D_MLSTM:D_MODEL, :], preferred_element_type=F32))
    x1 = x_ref[...] + mixed
    h2 = _rms(x1, g2_ref[...]).astype(BF16)
    mlp = None
    for c in range(D_FF // FF_CHUNK):
        a = jnp.dot(h2, w1_ref[:, c * FF_CHUNK:(c + 1) * FF_CHUNK],
                    preferred_element_type=F32)
        a = jnp.maximum(a, 0.0)
        part = jnp.dot((a * a).astype(BF16), w2_ref[c * FF_CHUNK:(c + 1) * FF_CHUNK, :],
                       preferred_element_type=F32)
        mlp = part if mlp is None else mlp + part
    out_ref[...] = _rms(x1 + mlp, g3_ref[...])


def _ffn_call(x, hm, y, wo, g2, w1, w2, g3):
    S = x.shape[0]
    T = FFN_ROWS
    single = dict(pipeline_mode=pl.Buffered(WEIGHT_BUFFERS))
    return pl.pallas_call(
        _ffn_kernel,
        grid=(S // T,),
        in_specs=[_row_spec(T, D_MODEL), _row_spec(T, D_MLSTM), _row_spec(T, D_SSM),
                  pl.BlockSpec(wo.shape, lambda i: (0, 0), **single),
                  _const_spec(g2.shape),
                  pl.BlockSpec(w1.shape, lambda i: (0, 0), **single),
                  pl.BlockSpec(w2.shape, lambda i: (0, 0), **single),
                  _const_spec(g3.shape)],
        out_specs=_row_spec(T, D_MODEL),
        out_shape=jax.ShapeDtypeStruct((S, D_MODEL), F32),
        compiler_params=_params(),
        name="ffn",
    )(x, hm, y, wo, g2, w1, w2, g3)


def _layer(x, mix_norm_w, w_in, conv_w, conv_b, i_bias, f_bias, mlstm_norm_w,
           lam_re, lam_im, log_dt, b_re, b_im, c_re, c_im, ssm_d,
           glu_w, glu_b, w_out, mlp_norm_w, w_ff1, w_ff2, out_norm_w):
    n_qk, n_vo, n_gate = 2 * D_MLSTM, 2 * D_MLSTM, 2 * N_HEADS
    w_rows = jnp.concatenate([w_in[:, :n_qk], w_in[:, n_qk + n_vo + n_gate:]],
                             axis=1).astype(BF16)
    w_cols = w_in[:, n_qk:n_qk + n_vo].T.astype(BF16)
    wg = w_in[:, n_qk + n_vo:n_qk + n_vo + n_gate].T.astype(BF16)
    qk, u, vt, ot, gr = _proj_call(x, mix_norm_w[None, :], w_rows, w_cols, wg)

    bias_r = jnp.concatenate([i_bias, f_bias]).astype(F32)[:, None]
    nw_cols = jnp.broadcast_to(mlstm_norm_w.astype(F32)[:, None], (D_MLSTM, LANES))
    hm = _mlstm_call(qk, vt, ot, gr, conv_w, conv_b[None, :], bias_r, nw_cols)

    a_tile, wb, wc = _s5_weights(lam_re, lam_im, log_dt, b_re, b_im, c_re, c_im)
    y = _s5_call(u, a_tile, wb, wc, ssm_d[None, :], glu_w.astype(BF16), glu_b[None, :])

    return _ffn_call(x, hm, y, w_out.astype(BF16), mlp_norm_w[None, :],
                     w_ff1.astype(BF16), w_ff2.astype(BF16), out_norm_w[None, :])


def kernel(x, mix_norm_w, w_in, conv_w, conv_b, i_bias, f_bias, mlstm_norm_w, ssm_lam_re, ssm_lam_im, ssm_log_dt, ssm_b_re, ssm_b_im, ssm_c_re, ssm_c_im, ssm_d, glu_w, glu_b, w_out, mlp_norm_w, w_ff1, w_ff2, final_norm_w):
    assert x.shape[0] == 1 and mix_norm_w.shape[0] == 1
    xs = x[0]
    out = _layer(xs, mix_norm_w[0], w_in[0], conv_w[0], conv_b[0], i_bias[0], f_bias[0],
                 mlstm_norm_w[0], ssm_lam_re[0], ssm_lam_im[0], ssm_log_dt[0],
                 ssm_b_re[0], ssm_b_im[0], ssm_c_re[0], ssm_c_im[0], ssm_d[0],
                 glu_w[0], glu_b[0], w_out[0], mlp_norm_w[0], w_ff1[0], w_ff2[0],
                 final_norm_w)
    return out[None]
```

```python
import math

import jax
import jax.numpy as jnp
from jax import lax
from jax.experimental import pallas as pl
from jax.experimental.pallas import tpu as pltpu

F32 = jnp.float32
BF16 = jnp.bfloat16

D_MODEL = 1024
D_MLSTM = 512
N_HEADS = 4
HEAD_DIM = 128
CONV_WIDTH = 4
D_SSM = 512
SSM_GROUP = 16
N_GROUPS = 32
SSM_STATE = 64
D_FF = 4096
EPS = 1e-6

SUBLANES = 8
LANES = 128

PROJ_ROWS = 512
MLSTM_ROWS = 512
MLSTM_CHUNK = 128
S5_ROWS = 512
S5_SUB = 256
S5_PITCH = 2 * S5_SUB + 4
FFN_ROWS = 512
FF_CHUNK = 1024
WEIGHT_BUFFERS = 1
VMEM_LIMIT = 56 * 1024 * 1024

GROUPS_PER_SUBLANE = N_GROUPS // SUBLANES
STATE_HALF = GROUPS_PER_SUBLANE * SSM_STATE
STATE_LANES = 2 * STATE_HALF
STATE_BLOCKS = STATE_LANES // LANES


def _const_spec(shape):
    return pl.BlockSpec(shape, lambda i: (0,) * len(shape))


def _row_spec(rows, cols):
    return pl.BlockSpec((rows, cols), lambda i: (i, 0))


def _lane_spec(rows, cols):
    return pl.BlockSpec((rows, cols), lambda i: (0, i))


def _params():
    return pltpu.CompilerParams(dimension_semantics=("arbitrary",),
                                vmem_limit_bytes=VMEM_LIMIT)


def _rms(x, g):
    r = lax.rsqrt(jnp.mean(x * x, axis=-1, keepdims=True) + EPS)
    return x * r * g


_NT_DIMS = (((1,), (1,)), ((), ()))


def _proj_kernel(x_ref, g_ref, w_ref, wt_ref, wg_ref, cw_ref, cb_ref,
                 qk_ref, u_ref, vt_ref, ot_ref, gr_ref, xpad_ref):
    T = PROJ_ROWS
    D = HEAD_DIM

    @pl.when(pl.program_id(0) == 0)
    def _():
        xpad_ref[0:SUBLANES, :] = jnp.zeros((SUBLANES, 2 * D_MLSTM), F32)

    h = _rms(x_ref[...], g_ref[...]).astype(BF16)
    xpad_ref[SUBLANES:SUBLANES + T, :] = jnp.dot(h, w_ref[:, 0:1024],
                                                 preferred_element_type=F32)

    def conv_silu(c):
        L = T // 4
        acc = cb_ref[...]
        for j in range(CONV_WIDTH):
            lo = c * L + SUBLANES - (CONV_WIDTH - 1) + j
            acc = acc + xpad_ref[lo:lo + L, :] * cw_ref[j:j + 1, :]
        qk = acc * jax.nn.sigmoid(acc)
        qk_ref[c * L:(c + 1) * L, 0:D_MLSTM] = (qk[:, 0:D_MLSTM] * (D ** -0.5)).astype(BF16)
        qk_ref[c * L:(c + 1) * L, D_MLSTM:] = qk[:, D_MLSTM:].astype(BF16)

    conv_silu(0)
    u_ref[...] = jnp.dot(h, w_ref[:, 1024:1536], preferred_element_type=F32)
    conv_silu(1)
    vt_ref[...] = lax.dot_general(wt_ref[0:D_MLSTM, :], h, _NT_DIMS,
                                  preferred_element_type=F32)
    conv_silu(2)
    ot_ref[...] = lax.dot_general(wt_ref[D_MLSTM:2 * D_MLSTM, :], h, _NT_DIMS,
                                  preferred_element_type=F32)
    conv_silu(3)
    gr_ref[...] = lax.dot_general(wg_ref[...], h, _NT_DIMS, preferred_element_type=F32)
    xpad_ref[0:SUBLANES, :] = xpad_ref[T:T + SUBLANES, :]


def _proj_call(x, g, w, wt, wg, cw, cb):
    S = x.shape[0]
    T = PROJ_ROWS
    return pl.pallas_call(
        _proj_kernel,
        grid=(S // T,),
        in_specs=[_row_spec(T, D_MODEL), _const_spec((1, D_MODEL)),
                  _const_spec(w.shape), _const_spec(wt.shape), _const_spec(wg.shape),
                  _const_spec(cw.shape), _const_spec(cb.shape)],
        out_specs=[_row_spec(T, 1024), _row_spec(T, 512),
                   _lane_spec(D_MLSTM, T), _lane_spec(D_MLSTM, T), _lane_spec(SUBLANES, T)],
        out_shape=[jax.ShapeDtypeStruct((S, 1024), BF16),
                   jax.ShapeDtypeStruct((S, 512), F32),
                   jax.ShapeDtypeStruct((D_MLSTM, S), F32),
                   jax.ShapeDtypeStruct((D_MLSTM, S), F32),
                   jax.ShapeDtypeStruct((SUBLANES, S), F32)],
        scratch_shapes=[pltpu.VMEM((T + SUBLANES, 2 * D_MLSTM), F32)],
        compiler_params=_params(),
        name="proj",
    )(x, g, w, wt, wg, cw, cb)


def _log_sigmoid(x):
    return jnp.minimum(x, 0.0) - jnp.log1p(jnp.exp(-jnp.abs(x)))


def _cummax_lanes(x, lane):
    d = 1
    while d < LANES:
        x = jnp.where(lane >= d, jnp.maximum(x, pltpu.roll(x, d, axis=1)), x)
        d *= 2
    return x


def _mlstm_kernel(qk_ref, vt_ref, ot_ref, gr_ref, br_ref, nw_ref, out_ref, c_ref, m_ref):
    T = MLSTM_ROWS
    L = MLSTM_CHUNK
    D = HEAD_DIM
    n_chunks = T // L

    @pl.when(pl.program_id(0) == 0)
    def _():
        c_ref[...] = jnp.zeros(c_ref.shape, F32)
        m_ref[...] = jnp.zeros(m_ref.shape, F32)

    lane = lax.broadcasted_iota(jnp.int32, (SUBLANES, L), 1)
    row_id = lax.broadcasted_iota(jnp.int32, (L, L), 0)
    col_id = lax.broadcasted_iota(jnp.int32, (L, L), 1)
    causal = row_id <= col_id
    triu = causal.astype(F32)
    g = gr_ref[...] + br_ref[...]
    logf = _log_sigmoid(g)
    m = m_ref[...]
    rows = []
    for c in range(n_chunks):
        i_pre = g[:, c * L:(c + 1) * L]
        lf = pltpu.roll(logf[:, c * L:(c + 1) * L], N_HEADS, axis=0)
        b = jnp.dot(lf, triu, preferred_element_type=F32,
                    precision=lax.Precision.HIGHEST)
        r = i_pre - b
        m_run = _cummax_lanes(r, lane)
        m_last = jnp.broadcast_to(m_run[:, L - 1:L], (SUBLANES, L))
        b_last = jnp.broadcast_to(b[:, L - 1:L], (SUBLANES, L))
        mu = jnp.maximum(m_run, m)
        mx = jnp.maximum(m, m_last)
        rows.append(dict(
            r=r, mu=mu,
            inter_scale=jnp.exp(m - mu),
            eb=jnp.exp(-b - mu),
            w=jnp.exp(r - m_last),
            s_old=jnp.exp(m - mx), s_new=jnp.exp(m_last - mx)))
        m = b_last + mx
    m_ref[...] = m
    r_pad = [jnp.zeros((SUBLANES, L), F32)] * (LANES // SUBLANES - n_chunks)
    r_cols = jnp.concatenate([rw["r"] for rw in rows] + r_pad, axis=0).T

    neg_inf = jnp.float32(-jnp.inf)
    for c in range(n_chunks):
        for h in range(N_HEADS):
            row = lambda name: rows[c][name][h:h + 1, :]
            q = qk_ref[c * L:(c + 1) * L, h * D:(h + 1) * D]
            k = qk_ref[c * L:(c + 1) * L, D_MLSTM + h * D:D_MLSTM + (h + 1) * D]
            vt = vt_ref[h * D:(h + 1) * D, c * L:(c + 1) * L]
            r_col = r_cols[:, c * SUBLANES + h:c * SUBLANES + h + 1]

            dmat = jnp.exp(jnp.where(causal, r_col - row("mu"), neg_inf))
            s_kq = lax.dot_general(k, q, _NT_DIMS, preferred_element_type=F32) * dmat
            cq = lax.dot_general(c_ref[h].astype(BF16), q, _NT_DIMS,
                                 preferred_element_type=F32)
            isc = row("inter_scale")
            num = isc * cq[0:D, :] + jnp.dot(vt.astype(BF16), s_kq.astype(BF16),
                                             preferred_element_type=F32)
            den = isc * cq[D:D + 1, :] + jnp.sum(s_kq, axis=0, keepdims=True)
            hh = num * (1.0 / jnp.maximum(jnp.abs(den), row("eb")))

            w = row("w")
            vw = jnp.concatenate([vt * w, jnp.broadcast_to(w, (SUBLANES, L))],
                                 axis=0).astype(BF16)
            c_loc = jnp.dot(vw, k, preferred_element_type=F32)
            c_ref[h] = row("s_old") * c_ref[h] + row("s_new") * c_loc

            mean = jnp.mean(hh, axis=0, keepdims=True)
            cen = hh - mean
            var = jnp.mean(cen * cen, axis=0, keepdims=True)
            hn = cen * lax.rsqrt(var + EPS) * nw_ref[h * D:(h + 1) * D, :]
            gate = jax.nn.sigmoid(ot_ref[h * D:(h + 1) * D, c * L:(c + 1) * L])
            out_ref[c * L:(c + 1) * L, h * D:(h + 1) * D] = (hn * gate).T.astype(BF16)


def _mlstm_call(qk, vt, ot, gr, bias_r, nw_cols):
    S = qk.shape[0]
    T = MLSTM_ROWS
    return pl.pallas_call(
        _mlstm_kernel,
        grid=(S // T,),
        in_specs=[_row_spec(T, 1024), _lane_spec(D_MLSTM, T), _lane_spec(D_MLSTM, T),
                  _lane_spec(SUBLANES, T),
                  _const_spec(bias_r.shape), _const_spec(nw_cols.shape)],
        out_specs=_row_spec(T, D_MLSTM),
        out_shape=jax.ShapeDtypeStruct((S, D_MLSTM), BF16),
        scratch_shapes=[pltpu.VMEM((N_HEADS, HEAD_DIM + SUBLANES, HEAD_DIM), F32),
                        pltpu.VMEM((SUBLANES, LANES), F32)],
        compiler_params=_params(),
        name="mlstm",
    )(qk, vt, ot, gr, bias_r, nw_cols)


def _gelu_tanh(x):
    c = math.sqrt(2.0 / math.pi)
    return x * (0.5 * (1.0 + jnp.tanh(c * (x + 0.044715 * (x * x * x)))))


def _s5_kernel(u_ref, a_ref, wb_ref, wc_ref, d_ref, gw_ref, gb_ref, out_ref,
               x_ref, *z_refs):
    T = S5_ROWS

    @pl.when(pl.program_id(0) == 0)
    def _():
        x_ref[...] = jnp.zeros(x_ref.shape, F32)

    R = S5_SUB
    n_sub = T // R
    half = STATE_BLOCKS // 2
    a_re = [a_ref[:, c * LANES:(c + 1) * LANES] for c in range(half)]
    a_im = [a_ref[:, (half + c) * LANES:(half + c + 1) * LANES] for c in range(half)]

    def expand_piece(sb, j):
        blk = u_ref[sb * R:(sb + 1) * R, (j // 2) * LANES:(j // 2 + 1) * LANES].astype(BF16)
        bu = jnp.dot(blk, wb_ref[j], preferred_element_type=F32)
        for c in range(STATE_BLOCKS):
            z_refs[sb][c // 2, pl.ds(j * S5_PITCH + c % 2, R, stride=2), :] = (
                bu[:, c * LANES:(c + 1) * LANES])

    def scan_piece(sb, x, t0, t1):
        z_ref = z_refs[sb]
        for t in range(t0, t1):
            tile = lambda c: (c // 2, pl.ds(2 * t + c % 2, SUBLANES, stride=S5_PITCH),
                              slice(None))
            new = list(x)
            for c in range(half):
                x_re, x_im = x[c], x[half + c]
                new[c] = a_re[c] * x_re - a_im[c] * x_im + z_ref[tile(c)]
                new[half + c] = a_re[c] * x_im + a_im[c] * x_re + z_ref[tile(half + c)]
                z_ref[tile(c)] = new[c]
                z_ref[tile(half + c)] = new[half + c]
            x = new
        return x

    def contract_piece(sb, j):
        xs = jnp.concatenate(
            [z_refs[sb][c // 2, pl.ds(j * S5_PITCH + c % 2, R, stride=2), :]
             for c in range(STATE_BLOCKS)], axis=1).astype(BF16)
        return jnp.dot(xs, wc_ref[j], preferred_element_type=F32)

    def finish(sb, parts):
        blocks = [parts[2 * b] + parts[2 * b + 1] for b in range(SUBLANES // 2)]
        y = jnp.concatenate(blocks, axis=1) + d_ref[...] * u_ref[sb * R:(sb + 1) * R, :]
        z = _gelu_tanh(y)
        gate = jax.nn.sigmoid(jnp.dot(z.astype(BF16), gw_ref[...],
                                      preferred_element_type=F32) + gb_ref[...])
        out_ref[sb * R:(sb + 1) * R, :] = (z * gate).astype(BF16)

    x = [x_ref[:, c * LANES:(c + 1) * LANES] for c in range(STATE_BLOCKS)]
    steps = R // SUBLANES
    for j in range(SUBLANES):
        expand_piece(0, j)
    for sb in range(n_sub + 1):
        parts = []
        for j in range(SUBLANES):
            if sb < n_sub:
                x = scan_piece(sb, x, j * steps, (j + 1) * steps)
            if sb + 1 < n_sub:
                expand_piece(sb + 1, j)
            if sb >= 1:
                parts.append(contract_piece(sb - 1, j))
        if sb >= 1:
            finish(sb - 1, parts)
    for c in range(STATE_BLOCKS):
        x_ref[:, c * LANES:(c + 1) * LANES] = x[c]


def _s5_call(u, a_tile, wb, wc, d, gw, gb):
    S = u.shape[0]
    T = S5_ROWS
    return pl.pallas_call(
        _s5_kernel,
        grid=(S // T,),
        in_specs=[_row_spec(T, D_SSM), _const_spec(a_tile.shape), _const_spec(wb.shape),
                  _const_spec(wc.shape), _const_spec(d.shape), _const_spec(gw.shape),
                  _const_spec(gb.shape)],
        out_specs=_row_spec(T, D_SSM),
        out_shape=jax.ShapeDtypeStruct((S, D_SSM), BF16),
        scratch_shapes=([pltpu.VMEM((SUBLANES, STATE_LANES), F32)]
                        + [pltpu.VMEM((STATE_BLOCKS // 2, SUBLANES * S5_PITCH, LANES), F32)]
                        * (T // S5_SUB)),
        compiler_params=_params(),
        name="s5",
    )(u, a_tile, wb, wc, d, gw, gb)


def _s5_weights(lam_re, lam_im, log_dt, b_re, b_im, c_re, c_im):
    lr, li = lam_re.astype(F32), lam_im.astype(F32)
    dt = jnp.exp(log_dt.astype(F32))[:, None]
    mag = jnp.exp(lr * dt)
    abar_re, abar_im = mag * jnp.cos(li * dt), mag * jnp.sin(li * dt)
    den = lr * lr + li * li
    coef_re = ((abar_re - 1.0) * lr + abar_im * li) / den
    coef_im = (abar_im * lr - (abar_re - 1.0) * li) / den
    br, bi = b_re.astype(F32), b_im.astype(F32)
    bbar_re = coef_re[..., None] * br - coef_im[..., None] * bi
    bbar_im = coef_re[..., None] * bi + coef_im[..., None] * br
    J, Q, P, Hc = SUBLANES, GROUPS_PER_SUBLANE, SSM_STATE, SSM_GROUP
    a_tile = jnp.concatenate([abar_re.reshape(J, Q * P),
                              abar_im.reshape(J, Q * P)], axis=1)
    eye = jnp.eye(Q, dtype=F32)
    half = (jnp.arange(J) % 2)[:, None, None, None] == jnp.arange(2)[None, :, None, None]

    def expand_in(b):
        return jnp.einsum('jqpc,qr->jqcrp', b.reshape(J, Q, P, Hc), eye).reshape(J, Q * Hc, Q * P)

    wb = jnp.concatenate([expand_in(bbar_re), expand_in(bbar_im)], axis=2)
    wb = jnp.where(half, wb[:, None], 0.0).reshape(J, 2 * Q * Hc, STATE_LANES)

    def expand_out(c):
        return jnp.einsum('jqcp,qr->jqprc', c.reshape(J, Q, Hc, P), eye).reshape(J, Q * P, Q * Hc)

    wc = jnp.concatenate([expand_out(c_re.astype(F32)), -expand_out(c_im.astype(F32))], axis=1)
    half_c = jnp.swapaxes(half, 1, 2)
    wc = jnp.where(half_c, wc[:, :, None, :], 0.0).reshape(J, STATE_LANES, 2 * Q * Hc)
    return a_tile, wb.astype(BF16), wc.astype(BF16)


def _ffn_kernel(x_ref, hm_ref, y_ref, wo_ref, g2_ref, w1_ref, w2_ref, g3_ref, out_ref):
    mixed = (jnp.dot(hm_ref[...], wo_ref[0:D_MLSTM, :], preferred_element_type=F32)
             + jnp.dot(y_ref[...], wo_ref[D_MLSTM:D_MODEL, :], preferred_element_type=F32))
    x1 = x_ref[...] + mixed
    h2 = _rms(x1, g2_ref[...]).astype(BF16)
    mlp = None
    for c in range(D_FF // FF_CHUNK):
        a = jnp.dot(h2, w1_ref[:, c * FF_CHUNK:(c + 1) * FF_CHUNK],
                    preferred_element_type=F32)
        a = jnp.maximum(a, 0.0)
        part = jnp.dot((a * a).astype(BF16), w2_ref[c * FF_CHUNK:(c + 1) * FF_CHUNK, :],
                       preferred_element_type=F32)
        mlp = part if mlp is None else mlp + part
    out_ref[...] = _rms(x1 + mlp, g3_ref[...])


def _ffn_call(x, hm, y, wo, g2, w1, w2, g3):
    S = x.shape[0]
    T = FFN_ROWS
    single = dict(pipeline_mode=pl.Buffered(WEIGHT_BUFFERS))
    return pl.pallas_call(
        _ffn_kernel,
        grid=(S // T,),
        in_specs=[_row_spec(T, D_MODEL), _row_spec(T, D_MLSTM), _row_spec(T, D_SSM),
                  pl.BlockSpec(wo.shape, lambda i: (0, 0), **single),
                  _const_spec(g2.shape),
                  pl.BlockSpec(w1.shape, lambda i: (0, 0), **single),
                  pl.BlockSpec(w2.shape, lambda i: (0, 0), **single),
                  _const_spec(g3.shape)],
        out_specs=_row_spec(T, D_MODEL),
        out_shape=jax.ShapeDtypeStruct((S, D_MODEL), F32),
        compiler_params=_params(),
        name="ffn",
    )(x, hm, y, wo, g2, w1, w2, g3)


def _layer(x, mix_norm_w, w_in, conv_w, conv_b, i_bias, f_bias, mlstm_norm_w,
           lam_re, lam_im, log_dt, b_re, b_im, c_re, c_im, ssm_d,
           glu_w, glu_b, w_out, mlp_norm_w, w_ff1, w_ff2, out_norm_w):
    n_qk, n_vo, n_gate = 2 * D_MLSTM, 2 * D_MLSTM, 2 * N_HEADS
    w_rows = jnp.concatenate([w_in[:, :n_qk], w_in[:, n_qk + n_vo + n_gate:]],
                             axis=1).astype(BF16)
    w_cols = w_in[:, n_qk:n_qk + n_vo].T.astype(BF16)
    wg = w_in[:, n_qk + n_vo:n_qk + n_vo + n_gate].T.astype(BF16)
    qk, u, vt, ot, gr = _proj_call(x, mix_norm_w[None, :], w_rows, w_cols, wg,
                                   conv_w, conv_b[None, :])

    bias_r = jnp.concatenate([i_bias, f_bias]).astype(F32)[:, None]
    nw_cols = jnp.broadcast_to(mlstm_norm_w.astype(F32)[:, None], (D_MLSTM, LANES))
    hm = _mlstm_call(qk, vt, ot, gr, bias_r, nw_cols)

    a_tile, wb, wc = _s5_weights(lam_re, lam_im, log_dt, b_re, b_im, c_re, c_im)
    y = _s5_call(u, a_tile, wb, wc, ssm_d[None, :], glu_w.astype(BF16), glu_b[None, :])

    return _ffn_call(x, hm, y, w_out.astype(BF16), mlp_norm_w[None, :],
                     w_ff1.astype(BF16), w_ff2.astype(BF16), out_norm_w[None, :])


def kernel(x, mix_norm_w, w_in, conv_w, conv_b, i_bias, f_bias, mlstm_norm_w, ssm_lam_re, ssm_lam_im, ssm_log_dt, ssm_b_re, ssm_b_im, ssm_c_re, ssm_c_im, ssm_d, glu_w, glu_b, w_out, mlp_norm_w, w_ff1, w_ff2, final_norm_w):
    assert x.shape[0] == 1 and mix_norm_w.shape[0] == 1
    xs = x[0]
    out = _layer(xs, mix_norm_w[0], w_in[0], conv_w[0], conv_b[0], i_bias[0], f_bias[0],
                 mlstm_norm_w[0], ssm_lam_re[0], ssm_lam_im[0], ssm_log_dt[0],
                 ssm_b_re[0], ssm_b_im[0], ssm_c_re[0], ssm_c_im[0], ssm_d[0],
                 glu_w[0], glu_b[0], w_out[0], mlp_norm_w[0], w_ff1[0], w_ff2[0],
                 final_norm_w)
    return out[None]
```

```python
import math

import jax
import jax.numpy as jnp
from jax import lax
from jax.experimental import pallas as pl
from jax.experimental.pallas import tpu as pltpu

F32 = jnp.float32
BF16 = jnp.bfloat16

D_MODEL = 1024
D_MLSTM = 512
N_HEADS = 4
HEAD_DIM = 128
CONV_WIDTH = 4
D_SSM = 512
SSM_GROUP = 16
N_GROUPS = 32
SSM_STATE = 64
D_FF = 4096
EPS = 1e-6

SUBLANES = 8
LANES = 128

PROJ_ROWS = 1024
MLSTM_ROWS = 512
MLSTM_CHUNK = 128
S5_ROWS = 512
S5_SUB = 256
S5_PITCH = 2 * S5_SUB + 4
FFN_ROWS = 1024
FF_CHUNK = 1024
WEIGHT_BUFFERS = 1
VMEM_LIMIT = 56 * 1024 * 1024

GROUPS_PER_SUBLANE = N_GROUPS // SUBLANES
STATE_HALF = GROUPS_PER_SUBLANE * SSM_STATE
STATE_LANES = 2 * STATE_HALF
STATE_BLOCKS = STATE_LANES // LANES


def _const_spec(shape):
    return pl.BlockSpec(shape, lambda i: (0,) * len(shape))


def _row_spec(rows, cols):
    return pl.BlockSpec((rows, cols), lambda i: (i, 0))


def _lane_spec(rows, cols):
    return pl.BlockSpec((rows, cols), lambda i: (0, i))


def _params(flags=None):
    return pltpu.CompilerParams(dimension_semantics=("arbitrary",),
                                vmem_limit_bytes=VMEM_LIMIT, flags=flags)


def _rms(x, g):
    r = lax.rsqrt(jnp.mean(x * x, axis=-1, keepdims=True) + EPS)
    return x * r * g


_NT_DIMS = (((1,), (1,)), ((), ()))


def _proj_kernel(x_ref, g_ref, w_ref, wt_ref, wg_ref, qk_ref, u_ref, vt_ref, ot_ref, gr_ref):
    h = _rms(x_ref[...], g_ref[...]).astype(BF16)
    qk_ref[...] = jnp.dot(h, w_ref[:, 0:1024], preferred_element_type=F32)
    u_ref[...] = jnp.dot(h, w_ref[:, 1024:1536], preferred_element_type=F32)
    vt_ref[...] = lax.dot_general(wt_ref[0:D_MLSTM, :], h, _NT_DIMS,
                                  preferred_element_type=F32)
    ot_ref[...] = lax.dot_general(wt_ref[D_MLSTM:2 * D_MLSTM, :], h, _NT_DIMS,
                                  preferred_element_type=F32)
    gr_ref[...] = lax.dot_general(wg_ref[...], h, _NT_DIMS, preferred_element_type=F32)


def _proj_call(x, g, w, wt, wg):
    S = x.shape[0]
    T = PROJ_ROWS
    return pl.pallas_call(
        _proj_kernel,
        grid=(S // T,),
        in_specs=[_row_spec(T, D_MODEL), _const_spec((1, D_MODEL)),
                  _const_spec(w.shape), _const_spec(wt.shape), _const_spec(wg.shape)],
        out_specs=[_row_spec(T, 1024), _row_spec(T, 512),
                   _lane_spec(D_MLSTM, T), _lane_spec(D_MLSTM, T), _lane_spec(SUBLANES, T)],
        out_shape=[jax.ShapeDtypeStruct((S, 1024), F32),
                   jax.ShapeDtypeStruct((S, 512), F32),
                   jax.ShapeDtypeStruct((D_MLSTM, S), F32),
                   jax.ShapeDtypeStruct((D_MLSTM, S), F32),
                   jax.ShapeDtypeStruct((SUBLANES, S), F32)],
        compiler_params=_params(),
        name="proj",
    )(x, g, w, wt, wg)


def _log_sigmoid(x):
    return jnp.minimum(x, 0.0) - jnp.log1p(jnp.exp(-jnp.abs(x)))


def _cummax_lanes(x, lane):
    d = 1
    while d < LANES:
        x = jnp.where(lane >= d, jnp.maximum(x, pltpu.roll(x, d, axis=1)), x)
        d *= 2
    return x


def _mlstm_kernel(qk_ref, vt_ref, ot_ref, gr_ref, cw_ref, cb_ref, br_ref, nw_ref,
                  out_ref, xpad_ref, c_ref, m_ref):
    T = MLSTM_ROWS
    L = MLSTM_CHUNK
    D = HEAD_DIM
    n_chunks = T // L

    @pl.when(pl.program_id(0) == 0)
    def _():
        xpad_ref[0:SUBLANES, :] = jnp.zeros((SUBLANES, 2 * D_MLSTM), F32)
        c_ref[...] = jnp.zeros(c_ref.shape, F32)
        m_ref[...] = jnp.zeros(m_ref.shape, F32)

    xpad_ref[SUBLANES:SUBLANES + T, :] = qk_ref[...]

    lane = lax.broadcasted_iota(jnp.int32, (SUBLANES, L), 1)
    row_id = lax.broadcasted_iota(jnp.int32, (L, L), 0)
    col_id = lax.broadcasted_iota(jnp.int32, (L, L), 1)
    causal = row_id <= col_id
    triu = causal.astype(F32)
    g = gr_ref[...] + br_ref[...]
    logf = _log_sigmoid(g)
    m = m_ref[...]
    rows = []
    for c in range(n_chunks):
        i_pre = g[:, c * L:(c + 1) * L]
        lf = pltpu.roll(logf[:, c * L:(c + 1) * L], N_HEADS, axis=0)
        b = jnp.dot(lf, triu, preferred_element_type=F32,
                    precision=lax.Precision.HIGHEST)
        r = i_pre - b
        m_run = _cummax_lanes(r, lane)
        m_last = jnp.broadcast_to(m_run[:, L - 1:L], (SUBLANES, L))
        b_last = jnp.broadcast_to(b[:, L - 1:L], (SUBLANES, L))
        mu = jnp.maximum(m_run, m)
        mx = jnp.maximum(m, m_last)
        rows.append(dict(
            r=r, mu=mu,
            inter_scale=jnp.exp(m - mu),
            eb=jnp.exp(-b - mu),
            w=jnp.exp(r - m_last),
            s_old=jnp.exp(m - mx), s_new=jnp.exp(m_last - mx)))
        m = b_last + mx
    m_ref[...] = m
    r_pad = [jnp.zeros((SUBLANES, L), F32)] * (LANES // SUBLANES - n_chunks)
    r_cols = jnp.concatenate([rw["r"] for rw in rows] + r_pad, axis=0).T

    neg_inf = jnp.float32(-jnp.inf)
    for c in range(n_chunks):
        acc = cb_ref[...]
        for j in range(CONV_WIDTH):
            lo = c * L + SUBLANES - (CONV_WIDTH - 1) + j
            acc = acc + xpad_ref[lo:lo + L, :] * cw_ref[j:j + 1, :]
        qk = acc * jax.nn.sigmoid(acc)

        for h in range(N_HEADS):
            row = lambda name: rows[c][name][h:h + 1, :]
            q = (qk[:, h * D:(h + 1) * D] * (D ** -0.5)).astype(BF16)
            k = qk[:, D_MLSTM + h * D:D_MLSTM + (h + 1) * D].astype(BF16)
            vt = vt_ref[h * D:(h + 1) * D, c * L:(c + 1) * L]
            r_col = r_cols[:, c * SUBLANES + h:c * SUBLANES + h + 1]

            dmat = jnp.exp(jnp.where(causal, r_col - row("mu"), neg_inf))
            s_kq = lax.dot_general(k, q, _NT_DIMS, preferred_element_type=F32) * dmat
            cq = lax.dot_general(c_ref[h].astype(BF16), q, _NT_DIMS,
                                 preferred_element_type=F32)
            isc = row("inter_scale")
            num = isc * cq[0:D, :] + jnp.dot(vt.astype(BF16), s_kq.astype(BF16),
                                             preferred_element_type=F32)
            den = isc * cq[D:D + 1, :] + jnp.sum(s_kq, axis=0, keepdims=True)
            hh = num * (1.0 / jnp.maximum(jnp.abs(den), row("eb")))

            w = row("w")
            vw = jnp.concatenate([vt * w, jnp.broadcast_to(w, (SUBLANES, L))],
                                 axis=0).astype(BF16)
            c_loc = jnp.dot(vw, k, preferred_element_type=F32)
            c_ref[h] = row("s_old") * c_ref[h] + row("s_new") * c_loc

            mean = jnp.mean(hh, axis=0, keepdims=True)
            cen = hh - mean
            var = jnp.mean(cen * cen, axis=0, keepdims=True)
            hn = cen * lax.rsqrt(var + EPS) * nw_ref[h * D:(h + 1) * D, :]
            gate = jax.nn.sigmoid(ot_ref[h * D:(h + 1) * D, c * L:(c + 1) * L])
            out_ref[c * L:(c + 1) * L, h * D:(h + 1) * D] = (hn * gate).T.astype(BF16)

    xpad_ref[0:SUBLANES, :] = xpad_ref[T:T + SUBLANES, :]


def _mlstm_call(qk, vt, ot, gr, cw, cb, bias_r, nw_cols):
    S = qk.shape[0]
    T = MLSTM_ROWS
    return pl.pallas_call(
        _mlstm_kernel,
        grid=(S // T,),
        in_specs=[_row_spec(T, 1024), _lane_spec(D_MLSTM, T), _lane_spec(D_MLSTM, T),
                  _lane_spec(SUBLANES, T),
                  _const_spec(cw.shape), _const_spec(cb.shape),
                  _const_spec(bias_r.shape), _const_spec(nw_cols.shape)],
        out_specs=_row_spec(T, D_MLSTM),
        out_shape=jax.ShapeDtypeStruct((S, D_MLSTM), BF16),
        scratch_shapes=[pltpu.VMEM((T + SUBLANES, 2 * D_MLSTM), F32),
                        pltpu.VMEM((N_HEADS, HEAD_DIM + SUBLANES, HEAD_DIM), F32),
                        pltpu.VMEM((SUBLANES, LANES), F32)],
        compiler_params=_params(),
        name="mlstm",
    )(qk, vt, ot, gr, cw, cb, bias_r, nw_cols)


def _gelu_tanh(x):
    c = math.sqrt(2.0 / math.pi)
    return x * (0.5 * (1.0 + jnp.tanh(c * (x + 0.044715 * (x * x * x)))))


def _s5_kernel(u_ref, a_ref, wb_ref, wc_ref, d_ref, gw_ref, gb_ref, out_ref,
               x_ref, *z_refs):
    T = S5_ROWS

    @pl.when(pl.program_id(0) == 0)
    def _():
        x_ref[...] = jnp.zeros(x_ref.shape, F32)

    R = S5_SUB
    n_sub = T // R
    half = STATE_BLOCKS // 2
    a_re = [a_ref[:, c * LANES:(c + 1) * LANES] for c in range(half)]
    a_im = [a_ref[:, (half + c) * LANES:(half + c + 1) * LANES] for c in range(half)]

    def expand_piece(sb, j):
        blk = u_ref[sb * R:(sb + 1) * R, (j // 2) * LANES:(j // 2 + 1) * LANES].astype(BF16)
        bu = jnp.dot(blk, wb_ref[j], preferred_element_type=F32)
        for c in range(STATE_BLOCKS):
            z_refs[sb][c // 2, pl.ds(j * S5_PITCH + c % 2, R, stride=2), :] = (
                bu[:, c * LANES:(c + 1) * LANES])

    def scan_piece(sb, x, t0, t1):
        z_ref = z_refs[sb]
        for t in range(t0, t1):
            tile = lambda c: (c // 2, pl.ds(2 * t + c % 2, SUBLANES, stride=S5_PITCH),
                              slice(None))
            new = list(x)
            for c in range(half):
                x_re, x_im = x[c], x[half + c]
                new[c] = a_re[c] * x_re - a_im[c] * x_im + z_ref[tile(c)]
                new[half + c] = a_re[c] * x_im + a_im[c] * x_re + z_ref[tile(half + c)]
                z_ref[tile(c)] = new[c]
                z_ref[tile(half + c)] = new[half + c]
            x = new
        return x

    def contract_piece(sb, j):
        xs = jnp.concatenate(
            [z_refs[sb][c // 2, pl.ds(j * S5_PITCH + c % 2, R, stride=2), :]
             for c in range(STATE_BLOCKS)], axis=1).astype(BF16)
        return jnp.dot(xs, wc_ref[j], preferred_element_type=F32)

    def finish(sb, parts):
        blocks = [parts[2 * b] + parts[2 * b + 1] for b in range(SUBLANES // 2)]
        y = jnp.concatenate(blocks, axis=1) + d_ref[...] * u_ref[sb * R:(sb + 1) * R, :]
        z = _gelu_tanh(y)
        gate = jax.nn.sigmoid(jnp.dot(z.astype(BF16), gw_ref[...],
                                      preferred_element_type=F32) + gb_ref[...])
        out_ref[sb * R:(sb + 1) * R, :] = (z * gate).astype(BF16)

    x = [x_ref[:, c * LANES:(c + 1) * LANES] for c in range(STATE_BLOCKS)]
    steps = R // SUBLANES
    for j in range(SUBLANES):
        expand_piece(0, j)
    for sb in range(n_sub + 1):
        parts = []
        for j in range(SUBLANES):
            if sb < n_sub:
                x = scan_piece(sb, x, j * steps, (j + 1) * steps)
            if sb + 1 < n_sub:
                expand_piece(sb + 1, j)
            if sb >= 1:
                parts.append(contract_piece(sb - 1, j))
        if sb >= 1:
            finish(sb - 1, parts)
    for c in range(STATE_BLOCKS):
        x_ref[:, c * LANES:(c + 1) * LANES] = x[c]


def _s5_call(u, a_tile, wb, wc, d, gw, gb):
    S = u.shape[0]
    T = S5_ROWS
    return pl.pallas_call(
        _s5_kernel,
        grid=(S // T,),
        in_specs=[_row_spec(T, D_SSM), _const_spec(a_tile.shape), _const_spec(wb.shape),
                  _const_spec(wc.shape), _const_spec(d.shape), _const_spec(gw.shape),
                  _const_spec(gb.shape)],
        out_specs=_row_spec(T, D_SSM),
        out_shape=jax.ShapeDtypeStruct((S, D_SSM), BF16),
        scratch_shapes=([pltpu.VMEM((SUBLANES, STATE_LANES), F32)]
                        + [pltpu.VMEM((STATE_BLOCKS // 2, SUBLANES * S5_PITCH, LANES), F32)]
                        * (T // S5_SUB)),
        compiler_params=_params(),
        name="s5",
    )(u, a_tile, wb, wc, d, gw, gb)


def _s5_weights(lam_re, lam_im, log_dt, b_re, b_im, c_re, c_im):
    lr, li = lam_re.astype(F32), lam_im.astype(F32)
    dt = jnp.exp(log_dt.astype(F32))[:, None]
    mag = jnp.exp(lr * dt)
    abar_re, abar_im = mag * jnp.cos(li * dt), mag * jnp.sin(li * dt)
    den = lr * lr + li * li
    coef_re = ((abar_re - 1.0) * lr + abar_im * li) / den
    coef_im = (abar_im * lr - (abar_re - 1.0) * li) / den
    br, bi = b_re.astype(F32), b_im.astype(F32)
    bbar_re = coef_re[..., None] * br - coef_im[..., None] * bi
    bbar_im = coef_re[..., None] * bi + coef_im[..., None] * br
    J, Q, P, Hc = SUBLANES, GROUPS_PER_SUBLANE, SSM_STATE, SSM_GROUP
    a_tile = jnp.concatenate([abar_re.reshape(J, Q * P),
                              abar_im.reshape(J, Q * P)], axis=1)
    eye = jnp.eye(Q, dtype=F32)
    half = (jnp.arange(J) % 2)[:, None, None, None] == jnp.arange(2)[None, :, None, None]

    def expand_in(b):
        return jnp.einsum('jqpc,qr->jqcrp', b.reshape(J, Q, P, Hc), eye).reshape(J, Q * Hc, Q * P)

    wb = jnp.concatenate([expand_in(bbar_re), expand_in(bbar_im)], axis=2)
    wb = jnp.where(half, wb[:, None], 0.0).reshape(J, 2 * Q * Hc, STATE_LANES)

    def expand_out(c):
        return jnp.einsum('jqcp,qr->jqprc', c.reshape(J, Q, Hc, P), eye).reshape(J, Q * P, Q * Hc)

    wc = jnp.concatenate([expand_out(c_re.astype(F32)), -expand_out(c_im.astype(F32))], axis=1)
    half_c = jnp.swapaxes(half, 1, 2)
    wc = jnp.where(half_c, wc[:, :, None, :], 0.0).reshape(J, STATE_LANES, 2 * Q * Hc)
    return a_tile, wb.astype(BF16), wc.astype(BF16)


def _ffn_kernel(x_ref, hm_ref, y_ref, wo_ref, g2_ref, w1_ref, w2_ref, g3_ref, out_ref):
    mixed = (jnp.dot(hm_ref[...], wo_ref[0:D_MLSTM, :], preferred_element_type=F32)
             + jnp.dot(y_ref[...], wo_ref[D_MLSTM:D_MODEL, :], preferred_element_type=F32))
    x1 = x_ref[...] + mixed
    h2 = _rms(x1, g2_ref[...]).astype(BF16)
    mlp = None
    for c in range(D_FF // FF_CHUNK):
        a = jnp.dot(h2, w1_ref[:, c * FF_CHUNK:(c + 1) * FF_CHUNK],
                    preferred_element_type=F32)
        a = jnp.maximum(a, 0.0)
        part = jnp.dot((a * a).astype(BF16), w2_ref[c * FF_CHUNK:(c + 1) * FF_CHUNK, :],
                       preferred_element_type=F32)
        mlp = part if mlp is None else mlp + part
    out_ref[...] = _rms(x1 + mlp, g3_ref[...])


def _ffn_call(x, hm, y, wo, g2, w1, w2, g3):
    S = x.shape[0]
    T = FFN_ROWS
    single = dict(pipeline_mode=pl.Buffered(WEIGHT_BUFFERS))
    return pl.pallas_call(
        _ffn_kernel,
        grid=(S // T,),
        in_specs=[_row_spec(T, D_MODEL), _row_spec(T, D_MLSTM), _row_spec(T, D_SSM),
                  pl.BlockSpec(wo.shape, lambda i: (0, 0), **single),
                  _const_spec(g2.shape),
                  pl.BlockSpec(w1.shape, lambda i: (0, 0), **single),
                  pl.BlockSpec(w2.shape, lambda i: (0, 0), **single),
                  _const_spec(g3.shape)],
        out_specs=_row_spec(T, D_MODEL),
        out_shape=jax.ShapeDtypeStruct((S, D_MODEL), F32),
        compiler_params=_params(),
        name="ffn",
    )(x, hm, y, wo, g2, w1, w2, g3)


def _layer(x, mix_norm_w, w_in, conv_w, conv_b, i_bias, f_bias, mlstm_norm_w,
           lam_re, lam_im, log_dt, b_re, b_im, c_re, c_im, ssm_d,
           glu_w, glu_b, w_out, mlp_norm_w, w_ff1, w_ff2, out_norm_w):
    n_qk, n_vo, n_gate = 2 * D_MLSTM, 2 * D_MLSTM, 2 * N_HEADS
    w_rows = jnp.concatenate([w_in[:, :n_qk], w_in[:, n_qk + n_vo + n_gate:]],
                             axis=1).astype(BF16)
    w_cols = w_in[:, n_qk:n_qk + n_vo].T.astype(BF16)
    wg = w_in[:, n_qk + n_vo:n_qk + n_vo + n_gate].T.astype(BF16)
    qk, u, vt, ot, gr = _proj_call(x, mix_norm_w[None, :], w_rows, w_cols, wg)

    bias_r = jnp.concatenate([i_bias, f_bias]).astype(F32)[:, None]
    nw_cols = jnp.broadcast_to(mlstm_norm_w.astype(F32)[:, None], (D_MLSTM, LANES))
    hm = _mlstm_call(qk, vt, ot, gr, conv_w, conv_b[None, :], bias_r, nw_cols)

    a_tile, wb, wc = _s5_weights(lam_re, lam_im, log_dt, b_re, b_im, c_re, c_im)
    y = _s5_call(u, a_tile, wb, wc, ssm_d[None, :], glu_w.astype(BF16), glu_b[None, :])

    return _ffn_call(x, hm, y, w_out.astype(BF16), mlp_norm_w[None, :],
                     w_ff1.astype(BF16), w_ff2.astype(BF16), out_norm_w[None, :])


def kernel(x, mix_norm_w, w_in, conv_w, conv_b, i_bias, f_bias, mlstm_norm_w, ssm_lam_re, ssm_lam_im, ssm_log_dt, ssm_b_re, ssm_b_im, ssm_c_re, ssm_c_im, ssm_d, glu_w, glu_b, w_out, mlp_norm_w, w_ff1, w_ff2, final_norm_w):
    assert x.shape[0] == 1 and mix_norm_w.shape[0] == 1
    xs = x[0]
    out = _layer(xs, mix_norm_w[0], w_in[0], conv_w[0], conv_b[0], i_bias[0], f_bias[0],
                 mlstm_norm_w[0], ssm_lam_re[0], ssm_lam_im[0], ssm_log_dt[0],
                 ssm_b_re[0], ssm_b_im[0], ssm_c_re[0], ssm_c_im[0], ssm_d[0],
                 glu_w[0], glu_b[0], w_out[0], mlp_norm_w[0], w_ff1[0], w_ff2[0],
                 final_norm_w)
    return out[None]
```

```python
import math

import jax
import jax.numpy as jnp
from jax import lax
from jax.experimental import pallas as pl
from jax.experimental.pallas import tpu as pltpu

F32 = jnp.float32
BF16 = jnp.bfloat16

D_MODEL = 1024
D_MLSTM = 512
N_HEADS = 4
HEAD_DIM = 128
CONV_WIDTH = 4
D_SSM = 512
SSM_GROUP = 16
N_GROUPS = 32
SSM_STATE = 64
D_FF = 4096
EPS = 1e-6

SUBLANES = 8
LANES = 128

PROJ_ROWS = 1024
MLSTM_ROWS = 512
MLSTM_CHUNK = 128
S5_ROWS = 512
S5_SUB = 256
S5_PITCH = 2 * S5_SUB + 4
FFN_ROWS = 1024
FF_CHUNK = 1024
WEIGHT_BUFFERS = 1
VMEM_LIMIT = 56 * 1024 * 1024

GROUPS_PER_SUBLANE = N_GROUPS // SUBLANES
STATE_HALF = GROUPS_PER_SUBLANE * SSM_STATE
STATE_LANES = 2 * STATE_HALF
STATE_BLOCKS = STATE_LANES // LANES


def _const_spec(shape):
    return pl.BlockSpec(shape, lambda i: (0,) * len(shape))


def _row_spec(rows, cols):
    return pl.BlockSpec((rows, cols), lambda i: (i, 0))


def _lane_spec(rows, cols):
    return pl.BlockSpec((rows, cols), lambda i: (0, i))


def _params(flags=None):
    return pltpu.CompilerParams(dimension_semantics=("arbitrary",),
                                vmem_limit_bytes=VMEM_LIMIT, flags=flags)


def _rms(x, g):
    r = lax.rsqrt(jnp.mean(x * x, axis=-1, keepdims=True) + EPS)
    return x * r * g


_NT_DIMS = (((1,), (1,)), ((), ()))


def _proj_kernel(x_ref, g_ref, w_ref, wt_ref, wg_ref, qk_ref, u_ref, vt_ref, ot_ref, gr_ref):
    h = _rms(x_ref[...], g_ref[...]).astype(BF16)
    qk_ref[...] = jnp.dot(h, w_ref[:, 0:1024], preferred_element_type=F32)
    u_ref[...] = jnp.dot(h, w_ref[:, 1024:1536], preferred_element_type=F32)
    vt_ref[...] = lax.dot_general(wt_ref[0:D_MLSTM, :], h, _NT_DIMS,
                                  preferred_element_type=F32)
    ot_ref[...] = lax.dot_general(wt_ref[D_MLSTM:2 * D_MLSTM, :], h, _NT_DIMS,
                                  preferred_element_type=F32)
    gr_ref[...] = lax.dot_general(wg_ref[...], h, _NT_DIMS, preferred_element_type=F32)


def _proj_call(x, g, w, wt, wg):
    S = x.shape[0]
    T = PROJ_ROWS
    return pl.pallas_call(
        _proj_kernel,
        grid=(S // T,),
        in_specs=[_row_spec(T, D_MODEL), _const_spec((1, D_MODEL)),
                  _const_spec(w.shape), _const_spec(wt.shape), _const_spec(wg.shape)],
        out_specs=[_row_spec(T, 1024), _row_spec(T, 512),
                   _lane_spec(D_MLSTM, T), _lane_spec(D_MLSTM, T), _lane_spec(SUBLANES, T)],
        out_shape=[jax.ShapeDtypeStruct((S, 1024), F32),
                   jax.ShapeDtypeStruct((S, 512), F32),
                   jax.ShapeDtypeStruct((D_MLSTM, S), F32),
                   jax.ShapeDtypeStruct((D_MLSTM, S), F32),
                   jax.ShapeDtypeStruct((SUBLANES, S), F32)],
        compiler_params=_params(),
        name="proj",
    )(x, g, w, wt, wg)


def _log_sigmoid(x):
    return jnp.minimum(x, 0.0) - jnp.log1p(jnp.exp(-jnp.abs(x)))


def _interleave(*stages):
    total = max(n for _, n in stages)
    done = [0] * len(stages)
    for tick in range(1, total + 1):
        for s, (gen, n) in enumerate(stages):
            want = -(-tick * n // total)
            while done[s] < want:
                next(gen, None)
                done[s] += 1
    for gen, _ in stages:
        for _ in gen:
            pass


_CUMMAX_LEVELS = LANES.bit_length() - 1
_GATE_PIECES = 3 + _CUMMAX_LEVELS
_QK_BLOCKS = 2 * D_MLSTM // LANES


def _conv_pieces(qk_ref, cw_ref, cb_ref, xpad_ref, qks_ref):
    T = MLSTM_ROWS
    L = MLSTM_CHUNK
    for k in range(_QK_BLOCKS):
        xpad_ref[k, SUBLANES:SUBLANES + T, :] = qk_ref[:, k * LANES:(k + 1) * LANES]
    yield
    for c in range(T // L):
        for k in range(_QK_BLOCKS):
            cols = slice(k * LANES, (k + 1) * LANES)
            for parity in range(2):
                acc = cb_ref[:, cols]
                for j in range(CONV_WIDTH):
                    start = c * L + SUBLANES - (CONV_WIDTH - 1) + parity + j
                    acc = acc + (xpad_ref[k, pl.ds(start, L // 2, stride=2), :]
                                 * cw_ref[j:j + 1, cols])
                act = acc * jax.nn.sigmoid(acc)
                if k < _QK_BLOCKS // 2:
                    act = act * (HEAD_DIM ** -0.5)
                qks_ref[k, pl.ds(c * L + parity, L // 2, stride=2), :] = act
            yield
    for k in range(_QK_BLOCKS):
        xpad_ref[k, 0:SUBLANES, :] = xpad_ref[k, T:T + SUBLANES, :]


def _gate_pieces(gr_ref, br_ref, m_ref, out):
    T = MLSTM_ROWS
    L = MLSTM_CHUNK
    n_chunks = T // L
    lane = lax.broadcasted_iota(jnp.int32, (SUBLANES, L), 1)
    row_id = lax.broadcasted_iota(jnp.int32, (L, L), 0)
    col_id = lax.broadcasted_iota(jnp.int32, (L, L), 1)
    triu = (row_id <= col_id).astype(BF16)
    g = gr_ref[...] + br_ref[...]
    logf = _log_sigmoid(g)
    yield
    lf = jnp.concatenate([pltpu.roll(logf[:, c * L:(c + 1) * L], N_HEADS, axis=0)
                          for c in range(n_chunks)], axis=0)
    hi = lf.astype(BF16)
    rest = lf - hi.astype(F32)
    mid = rest.astype(BF16)
    lo = (rest - mid.astype(F32)).astype(BF16)
    terms = jnp.dot(jnp.concatenate([hi, mid, lo], axis=0), triu,
                    preferred_element_type=F32)
    n_rows = n_chunks * SUBLANES
    b_all = terms[0:n_rows] + terms[n_rows:2 * n_rows] + terms[2 * n_rows:3 * n_rows]
    bs = [b_all[c * SUBLANES:(c + 1) * SUBLANES] for c in range(n_chunks)]
    rs = [g[:, c * L:(c + 1) * L] - bs[c] for c in range(n_chunks)]
    yield
    m_runs = list(rs)
    for level in range(_CUMMAX_LEVELS):
        d = 1 << level
        m_runs = [jnp.where(lane >= d, jnp.maximum(x, pltpu.roll(x, d, axis=1)), x)
                  for x in m_runs]
        yield
    m = m_ref[...]
    rows = []
    for c in range(n_chunks):
        b, r, m_run = bs[c], rs[c], m_runs[c]
        m_last = jnp.broadcast_to(m_run[:, L - 1:L], (SUBLANES, L))
        b_last = jnp.broadcast_to(b[:, L - 1:L], (SUBLANES, L))
        mu = jnp.maximum(m_run, m)
        mx = jnp.maximum(m, m_last)
        rows.append(dict(
            mu=mu,
            inter_scale=jnp.exp(m - mu),
            eb=jnp.exp(-b - mu),
            w=jnp.exp(r - m_last),
            s_old=jnp.exp(m - mx), s_new=jnp.exp(m_last - mx)))
        m = b_last + mx
    m_ref[...] = m
    r_pad = [jnp.zeros((SUBLANES, L), F32)] * (LANES // SUBLANES - n_chunks)
    out["rows"] = rows
    out["r_cols"] = jnp.concatenate(rs + r_pad, axis=0).T


def _mlstm_kernel(qk_ref, vt_ref, ot_ref, gr_ref, cw_ref, cb_ref, br_ref, nw_ref,
                  out_ref, xpad_ref, qks_ref, c_ref, m_ref):
    T = MLSTM_ROWS
    L = MLSTM_CHUNK
    D = HEAD_DIM
    n_chunks = T // L

    @pl.when(pl.program_id(0) == 0)
    def _():
        xpad_ref[:, 0:SUBLANES, :] = jnp.zeros((_QK_BLOCKS, SUBLANES, LANES), F32)
        c_ref[...] = jnp.zeros(c_ref.shape, F32)
        m_ref[...] = jnp.zeros(m_ref.shape, F32)

    gates = {}
    _interleave((_gate_pieces(gr_ref, br_ref, m_ref, gates), _GATE_PIECES),
                (_conv_pieces(qk_ref, cw_ref, cb_ref, xpad_ref, qks_ref),
                 1 + n_chunks * _QK_BLOCKS))
    rows, r_cols = gates["rows"], gates["r_cols"]

    row_id = lax.broadcasted_iota(jnp.int32, (L, L), 0)
    col_id = lax.broadcasted_iota(jnp.int32, (L, L), 1)
    causal = row_id <= col_id
    neg_inf = jnp.float32(-jnp.inf)
    for c in range(n_chunks):
        for h in range(N_HEADS):
            row = lambda name: rows[c][name][h:h + 1, :]
            q = qks_ref[h, c * L:(c + 1) * L, :].astype(BF16)
            k = qks_ref[N_HEADS + h, c * L:(c + 1) * L, :].astype(BF16)
            vt = vt_ref[h * D:(h + 1) * D, c * L:(c + 1) * L]
            r_col = r_cols[:, c * SUBLANES + h:c * SUBLANES + h + 1]

            dmat = jnp.exp(jnp.where(causal, r_col - row("mu"), neg_inf))
            s_kq = lax.dot_general(k, q, _NT_DIMS, preferred_element_type=F32) * dmat
            cq = lax.dot_general(c_ref[h].astype(BF16), q, _NT_DIMS,
                                 preferred_element_type=F32)
            isc = row("inter_scale")
            num = isc * cq[0:D, :] + jnp.dot(vt.astype(BF16), s_kq.astype(BF16),
                                             preferred_element_type=F32)
            den = isc * cq[D:D + 1, :] + jnp.sum(s_kq, axis=0, keepdims=True)
            hh = num * (1.0 / jnp.maximum(jnp.abs(den), row("eb")))

            w = row("w")
            vw = jnp.concatenate([vt * w, jnp.broadcast_to(w, (SUBLANES, L))],
                                 axis=0).astype(BF16)
            c_loc = jnp.dot(vw, k, preferred_element_type=F32)
            c_ref[h] = row("s_old") * c_ref[h] + row("s_new") * c_loc

            mean = jnp.mean(hh, axis=0, keepdims=True)
            cen = hh - mean
            var = jnp.mean(cen * cen, axis=0, keepdims=True)
            hn = cen * lax.rsqrt(var + EPS) * nw_ref[h * D:(h + 1) * D, :]
            gate = jax.nn.sigmoid(ot_ref[h * D:(h + 1) * D, c * L:(c + 1) * L])
            out_ref[c * L:(c + 1) * L, h * D:(h + 1) * D] = (hn * gate).T.astype(BF16)


def _mlstm_call(qk, vt, ot, gr, cw, cb, bias_r, nw_cols):
    S = qk.shape[0]
    T = MLSTM_ROWS
    return pl.pallas_call(
        _mlstm_kernel,
        grid=(S // T,),
        in_specs=[_row_spec(T, 1024), _lane_spec(D_MLSTM, T), _lane_spec(D_MLSTM, T),
                  _lane_spec(SUBLANES, T),
                  _const_spec(cw.shape), _const_spec(cb.shape),
                  _const_spec(bias_r.shape), _const_spec(nw_cols.shape)],
        out_specs=_row_spec(T, D_MLSTM),
        out_shape=jax.ShapeDtypeStruct((S, D_MLSTM), BF16),
        scratch_shapes=[pltpu.VMEM((_QK_BLOCKS, T + SUBLANES, LANES), F32),
                        pltpu.VMEM((_QK_BLOCKS, T, LANES), F32),
                        pltpu.VMEM((N_HEADS, HEAD_DIM + SUBLANES, HEAD_DIM), F32),
                        pltpu.VMEM((SUBLANES, LANES), F32)],
        compiler_params=_params(),
        name="mlstm",
    )(qk, vt, ot, gr, cw, cb, bias_r, nw_cols)


def _gelu_tanh(x):
    c = math.sqrt(2.0 / math.pi)
    return x * (0.5 * (1.0 + jnp.tanh(c * (x + 0.044715 * (x * x * x)))))


def _s5_kernel(u_ref, a_ref, wb_ref, wc_ref, d_ref, gw_ref, gb_ref, out_ref,
               x_ref, *z_refs):
    T = S5_ROWS

    @pl.when(pl.program_id(0) == 0)
    def _():
        x_ref[...] = jnp.zeros(x_ref.shape, F32)

    R = S5_SUB
    n_sub = T // R
    half = STATE_BLOCKS // 2
    a_re = [a_ref[:, c * LANES:(c + 1) * LANES] for c in range(half)]
    a_im = [a_ref[:, (half + c) * LANES:(half + c + 1) * LANES] for c in range(half)]

    def expand_piece(sb, j):
        blk = u_ref[sb * R:(sb + 1) * R, (j // 2) * LANES:(j // 2 + 1) * LANES].astype(BF16)
        bu = jnp.dot(blk, wb_ref[j], preferred_element_type=F32)
        for c in range(STATE_BLOCKS):
            z_refs[sb][c // 2, pl.ds(j * S5_PITCH + c % 2, R, stride=2), :] = (
                bu[:, c * LANES:(c + 1) * LANES])

    def scan_piece(sb, x, t0, t1):
        z_ref = z_refs[sb]
        for t in range(t0, t1):
            tile = lambda c: (c // 2, pl.ds(2 * t + c % 2, SUBLANES, stride=S5_PITCH),
                              slice(None))
            new = list(x)
            for c in range(half):
                x_re, x_im = x[c], x[half + c]
                new[c] = a_re[c] * x_re - a_im[c] * x_im + z_ref[tile(c)]
                new[half + c] = a_re[c] * x_im + a_im[c] * x_re + z_ref[tile(half + c)]
                z_ref[tile(c)] = new[c]
                z_ref[tile(half + c)] = new[half + c]
            x = new
        return x

    def contract_piece(sb, j):
        xs = jnp.concatenate(
            [z_refs[sb][c // 2, pl.ds(j * S5_PITCH + c % 2, R, stride=2), :]
             for c in range(STATE_BLOCKS)], axis=1).astype(BF16)
        return jnp.dot(xs, wc_ref[j], preferred_element_type=F32)

    def finish(sb, parts):
        blocks = [parts[2 * b] + parts[2 * b + 1] for b in range(SUBLANES // 2)]
        y = jnp.concatenate(blocks, axis=1) + d_ref[...] * u_ref[sb * R:(sb + 1) * R, :]
        z = _gelu_tanh(y)
        gate = jax.nn.sigmoid(jnp.dot(z.astype(BF16), gw_ref[...],
                                      preferred_element_type=F32) + gb_ref[...])
        out_ref[sb * R:(sb + 1) * R, :] = (z * gate).astype(BF16)

    x = [x_ref[:, c * LANES:(c + 1) * LANES] for c in range(STATE_BLOCKS)]
    steps = R // SUBLANES
    for j in range(SUBLANES):
        expand_piece(0, j)
    for sb in range(n_sub + 1):
        parts = []
        for j in range(SUBLANES):
            if sb < n_sub:
                x = scan_piece(sb, x, j * steps, (j + 1) * steps)
            if sb + 1 < n_sub:
                expand_piece(sb + 1, j)
            if sb >= 1:
                parts.append(contract_piece(sb - 1, j))
        if sb >= 1:
            finish(sb - 1, parts)
    for c in range(STATE_BLOCKS):
        x_ref[:, c * LANES:(c + 1) * LANES] = x[c]


def _s5_call(u, a_tile, wb, wc, d, gw, gb):
    S = u.shape[0]
    T = S5_ROWS
    return pl.pallas_call(
        _s5_kernel,
        grid=(S // T,),
        in_specs=[_row_spec(T, D_SSM), _const_spec(a_tile.shape), _const_spec(wb.shape),
                  _const_spec(wc.shape), _const_spec(d.shape), _const_spec(gw.shape),
                  _const_spec(gb.shape)],
        out_specs=_row_spec(T, D_SSM),
        out_shape=jax.ShapeDtypeStruct((S, D_SSM), BF16),
        scratch_shapes=([pltpu.VMEM((SUBLANES, STATE_LANES), F32)]
                        + [pltpu.VMEM((STATE_BLOCKS // 2, SUBLANES * S5_PITCH, LANES), F32)]
                        * (T // S5_SUB)),
        compiler_params=_params(),
        name="s5",
    )(u, a_tile, wb, wc, d, gw, gb)


def _s5_weights(lam_re, lam_im, log_dt, b_re, b_im, c_re, c_im):
    lr, li = lam_re.astype(F32), lam_im.astype(F32)
    dt = jnp.exp(log_dt.astype(F32))[:, None]
    mag = jnp.exp(lr * dt)
    abar_re, abar_im = mag * jnp.cos(li * dt), mag * jnp.sin(li * dt)
    den = lr * lr + li * li
    coef_re = ((abar_re - 1.0) * lr + abar_im * li) / den
    coef_im = (abar_im * lr - (abar_re - 1.0) * li) / den
    br, bi = b_re.astype(F32), b_im.astype(F32)
    bbar_re = coef_re[..., None] * br - coef_im[..., None] * bi
    bbar_im = coef_re[..., None] * bi + coef_im[..., None] * br
    J, Q, P, Hc = SUBLANES, GROUPS_PER_SUBLANE, SSM_STATE, SSM_GROUP
    a_tile = jnp.concatenate([abar_re.reshape(J, Q * P),
                              abar_im.reshape(J, Q * P)], axis=1)
    eye = jnp.eye(Q, dtype=F32)
    half = (jnp.arange(J) % 2)[:, None, None, None] == jnp.arange(2)[None, :, None, None]

    def expand_in(b):
        return jnp.einsum('jqpc,qr->jqcrp', b.reshape(J, Q, P, Hc), eye).reshape(J, Q * Hc, Q * P)

    wb = jnp.concatenate([expand_in(bbar_re), expand_in(bbar_im)], axis=2)
    wb = jnp.where(half, wb[:, None], 0.0).reshape(J, 2 * Q * Hc, STATE_LANES)

    def expand_out(c):
        return jnp.einsum('jqcp,qr->jqprc', c.reshape(J, Q, Hc, P), eye).reshape(J, Q * P, Q * Hc)

    wc = jnp.concatenate([expand_out(c_re.astype(F32)), -expand_out(c_im.astype(F32))], axis=1)
    half_c = jnp.swapaxes(half, 1, 2)
    wc = jnp.where(half_c, wc[:, :, None, :], 0.0).reshape(J, STATE_LANES, 2 * Q * Hc)
    return a_tile, wb.astype(BF16), wc.astype(BF16)


def _ffn_kernel(x_ref, hm_ref, y_ref, wo_ref, g2_ref, w1_ref, w2_ref, g3_ref, out_ref):
    mixed = (jnp.dot(hm_ref[...], wo_ref[0:D_MLSTM, :], preferred_element_type=F32)
             + jnp.dot(y_ref[...], wo_ref[D_MLSTM:D_MODEL, :], preferred_element_type=F32))
    x1 = x_ref[...] + mixed
    h2 = _rms(x1, g2_ref[...]).astype(BF16)
    mlp = None
    for c in range(D_FF // FF_CHUNK):
        a = jnp.dot(h2, w1_ref[:, c * FF_CHUNK:(c + 1) * FF_CHUNK],
                    preferred_element_type=F32)
        a = jnp.maximum(a, 0.0)
        part = jnp.dot((a * a).astype(BF16), w2_ref[c * FF_CHUNK:(c + 1) * FF_CHUNK, :],
                       preferred_element_type=F32)
        mlp = part if mlp is None else mlp + part
    out_ref[...] = _rms(x1 + mlp, g3_ref[...])


def _ffn_call(x, hm, y, wo, g2, w1, w2, g3):
    S = x.shape[0]
    T = FFN_ROWS
    single = dict(pipeline_mode=pl.Buffered(WEIGHT_BUFFERS))
    return pl.pallas_call(
        _ffn_kernel,
        grid=(S // T,),
        in_specs=[_row_spec(T, D_MODEL), _row_spec(T, D_MLSTM), _row_spec(T, D_SSM),
                  pl.BlockSpec(wo.shape, lambda i: (0, 0), **single),
                  _const_spec(g2.shape),
                  pl.BlockSpec(w1.shape, lambda i: (0, 0), **single),
                  pl.BlockSpec(w2.shape, lambda i: (0, 0), **single),
                  _const_spec(g3.shape)],
        out_specs=_row_spec(T, D_MODEL),
        out_shape=jax.ShapeDtypeStruct((S, D_MODEL), F32),
        compiler_params=_params(),
        name="ffn",
    )(x, hm, y, wo, g2, w1, w2, g3)


def _layer(x, mix_norm_w, w_in, conv_w, conv_b, i_bias, f_bias, mlstm_norm_w,
           lam_re, lam_im, log_dt, b_re, b_im, c_re, c_im, ssm_d,
           glu_w, glu_b, w_out, mlp_norm_w, w_ff1, w_ff2, out_norm_w):
    n_qk, n_vo, n_gate = 2 * D_MLSTM, 2 * D_MLSTM, 2 * N_HEADS
    w_rows = jnp.concatenate([w_in[:, :n_qk], w_in[:, n_qk + n_vo + n_gate:]],
                             axis=1).astype(BF16)
    w_cols = w_in[:, n_qk:n_qk + n_vo].T.astype(BF16)
    wg = w_in[:, n_qk + n_vo:n_qk + n_vo + n_gate].T.astype(BF16)
    qk, u, vt, ot, gr = _proj_call(x, mix_norm_w[None, :], w_rows, w_cols, wg)

    bias_r = jnp.concatenate([i_bias, f_bias]).astype(F32)[:, None]
    nw_cols = jnp.broadcast_to(mlstm_norm_w.astype(F32)[:, None], (D_MLSTM, LANES))
    hm = _mlstm_call(qk, vt, ot, gr, conv_w, conv_b[None, :], bias_r, nw_cols)

    a_tile, wb, wc = _s5_weights(lam_re, lam_im, log_dt, b_re, b_im, c_re, c_im)
    y = _s5_call(u, a_tile, wb, wc, ssm_d[None, :], glu_w.astype(BF16), glu_b[None, :])

    return _ffn_call(x, hm, y, w_out.astype(BF16), mlp_norm_w[None, :],
                     w_ff1.astype(BF16), w_ff2.astype(BF16), out_norm_w[None, :])


def kernel(x, mix_norm_w, w_in, conv_w, conv_b, i_bias, f_bias, mlstm_norm_w, ssm_lam_re, ssm_lam_im, ssm_log_dt, ssm_b_re, ssm_b_im, ssm_c_re, ssm_c_im, ssm_d, glu_w, glu_b, w_out, mlp_norm_w, w_ff1, w_ff2, final_norm_w):
    assert x.shape[0] == 1 and mix_norm_w.shape[0] == 1
    xs = x[0]
    out = _layer(xs, mix_norm_w[0], w_in[0], conv_w[0], conv_b[0], i_bias[0], f_bias[0],
                 mlstm_norm_w[0], ssm_lam_re[0], ssm_lam_im[0], ssm_log_dt[0],
                 ssm_b_re[0], ssm_b_im[0], ssm_c_re[0], ssm_c_im[0], ssm_d[0],
                 glu_w[0], glu_b[0], w_out[0], mlp_norm_w[0], w_ff1[0], w_ff2[0],
                 final_norm_w)
    return out[None]
```

```python
import math

import jax
import jax.numpy as jnp
from jax import lax
from jax.experimental import pallas as pl
from jax.experimental.pallas import tpu as pltpu

F32 = jnp.float32
BF16 = jnp.bfloat16

D_MODEL = 1024
D_MLSTM = 512
N_HEADS = 4
HEAD_DIM = 128
CONV_WIDTH = 4
D_SSM = 512
SSM_GROUP = 16
N_GROUPS = 32
SSM_STATE = 64
D_FF = 4096
EPS = 1e-6

SUBLANES = 8
LANES = 128

PROJ_ROWS = 1024
MLSTM_ROWS = 1024
MLSTM_CHUNK = 128
S5_ROWS = 1024
S5_SUB = 256
S5_PITCH = 2 * S5_SUB + 4
FFN_ROWS = 1024
FF_CHUNK = 1024
WEIGHT_BUFFERS = 1
VMEM_LIMIT = 56 * 1024 * 1024

GROUPS_PER_SUBLANE = N_GROUPS // SUBLANES
STATE_HALF = GROUPS_PER_SUBLANE * SSM_STATE
STATE_LANES = 2 * STATE_HALF
STATE_BLOCKS = STATE_LANES // LANES


def _const_spec(shape):
    return pl.BlockSpec(shape, lambda i: (0,) * len(shape))


def _row_spec(rows, cols):
    return pl.BlockSpec((rows, cols), lambda i: (i, 0))


def _lane_spec(rows, cols):
    return pl.BlockSpec((rows, cols), lambda i: (0, i))


def _params(flags=None):
    return pltpu.CompilerParams(dimension_semantics=("arbitrary",),
                                vmem_limit_bytes=VMEM_LIMIT, flags=flags)


def _rms(x, g):
    r = lax.rsqrt(jnp.mean(x * x, axis=-1, keepdims=True) + EPS)
    return x * r * g


_NT_DIMS = (((1,), (1,)), ((), ()))


def _proj_kernel(x_ref, g_ref, w_ref, wt_ref, wg_ref, qk_ref, u_ref, vt_ref, ot_ref, gr_ref):
    h = _rms(x_ref[...], g_ref[...]).astype(BF16)
    qk_ref[...] = jnp.dot(h, w_ref[:, 0:1024], preferred_element_type=F32)
    u_ref[...] = jnp.dot(h, w_ref[:, 1024:1536], preferred_element_type=F32)
    vt_ref[...] = lax.dot_general(wt_ref[0:D_MLSTM, :], h, _NT_DIMS,
                                  preferred_element_type=F32)
    ot_ref[...] = lax.dot_general(wt_ref[D_MLSTM:2 * D_MLSTM, :], h, _NT_DIMS,
                                  preferred_element_type=F32)
    gr_ref[...] = lax.dot_general(wg_ref[...], h, _NT_DIMS, preferred_element_type=F32)


def _proj_call(x, g, w, wt, wg):
    S = x.shape[0]
    T = PROJ_ROWS
    return pl.pallas_call(
        _proj_kernel,
        grid=(S // T,),
        in_specs=[_row_spec(T, D_MODEL), _const_spec((1, D_MODEL)),
                  _const_spec(w.shape), _const_spec(wt.shape), _const_spec(wg.shape)],
        out_specs=[_row_spec(T, 1024), _row_spec(T, 512),
                   _lane_spec(D_MLSTM, T), _lane_spec(D_MLSTM, T), _lane_spec(SUBLANES, T)],
        out_shape=[jax.ShapeDtypeStruct((S, 1024), F32),
                   jax.ShapeDtypeStruct((S, 512), F32),
                   jax.ShapeDtypeStruct((D_MLSTM, S), F32),
                   jax.ShapeDtypeStruct((D_MLSTM, S), F32),
                   jax.ShapeDtypeStruct((SUBLANES, S), F32)],
        compiler_params=_params(),
        name="proj",
    )(x, g, w, wt, wg)


def _log_sigmoid(x):
    return jnp.minimum(x, 0.0) - jnp.log1p(jnp.exp(-jnp.abs(x)))


def _interleave(*stages):
    total = max(n for _, n in stages)
    done = [0] * len(stages)
    for tick in range(1, total + 1):
        for s, (gen, n) in enumerate(stages):
            want = -(-tick * n // total)
            while done[s] < want:
                next(gen, None)
                done[s] += 1
    for gen, _ in stages:
        for _ in gen:
            pass


_CUMMAX_LEVELS = LANES.bit_length() - 1
_GATE_PIECES = 3 + _CUMMAX_LEVELS
_QK_BLOCKS = 2 * D_MLSTM // LANES


def _conv_pieces(qk_ref, cw_ref, cb_ref, xpad_ref, qks_ref):
    T = MLSTM_ROWS
    L = MLSTM_CHUNK
    for k in range(_QK_BLOCKS):
        xpad_ref[k, SUBLANES:SUBLANES + T, :] = qk_ref[:, k * LANES:(k + 1) * LANES]
    yield
    for c in range(T // L):
        for k in range(_QK_BLOCKS):
            cols = slice(k * LANES, (k + 1) * LANES)
            for parity in range(2):
                acc = cb_ref[:, cols]
                for j in range(CONV_WIDTH):
                    start = c * L + SUBLANES - (CONV_WIDTH - 1) + parity + j
                    acc = acc + (xpad_ref[k, pl.ds(start, L // 2, stride=2), :]
                                 * cw_ref[j:j + 1, cols])
                act = acc * jax.nn.sigmoid(acc)
                if k < _QK_BLOCKS // 2:
                    act = act * (HEAD_DIM ** -0.5)
                qks_ref[k, pl.ds(c * L + parity, L // 2, stride=2), :] = act
            yield
    for k in range(_QK_BLOCKS):
        xpad_ref[k, 0:SUBLANES, :] = xpad_ref[k, T:T + SUBLANES, :]


def _gate_pieces(gr_ref, br_ref, m_ref, out):
    T = MLSTM_ROWS
    L = MLSTM_CHUNK
    n_chunks = T // L
    lane = lax.broadcasted_iota(jnp.int32, (SUBLANES, L), 1)
    row_id = lax.broadcasted_iota(jnp.int32, (L, L), 0)
    col_id = lax.broadcasted_iota(jnp.int32, (L, L), 1)
    triu = (row_id <= col_id).astype(BF16)
    g = gr_ref[...] + br_ref[...]
    logf = _log_sigmoid(g)
    yield
    lf = jnp.concatenate([pltpu.roll(logf[:, c * L:(c + 1) * L], N_HEADS, axis=0)
                          for c in range(n_chunks)], axis=0)
    hi = lf.astype(BF16)
    rest = lf - hi.astype(F32)
    mid = rest.astype(BF16)
    lo = (rest - mid.astype(F32)).astype(BF16)
    terms = jnp.dot(jnp.concatenate([hi, mid, lo], axis=0), triu,
                    preferred_element_type=F32)
    n_rows = n_chunks * SUBLANES
    b_all = terms[0:n_rows] + terms[n_rows:2 * n_rows] + terms[2 * n_rows:3 * n_rows]
    bs = [b_all[c * SUBLANES:(c + 1) * SUBLANES] for c in range(n_chunks)]
    rs = [g[:, c * L:(c + 1) * L] - bs[c] for c in range(n_chunks)]
    yield
    m_runs = list(rs)
    for level in range(_CUMMAX_LEVELS):
        d = 1 << level
        m_runs = [jnp.where(lane >= d, jnp.maximum(x, pltpu.roll(x, d, axis=1)), x)
                  for x in m_runs]
        yield
    m = m_ref[...]
    rows = []
    for c in range(n_chunks):
        b, r, m_run = bs[c], rs[c], m_runs[c]
        m_last = jnp.broadcast_to(m_run[:, L - 1:L], (SUBLANES, L))
        b_last = jnp.broadcast_to(b[:, L - 1:L], (SUBLANES, L))
        mu = jnp.maximum(m_run, m)
        mx = jnp.maximum(m, m_last)
        rows.append(dict(
            mu=mu,
            inter_scale=jnp.exp(m - mu),
            eb=jnp.exp(-b - mu),
            w=jnp.exp(r - m_last),
            s_old=jnp.exp(m - mx), s_new=jnp.exp(m_last - mx)))
        m = b_last + mx
    m_ref[...] = m
    r_pad = [jnp.zeros((SUBLANES, L), F32)] * (LANES // SUBLANES - n_chunks)
    out["rows"] = rows
    out["r_cols"] = jnp.concatenate(rs + r_pad, axis=0).T


def _mlstm_kernel(qk_ref, vt_ref, ot_ref, gr_ref, cw_ref, cb_ref, br_ref, nw_ref,
                  out_ref, xpad_ref, qks_ref, c_ref, m_ref):
    T = MLSTM_ROWS
    L = MLSTM_CHUNK
    D = HEAD_DIM
    n_chunks = T // L

    @pl.when(pl.program_id(0) == 0)
    def _():
        xpad_ref[:, 0:SUBLANES, :] = jnp.zeros((_QK_BLOCKS, SUBLANES, LANES), F32)
        c_ref[...] = jnp.zeros(c_ref.shape, F32)
        m_ref[...] = jnp.zeros(m_ref.shape, F32)

    gates = {}
    _interleave((_gate_pieces(gr_ref, br_ref, m_ref, gates), _GATE_PIECES),
                (_conv_pieces(qk_ref, cw_ref, cb_ref, xpad_ref, qks_ref),
                 1 + n_chunks * _QK_BLOCKS))
    rows, r_cols = gates["rows"], gates["r_cols"]

    row_id = lax.broadcasted_iota(jnp.int32, (L, L), 0)
    col_id = lax.broadcasted_iota(jnp.int32, (L, L), 1)
    causal = row_id <= col_id
    neg_inf = jnp.float32(-jnp.inf)
    for c in range(n_chunks):
        for h in range(N_HEADS):
            row = lambda name: rows[c][name][h:h + 1, :]
            q = qks_ref[h, c * L:(c + 1) * L, :].astype(BF16)
            k = qks_ref[N_HEADS + h, c * L:(c + 1) * L, :].astype(BF16)
            vt = vt_ref[h * D:(h + 1) * D, c * L:(c + 1) * L]
            r_col = r_cols[:, c * SUBLANES + h:c * SUBLANES + h + 1]

            dmat = jnp.exp(jnp.where(causal, r_col - row("mu"), neg_inf))
            s_kq = lax.dot_general(k, q, _NT_DIMS, preferred_element_type=F32) * dmat
            cq = lax.dot_general(c_ref[h].astype(BF16), q, _NT_DIMS,
                                 preferred_element_type=F32)
            isc = row("inter_scale")
            num = isc * cq[0:D, :] + jnp.dot(vt.astype(BF16), s_kq.astype(BF16),
                                             preferred_element_type=F32)
            den = isc * cq[D:D + 1, :] + jnp.sum(s_kq, axis=0, keepdims=True)
            hh = num * (1.0 / jnp.maximum(jnp.abs(den), row("eb")))

            w = row("w")
            vw = jnp.concatenate([vt * w, jnp.broadcast_to(w, (SUBLANES, L))],
                                 axis=0).astype(BF16)
            c_loc = jnp.dot(vw, k, preferred_element_type=F32)
            c_ref[h] = row("s_old") * c_ref[h] + row("s_new") * c_loc

            mean = jnp.mean(hh, axis=0, keepdims=True)
            cen = hh - mean
            var = jnp.mean(cen * cen, axis=0, keepdims=True)
            hn = cen * lax.rsqrt(var + EPS) * nw_ref[h * D:(h + 1) * D, :]
            gate = jax.nn.sigmoid(ot_ref[h * D:(h + 1) * D, c * L:(c + 1) * L])
            out_ref[c * L:(c + 1) * L, h * D:(h + 1) * D] = (hn * gate).T.astype(BF16)


def _mlstm_call(qk, vt, ot, gr, cw, cb, bias_r, nw_cols):
    S = qk.shape[0]
    T = MLSTM_ROWS
    return pl.pallas_call(
        _mlstm_kernel,
        grid=(S // T,),
        in_specs=[_row_spec(T, 1024), _lane_spec(D_MLSTM, T), _lane_spec(D_MLSTM, T),
                  _lane_spec(SUBLANES, T),
                  _const_spec(cw.shape), _const_spec(cb.shape),
                  _const_spec(bias_r.shape), _const_spec(nw_cols.shape)],
        out_specs=_row_spec(T, D_MLSTM),
        out_shape=jax.ShapeDtypeStruct((S, D_MLSTM), BF16),
        scratch_shapes=[pltpu.VMEM((_QK_BLOCKS, T + SUBLANES, LANES), F32),
                        pltpu.VMEM((_QK_BLOCKS, T, LANES), F32),
                        pltpu.VMEM((N_HEADS, HEAD_DIM + SUBLANES, HEAD_DIM), F32),
                        pltpu.VMEM((SUBLANES, LANES), F32)],
        compiler_params=_params(),
        name="mlstm",
    )(qk, vt, ot, gr, cw, cb, bias_r, nw_cols)


def _gelu_tanh(x):
    c = math.sqrt(2.0 / math.pi)
    return x * (0.5 * (1.0 + jnp.tanh(c * (x + 0.044715 * (x * x * x)))))


def _s5_kernel(u_ref, a_ref, wb_ref, wc_ref, d_ref, gw_ref, gb_ref, out_ref,
               x_ref, *z_refs):
    T = S5_ROWS

    @pl.when(pl.program_id(0) == 0)
    def _():
        x_ref[...] = jnp.zeros(x_ref.shape, F32)

    R = S5_SUB
    n_sub = T // R
    half = STATE_BLOCKS // 2
    a_re = [a_ref[:, c * LANES:(c + 1) * LANES] for c in range(half)]
    a_im = [a_ref[:, (half + c) * LANES:(half + c + 1) * LANES] for c in range(half)]

    def expand_piece(sb, j):
        blk = u_ref[sb * R:(sb + 1) * R, (j // 2) * LANES:(j // 2 + 1) * LANES].astype(BF16)
        bu = jnp.dot(blk, wb_ref[j], preferred_element_type=F32)
        for c in range(STATE_BLOCKS):
            z_refs[sb][c // 2, pl.ds(j * S5_PITCH + c % 2, R, stride=2), :] = (
                bu[:, c * LANES:(c + 1) * LANES])

    def scan_piece(sb, x, t0, t1):
        z_ref = z_refs[sb]
        for t in range(t0, t1):
            tile = lambda c: (c // 2, pl.ds(2 * t + c % 2, SUBLANES, stride=S5_PITCH),
                              slice(None))
            new = list(x)
            for c in range(half):
                x_re, x_im = x[c], x[half + c]
                new[c] = a_re[c] * x_re - a_im[c] * x_im + z_ref[tile(c)]
                new[half + c] = a_re[c] * x_im + a_im[c] * x_re + z_ref[tile(half + c)]
                z_ref[tile(c)] = new[c]
                z_ref[tile(half + c)] = new[half + c]
            x = new
        return x

    def contract_piece(sb, j):
        xs = jnp.concatenate(
            [z_refs[sb][c // 2, pl.ds(j * S5_PITCH + c % 2, R, stride=2), :]
             for c in range(STATE_BLOCKS)], axis=1).astype(BF16)
        return jnp.dot(xs, wc_ref[j], preferred_element_type=F32)

    def finish(sb, parts):
        blocks = [parts[2 * b] + parts[2 * b + 1] for b in range(SUBLANES // 2)]
        y = jnp.concatenate(blocks, axis=1) + d_ref[...] * u_ref[sb * R:(sb + 1) * R, :]
        z = _gelu_tanh(y)
        gate = jax.nn.sigmoid(jnp.dot(z.astype(BF16), gw_ref[...],
                                      preferred_element_type=F32) + gb_ref[...])
        out_ref[sb * R:(sb + 1) * R, :] = (z * gate).astype(BF16)

    x = [x_ref[:, c * LANES:(c + 1) * LANES] for c in range(STATE_BLOCKS)]
    steps = R // SUBLANES
    for j in range(SUBLANES):
        expand_piece(0, j)
    for sb in range(n_sub + 1):
        parts = []
        for j in range(SUBLANES):
            if sb < n_sub:
                x = scan_piece(sb, x, j * steps, (j + 1) * steps)
            if sb + 1 < n_sub:
                expand_piece(sb + 1, j)
            if sb >= 1:
                parts.append(contract_piece(sb - 1, j))
        if sb >= 1:
            finish(sb - 1, parts)
    for c in range(STATE_BLOCKS):
        x_ref[:, c * LANES:(c + 1) * LANES] = x[c]


def _s5_call(u, a_tile, wb, wc, d, gw, gb):
    S = u.shape[0]
    T = S5_ROWS
    return pl.pallas_call(
        _s5_kernel,
        grid=(S // T,),
        in_specs=[_row_spec(T, D_SSM), _const_spec(a_tile.shape), _const_spec(wb.shape),
                  _const_spec(wc.shape), _const_spec(d.shape), _const_spec(gw.shape),
                  _const_spec(gb.shape)],
        out_specs=_row_spec(T, D_SSM),
        out_shape=jax.ShapeDtypeStruct((S, D_SSM), BF16),
        scratch_shapes=([pltpu.VMEM((SUBLANES, STATE_LANES), F32)]
                        + [pltpu.VMEM((STATE_BLOCKS // 2, SUBLANES * S5_PITCH, LANES), F32)]
                        * (T // S5_SUB)),
        compiler_params=_params(),
        name="s5",
    )(u, a_tile, wb, wc, d, gw, gb)


def _s5_weights(lam_re, lam_im, log_dt, b_re, b_im, c_re, c_im):
    lr, li = lam_re.astype(F32), lam_im.astype(F32)
    dt = jnp.exp(log_dt.astype(F32))[:, None]
    mag = jnp.exp(lr * dt)
    abar_re, abar_im = mag * jnp.cos(li * dt), mag * jnp.sin(li * dt)
    den = lr * lr + li * li
    coef_re = ((abar_re - 1.0) * lr + abar_im * li) / den
    coef_im = (abar_im * lr - (abar_re - 1.0) * li) / den
    br, bi = b_re.astype(F32), b_im.astype(F32)
    bbar_re = coef_re[..., None] * br - coef_im[..., None] * bi
    bbar_im = coef_re[..., None] * bi + coef_im[..., None] * br
    J, Q, P, Hc = SUBLANES, GROUPS_PER_SUBLANE, SSM_STATE, SSM_GROUP
    a_tile = jnp.concatenate([abar_re.reshape(J, Q * P),
                              abar_im.reshape(J, Q * P)], axis=1)
    eye = jnp.eye(Q, dtype=F32)
    half = (jnp.arange(J) % 2)[:, None, None, None] == jnp.arange(2)[None, :, None, None]

    def expand_in(b):
        return jnp.einsum('jqpc,qr->jqcrp', b.reshape(J, Q, P, Hc), eye).reshape(J, Q * Hc, Q * P)

    wb = jnp.concatenate([expand_in(bbar_re), expand_in(bbar_im)], axis=2)
    wb = jnp.where(half, wb[:, None], 0.0).reshape(J, 2 * Q * Hc, STATE_LANES)

    def expand_out(c):
        return jnp.einsum('jqcp,qr->jqprc', c.reshape(J, Q, Hc, P), eye).reshape(J, Q * P, Q * Hc)

    wc = jnp.concatenate([expand_out(c_re.astype(F32)), -expand_out(c_im.astype(F32))], axis=1)
    half_c = jnp.swapaxes(half, 1, 2)
    wc = jnp.where(half_c, wc[:, :, None, :], 0.0).reshape(J, STATE_LANES, 2 * Q * Hc)
    return a_tile, wb.astype(BF16), wc.astype(BF16)


def _ffn_kernel(x_ref, hm_ref, y_ref, wo_ref, g2_ref, w1_ref, w2_ref, g3_ref, out_ref):
    mixed = (jnp.dot(hm_ref[...], wo_ref[0:D_MLSTM, :], preferred_element_type=F32)
             + jnp.dot(y_ref[...], wo_ref[D_MLSTM:D_MODEL, :], preferred_element_type=F32))
    x1 = x_ref[...] + mixed
    h2 = _rms(x1, g2_ref[...]).astype(BF16)
    mlp = None
    for c in range(D_FF // FF_CHUNK):
        a = jnp.dot(h2, w1_ref[:, c * FF_CHUNK:(c + 1) * FF_CHUNK],
                    preferred_element_type=F32)
        a = jnp.maximum(a, 0.0)
        part = jnp.dot((a * a).astype(BF16), w2_ref[c * FF_CHUNK:(c + 1) * FF_CHUNK, :],
                       preferred_element_type=F32)
        mlp = part if mlp is None else mlp + part
    out_ref[...] = _rms(x1 + mlp, g3_ref[...])


def _ffn_call(x, hm, y, wo, g2, w1, w2, g3):
    S = x.shape[0]
    T = FFN_ROWS
    single = dict(pipeline_mode=pl.Buffered(WEIGHT_BUFFERS))
    return pl.pallas_call(
        _ffn_kernel,
        grid=(S // T,),
        in_specs=[_row_spec(T, D_MODEL), _row_spec(T, D_MLSTM), _row_spec(T, D_SSM),
                  pl.BlockSpec(wo.shape, lambda i: (0, 0), **single),
                  _const_spec(g2.shape),
                  pl.BlockSpec(w1.shape, lambda i: (0, 0), **single),
                  pl.BlockSpec(w2.shape, lambda i: (0, 0), **single),
                  _const_spec(g3.shape)],
        out_specs=_row_spec(T, D_MODEL),
        out_shape=jax.ShapeDtypeStruct((S, D_MODEL), F32),
        compiler_params=_params(),
        name="ffn",
    )(x, hm, y, wo, g2, w1, w2, g3)


def _layer(x, mix_norm_w, w_in, conv_w, conv_b, i_bias, f_bias, mlstm_norm_w,
           lam_re, lam_im, log_dt, b_re, b_im, c_re, c_im, ssm_d,
           glu_w, glu_b, w_out, mlp_norm_w, w_ff1, w_ff2, out_norm_w):
    n_qk, n_vo, n_gate = 2 * D_MLSTM, 2 * D_MLSTM, 2 * N_HEADS
    w_rows = jnp.concatenate([w_in[:, :n_qk], w_in[:, n_qk + n_vo + n_gate:]],
                             axis=1).astype(BF16)
    w_cols = w_in[:, n_qk:n_qk + n_vo].T.astype(BF16)
    wg = w_in[:, n_qk + n_vo:n_qk + n_vo + n_gate].T.astype(BF16)
    qk, u, vt, ot, gr = _proj_call(x, mix_norm_w[None, :], w_rows, w_cols, wg)

    bias_r = jnp.concatenate([i_bias, f_bias]).astype(F32)[:, None]
    nw_cols = jnp.broadcast_to(mlstm_norm_w.astype(F32)[:, None], (D_MLSTM, LANES))
    hm = _mlstm_call(qk, vt, ot, gr, conv_w, conv_b[None, :], bias_r, nw_cols)

    a_tile, wb, wc = _s5_weights(lam_re, lam_im, log_dt, b_re, b_im, c_re, c_im)
    y = _s5_call(u, a_tile, wb, wc, ssm_d[None, :], glu_w.astype(BF16), glu_b[None, :])

    return _ffn_call(x, hm, y, w_out.astype(BF16), mlp_norm_w[None, :],
                     w_ff1.astype(BF16), w_ff2.astype(BF16), out_norm_w[None, :])


def kernel(x, mix_norm_w, w_in, conv_w, conv_b, i_bias, f_bias, mlstm_norm_w, ssm_lam_re, ssm_lam_im, ssm_log_dt, ssm_b_re, ssm_b_im, ssm_c_re, ssm_c_im, ssm_d, glu_w, glu_b, w_out, mlp_norm_w, w_ff1, w_ff2, final_norm_w):
    assert x.shape[0] == 1 and mix_norm_w.shape[0] == 1
    xs = x[0]
    out = _layer(xs, mix_norm_w[0], w_in[0], conv_w[0], conv_b[0], i_bias[0], f_bias[0],
                 mlstm_norm_w[0], ssm_lam_re[0], ssm_lam_im[0], ssm_log_dt[0],
                 ssm_b_re[0], ssm_b_im[0], ssm_c_re[0], ssm_c_im[0], ssm_d[0],
                 glu_w[0], glu_b[0], w_out[0], mlp_norm_w[0], w_ff1[0], w_ff2[0],
                 final_norm_w)
    return out[None]
```

```python
import math

import jax
import jax.numpy as jnp
from jax import lax
from jax.experimental import pallas as pl
from jax.experimental.pallas import tpu as pltpu

F32 = jnp.float32
BF16 = jnp.bfloat16

D_MODEL = 1024
D_MLSTM = 512
N_HEADS = 4
HEAD_DIM = 128
CONV_WIDTH = 4
D_SSM = 512
SSM_GROUP = 16
N_GROUPS = 32
SSM_STATE = 64
D_FF = 4096
EPS = 1e-6

SUBLANES = 8
LANES = 128

PROJ_ROWS = 1024
MLSTM_ROWS = 1024
MLSTM_CHUNK = 128
S5_ROWS = 1024
S5_SUB = 256
S5_PITCH = S5_SUB + 4
FFN_ROWS = 1024
FF_CHUNK = 1024
WEIGHT_BUFFERS = 1
VMEM_LIMIT = 56 * 1024 * 1024

GROUPS_PER_SUBLANE = N_GROUPS // SUBLANES
STATE_HALF = GROUPS_PER_SUBLANE * SSM_STATE
STATE_LANES = 2 * STATE_HALF
STATE_BLOCKS = STATE_LANES // LANES


def _const_spec(shape):
    return pl.BlockSpec(shape, lambda i: (0,) * len(shape))


def _row_spec(rows, cols):
    return pl.BlockSpec((rows, cols), lambda i: (i, 0))


def _lane_spec(rows, cols):
    return pl.BlockSpec((rows, cols), lambda i: (0, i))


def _params(flags=None):
    return pltpu.CompilerParams(dimension_semantics=("arbitrary",),
                                vmem_limit_bytes=VMEM_LIMIT, flags=flags)


def _rms(x, g):
    r = lax.rsqrt(jnp.mean(x * x, axis=-1, keepdims=True) + EPS)
    return x * r * g


_NT_DIMS = (((1,), (1,)), ((), ()))


def _proj_kernel(x_ref, g_ref, w_ref, wt_ref, wg_ref, qk_ref, u_ref, vt_ref, ot_ref, gr_ref):
    h = _rms(x_ref[...], g_ref[...]).astype(BF16)
    qk_ref[...] = jnp.dot(h, w_ref[:, 0:1024], preferred_element_type=F32)
    u = jnp.dot(h, w_ref[:, 1024:1536], preferred_element_type=F32)
    for b in range(D_SSM // LANES):
        u_ref[b] = u[:, b * LANES:(b + 1) * LANES]
    vt_ref[...] = lax.dot_general(wt_ref[0:D_MLSTM, :], h, _NT_DIMS,
                                  preferred_element_type=F32)
    ot_ref[...] = lax.dot_general(wt_ref[D_MLSTM:2 * D_MLSTM, :], h, _NT_DIMS,
                                  preferred_element_type=F32)
    gr_ref[...] = lax.dot_general(wg_ref[...], h, _NT_DIMS, preferred_element_type=F32)


def _proj_call(x, g, w, wt, wg):
    S = x.shape[0]
    T = PROJ_ROWS
    return pl.pallas_call(
        _proj_kernel,
        grid=(S // T,),
        in_specs=[_row_spec(T, D_MODEL), _const_spec((1, D_MODEL)),
                  _const_spec(w.shape), _const_spec(wt.shape), _const_spec(wg.shape)],
        out_specs=[_row_spec(T, 1024),
                   pl.BlockSpec((D_SSM // LANES, T, LANES), lambda i: (0, i, 0)),
                   _lane_spec(D_MLSTM, T), _lane_spec(D_MLSTM, T), _lane_spec(SUBLANES, T)],
        out_shape=[jax.ShapeDtypeStruct((S, 1024), F32),
                   jax.ShapeDtypeStruct((D_SSM // LANES, S, LANES), F32),
                   jax.ShapeDtypeStruct((D_MLSTM, S), F32),
                   jax.ShapeDtypeStruct((D_MLSTM, S), F32),
                   jax.ShapeDtypeStruct((SUBLANES, S), F32)],
        compiler_params=_params(),
        name="proj",
    )(x, g, w, wt, wg)


def _log_sigmoid(x):
    return jnp.minimum(x, 0.0) - jnp.log1p(jnp.exp(-jnp.abs(x)))


def _interleave(*stages):
    total = max(n for _, n in stages)
    done = [0] * len(stages)
    for tick in range(1, total + 1):
        for s, (gen, n) in enumerate(stages):
            want = -(-tick * n // total)
            while done[s] < want:
                next(gen, None)
                done[s] += 1
    for gen, _ in stages:
        for _ in gen:
            pass


_CUMMAX_LEVELS = LANES.bit_length() - 1
_GATE_PIECES = 3 + _CUMMAX_LEVELS
_QK_BLOCKS = 2 * D_MLSTM // LANES


def _conv_pieces(qk_ref, cw_ref, cb_ref, xpad_ref, qks_ref):
    T = MLSTM_ROWS
    L = MLSTM_CHUNK
    for k in range(_QK_BLOCKS):
        xpad_ref[k, SUBLANES:SUBLANES + T, :] = qk_ref[:, k * LANES:(k + 1) * LANES]
    yield
    for c in range(T // L):
        for k in range(_QK_BLOCKS):
            cols = slice(k * LANES, (k + 1) * LANES)
            for parity in range(2):
                acc = cb_ref[:, cols]
                for j in range(CONV_WIDTH):
                    start = c * L + SUBLANES - (CONV_WIDTH - 1) + parity + j
                    acc = acc + (xpad_ref[k, pl.ds(start, L // 2, stride=2), :]
                                 * cw_ref[j:j + 1, cols])
                act = acc * jax.nn.sigmoid(acc)
                if k < _QK_BLOCKS // 2:
                    act = act * (HEAD_DIM ** -0.5)
                qks_ref[k, pl.ds(c * L + parity, L // 2, stride=2), :] = act
            yield
    for k in range(_QK_BLOCKS):
        xpad_ref[k, 0:SUBLANES, :] = xpad_ref[k, T:T + SUBLANES, :]


def _gate_pieces(gr_ref, br_ref, m_ref, out):
    T = MLSTM_ROWS
    L = MLSTM_CHUNK
    n_chunks = T // L
    lane = lax.broadcasted_iota(jnp.int32, (SUBLANES, L), 1)
    row_id = lax.broadcasted_iota(jnp.int32, (L, L), 0)
    col_id = lax.broadcasted_iota(jnp.int32, (L, L), 1)
    triu = (row_id <= col_id).astype(BF16)
    g = gr_ref[...] + br_ref[...]
    logf = _log_sigmoid(g)
    yield
    lf = jnp.concatenate([pltpu.roll(logf[:, c * L:(c + 1) * L], N_HEADS, axis=0)
                          for c in range(n_chunks)], axis=0)
    hi = lf.astype(BF16)
    rest = lf - hi.astype(F32)
    mid = rest.astype(BF16)
    lo = (rest - mid.astype(F32)).astype(BF16)
    terms = jnp.dot(jnp.concatenate([hi, mid, lo], axis=0), triu,
                    preferred_element_type=F32)
    n_rows = n_chunks * SUBLANES
    b_all = terms[0:n_rows] + terms[n_rows:2 * n_rows] + terms[2 * n_rows:3 * n_rows]
    bs = [b_all[c * SUBLANES:(c + 1) * SUBLANES] for c in range(n_chunks)]
    rs = [g[:, c * L:(c + 1) * L] - bs[c] for c in range(n_chunks)]
    yield
    m_runs = list(rs)
    for level in range(_CUMMAX_LEVELS):
        d = 1 << level
        m_runs = [jnp.where(lane >= d, jnp.maximum(x, pltpu.roll(x, d, axis=1)), x)
                  for x in m_runs]
        yield
    m = m_ref[...]
    rows = []
    for c in range(n_chunks):
        b, r, m_run = bs[c], rs[c], m_runs[c]
        m_last = jnp.broadcast_to(m_run[:, L - 1:L], (SUBLANES, L))
        b_last = jnp.broadcast_to(b[:, L - 1:L], (SUBLANES, L))
        mu = jnp.maximum(m_run, m)
        mx = jnp.maximum(m, m_last)
        rows.append(dict(
            mu=mu,
            inter_scale=jnp.exp(m - mu),
            eb=jnp.exp(-b - mu),
            w=jnp.exp(r - m_last),
            s_old=jnp.exp(m - mx), s_new=jnp.exp(m_last - mx)))
        m = b_last + mx
    m_ref[...] = m
    r_pad = [jnp.zeros((SUBLANES, L), F32)] * (LANES // SUBLANES - n_chunks)
    out["rows"] = rows
    out["r_cols"] = jnp.concatenate(rs + r_pad, axis=0).T


def _mlstm_kernel(qk_ref, vt_ref, ot_ref, gr_ref, cw_ref, cb_ref, br_ref, nw_ref,
                  out_ref, xpad_ref, qks_ref, c_ref, m_ref):
    T = MLSTM_ROWS
    L = MLSTM_CHUNK
    D = HEAD_DIM
    n_chunks = T // L

    @pl.when(pl.program_id(0) == 0)
    def _():
        xpad_ref[:, 0:SUBLANES, :] = jnp.zeros((_QK_BLOCKS, SUBLANES, LANES), F32)
        c_ref[...] = jnp.zeros(c_ref.shape, F32)
        m_ref[...] = jnp.zeros(m_ref.shape, F32)

    gates = {}
    _interleave((_gate_pieces(gr_ref, br_ref, m_ref, gates), _GATE_PIECES),
                (_conv_pieces(qk_ref, cw_ref, cb_ref, xpad_ref, qks_ref),
                 1 + n_chunks * _QK_BLOCKS))
    rows, r_cols = gates["rows"], gates["r_cols"]

    row_id = lax.broadcasted_iota(jnp.int32, (L, L), 0)
    col_id = lax.broadcasted_iota(jnp.int32, (L, L), 1)
    causal = row_id <= col_id
    neg_inf = jnp.float32(-jnp.inf)
    for c in range(n_chunks):
        for h in range(N_HEADS):
            row = lambda name: rows[c][name][h:h + 1, :]
            q = qks_ref[h, c * L:(c + 1) * L, :].astype(BF16)
            k = qks_ref[N_HEADS + h, c * L:(c + 1) * L, :].astype(BF16)
            vt = vt_ref[h * D:(h + 1) * D, c * L:(c + 1) * L]
            r_col = r_cols[:, c * SUBLANES + h:c * SUBLANES + h + 1]

            dmat = jnp.exp(jnp.where(causal, r_col - row("mu"), neg_inf))
            s_kq = lax.dot_general(k, q, _NT_DIMS, preferred_element_type=F32) * dmat
            cq = lax.dot_general(c_ref[h].astype(BF16), q, _NT_DIMS,
                                 preferred_element_type=F32)
            isc = row("inter_scale")
            num = isc * cq[0:D, :] + jnp.dot(vt.astype(BF16), s_kq.astype(BF16),
                                             preferred_element_type=F32)
            den = isc * cq[D:D + 1, :] + jnp.sum(s_kq, axis=0, keepdims=True)
            hh = num * (1.0 / jnp.maximum(jnp.abs(den), row("eb")))

            w = row("w")
            vw = jnp.concatenate([vt * w, jnp.broadcast_to(w, (SUBLANES, L))],
                                 axis=0).astype(BF16)
            c_loc = jnp.dot(vw, k, preferred_element_type=F32)
            c_ref[h] = row("s_old") * c_ref[h] + row("s_new") * c_loc

            mean = jnp.mean(hh, axis=0, keepdims=True)
            cen = hh - mean
            var = jnp.mean(cen * cen, axis=0, keepdims=True)
            hn = cen * lax.rsqrt(var + EPS) * nw_ref[h * D:(h + 1) * D, :]
            gate = jax.nn.sigmoid(ot_ref[h * D:(h + 1) * D, c * L:(c + 1) * L])
            out_ref[c * L:(c + 1) * L, h * D:(h + 1) * D] = (hn * gate).T.astype(BF16)


def _mlstm_call(qk, vt, ot, gr, cw, cb, bias_r, nw_cols):
    S = qk.shape[0]
    T = MLSTM_ROWS
    return pl.pallas_call(
        _mlstm_kernel,
        grid=(S // T,),
        in_specs=[_row_spec(T, 1024), _lane_spec(D_MLSTM, T), _lane_spec(D_MLSTM, T),
                  _lane_spec(SUBLANES, T),
                  _const_spec(cw.shape), _const_spec(cb.shape),
                  _const_spec(bias_r.shape), _const_spec(nw_cols.shape)],
        out_specs=_row_spec(T, D_MLSTM),
        out_shape=jax.ShapeDtypeStruct((S, D_MLSTM), BF16),
        scratch_shapes=[pltpu.VMEM((_QK_BLOCKS, T + SUBLANES, LANES), F32),
                        pltpu.VMEM((_QK_BLOCKS, T, LANES), F32),
                        pltpu.VMEM((N_HEADS, HEAD_DIM + SUBLANES, HEAD_DIM), F32),
                        pltpu.VMEM((SUBLANES, LANES), F32)],
        compiler_params=_params(),
        name="mlstm",
    )(qk, vt, ot, gr, cw, cb, bias_r, nw_cols)


def _gelu_tanh(x):
    c = math.sqrt(2.0 / math.pi)
    return x * (0.5 * (1.0 + jnp.tanh(c * (x + 0.044715 * (x * x * x)))))


def _s5_kernel(u_ref, a2_ref, wb_ref, wc_ref, wd_ref, d_ref, gw_ref, gb_ref, out_ref,
               x_ref, y_ref, *z_refs):
    T = S5_ROWS

    @pl.when(pl.program_id(0) == 0)
    def _():
        x_ref[...] = jnp.zeros(x_ref.shape, F32)

    R = S5_SUB
    P = R // 2
    n_sub = T // R
    half = STATE_BLOCKS // 2
    n_blocks = D_SSM // LANES
    a_re = [a2_ref[:, c * LANES:(c + 1) * LANES] for c in range(half)]
    a_im = [a2_ref[:, (half + c) * LANES:(half + c + 1) * LANES] for c in range(half)]

    def pair_rows(sb, b):
        even = u_ref[b, pl.ds(sb * R, P, stride=2), :]
        odd = u_ref[b, pl.ds(sb * R + 1, P, stride=2), :]
        return jnp.concatenate([even, odd], axis=1).astype(BF16)

    def expand_piece(sb, j):
        w = jnp.dot(pair_rows(sb, j // 2), wb_ref[j], preferred_element_type=F32)
        for c in range(STATE_BLOCKS):
            z_refs[sb][c // 2, pl.ds(j * S5_PITCH + c % 2, P, stride=2), :] = (
                w[:, c * LANES:(c + 1) * LANES])

    def scan_piece(sb, x, i0, i1):
        z_ref = z_refs[sb]
        for i in range(i0, i1):
            tile = lambda c: (c // 2, pl.ds(2 * i + c % 2, SUBLANES, stride=S5_PITCH),
                              slice(None))
            new = list(x)
            for c in range(half):
                x_re, x_im = x[c], x[half + c]
                new[c] = a_re[c] * x_re - a_im[c] * x_im + z_ref[tile(c)]
                new[half + c] = a_re[c] * x_im + a_im[c] * x_re + z_ref[tile(half + c)]
                z_ref[tile(c)] = x_re
                z_ref[tile(half + c)] = x_im
            x = new
        return x

    def contract_piece(sb, j):
        xs = jnp.concatenate(
            [z_refs[sb][c // 2, pl.ds(j * S5_PITCH + c % 2, P, stride=2), :]
             for c in range(STATE_BLOCKS)], axis=1).astype(BF16)
        return jnp.dot(xs, wc_ref[j], preferred_element_type=F32)

    def finish(sb, parts):
        for b in range(n_blocks):
            yb = (parts[2 * b] + parts[2 * b + 1]
                  + jnp.dot(pair_rows(sb, b), wd_ref[b], preferred_element_type=F32))
            y_ref[b, pl.ds(sb * R, P, stride=2), :] = yb[:, 0:LANES]
            y_ref[b, pl.ds(sb * R + 1, P, stride=2), :] = yb[:, LANES:2 * LANES]
        rows = slice(sb * R, (sb + 1) * R)
        y = jnp.concatenate([y_ref[b, rows, :] for b in range(n_blocks)], axis=1)
        u = jnp.concatenate([u_ref[b, rows, :] for b in range(n_blocks)], axis=1)
        z = _gelu_tanh(y + d_ref[...] * u)
        gate = jax.nn.sigmoid(jnp.dot(z.astype(BF16), gw_ref[...],
                                      preferred_element_type=F32) + gb_ref[...])
        out_ref[rows, :] = (z * gate).astype(BF16)

    x = [x_ref[:, c * LANES:(c + 1) * LANES] for c in range(STATE_BLOCKS)]
    steps = P // SUBLANES
    for j in range(SUBLANES):
        expand_piece(0, j)
    for sb in range(n_sub + 1):
        parts = []
        for j in range(SUBLANES):
            if sb < n_sub:
                x = scan_piece(sb, x, j * steps, (j + 1) * steps)
            if sb + 1 < n_sub:
                expand_piece(sb + 1, j)
            if sb >= 1:
                parts.append(contract_piece(sb - 1, j))
        if sb >= 1:
            finish(sb - 1, parts)
    for c in range(STATE_BLOCKS):
        x_ref[:, c * LANES:(c + 1) * LANES] = x[c]


def _s5_call(u, a2_tile, wb, wc, wd, d, gw, gb):
    S = u.shape[1]
    T = S5_ROWS
    n_blocks = D_SSM // LANES
    return pl.pallas_call(
        _s5_kernel,
        grid=(S // T,),
        in_specs=[pl.BlockSpec((n_blocks, T, LANES), lambda i: (0, i, 0)),
                  _const_spec(a2_tile.shape), _const_spec(wb.shape), _const_spec(wc.shape),
                  _const_spec(wd.shape), _const_spec(d.shape), _const_spec(gw.shape),
                  _const_spec(gb.shape)],
        out_specs=_row_spec(T, D_SSM),
        out_shape=jax.ShapeDtypeStruct((S, D_SSM), BF16),
        scratch_shapes=([pltpu.VMEM((SUBLANES, STATE_LANES), F32),
                         pltpu.VMEM((n_blocks, T, LANES), F32)]
                        + [pltpu.VMEM((STATE_BLOCKS // 2, SUBLANES * S5_PITCH, LANES), F32)]
                        * (T // S5_SUB)),
        compiler_params=_params(),
        name="s5",
    )(u, a2_tile, wb, wc, wd, d, gw, gb)


def _s5_weights(lam_re, lam_im, log_dt, b_re, b_im, c_re, c_im):
    cmul = lambda xr, xi, yr, yi: (xr * yr - xi * yi, xr * yi + xi * yr)
    lr, li = lam_re.astype(F32), lam_im.astype(F32)
    dt = jnp.exp(log_dt.astype(F32))[:, None]
    mag = jnp.exp(lr * dt)
    a_re, a_im = mag * jnp.cos(li * dt), mag * jnp.sin(li * dt)
    a2_re, a2_im = cmul(a_re, a_im, a_re, a_im)
    den = lr * lr + li * li
    coef_re = ((a_re - 1.0) * lr + a_im * li) / den
    coef_im = (a_im * lr - (a_re - 1.0) * li) / den
    bb_re, bb_im = cmul(coef_re[..., None], coef_im[..., None],
                        b_re.astype(F32), b_im.astype(F32))
    abb_re, abb_im = cmul(a_re[..., None], a_im[..., None], bb_re, bb_im)
    cr, ci = c_re.astype(F32), c_im.astype(F32)
    ca_re, ca_im = cmul(cr, ci, a_re[:, None, :], a_im[:, None, :])
    ca2_re, ca2_im = cmul(cr, ci, a2_re[:, None, :], a2_im[:, None, :])

    J, Q, P, Hc = SUBLANES, GROUPS_PER_SUBLANE, SSM_STATE, SSM_GROUP
    a2_tile = jnp.concatenate([a2_re.reshape(J, Q * P), a2_im.reshape(J, Q * P)], axis=1)
    eye = jnp.eye(Q, dtype=F32)
    half = (jnp.arange(J) % 2)[:, None, None, None] == jnp.arange(2)[None, :, None, None]

    def in_weights(w_re, w_im):
        def blocks(w):
            return jnp.einsum('jqpc,qr->jqcrp', w.reshape(J, Q, P, Hc),
                              eye).reshape(J, Q * Hc, Q * P)
        w = jnp.concatenate([blocks(w_re), blocks(w_im)], axis=2)
        return jnp.where(half, w[:, None], 0.0).reshape(J, 2 * Q * Hc, STATE_LANES)

    def out_weights(w_re, w_im):
        def blocks(w):
            return jnp.einsum('jqcp,qr->jqprc', w.reshape(J, Q, Hc, P),
                              eye).reshape(J, Q * P, Q * Hc)
        w = jnp.concatenate([blocks(w_re), -blocks(w_im)], axis=1)
        half_c = jnp.swapaxes(half, 1, 2)
        return jnp.where(half_c, w[:, :, None, :], 0.0).reshape(J, STATE_LANES, 2 * Q * Hc)

    wb = jnp.concatenate([in_weights(abb_re, abb_im), in_weights(bb_re, bb_im)], axis=1)
    wc = jnp.concatenate([out_weights(ca_re, ca_im), out_weights(ca2_re, ca2_im)], axis=2)
    k0 = jnp.einsum('gop,gpc->goc', cr, bb_re) - jnp.einsum('gop,gpc->goc', ci, bb_im)
    k1 = jnp.einsum('gop,gpc->goc', cr, abb_re) - jnp.einsum('gop,gpc->goc', ci, abb_im)
    n_blocks, per = D_SSM // LANES, LANES // Hc
    eye8 = jnp.eye(per, dtype=F32)

    def lane_block(k):
        return jnp.einsum('bgoc,gh->bgcho', k.reshape(n_blocks, per, Hc, Hc),
                          eye8).reshape(n_blocks, LANES, LANES)

    zero = jnp.zeros((n_blocks, LANES, LANES), F32)
    wd = jnp.concatenate([jnp.concatenate([lane_block(k0), lane_block(k1)], axis=2),
                          jnp.concatenate([zero, lane_block(k0)], axis=2)], axis=1)
    return a2_tile, wb.astype(BF16), wc.astype(BF16), wd.astype(BF16)


def _ffn_kernel(x_ref, hm_ref, y_ref, wo_ref, g2_ref, w1_ref, w2_ref, g3_ref, out_ref):
    mixed = (jnp.dot(hm_ref[...], wo_ref[0:D_MLSTM, :], preferred_element_type=F32)
             + jnp.dot(y_ref[...], wo_ref[D_MLSTM:D_MODEL, :], preferred_element_type=F32))
    x1 = x_ref[...] + mixed
    h2 = _rms(x1, g2_ref[...]).astype(BF16)
    mlp = None
    for c in range(D_FF // FF_CHUNK):
        a = jnp.dot(h2, w1_ref[:, c * FF_CHUNK:(c + 1) * FF_CHUNK],
                    preferred_element_type=F32)
        a = jnp.maximum(a, 0.0)
        part = jnp.dot((a * a).astype(BF16), w2_ref[c * FF_CHUNK:(c + 1) * FF_CHUNK, :],
                       preferred_element_type=F32)
        mlp = part if mlp is None else mlp + part
    out_ref[...] = _rms(x1 + mlp, g3_ref[...])


def _ffn_call(x, hm, y, wo, g2, w1, w2, g3):
    S = x.shape[0]
    T = FFN_ROWS
    single = dict(pipeline_mode=pl.Buffered(WEIGHT_BUFFERS))
    return pl.pallas_call(
        _ffn_kernel,
        grid=(S // T,),
        in_specs=[_row_spec(T, D_MODEL), _row_spec(T, D_MLSTM), _row_spec(T, D_SSM),
                  pl.BlockSpec(wo.shape, lambda i: (0, 0), **single),
                  _const_spec(g2.shape),
                  pl.BlockSpec(w1.shape, lambda i: (0, 0), **single),
                  pl.BlockSpec(w2.shape, lambda i: (0, 0), **single),
                  _const_spec(g3.shape)],
        out_specs=_row_spec(T, D_MODEL),
        out_shape=jax.ShapeDtypeStruct((S, D_MODEL), F32),
        compiler_params=_params(),
        name="ffn",
    )(x, hm, y, wo, g2, w1, w2, g3)


def _layer(x, mix_norm_w, w_in, conv_w, conv_b, i_bias, f_bias, mlstm_norm_w,
           lam_re, lam_im, log_dt, b_re, b_im, c_re, c_im, ssm_d,
           glu_w, glu_b, w_out, mlp_norm_w, w_ff1, w_ff2, out_norm_w):
    n_qk, n_vo, n_gate = 2 * D_MLSTM, 2 * D_MLSTM, 2 * N_HEADS
    w_rows = jnp.concatenate([w_in[:, :n_qk], w_in[:, n_qk + n_vo + n_gate:]],
                             axis=1).astype(BF16)
    w_cols = w_in[:, n_qk:n_qk + n_vo].T.astype(BF16)
    wg = w_in[:, n_qk + n_vo:n_qk + n_vo + n_gate].T.astype(BF16)
    qk, u, vt, ot, gr = _proj_call(x, mix_norm_w[None, :], w_rows, w_cols, wg)

    bias_r = jnp.concatenate([i_bias, f_bias]).astype(F32)[:, None]
    nw_cols = jnp.broadcast_to(mlstm_norm_w.astype(F32)[:, None], (D_MLSTM, LANES))
    hm = _mlstm_call(qk, vt, ot, gr, conv_w, conv_b[None, :], bias_r, nw_cols)

    a2_tile, wb, wc, wd = _s5_weights(lam_re, lam_im, log_dt, b_re, b_im, c_re, c_im)
    y = _s5_call(u, a2_tile, wb, wc, wd, ssm_d[None, :], glu_w.astype(BF16),
                 glu_b[None, :])

    return _ffn_call(x, hm, y, w_out.astype(BF16), mlp_norm_w[None, :],
                     w_ff1.astype(BF16), w_ff2.astype(BF16), out_norm_w[None, :])


def kernel(x, mix_norm_w, w_in, conv_w, conv_b, i_bias, f_bias, mlstm_norm_w, ssm_lam_re, ssm_lam_im, ssm_log_dt, ssm_b_re, ssm_b_im, ssm_c_re, ssm_c_im, ssm_d, glu_w, glu_b, w_out, mlp_norm_w, w_ff1, w_ff2, final_norm_w):
    assert x.shape[0] == 1 and mix_norm_w.shape[0] == 1
    xs = x[0]
    out = _layer(xs, mix_norm_w[0], w_in[0], conv_w[0], conv_b[0], i_bias[0], f_bias[0],
                 mlstm_norm_w[0], ssm_lam_re[0], ssm_lam_im[0], ssm_log_dt[0],
                 ssm_b_re[0], ssm_b_im[0], ssm_c_re[0], ssm_c_im[0], ssm_d[0],
                 glu_w[0], glu_b[0], w_out[0], mlp_norm_w[0], w_ff1[0], w_ff2[0],
                 final_norm_w)
    return out[None]
```

```python
import math

import jax
import jax.numpy as jnp
from jax import lax
from jax.experimental import pallas as pl
from jax.experimental.pallas import tpu as pltpu

F32 = jnp.float32
BF16 = jnp.bfloat16

D_MODEL = 1024
D_MLSTM = 512
N_HEADS = 4
HEAD_DIM = 128
CONV_WIDTH = 4
D_SSM = 512
SSM_GROUP = 16
N_GROUPS = 32
SSM_STATE = 64
D_FF = 4096
EPS = 1e-6

SUBLANES = 8
LANES = 128

PROJ_ROWS = 1024
MLSTM_ROWS = 1024
MLSTM_CHUNK = 128
S5_ROWS = 1024
S5_SUB = 512
S5_PITCH = S5_SUB + 4
FFN_ROWS = 1024
FF_CHUNK = 1024
WEIGHT_BUFFERS = 1
VMEM_LIMIT = 56 * 1024 * 1024

GROUPS_PER_SUBLANE = N_GROUPS // SUBLANES
STATE_HALF = GROUPS_PER_SUBLANE * SSM_STATE
STATE_LANES = 2 * STATE_HALF
STATE_BLOCKS = STATE_LANES // LANES


def _const_spec(shape):
    return pl.BlockSpec(shape, lambda i: (0,) * len(shape))


def _row_spec(rows, cols):
    return pl.BlockSpec((rows, cols), lambda i: (i, 0))


def _lane_spec(rows, cols):
    return pl.BlockSpec((rows, cols), lambda i: (0, i))


def _params(flags=None):
    return pltpu.CompilerParams(dimension_semantics=("arbitrary",),
                                vmem_limit_bytes=VMEM_LIMIT, flags=flags)


def _rms(x, g):
    r = lax.rsqrt(jnp.mean(x * x, axis=-1, keepdims=True) + EPS)
    return x * r * g


_NT_DIMS = (((1,), (1,)), ((), ()))


def _proj_kernel(x_ref, g_ref, w_ref, wt_ref, wg_ref, qk_ref, u_ref, vt_ref, ot_ref, gr_ref):
    h = _rms(x_ref[...], g_ref[...]).astype(BF16)
    qk_ref[...] = jnp.dot(h, w_ref[:, 0:1024], preferred_element_type=F32)
    u = jnp.dot(h, w_ref[:, 1024:1536], preferred_element_type=F32)
    for b in range(D_SSM // LANES):
        u_ref[b] = u[:, b * LANES:(b + 1) * LANES]
    vt_ref[...] = lax.dot_general(wt_ref[0:D_MLSTM, :], h, _NT_DIMS,
                                  preferred_element_type=F32)
    ot_ref[...] = lax.dot_general(wt_ref[D_MLSTM:2 * D_MLSTM, :], h, _NT_DIMS,
                                  preferred_element_type=F32)
    gr_ref[...] = lax.dot_general(wg_ref[...], h, _NT_DIMS, preferred_element_type=F32)


def _proj_call(x, g, w, wt, wg):
    S = x.shape[0]
    T = PROJ_ROWS
    return pl.pallas_call(
        _proj_kernel,
        grid=(S // T,),
        in_specs=[_row_spec(T, D_MODEL), _const_spec((1, D_MODEL)),
                  _const_spec(w.shape), _const_spec(wt.shape), _const_spec(wg.shape)],
        out_specs=[_row_spec(T, 1024),
                   pl.BlockSpec((D_SSM // LANES, T, LANES), lambda i: (0, i, 0)),
                   _lane_spec(D_MLSTM, T), _lane_spec(D_MLSTM, T), _lane_spec(SUBLANES, T)],
        out_shape=[jax.ShapeDtypeStruct((S, 1024), F32),
                   jax.ShapeDtypeStruct((D_SSM // LANES, S, LANES), F32),
                   jax.ShapeDtypeStruct((D_MLSTM, S), F32),
                   jax.ShapeDtypeStruct((D_MLSTM, S), F32),
                   jax.ShapeDtypeStruct((SUBLANES, S), F32)],
        compiler_params=_params(),
        name="proj",
    )(x, g, w, wt, wg)


def _log_sigmoid(x):
    return jnp.minimum(x, 0.0) - jnp.log1p(jnp.exp(-jnp.abs(x)))


def _interleave(*stages):
    total = max(n for _, n in stages)
    done = [0] * len(stages)
    for tick in range(1, total + 1):
        for s, (gen, n) in enumerate(stages):
            want = -(-tick * n // total)
            while done[s] < want:
                next(gen, None)
                done[s] += 1
    for gen, _ in stages:
        for _ in gen:
            pass


_CUMMAX_LEVELS = LANES.bit_length() - 1
_GATE_PIECES = 3 + _CUMMAX_LEVELS
_QK_BLOCKS = 2 * D_MLSTM // LANES


def _conv_pieces(qk_ref, cw_ref, cb_ref, xpad_ref, qks_ref):
    T = MLSTM_ROWS
    L = MLSTM_CHUNK
    for k in range(_QK_BLOCKS):
        xpad_ref[k, SUBLANES:SUBLANES + T, :] = qk_ref[:, k * LANES:(k + 1) * LANES]
    yield
    for c in range(T // L):
        for k in range(_QK_BLOCKS):
            cols = slice(k * LANES, (k + 1) * LANES)
            for parity in range(2):
                acc = cb_ref[:, cols]
                for j in range(CONV_WIDTH):
                    start = c * L + SUBLANES - (CONV_WIDTH - 1) + parity + j
                    acc = acc + (xpad_ref[k, pl.ds(start, L // 2, stride=2), :]
                                 * cw_ref[j:j + 1, cols])
                act = acc * jax.nn.sigmoid(acc)
                if k < _QK_BLOCKS // 2:
                    act = act * (HEAD_DIM ** -0.5)
                qks_ref[k, pl.ds(c * L + parity, L // 2, stride=2), :] = act
            yield
    for k in range(_QK_BLOCKS):
        xpad_ref[k, 0:SUBLANES, :] = xpad_ref[k, T:T + SUBLANES, :]


def _gate_pieces(gr_ref, br_ref, m_ref, out):
    T = MLSTM_ROWS
    L = MLSTM_CHUNK
    n_chunks = T // L
    lane = lax.broadcasted_iota(jnp.int32, (SUBLANES, L), 1)
    row_id = lax.broadcasted_iota(jnp.int32, (L, L), 0)
    col_id = lax.broadcasted_iota(jnp.int32, (L, L), 1)
    triu = (row_id <= col_id).astype(BF16)
    g = gr_ref[...] + br_ref[...]
    logf = _log_sigmoid(g)
    yield
    lf = jnp.concatenate([pltpu.roll(logf[:, c * L:(c + 1) * L], N_HEADS, axis=0)
                          for c in range(n_chunks)], axis=0)
    hi = lf.astype(BF16)
    rest = lf - hi.astype(F32)
    mid = rest.astype(BF16)
    lo = (rest - mid.astype(F32)).astype(BF16)
    terms = jnp.dot(jnp.concatenate([hi, mid, lo], axis=0), triu,
                    preferred_element_type=F32)
    n_rows = n_chunks * SUBLANES
    b_all = terms[0:n_rows] + terms[n_rows:2 * n_rows] + terms[2 * n_rows:3 * n_rows]
    bs = [b_all[c * SUBLANES:(c + 1) * SUBLANES] for c in range(n_chunks)]
    rs = [g[:, c * L:(c + 1) * L] - bs[c] for c in range(n_chunks)]
    yield
    m_runs = list(rs)
    for level in range(_CUMMAX_LEVELS):
        d = 1 << level
        m_runs = [jnp.where(lane >= d, jnp.maximum(x, pltpu.roll(x, d, axis=1)), x)
                  for x in m_runs]
        yield
    m = m_ref[...]
    rows = []
    for c in range(n_chunks):
        b, r, m_run = bs[c], rs[c], m_runs[c]
        m_last = jnp.broadcast_to(m_run[:, L - 1:L], (SUBLANES, L))
        b_last = jnp.broadcast_to(b[:, L - 1:L], (SUBLANES, L))
        mu = jnp.maximum(m_run, m)
        mx = jnp.maximum(m, m_last)
        rows.append(dict(
            mu=mu,
            inter_scale=jnp.exp(m - mu),
            eb=jnp.exp(-b - mu),
            w=jnp.exp(r - m_last),
            s_old=jnp.exp(m - mx), s_new=jnp.exp(m_last - mx)))
        m = b_last + mx
    m_ref[...] = m
    r_pad = [jnp.zeros((SUBLANES, L), F32)] * (LANES // SUBLANES - n_chunks)
    out["rows"] = rows
    out["r_cols"] = jnp.concatenate(rs + r_pad, axis=0).T


def _mlstm_kernel(qk_ref, vt_ref, ot_ref, gr_ref, cw_ref, cb_ref, br_ref, nw_ref,
                  out_ref, xpad_ref, qks_ref, c_ref, m_ref):
    T = MLSTM_ROWS
    L = MLSTM_CHUNK
    D = HEAD_DIM
    n_chunks = T // L

    @pl.when(pl.program_id(0) == 0)
    def _():
        xpad_ref[:, 0:SUBLANES, :] = jnp.zeros((_QK_BLOCKS, SUBLANES, LANES), F32)
        c_ref[...] = jnp.zeros(c_ref.shape, F32)
        m_ref[...] = jnp.zeros(m_ref.shape, F32)

    gates = {}
    _interleave((_gate_pieces(gr_ref, br_ref, m_ref, gates), _GATE_PIECES),
                (_conv_pieces(qk_ref, cw_ref, cb_ref, xpad_ref, qks_ref),
                 1 + n_chunks * _QK_BLOCKS))
    rows, r_cols = gates["rows"], gates["r_cols"]

    row_id = lax.broadcasted_iota(jnp.int32, (L, L), 0)
    col_id = lax.broadcasted_iota(jnp.int32, (L, L), 1)
    causal = row_id <= col_id
    neg_inf = jnp.float32(-jnp.inf)
    for c in range(n_chunks):
        for h in range(N_HEADS):
            row = lambda name: rows[c][name][h:h + 1, :]
            q = qks_ref[h, c * L:(c + 1) * L, :].astype(BF16)
            k = qks_ref[N_HEADS + h, c * L:(c + 1) * L, :].astype(BF16)
            vt = vt_ref[h * D:(h + 1) * D, c * L:(c + 1) * L]
            r_col = r_cols[:, c * SUBLANES + h:c * SUBLANES + h + 1]

            dmat = jnp.exp(jnp.where(causal, r_col - row("mu"), neg_inf))
            s_kq = lax.dot_general(k, q, _NT_DIMS, preferred_element_type=F32) * dmat
            cq = lax.dot_general(c_ref[h].astype(BF16), q, _NT_DIMS,
                                 preferred_element_type=F32)
            isc = row("inter_scale")
            num = isc * cq[0:D, :] + jnp.dot(vt.astype(BF16), s_kq.astype(BF16),
                                             preferred_element_type=F32)
            den = isc * cq[D:D + 1, :] + jnp.sum(s_kq, axis=0, keepdims=True)
            hh = num * (1.0 / jnp.maximum(jnp.abs(den), row("eb")))

            w = row("w")
            vw = jnp.concatenate([vt * w, jnp.broadcast_to(w, (SUBLANES, L))],
                                 axis=0).astype(BF16)
            c_loc = jnp.dot(vw, k, preferred_element_type=F32)
            c_ref[h] = row("s_old") * c_ref[h] + row("s_new") * c_loc

            mean = jnp.mean(hh, axis=0, keepdims=True)
            cen = hh - mean
            var = jnp.mean(cen * cen, axis=0, keepdims=True)
            hn = cen * lax.rsqrt(var + EPS) * nw_ref[h * D:(h + 1) * D, :]
            gate = jax.nn.sigmoid(ot_ref[h * D:(h + 1) * D, c * L:(c + 1) * L])
            out_ref[c * L:(c + 1) * L, h * D:(h + 1) * D] = (hn * gate).T.astype(BF16)


def _mlstm_call(qk, vt, ot, gr, cw, cb, bias_r, nw_cols):
    S = qk.shape[0]
    T = MLSTM_ROWS
    return pl.pallas_call(
        _mlstm_kernel,
        grid=(S // T,),
        in_specs=[_row_spec(T, 1024), _lane_spec(D_MLSTM, T), _lane_spec(D_MLSTM, T),
                  _lane_spec(SUBLANES, T),
                  _const_spec(cw.shape), _const_spec(cb.shape),
                  _const_spec(bias_r.shape), _const_spec(nw_cols.shape)],
        out_specs=_row_spec(T, D_MLSTM),
        out_shape=jax.ShapeDtypeStruct((S, D_MLSTM), BF16),
        scratch_shapes=[pltpu.VMEM((_QK_BLOCKS, T + SUBLANES, LANES), F32),
                        pltpu.VMEM((_QK_BLOCKS, T, LANES), F32),
                        pltpu.VMEM((N_HEADS, HEAD_DIM + SUBLANES, HEAD_DIM), F32),
                        pltpu.VMEM((SUBLANES, LANES), F32)],
        compiler_params=_params(),
        name="mlstm",
    )(qk, vt, ot, gr, cw, cb, bias_r, nw_cols)


def _gelu_tanh(x):
    c = math.sqrt(2.0 / math.pi)
    return x * (0.5 * (1.0 + jnp.tanh(c * (x + 0.044715 * (x * x * x)))))


def _s5_kernel(u_ref, a2_ref, wb_ref, wc_ref, wd_ref, d_ref, gw_ref, gb_ref, out_ref,
               x_ref, y_ref, *z_refs):
    T = S5_ROWS

    @pl.when(pl.program_id(0) == 0)
    def _():
        x_ref[...] = jnp.zeros(x_ref.shape, F32)

    R = S5_SUB
    P = R // 2
    n_sub = T // R
    half = STATE_BLOCKS // 2
    n_blocks = D_SSM // LANES
    a_re = [a2_ref[:, c * LANES:(c + 1) * LANES] for c in range(half)]
    a_im = [a2_ref[:, (half + c) * LANES:(half + c + 1) * LANES] for c in range(half)]

    def pair_rows(sb, b):
        even = u_ref[b, pl.ds(sb * R, P, stride=2), :]
        odd = u_ref[b, pl.ds(sb * R + 1, P, stride=2), :]
        return jnp.concatenate([even, odd], axis=1).astype(BF16)

    def expand_piece(sb, j):
        w = jnp.dot(pair_rows(sb, j // 2), wb_ref[j], preferred_element_type=F32)
        for c in range(STATE_BLOCKS):
            z_refs[sb][c // 2, pl.ds(j * S5_PITCH + c % 2, P, stride=2), :] = (
                w[:, c * LANES:(c + 1) * LANES])

    def scan_piece(sb, x, i0, i1):
        z_ref = z_refs[sb]
        for i in range(i0, i1):
            tile = lambda c: (c // 2, pl.ds(2 * i + c % 2, SUBLANES, stride=S5_PITCH),
                              slice(None))
            new = list(x)
            for c in range(half):
                x_re, x_im = x[c], x[half + c]
                new[c] = a_re[c] * x_re - a_im[c] * x_im + z_ref[tile(c)]
                new[half + c] = a_re[c] * x_im + a_im[c] * x_re + z_ref[tile(half + c)]
                z_ref[tile(c)] = x_re
                z_ref[tile(half + c)] = x_im
            x = new
        return x

    def contract_piece(sb, j):
        xs = jnp.concatenate(
            [z_refs[sb][c // 2, pl.ds(j * S5_PITCH + c % 2, P, stride=2), :]
             for c in range(STATE_BLOCKS)], axis=1).astype(BF16)
        return jnp.dot(xs, wc_ref[j], preferred_element_type=F32)

    def finish(sb, parts):
        for b in range(n_blocks):
            yb = (parts[2 * b] + parts[2 * b + 1]
                  + jnp.dot(pair_rows(sb, b), wd_ref[b], preferred_element_type=F32))
            y_ref[b, pl.ds(sb * R, P, stride=2), :] = yb[:, 0:LANES]
            y_ref[b, pl.ds(sb * R + 1, P, stride=2), :] = yb[:, LANES:2 * LANES]
        rows = slice(sb * R, (sb + 1) * R)
        y = jnp.concatenate([y_ref[b, rows, :] for b in range(n_blocks)], axis=1)
        u = jnp.concatenate([u_ref[b, rows, :] for b in range(n_blocks)], axis=1)
        z = _gelu_tanh(y + d_ref[...] * u)
        gate = jax.nn.sigmoid(jnp.dot(z.astype(BF16), gw_ref[...],
                                      preferred_element_type=F32) + gb_ref[...])
        out_ref[rows, :] = (z * gate).astype(BF16)

    x = [x_ref[:, c * LANES:(c + 1) * LANES] for c in range(STATE_BLOCKS)]
    steps = P // SUBLANES
    for j in range(SUBLANES):
        expand_piece(0, j)
    for sb in range(n_sub + 1):
        parts = []
        for j in range(SUBLANES):
            if sb < n_sub:
                x = scan_piece(sb, x, j * steps, (j + 1) * steps)
            if sb + 1 < n_sub:
                expand_piece(sb + 1, j)
            if sb >= 1:
                parts.append(contract_piece(sb - 1, j))
        if sb >= 1:
            finish(sb - 1, parts)
    for c in range(STATE_BLOCKS):
        x_ref[:, c * LANES:(c + 1) * LANES] = x[c]


def _s5_call(u, a2_tile, wb, wc, wd, d, gw, gb):
    S = u.shape[1]
    T = S5_ROWS
    n_blocks = D_SSM // LANES
    return pl.pallas_call(
        _s5_kernel,
        grid=(S // T,),
        in_specs=[pl.BlockSpec((n_blocks, T, LANES), lambda i: (0, i, 0)),
                  _const_spec(a2_tile.shape), _const_spec(wb.shape), _const_spec(wc.shape),
                  _const_spec(wd.shape), _const_spec(d.shape), _const_spec(gw.shape),
                  _const_spec(gb.shape)],
        out_specs=_row_spec(T, D_SSM),
        out_shape=jax.ShapeDtypeStruct((S, D_SSM), BF16),
        scratch_shapes=([pltpu.VMEM((SUBLANES, STATE_LANES), F32),
                         pltpu.VMEM((n_blocks, T, LANES), F32)]
                        + [pltpu.VMEM((STATE_BLOCKS // 2, SUBLANES * S5_PITCH, LANES), F32)]
                        * (T // S5_SUB)),
        compiler_params=_params(),
        name="s5",
    )(u, a2_tile, wb, wc, wd, d, gw, gb)


def _s5_weights(lam_re, lam_im, log_dt, b_re, b_im, c_re, c_im):
    cmul = lambda xr, xi, yr, yi: (xr * yr - xi * yi, xr * yi + xi * yr)
    lr, li = lam_re.astype(F32), lam_im.astype(F32)
    dt = jnp.exp(log_dt.astype(F32))[:, None]
    mag = jnp.exp(lr * dt)
    a_re, a_im = mag * jnp.cos(li * dt), mag * jnp.sin(li * dt)
    a2_re, a2_im = cmul(a_re, a_im, a_re, a_im)
    den = lr * lr + li * li
    coef_re = ((a_re - 1.0) * lr + a_im * li) / den
    coef_im = (a_im * lr - (a_re - 1.0) * li) / den
    bb_re, bb_im = cmul(coef_re[..., None], coef_im[..., None],
                        b_re.astype(F32), b_im.astype(F32))
    abb_re, abb_im = cmul(a_re[..., None], a_im[..., None], bb_re, bb_im)
    cr, ci = c_re.astype(F32), c_im.astype(F32)
    ca_re, ca_im = cmul(cr, ci, a_re[:, None, :], a_im[:, None, :])
    ca2_re, ca2_im = cmul(cr, ci, a2_re[:, None, :], a2_im[:, None, :])

    J, Q, P, Hc = SUBLANES, GROUPS_PER_SUBLANE, SSM_STATE, SSM_GROUP
    a2_tile = jnp.concatenate([a2_re.reshape(J, Q * P), a2_im.reshape(J, Q * P)], axis=1)
    eye = jnp.eye(Q, dtype=F32)
    half = (jnp.arange(J) % 2)[:, None, None, None] == jnp.arange(2)[None, :, None, None]

    def in_weights(w_re, w_im):
        def blocks(w):
            return jnp.einsum('jqpc,qr->jqcrp', w.reshape(J, Q, P, Hc),
                              eye).reshape(J, Q * Hc, Q * P)
        w = jnp.concatenate([blocks(w_re), blocks(w_im)], axis=2)
        return jnp.where(half, w[:, None], 0.0).reshape(J, 2 * Q * Hc, STATE_LANES)

    def out_weights(w_re, w_im):
        def blocks(w):
            return jnp.einsum('jqcp,qr->jqprc', w.reshape(J, Q, Hc, P),
                              eye).reshape(J, Q * P, Q * Hc)
        w = jnp.concatenate([blocks(w_re), -blocks(w_im)], axis=1)
        half_c = jnp.swapaxes(half, 1, 2)
        return jnp.where(half_c, w[:, :, None, :], 0.0).reshape(J, STATE_LANES, 2 * Q * Hc)

    wb = jnp.concatenate([in_weights(abb_re, abb_im), in_weights(bb_re, bb_im)], axis=1)
    wc = jnp.concatenate([out_weights(ca_re, ca_im), out_weights(ca2_re, ca2_im)], axis=2)
    k0 = jnp.einsum('gop,gpc->goc', cr, bb_re) - jnp.einsum('gop,gpc->goc', ci, bb_im)
    k1 = jnp.einsum('gop,gpc->goc', cr, abb_re) - jnp.einsum('gop,gpc->goc', ci, abb_im)
    n_blocks, per = D_SSM // LANES, LANES // Hc
    eye8 = jnp.eye(per, dtype=F32)

    def lane_block(k):
        return jnp.einsum('bgoc,gh->bgcho', k.reshape(n_blocks, per, Hc, Hc),
                          eye8).reshape(n_blocks, LANES, LANES)

    zero = jnp.zeros((n_blocks, LANES, LANES), F32)
    wd = jnp.concatenate([jnp.concatenate([lane_block(k0), lane_block(k1)], axis=2),
                          jnp.concatenate([zero, lane_block(k0)], axis=2)], axis=1)
    return a2_tile, wb.astype(BF16), wc.astype(BF16), wd.astype(BF16)


def _ffn_kernel(x_ref, hm_ref, y_ref, wo_ref, g2_ref, w1_ref, w2_ref, g3_ref, out_ref):
    mixed = (jnp.dot(hm_ref[...], wo_ref[0:D_MLSTM, :], preferred_element_type=F32)
             + jnp.dot(y_ref[...], wo_ref[D_MLSTM:D_MODEL, :], preferred_element_type=F32))
    x1 = x_ref[...] + mixed
    h2 = _rms(x1, g2_ref[...]).astype(BF16)
    mlp = None
    for c in range(D_FF // FF_CHUNK):
        a = jnp.dot(h2, w1_ref[:, c * FF_CHUNK:(c + 1) * FF_CHUNK],
                    preferred_element_type=F32)
        a = jnp.maximum(a, 0.0)
        part = jnp.dot((a * a).astype(BF16), w2_ref[c * FF_CHUNK:(c + 1) * FF_CHUNK, :],
                       preferred_element_type=F32)
        mlp = part if mlp is None else mlp + part
    out_ref[...] = _rms(x1 + mlp, g3_ref[...])


def _ffn_call(x, hm, y, wo, g2, w1, w2, g3):
    S = x.shape[0]
    T = FFN_ROWS
    single = dict(pipeline_mode=pl.Buffered(WEIGHT_BUFFERS))
    return pl.pallas_call(
        _ffn_kernel,
        grid=(S // T,),
        in_specs=[_row_spec(T, D_MODEL), _row_spec(T, D_MLSTM), _row_spec(T, D_SSM),
                  pl.BlockSpec(wo.shape, lambda i: (0, 0), **single),
                  _const_spec(g2.shape),
                  pl.BlockSpec(w1.shape, lambda i: (0, 0), **single),
                  pl.BlockSpec(w2.shape, lambda i: (0, 0), **single),
                  _const_spec(g3.shape)],
        out_specs=_row_spec(T, D_MODEL),
        out_shape=jax.ShapeDtypeStruct((S, D_MODEL), F32),
        compiler_params=_params(),
        name="ffn",
    )(x, hm, y, wo, g2, w1, w2, g3)


def _layer(x, mix_norm_w, w_in, conv_w, conv_b, i_bias, f_bias, mlstm_norm_w,
           lam_re, lam_im, log_dt, b_re, b_im, c_re, c_im, ssm_d,
           glu_w, glu_b, w_out, mlp_norm_w, w_ff1, w_ff2, out_norm_w):
    n_qk, n_vo, n_gate = 2 * D_MLSTM, 2 * D_MLSTM, 2 * N_HEADS
    w_rows = jnp.concatenate([w_in[:, :n_qk], w_in[:, n_qk + n_vo + n_gate:]],
                             axis=1).astype(BF16)
    w_cols = w_in[:, n_qk:n_qk + n_vo].T.astype(BF16)
    wg = w_in[:, n_qk + n_vo:n_qk + n_vo + n_gate].T.astype(BF16)
    qk, u, vt, ot, gr = _proj_call(x, mix_norm_w[None, :], w_rows, w_cols, wg)

    bias_r = jnp.concatenate([i_bias, f_bias]).astype(F32)[:, None]
    nw_cols = jnp.broadcast_to(mlstm_norm_w.astype(F32)[:, None], (D_MLSTM, LANES))
    hm = _mlstm_call(qk, vt, ot, gr, conv_w, conv_b[None, :], bias_r, nw_cols)

    a2_tile, wb, wc, wd = _s5_weights(lam_re, lam_im, log_dt, b_re, b_im, c_re, c_im)
    y = _s5_call(u, a2_tile, wb, wc, wd, ssm_d[None, :], glu_w.astype(BF16),
                 glu_b[None, :])

    return _ffn_call(x, hm, y, w_out.astype(BF16), mlp_norm_w[None, :],
                     w_ff1.astype(BF16), w_ff2.astype(BF16), out_norm_w[None, :])


def kernel(x, mix_norm_w, w_in, conv_w, conv_b, i_bias, f_bias, mlstm_norm_w, ssm_lam_re, ssm_lam_im, ssm_log_dt, ssm_b_re, ssm_b_im, ssm_c_re, ssm_c_im, ssm_d, glu_w, glu_b, w_out, mlp_norm_w, w_ff1, w_ff2, final_norm_w):
    assert x.shape[0] == 1 and mix_norm_w.shape[0] == 1
    xs = x[0]
    out = _layer(xs, mix_norm_w[0], w_in[0], conv_w[0], conv_b[0], i_bias[0], f_bias[0],
                 mlstm_norm_w[0], ssm_lam_re[0], ssm_lam_im[0], ssm_log_dt[0],
                 ssm_b_re[0], ssm_b_im[0], ssm_c_re[0], ssm_c_im[0], ssm_d[0],
                 glu_w[0], glu_b[0], w_out[0], mlp_norm_w[0], w_ff1[0], w_ff2[0],
                 final_norm_w)
    return out[None]
```

```python
import math

import jax
import jax.numpy as jnp
from jax import lax
from jax.experimental import pallas as pl
from jax.experimental.pallas import tpu as pltpu

F32 = jnp.float32
BF16 = jnp.bfloat16

D_MODEL = 1024
D_MLSTM = 512
N_HEADS = 4
HEAD_DIM = 128
CONV_WIDTH = 4
D_SSM = 512
SSM_GROUP = 16
N_GROUPS = 32
SSM_STATE = 64
D_FF = 4096
EPS = 1e-6

SUBLANES = 8
LANES = 128

PROJ_ROWS = 1024
MLSTM_ROWS = 1024
MLSTM_CHUNK = 128
S5_ROWS = 1024
S5_SUB = 512
S5_PITCH = S5_SUB + 4
FFN_ROWS = 1024
FF_CHUNK = 1024
WEIGHT_BUFFERS = 1
VMEM_LIMIT = 56 * 1024 * 1024

GROUPS_PER_SUBLANE = N_GROUPS // SUBLANES
STATE_HALF = GROUPS_PER_SUBLANE * SSM_STATE
STATE_LANES = 2 * STATE_HALF
STATE_BLOCKS = STATE_LANES // LANES


def _const_spec(shape):
    return pl.BlockSpec(shape, lambda i: (0,) * len(shape))


def _row_spec(rows, cols):
    return pl.BlockSpec((rows, cols), lambda i: (i, 0))


def _lane_spec(rows, cols):
    return pl.BlockSpec((rows, cols), lambda i: (0, i))


def _params(flags=None):
    return pltpu.CompilerParams(dimension_semantics=("arbitrary",),
                                vmem_limit_bytes=VMEM_LIMIT, flags=flags)


def _rms(x, g):
    r = lax.rsqrt(jnp.mean(x * x, axis=-1, keepdims=True) + EPS)
    return x * r * g


_NT_DIMS = (((1,), (1,)), ((), ()))


def _proj_kernel(x_ref, g_ref, w_ref, wt_ref, wg_ref, wo_ref, w1_ref, w2_ref,
                 qk_ref, u_ref, vt_ref, ot_ref, gr_ref, wo_bf_ref, w1_bf_ref, w2_bf_ref):
    wo_bf_ref[...] = wo_ref[...].astype(BF16)
    w1_bf_ref[...] = w1_ref[...].astype(BF16)
    w2_bf_ref[...] = w2_ref[...].astype(BF16)

    h = _rms(x_ref[...], g_ref[...]).astype(BF16)
    qk_ref[...] = jnp.dot(h, w_ref[:, 0:1024], preferred_element_type=F32)
    u = jnp.dot(h, w_ref[:, 1024:1536], preferred_element_type=F32)
    for b in range(D_SSM // LANES):
        u_ref[b] = u[:, b * LANES:(b + 1) * LANES]
    vt_ref[...] = lax.dot_general(wt_ref[0:D_MLSTM, :], h, _NT_DIMS,
                                  preferred_element_type=F32)
    ot_ref[...] = lax.dot_general(wt_ref[D_MLSTM:2 * D_MLSTM, :], h, _NT_DIMS,
                                  preferred_element_type=F32)
    gr_ref[...] = lax.dot_general(wg_ref[...], h, _NT_DIMS, preferred_element_type=F32)


def _proj_call(x, g, w, wt, wg, wo, w1, w2):
    S = x.shape[0]
    T = PROJ_ROWS
    n = S // T
    wo_spec = _row_spec(wo.shape[0] // n, wo.shape[1])
    w1_spec = pl.BlockSpec((w1.shape[0], w1.shape[1] // n), lambda i: (0, i))
    w2_spec = _row_spec(w2.shape[0] // n, w2.shape[1])
    return pl.pallas_call(
        _proj_kernel,
        grid=(n,),
        in_specs=[_row_spec(T, D_MODEL), _const_spec((1, D_MODEL)),
                  _const_spec(w.shape), _const_spec(wt.shape), _const_spec(wg.shape),
                  wo_spec, w1_spec, w2_spec],
        out_specs=[_row_spec(T, 1024),
                   pl.BlockSpec((D_SSM // LANES, T, LANES), lambda i: (0, i, 0)),
                   _lane_spec(D_MLSTM, T), _lane_spec(D_MLSTM, T), _lane_spec(SUBLANES, T),
                   wo_spec, w1_spec, w2_spec],
        out_shape=[jax.ShapeDtypeStruct((S, 1024), F32),
                   jax.ShapeDtypeStruct((D_SSM // LANES, S, LANES), F32),
                   jax.ShapeDtypeStruct((D_MLSTM, S), F32),
                   jax.ShapeDtypeStruct((D_MLSTM, S), F32),
                   jax.ShapeDtypeStruct((SUBLANES, S), F32),
                   jax.ShapeDtypeStruct(wo.shape, BF16),
                   jax.ShapeDtypeStruct(w1.shape, BF16),
                   jax.ShapeDtypeStruct(w2.shape, BF16)],
        compiler_params=_params(),
        name="proj",
    )(x, g, w, wt, wg, wo, w1, w2)


def _log_sigmoid(x):
    return jnp.minimum(x, 0.0) - jnp.log1p(jnp.exp(-jnp.abs(x)))


def _interleave(*stages):
    total = max(n for _, n in stages)
    done = [0] * len(stages)
    for tick in range(1, total + 1):
        for s, (gen, n) in enumerate(stages):
            want = -(-tick * n // total)
            while done[s] < want:
                next(gen, None)
                done[s] += 1
    for gen, _ in stages:
        for _ in gen:
            pass


_LOG2E = math.log2(math.e)
_CUMMAX_LEVELS = LANES.bit_length() - 1
_GATE_PIECES = 3 + _CUMMAX_LEVELS
_QK_BLOCKS = 2 * D_MLSTM // LANES


def _conv_pieces(qk_ref, cw_ref, cb_ref, xpad_ref, qks_ref):
    T = MLSTM_ROWS
    L = MLSTM_CHUNK
    for k in range(_QK_BLOCKS):
        xpad_ref[k, SUBLANES:SUBLANES + T, :] = qk_ref[:, k * LANES:(k + 1) * LANES]
    yield
    for c in range(T // L):
        for k in range(_QK_BLOCKS):
            cols = slice(k * LANES, (k + 1) * LANES)
            for parity in range(2):
                acc = cb_ref[:, cols]
                for j in range(CONV_WIDTH):
                    start = c * L + SUBLANES - (CONV_WIDTH - 1) + parity + j
                    acc = acc + (xpad_ref[k, pl.ds(start, L // 2, stride=2), :]
                                 * cw_ref[j:j + 1, cols])
                act = acc * jax.nn.sigmoid(acc)
                if k < _QK_BLOCKS // 2:
                    act = act * (HEAD_DIM ** -0.5)
                qks_ref[k, pl.ds(c * L + parity, L // 2, stride=2), :] = act
            yield
    for k in range(_QK_BLOCKS):
        xpad_ref[k, 0:SUBLANES, :] = xpad_ref[k, T:T + SUBLANES, :]


def _gate_pieces(gr_ref, br_ref, m_ref, out):
    T = MLSTM_ROWS
    L = MLSTM_CHUNK
    n_chunks = T // L
    lane = lax.broadcasted_iota(jnp.int32, (SUBLANES, L), 1)
    row_id = lax.broadcasted_iota(jnp.int32, (L, L), 0)
    col_id = lax.broadcasted_iota(jnp.int32, (L, L), 1)
    triu = (row_id <= col_id).astype(BF16)
    g = gr_ref[...] + br_ref[...]
    logf = _log_sigmoid(g)
    yield
    lf = jnp.concatenate([pltpu.roll(logf[:, c * L:(c + 1) * L], N_HEADS, axis=0)
                          for c in range(n_chunks)], axis=0)
    hi = lf.astype(BF16)
    rest = lf - hi.astype(F32)
    mid = rest.astype(BF16)
    lo = (rest - mid.astype(F32)).astype(BF16)
    terms = jnp.dot(jnp.concatenate([hi, mid, lo], axis=0), triu,
                    preferred_element_type=F32)
    n_rows = n_chunks * SUBLANES
    b_all = terms[0:n_rows] + terms[n_rows:2 * n_rows] + terms[2 * n_rows:3 * n_rows]
    bs = [b_all[c * SUBLANES:(c + 1) * SUBLANES] for c in range(n_chunks)]
    rs = [g[:, c * L:(c + 1) * L] - bs[c] for c in range(n_chunks)]
    yield
    m_runs = list(rs)
    for level in range(_CUMMAX_LEVELS):
        d = 1 << level
        m_runs = [jnp.where(lane >= d, jnp.maximum(x, pltpu.roll(x, d, axis=1)), x)
                  for x in m_runs]
        yield
    m = m_ref[...]
    rows = []
    for c in range(n_chunks):
        b, r, m_run = bs[c], rs[c], m_runs[c]
        m_last = jnp.broadcast_to(m_run[:, L - 1:L], (SUBLANES, L))
        b_last = jnp.broadcast_to(b[:, L - 1:L], (SUBLANES, L))
        mu = jnp.maximum(m_run, m)
        mx = jnp.maximum(m, m_last)
        rows.append(dict(
            mu_log2=mu * _LOG2E,
            inter_scale=jnp.exp(m - mu),
            eb=jnp.exp(-b - mu),
            w_new=jnp.exp(r - mx),
            s_old=jnp.exp(m - mx)))
        m = b_last + mx
    m_ref[...] = m
    r_pad = [jnp.zeros((SUBLANES, L), F32)] * (LANES // SUBLANES - n_chunks)
    out["rows"] = rows
    out["r_cols"] = (jnp.concatenate(rs + r_pad, axis=0) * _LOG2E).T


def _mlstm_kernel(qk_ref, vt_ref, ot_ref, gr_ref, cw_ref, cb_ref, br_ref, nw_ref,
                  out_ref, xpad_ref, qks_ref, c_ref, m_ref):
    T = MLSTM_ROWS
    L = MLSTM_CHUNK
    D = HEAD_DIM
    n_chunks = T // L

    @pl.when(pl.program_id(0) == 0)
    def _():
        xpad_ref[:, 0:SUBLANES, :] = jnp.zeros((_QK_BLOCKS, SUBLANES, LANES), F32)
        c_ref[...] = jnp.zeros(c_ref.shape, F32)
        m_ref[...] = jnp.zeros(m_ref.shape, F32)

    gates = {}
    _interleave((_gate_pieces(gr_ref, br_ref, m_ref, gates), _GATE_PIECES),
                (_conv_pieces(qk_ref, cw_ref, cb_ref, xpad_ref, qks_ref),
                 1 + n_chunks * _QK_BLOCKS))
    rows, r_cols = gates["rows"], gates["r_cols"]

    row_id = lax.broadcasted_iota(jnp.int32, (L, L), 0)
    col_id = lax.broadcasted_iota(jnp.int32, (L, L), 1)
    causal = row_id <= col_id
    neg_inf = jnp.float32(-jnp.inf)
    for c in range(n_chunks):
        for h in range(N_HEADS):
            row = lambda name: rows[c][name][h:h + 1, :]
            q = qks_ref[h, c * L:(c + 1) * L, :].astype(BF16)
            k = qks_ref[N_HEADS + h, c * L:(c + 1) * L, :].astype(BF16)
            vt = vt_ref[h * D:(h + 1) * D, c * L:(c + 1) * L]
            r_col = r_cols[:, c * SUBLANES + h:c * SUBLANES + h + 1]

            dmat = jnp.exp2(jnp.where(causal, r_col - row("mu_log2"), neg_inf))
            s_kq = lax.dot_general(k, q, _NT_DIMS, preferred_element_type=F32) * dmat
            cq = lax.dot_general(c_ref[h].astype(BF16), q, _NT_DIMS,
                                 preferred_element_type=F32)
            isc = row("inter_scale")
            num = isc * cq[0:D, :] + jnp.dot(vt.astype(BF16), s_kq.astype(BF16),
                                             preferred_element_type=F32)
            den = isc * cq[D:D + 1, :] + jnp.sum(s_kq, axis=0, keepdims=True)
            hh = num * (1.0 / jnp.maximum(jnp.abs(den), row("eb")))

            w = row("w_new")
            vw = jnp.concatenate([vt * w, jnp.broadcast_to(w, (SUBLANES, L))],
                                 axis=0).astype(BF16)
            c_loc = jnp.dot(vw, k, preferred_element_type=F32)
            c_ref[h] = row("s_old") * c_ref[h] + c_loc

            mean = jnp.mean(hh, axis=0, keepdims=True)
            cen = hh - mean
            var = jnp.mean(cen * cen, axis=0, keepdims=True)
            hn = cen * lax.rsqrt(var + EPS) * nw_ref[h * D:(h + 1) * D, :]
            gate = jax.nn.sigmoid(ot_ref[h * D:(h + 1) * D, c * L:(c + 1) * L])
            out_ref[c * L:(c + 1) * L, h * D:(h + 1) * D] = (hn * gate).T.astype(BF16)


def _mlstm_call(qk, vt, ot, gr, cw, cb, bias_r, nw_cols):
    S = qk.shape[0]
    T = MLSTM_ROWS
    return pl.pallas_call(
        _mlstm_kernel,
        grid=(S // T,),
        in_specs=[_row_spec(T, 1024), _lane_spec(D_MLSTM, T), _lane_spec(D_MLSTM, T),
                  _lane_spec(SUBLANES, T),
                  _const_spec(cw.shape), _const_spec(cb.shape),
                  _const_spec(bias_r.shape), _const_spec(nw_cols.shape)],
        out_specs=_row_spec(T, D_MLSTM),
        out_shape=jax.ShapeDtypeStruct((S, D_MLSTM), BF16),
        scratch_shapes=[pltpu.VMEM((_QK_BLOCKS, T + SUBLANES, LANES), F32),
                        pltpu.VMEM((_QK_BLOCKS, T, LANES), F32),
                        pltpu.VMEM((N_HEADS, HEAD_DIM + SUBLANES, HEAD_DIM), F32),
                        pltpu.VMEM((SUBLANES, LANES), F32)],
        compiler_params=_params(),
        name="mlstm",
    )(qk, vt, ot, gr, cw, cb, bias_r, nw_cols)


def _gelu_tanh(x):
    c = math.sqrt(2.0 / math.pi)
    return x * (0.5 * (1.0 + jnp.tanh(c * (x + 0.044715 * (x * x * x)))))


def _s5_kernel(u_ref, a2_ref, wb_ref, wc_ref, wd_ref, d_ref, gw_ref, gb_ref, out_ref,
               x_ref, y_ref, *z_refs):
    T = S5_ROWS

    @pl.when(pl.program_id(0) == 0)
    def _():
        x_ref[...] = jnp.zeros(x_ref.shape, F32)

    R = S5_SUB
    P = R // 2
    n_sub = T // R
    half = STATE_BLOCKS // 2
    n_blocks = D_SSM // LANES
    a_re = [a2_ref[:, c * LANES:(c + 1) * LANES] for c in range(half)]
    a_im = [a2_ref[:, (half + c) * LANES:(half + c + 1) * LANES] for c in range(half)]

    def pair_rows(sb, b):
        even = u_ref[b, pl.ds(sb * R, P, stride=2), :]
        odd = u_ref[b, pl.ds(sb * R + 1, P, stride=2), :]
        return jnp.concatenate([even, odd], axis=1).astype(BF16)

    def expand_piece(sb, j):
        w = jnp.dot(pair_rows(sb, j // 2), wb_ref[j], preferred_element_type=F32)
        for c in range(STATE_BLOCKS):
            z_refs[sb][c // 2, pl.ds(j * S5_PITCH + c % 2, P, stride=2), :] = (
                w[:, c * LANES:(c + 1) * LANES])

    def scan_piece(sb, x, i0, i1):
        z_ref = z_refs[sb]
        for i in range(i0, i1):
            tile = lambda c: (c // 2, pl.ds(2 * i + c % 2, SUBLANES, stride=S5_PITCH),
                              slice(None))
            new = list(x)
            for c in range(half):
                x_re, x_im = x[c], x[half + c]
                new[c] = a_re[c] * x_re - a_im[c] * x_im + z_ref[tile(c)]
                new[half + c] = a_re[c] * x_im + a_im[c] * x_re + z_ref[tile(half + c)]
                z_ref[tile(c)] = x_re
                z_ref[tile(half + c)] = x_im
            x = new
        return x

    def contract_piece(sb, j):
        xs = jnp.concatenate(
            [z_refs[sb][c // 2, pl.ds(j * S5_PITCH + c % 2, P, stride=2), :]
             for c in range(STATE_BLOCKS)], axis=1).astype(BF16)
        return jnp.dot(xs, wc_ref[j], preferred_element_type=F32)

    def finish(sb, parts):
        for b in range(n_blocks):
            yb = (parts[2 * b] + parts[2 * b + 1]
                  + jnp.dot(pair_rows(sb, b), wd_ref[b], preferred_element_type=F32))
            y_ref[b, pl.ds(sb * R, P, stride=2), :] = yb[:, 0:LANES]
            y_ref[b, pl.ds(sb * R + 1, P, stride=2), :] = yb[:, LANES:2 * LANES]
        rows = slice(sb * R, (sb + 1) * R)
        y = jnp.concatenate([y_ref[b, rows, :] for b in range(n_blocks)], axis=1)
        u = jnp.concatenate([u_ref[b, rows, :] for b in range(n_blocks)], axis=1)
        z = _gelu_tanh(y + d_ref[...] * u)
        gate = jax.nn.sigmoid(jnp.dot(z.astype(BF16), gw_ref[...],
                                      preferred_element_type=F32) + gb_ref[...])
        out_ref[rows, :] = (z * gate).astype(BF16)

    x = [x_ref[:, c * LANES:(c + 1) * LANES] for c in range(STATE_BLOCKS)]
    steps = P // SUBLANES
    for j in range(SUBLANES):
        expand_piece(0, j)
    for sb in range(n_sub + 1):
        parts = []
        for j in range(SUBLANES):
            if sb < n_sub:
                x = scan_piece(sb, x, j * steps, (j + 1) * steps)
            if sb + 1 < n_sub:
                expand_piece(sb + 1, j)
            if sb >= 1:
                parts.append(contract_piece(sb - 1, j))
        if sb >= 1:
            finish(sb - 1, parts)
    for c in range(STATE_BLOCKS):
        x_ref[:, c * LANES:(c + 1) * LANES] = x[c]


def _s5_call(u, a2_tile, wb, wc, wd, d, gw, gb):
    S = u.shape[1]
    T = S5_ROWS
    n_blocks = D_SSM // LANES
    return pl.pallas_call(
        _s5_kernel,
        grid=(S // T,),
        in_specs=[pl.BlockSpec((n_blocks, T, LANES), lambda i: (0, i, 0)),
                  _const_spec(a2_tile.shape), _const_spec(wb.shape), _const_spec(wc.shape),
                  _const_spec(wd.shape), _const_spec(d.shape), _const_spec(gw.shape),
                  _const_spec(gb.shape)],
        out_specs=_row_spec(T, D_SSM),
        out_shape=jax.ShapeDtypeStruct((S, D_SSM), BF16),
        scratch_shapes=([pltpu.VMEM((SUBLANES, STATE_LANES), F32),
                         pltpu.VMEM((n_blocks, T, LANES), F32)]
                        + [pltpu.VMEM((STATE_BLOCKS // 2, SUBLANES * S5_PITCH, LANES), F32)]
                        * (T // S5_SUB)),
        compiler_params=_params(),
        name="s5",
    )(u, a2_tile, wb, wc, wd, d, gw, gb)


def _s5_weights(lam_re, lam_im, log_dt, b_re, b_im, c_re, c_im):
    cmul = lambda xr, xi, yr, yi: (xr * yr - xi * yi, xr * yi + xi * yr)
    lr, li = lam_re.astype(F32), lam_im.astype(F32)
    dt = jnp.exp(log_dt.astype(F32))[:, None]
    mag = jnp.exp(lr * dt)
    a_re, a_im = mag * jnp.cos(li * dt), mag * jnp.sin(li * dt)
    a2_re, a2_im = cmul(a_re, a_im, a_re, a_im)
    den = lr * lr + li * li
    coef_re = ((a_re - 1.0) * lr + a_im * li) / den
    coef_im = (a_im * lr - (a_re - 1.0) * li) / den
    bb_re, bb_im = cmul(coef_re[..., None], coef_im[..., None],
                        b_re.astype(F32), b_im.astype(F32))
    abb_re, abb_im = cmul(a_re[..., None], a_im[..., None], bb_re, bb_im)
    cr, ci = c_re.astype(F32), c_im.astype(F32)
    ca_re, ca_im = cmul(cr, ci, a_re[:, None, :], a_im[:, None, :])
    ca2_re, ca2_im = cmul(cr, ci, a2_re[:, None, :], a2_im[:, None, :])

    J, Q, P, Hc = SUBLANES, GROUPS_PER_SUBLANE, SSM_STATE, SSM_GROUP
    a2_tile = jnp.concatenate([a2_re.reshape(J, Q * P), a2_im.reshape(J, Q * P)], axis=1)
    eye = jnp.eye(Q, dtype=F32)
    half = (jnp.arange(J) % 2)[:, None, None, None] == jnp.arange(2)[None, :, None, None]

    def in_weights(w_re, w_im):
        def blocks(w):
            return jnp.einsum('jqpc,qr->jqcrp', w.reshape(J, Q, P, Hc),
                              eye).reshape(J, Q * Hc, Q * P)
        w = jnp.concatenate([blocks(w_re), blocks(w_im)], axis=2)
        return jnp.where(half, w[:, None], 0.0).reshape(J, 2 * Q * Hc, STATE_LANES)

    def out_weights(w_re, w_im):
        def blocks(w):
            return jnp.einsum('jqcp,qr->jqprc', w.reshape(J, Q, Hc, P),
                              eye).reshape(J, Q * P, Q * Hc)
        w = jnp.concatenate([blocks(w_re), -blocks(w_im)], axis=1)
        half_c = jnp.swapaxes(half, 1, 2)
        return jnp.where(half_c, w[:, :, None, :], 0.0).reshape(J, STATE_LANES, 2 * Q * Hc)

    wb = jnp.concatenate([in_weights(abb_re, abb_im), in_weights(bb_re, bb_im)], axis=1)
    wc = jnp.concatenate([out_weights(ca_re, ca_im), out_weights(ca2_re, ca2_im)], axis=2)
    k0 = jnp.einsum('gop,gpc->goc', cr, bb_re) - jnp.einsum('gop,gpc->goc', ci, bb_im)
    k1 = jnp.einsum('gop,gpc->goc', cr, abb_re) - jnp.einsum('gop,gpc->goc', ci, abb_im)
    n_blocks, per = D_SSM // LANES, LANES // Hc
    eye8 = jnp.eye(per, dtype=F32)

    def lane_block(k):
        return jnp.einsum('bgoc,gh->bgcho', k.reshape(n_blocks, per, Hc, Hc),
                          eye8).reshape(n_blocks, LANES, LANES)

    zero = jnp.zeros((n_blocks, LANES, LANES), F32)
    wd = jnp.concatenate([jnp.concatenate([lane_block(k0), lane_block(k1)], axis=2),
                          jnp.concatenate([zero, lane_block(k0)], axis=2)], axis=1)
    return a2_tile, wb.astype(BF16), wc.astype(BF16), wd.astype(BF16)


def _ffn_kernel(x_ref, hm_ref, y_ref, wo_ref, g2_ref, w1_ref, w2_ref, g3_ref, out_ref):
    mixed = (jnp.dot(hm_ref[...], wo_ref[0:D_MLSTM, :], preferred_element_type=F32)
             + jnp.dot(y_ref[...], wo_ref[D_MLSTM:D_MODEL, :], preferred_element_type=F32))
    x1 = x_ref[...] + mixed
    h2 = _rms(x1, g2_ref[...]).astype(BF16)
    mlp = None
    for c in range(D_FF // FF_CHUNK):
        a = jnp.dot(h2, w1_ref[:, c * FF_CHUNK:(c + 1) * FF_CHUNK],
                    preferred_element_type=F32)
        a = jnp.maximum(a, 0.0)
        part = jnp.dot((a * a).astype(BF16), w2_ref[c * FF_CHUNK:(c + 1) * FF_CHUNK, :],
                       preferred_element_type=F32)
        mlp = part if mlp is None else mlp + part
    out_ref[...] = _rms(x1 + mlp, g3_ref[...])


def _ffn_call(x, hm, y, wo, g2, w1, w2, g3):
    S = x.shape[0]
    T = FFN_ROWS
    single = dict(pipeline_mode=pl.Buffered(WEIGHT_BUFFERS))
    return pl.pallas_call(
        _ffn_kernel,
        grid=(S // T,),
        in_specs=[_row_spec(T, D_MODEL), _row_spec(T, D_MLSTM), _row_spec(T, D_SSM),
                  pl.BlockSpec(wo.shape, lambda i: (0, 0), **single),
                  _const_spec(g2.shape),
                  pl.BlockSpec(w1.shape, lambda i: (0, 0), **single),
                  pl.BlockSpec(w2.shape, lambda i: (0, 0), **single),
                  _const_spec(g3.shape)],
        out_specs=_row_spec(T, D_MODEL),
        out_shape=jax.ShapeDtypeStruct((S, D_MODEL), F32),
        compiler_params=_params(),
        name="ffn",
    )(x, hm, y, wo, g2, w1, w2, g3)


def _layer(x, mix_norm_w, w_in, conv_w, conv_b, i_bias, f_bias, mlstm_norm_w,
           lam_re, lam_im, log_dt, b_re, b_im, c_re, c_im, ssm_d,
           glu_w, glu_b, w_out, mlp_norm_w, w_ff1, w_ff2, out_norm_w):
    n_qk, n_vo, n_gate = 2 * D_MLSTM, 2 * D_MLSTM, 2 * N_HEADS
    w_rows = jnp.concatenate([w_in[:, :n_qk], w_in[:, n_qk + n_vo + n_gate:]],
                             axis=1).astype(BF16)
    w_cols = w_in[:, n_qk:n_qk + n_vo].T.astype(BF16)
    wg = w_in[:, n_qk + n_vo:n_qk + n_vo + n_gate].T.astype(BF16)
    qk, u, vt, ot, gr, wo_bf, w1_bf, w2_bf = _proj_call(
        x, mix_norm_w[None, :], w_rows, w_cols, wg, w_out, w_ff1, w_ff2)

    bias_r = jnp.concatenate([i_bias, f_bias]).astype(F32)[:, None]
    nw_cols = jnp.broadcast_to(mlstm_norm_w.astype(F32)[:, None], (D_MLSTM, LANES))
    hm = _mlstm_call(qk, vt, ot, gr, conv_w, conv_b[None, :], bias_r, nw_cols)

    a2_tile, wb, wc, wd = _s5_weights(lam_re, lam_im, log_dt, b_re, b_im, c_re, c_im)
    y = _s5_call(u, a2_tile, wb, wc, wd, ssm_d[None, :], glu_w.astype(BF16),
                 glu_b[None, :])

    return _ffn_call(x, hm, y, wo_bf, mlp_norm_w[None, :], w1_bf, w2_bf,
                     out_norm_w[None, :])


def kernel(x, mix_norm_w, w_in, conv_w, conv_b, i_bias, f_bias, mlstm_norm_w, ssm_lam_re, ssm_lam_im, ssm_log_dt, ssm_b_re, ssm_b_im, ssm_c_re, ssm_c_im, ssm_d, glu_w, glu_b, w_out, mlp_norm_w, w_ff1, w_ff2, final_norm_w):
    assert x.shape[0] == 1 and mix_norm_w.shape[0] == 1
    xs = x[0]
    out = _layer(xs, mix_norm_w[0], w_in[0], conv_w[0], conv_b[0], i_bias[0], f_bias[0],
                 mlstm_norm_w[0], ssm_lam_re[0], ssm_lam_im[0], ssm_log_dt[0],
                 ssm_b_re[0], ssm_b_im[0], ssm_c_re[0], ssm_c_im[0], ssm_d[0],
                 glu_w[0], glu_b[0], w_out[0], mlp_norm_w[0], w_ff1[0], w_ff2[0],
                 final_norm_w)
    return out[None]
```

```python
import math

import jax
import jax.numpy as jnp
from jax import lax
from jax.experimental import pallas as pl
from jax.experimental.pallas import tpu as pltpu

F32 = jnp.float32
BF16 = jnp.bfloat16

D_MODEL = 1024
D_MLSTM = 512
D_QK = 2 * D_MLSTM
N_HEADS = 4
HEAD_DIM = 128
CONV_WIDTH = 4
D_SSM = 512
SSM_GROUP = 16
N_GROUPS = 32
SSM_STATE = 64
D_FF = 4096
EPS = 1e-6

SUBLANES = 8
LANES = 128

PROJ_ROWS = 1024
MLSTM_ROWS = 1024
MLSTM_CHUNK = 128
S5_ROWS = 1024
S5_SUB = 512
S5_PITCH = S5_SUB + 4
FFN_ROWS = 1024
FF_CHUNK = 1024
WEIGHT_BUFFERS = 1
VMEM_LIMIT = 56 * 1024 * 1024

GROUPS_PER_SUBLANE = N_GROUPS // SUBLANES
STATE_HALF = GROUPS_PER_SUBLANE * SSM_STATE
STATE_LANES = 2 * STATE_HALF
STATE_BLOCKS = STATE_LANES // LANES


def _const_spec(shape):
    return pl.BlockSpec(shape, lambda i: (0,) * len(shape))


def _row_spec(rows, cols):
    return pl.BlockSpec((rows, cols), lambda i: (i, 0))


def _lane_spec(rows, cols):
    return pl.BlockSpec((rows, cols), lambda i: (0, i))


def _params():
    return pltpu.CompilerParams(dimension_semantics=("arbitrary",),
                                vmem_limit_bytes=VMEM_LIMIT)


def _rms(x, g):
    r = lax.rsqrt(jnp.mean(x * x, axis=-1, keepdims=True) + EPS)
    return x * r * g


_NT_DIMS = (((1,), (1,)), ((), ()))


def _proj_kernel(x_ref, g_ref, w_ref, wt_ref, wg_ref, wo_ref, w1_ref, w2_ref,
                 qk_ref, u_ref, vt_ref, ot_ref, gr_ref, wo_bf_ref, w1_bf_ref, w2_bf_ref):
    wo_bf_ref[...] = wo_ref[...].astype(BF16)
    w1_bf_ref[...] = w1_ref[...].astype(BF16)
    w2_bf_ref[...] = w2_ref[...].astype(BF16)

    h = _rms(x_ref[...], g_ref[...]).astype(BF16)
    qk_ref[...] = jnp.dot(h, w_ref[:, 0:D_QK], preferred_element_type=F32)
    u = jnp.dot(h, w_ref[:, D_QK:D_QK + D_SSM], preferred_element_type=F32)
    for b in range(D_SSM // LANES):
        u_ref[b] = u[:, b * LANES:(b + 1) * LANES]
    vt_ref[...] = lax.dot_general(wt_ref[0:D_MLSTM, :], h, _NT_DIMS,
                                  preferred_element_type=F32)
    ot_ref[...] = lax.dot_general(wt_ref[D_MLSTM:2 * D_MLSTM, :], h, _NT_DIMS,
                                  preferred_element_type=F32)
    gr_ref[...] = lax.dot_general(wg_ref[...], h, _NT_DIMS, preferred_element_type=F32)


def _proj_call(x, g, w, wt, wg, wo, w1, w2):
    S = x.shape[0]
    T = PROJ_ROWS
    n = S // T
    wo_spec = _row_spec(wo.shape[0] // n, wo.shape[1])
    w1_spec = pl.BlockSpec((w1.shape[0], w1.shape[1] // n), lambda i: (0, i))
    w2_spec = _row_spec(w2.shape[0] // n, w2.shape[1])
    return pl.pallas_call(
        _proj_kernel,
        grid=(n,),
        in_specs=[_row_spec(T, D_MODEL), _const_spec((1, D_MODEL)),
                  _const_spec(w.shape), _const_spec(wt.shape), _const_spec(wg.shape),
                  wo_spec, w1_spec, w2_spec],
        out_specs=[_row_spec(T, D_QK),
                   pl.BlockSpec((D_SSM // LANES, T, LANES), lambda i: (0, i, 0)),
                   _lane_spec(D_MLSTM, T), _lane_spec(D_MLSTM, T), _lane_spec(SUBLANES, T),
                   wo_spec, w1_spec, w2_spec],
        out_shape=[jax.ShapeDtypeStruct((S, D_QK), F32),
                   jax.ShapeDtypeStruct((D_SSM // LANES, S, LANES), F32),
                   jax.ShapeDtypeStruct((D_MLSTM, S), F32),
                   jax.ShapeDtypeStruct((D_MLSTM, S), F32),
                   jax.ShapeDtypeStruct((SUBLANES, S), F32),
                   jax.ShapeDtypeStruct(wo.shape, BF16),
                   jax.ShapeDtypeStruct(w1.shape, BF16),
                   jax.ShapeDtypeStruct(w2.shape, BF16)],
        compiler_params=_params(),
        name="proj",
    )(x, g, w, wt, wg, wo, w1, w2)


def _log_sigmoid(x):
    return jnp.minimum(x, 0.0) - jnp.log1p(jnp.exp(-jnp.abs(x)))


def _interleave(*stages):
    total = max(n for _, n in stages)
    done = [0] * len(stages)
    for tick in range(1, total + 1):
        for s, (gen, n) in enumerate(stages):
            want = -(-tick * n // total)
            while done[s] < want:
                next(gen, None)
                done[s] += 1
    for gen, _ in stages:
        for _ in gen:
            pass


_LOG2E = math.log2(math.e)
_CUMMAX_LEVELS = LANES.bit_length() - 1
_GATE_PIECES = 3 + _CUMMAX_LEVELS
_QK_BLOCKS = 2 * D_MLSTM // LANES


def _conv_pieces(qk_ref, cw_ref, cb_ref, xpad_ref, qks_ref):
    T = MLSTM_ROWS
    L = MLSTM_CHUNK
    for k in range(_QK_BLOCKS):
        xpad_ref[k, SUBLANES:SUBLANES + T, :] = qk_ref[:, k * LANES:(k + 1) * LANES]
    yield
    for c in range(T // L):
        for k in range(_QK_BLOCKS):
            cols = slice(k * LANES, (k + 1) * LANES)
            for parity in range(2):
                acc = cb_ref[:, cols]
                for j in range(CONV_WIDTH):
                    start = c * L + SUBLANES - (CONV_WIDTH - 1) + parity + j
                    acc = acc + (xpad_ref[k, pl.ds(start, L // 2, stride=2), :]
                                 * cw_ref[j:j + 1, cols])
                act = acc * jax.nn.sigmoid(acc)
                if k < _QK_BLOCKS // 2:
                    act = act * (HEAD_DIM ** -0.5)
                qks_ref[k, pl.ds(c * L + parity, L // 2, stride=2), :] = act
            yield
    for k in range(_QK_BLOCKS):
        xpad_ref[k, 0:SUBLANES, :] = xpad_ref[k, T:T + SUBLANES, :]


def _gate_pieces(gr_ref, br_ref, m_ref, out):
    T = MLSTM_ROWS
    L = MLSTM_CHUNK
    n_chunks = T // L
    lane = lax.broadcasted_iota(jnp.int32, (SUBLANES, L), 1)
    row_id = lax.broadcasted_iota(jnp.int32, (L, L), 0)
    col_id = lax.broadcasted_iota(jnp.int32, (L, L), 1)
    triu = (row_id <= col_id).astype(BF16)
    g = gr_ref[...] + br_ref[...]
    logf = _log_sigmoid(g)
    yield
    lf = jnp.concatenate([pltpu.roll(logf[:, c * L:(c + 1) * L], N_HEADS, axis=0)
                          for c in range(n_chunks)], axis=0)
    hi = lf.astype(BF16)
    rest = lf - hi.astype(F32)
    mid = rest.astype(BF16)
    lo = (rest - mid.astype(F32)).astype(BF16)
    terms = jnp.dot(jnp.concatenate([hi, mid, lo], axis=0), triu,
                    preferred_element_type=F32)
    n_rows = n_chunks * SUBLANES
    b_all = terms[0:n_rows] + terms[n_rows:2 * n_rows] + terms[2 * n_rows:3 * n_rows]
    bs = [b_all[c * SUBLANES:(c + 1) * SUBLANES] for c in range(n_chunks)]
    rs = [g[:, c * L:(c + 1) * L] - bs[c] for c in range(n_chunks)]
    yield
    m_runs = list(rs)
    for level in range(_CUMMAX_LEVELS):
        d = 1 << level
        m_runs = [jnp.where(lane >= d, jnp.maximum(x, pltpu.roll(x, d, axis=1)), x)
                  for x in m_runs]
        yield
    m = m_ref[...]
    rows = []
    for c in range(n_chunks):
        b, r, m_run = bs[c], rs[c], m_runs[c]
        m_last = jnp.broadcast_to(m_run[:, L - 1:L], (SUBLANES, L))
        b_last = jnp.broadcast_to(b[:, L - 1:L], (SUBLANES, L))
        mu = jnp.maximum(m_run, m)
        mx = jnp.maximum(m, m_last)
        rows.append(dict(
            mu_log2=mu * _LOG2E,
            inter_scale=jnp.exp(m - mu),
            eb=jnp.exp(-b - mu),
            w_new=jnp.exp(r - mx),
            s_old=jnp.exp(m - mx)))
        m = b_last + mx
    m_ref[...] = m
    r_pad = [jnp.zeros((SUBLANES, L), F32)] * (LANES // SUBLANES - n_chunks)
    out["rows"] = rows
    out["r_cols"] = (jnp.concatenate(rs + r_pad, axis=0) * _LOG2E).T


_GATE_ROWS = ("mu_log2", "inter_scale", "eb", "w_new", "s_old")


def _store_gates(gates, grow_ref, rcol_ref):
    for c, chunk in enumerate(gates["rows"]):
        for q, name in enumerate(_GATE_ROWS):
            grow_ref[c * len(_GATE_ROWS) + q] = chunk[name]
    rcol_ref[...] = gates["r_cols"]


def _head_pieces(qks_ref, vt_ref, ot_ref, nw_ref, grow_ref, rcol_ref, c_ref, out_ref):
    L = MLSTM_CHUNK
    D = HEAD_DIM
    row_id = lax.broadcasted_iota(jnp.int32, (L, L), 0)
    col_id = lax.broadcasted_iota(jnp.int32, (L, L), 1)
    causal = row_id <= col_id
    neg_inf = jnp.float32(-jnp.inf)
    for c in range(MLSTM_ROWS // L):
        for h in range(N_HEADS):
            row = lambda name: grow_ref[c * len(_GATE_ROWS) + _GATE_ROWS.index(name),
                                        h:h + 1, :]
            q = qks_ref[h, c * L:(c + 1) * L, :].astype(BF16)
            k = qks_ref[N_HEADS + h, c * L:(c + 1) * L, :].astype(BF16)
            vt = vt_ref[h * D:(h + 1) * D, c * L:(c + 1) * L]
            r_col = rcol_ref[:, c * SUBLANES + h:c * SUBLANES + h + 1]

            dmat = jnp.exp2(jnp.where(causal, r_col - row("mu_log2"), neg_inf))
            s_kq = lax.dot_general(k, q, _NT_DIMS, preferred_element_type=F32) * dmat
            cq = lax.dot_general(c_ref[h].astype(BF16), q, _NT_DIMS,
                                 preferred_element_type=F32)
            isc = row("inter_scale")
            num = isc * cq[0:D, :] + jnp.dot(vt.astype(BF16), s_kq.astype(BF16),
                                             preferred_element_type=F32)
            den = isc * cq[D:D + 1, :] + jnp.sum(s_kq, axis=0, keepdims=True)
            hh = num * (1.0 / jnp.maximum(jnp.abs(den), row("eb")))

            w = row("w_new")
            vw = jnp.concatenate([vt * w, jnp.broadcast_to(w, (SUBLANES, L))],
                                 axis=0).astype(BF16)
            c_loc = jnp.dot(vw, k, preferred_element_type=F32)
            c_ref[h] = row("s_old") * c_ref[h] + c_loc

            mean = jnp.mean(hh, axis=0, keepdims=True)
            cen = hh - mean
            var = jnp.mean(cen * cen, axis=0, keepdims=True)
            hn = cen * lax.rsqrt(var + EPS) * nw_ref[h * D:(h + 1) * D, :]
            gate = jax.nn.sigmoid(ot_ref[h * D:(h + 1) * D, c * L:(c + 1) * L])
            out_ref[c * L:(c + 1) * L, h * D:(h + 1) * D] = (hn * gate).T.astype(BF16)
            yield


def _mlstm_kernel(qk_ref, vt_ref, ot_ref, gr_ref, gr_next_ref, cw_ref, cb_ref, br_ref, nw_ref,
                  out_ref, xpad_ref, qks_ref, c_ref, m_ref, grow_ref, rcol_ref):
    n_chunks = MLSTM_ROWS // MLSTM_CHUNK

    @pl.when(pl.program_id(0) == 0)
    def _():
        xpad_ref[:, 0:SUBLANES, :] = jnp.zeros((_QK_BLOCKS, SUBLANES, LANES), F32)
        c_ref[...] = jnp.zeros(c_ref.shape, F32)
        m_ref[...] = jnp.zeros(m_ref.shape, F32)
        first = {}
        for _ in _gate_pieces(gr_ref, br_ref, m_ref, first):
            pass
        _store_gates(first, grow_ref, rcol_ref)

    for _ in _conv_pieces(qk_ref, cw_ref, cb_ref, xpad_ref, qks_ref):
        pass
    nxt = {}
    _interleave((_gate_pieces(gr_next_ref, br_ref, m_ref, nxt), _GATE_PIECES),
                (_head_pieces(qks_ref, vt_ref, ot_ref, nw_ref, grow_ref, rcol_ref, c_ref,
                              out_ref), n_chunks * N_HEADS))
    _store_gates(nxt, grow_ref, rcol_ref)


def _mlstm_call(qk, vt, ot, gr, cw, cb, bias_r, nw_cols):
    S = qk.shape[0]
    T = MLSTM_ROWS
    n = S // T
    n_chunks = T // MLSTM_CHUNK
    return pl.pallas_call(
        _mlstm_kernel,
        grid=(n,),
        in_specs=[_row_spec(T, D_QK), _lane_spec(D_MLSTM, T), _lane_spec(D_MLSTM, T),
                  _lane_spec(SUBLANES, T),
                  pl.BlockSpec((SUBLANES, T), lambda i: (0, jnp.minimum(i + 1, n - 1))),
                  _const_spec(cw.shape), _const_spec(cb.shape),
                  _const_spec(bias_r.shape), _const_spec(nw_cols.shape)],
        out_specs=_row_spec(T, D_MLSTM),
        out_shape=jax.ShapeDtypeStruct((S, D_MLSTM), BF16),
        scratch_shapes=[pltpu.VMEM((_QK_BLOCKS, T + SUBLANES, LANES), F32),
                        pltpu.VMEM((_QK_BLOCKS, T, LANES), F32),
                        pltpu.VMEM((N_HEADS, HEAD_DIM + SUBLANES, HEAD_DIM), F32),
                        pltpu.VMEM((SUBLANES, LANES), F32),
                        pltpu.VMEM((n_chunks * len(_GATE_ROWS), SUBLANES, MLSTM_CHUNK), F32),
                        pltpu.VMEM((MLSTM_CHUNK, LANES), F32)],
        compiler_params=_params(),
        name="mlstm",
    )(qk, vt, ot, gr, gr, cw, cb, bias_r, nw_cols)


def _gelu_tanh(x):
    c = math.sqrt(2.0 / math.pi)
    return x * (0.5 * (1.0 + jnp.tanh(c * (x + 0.044715 * (x * x * x)))))


def _s5_kernel(u_ref, a2_ref, wb_ref, wc_ref, wd_ref, d_ref, gw_ref, gb_ref, out_ref,
               x_ref, y_ref, *z_refs):
    T = S5_ROWS

    @pl.when(pl.program_id(0) == 0)
    def _():
        x_ref[...] = jnp.zeros(x_ref.shape, F32)

    R = S5_SUB
    P = R // 2
    n_sub = T // R
    half = STATE_BLOCKS // 2
    n_blocks = D_SSM // LANES
    a_re = [a2_ref[:, c * LANES:(c + 1) * LANES] for c in range(half)]
    a_im = [a2_ref[:, (half + c) * LANES:(half + c + 1) * LANES] for c in range(half)]

    def pair_rows(sb, b):
        even = u_ref[b, pl.ds(sb * R, P, stride=2), :]
        odd = u_ref[b, pl.ds(sb * R + 1, P, stride=2), :]
        return jnp.concatenate([even, odd], axis=1).astype(BF16)

    def expand_piece(sb, j):
        w = jnp.dot(pair_rows(sb, j // 2), wb_ref[j], preferred_element_type=F32)
        for c in range(STATE_BLOCKS):
            z_refs[sb][c // 2, pl.ds(j * S5_PITCH + c % 2, P, stride=2), :] = (
                w[:, c * LANES:(c + 1) * LANES])

    def scan_piece(sb, x, i0, i1):
        z_ref = z_refs[sb]
        for i in range(i0, i1):
            tile = lambda c: (c // 2, pl.ds(2 * i + c % 2, SUBLANES, stride=S5_PITCH),
                              slice(None))
            new = list(x)
            for c in range(half):
                x_re, x_im = x[c], x[half + c]
                new[c] = a_re[c] * x_re - a_im[c] * x_im + z_ref[tile(c)]
                new[half + c] = a_re[c] * x_im + a_im[c] * x_re + z_ref[tile(half + c)]
                z_ref[tile(c)] = x_re
                z_ref[tile(half + c)] = x_im
            x = new
        return x

    def contract_piece(sb, j):
        xs = jnp.concatenate(
            [z_refs[sb][c // 2, pl.ds(j * S5_PITCH + c % 2, P, stride=2), :]
             for c in range(STATE_BLOCKS)], axis=1).astype(BF16)
        return jnp.dot(xs, wc_ref[j], preferred_element_type=F32)

    def finish(sb, parts):
        for b in range(n_blocks):
            yb = (parts[2 * b] + parts[2 * b + 1]
                  + jnp.dot(pair_rows(sb, b), wd_ref[b], preferred_element_type=F32))
            y_ref[b, pl.ds(sb * R, P, stride=2), :] = yb[:, 0:LANES]
            y_ref[b, pl.ds(sb * R + 1, P, stride=2), :] = yb[:, LANES:2 * LANES]
        rows = slice(sb * R, (sb + 1) * R)
        y = jnp.concatenate([y_ref[b, rows, :] for b in range(n_blocks)], axis=1)
        u = jnp.concatenate([u_ref[b, rows, :] for b in range(n_blocks)], axis=1)
        z = _gelu_tanh(y + d_ref[...] * u)
        gate = jax.nn.sigmoid(jnp.dot(z.astype(BF16), gw_ref[...],
                                      preferred_element_type=F32) + gb_ref[...])
        out_ref[rows, :] = (z * gate).astype(BF16)

    x = [x_ref[:, c * LANES:(c + 1) * LANES] for c in range(STATE_BLOCKS)]
    steps = P // SUBLANES
    for j in range(SUBLANES):
        expand_piece(0, j)
    for sb in range(n_sub + 1):
        parts = []
        for j in range(SUBLANES):
            if sb < n_sub:
                x = scan_piece(sb, x, j * steps, (j + 1) * steps)
            if sb + 1 < n_sub:
                expand_piece(sb + 1, j)
            if sb >= 1:
                parts.append(contract_piece(sb - 1, j))
        if sb >= 1:
            finish(sb - 1, parts)
    for c in range(STATE_BLOCKS):
        x_ref[:, c * LANES:(c + 1) * LANES] = x[c]


def _s5_call(u, a2_tile, wb, wc, wd, d, gw, gb):
    S = u.shape[1]
    T = S5_ROWS
    n_blocks = D_SSM // LANES
    return pl.pallas_call(
        _s5_kernel,
        grid=(S // T,),
        in_specs=[pl.BlockSpec((n_blocks, T, LANES), lambda i: (0, i, 0)),
                  _const_spec(a2_tile.shape), _const_spec(wb.shape), _const_spec(wc.shape),
                  _const_spec(wd.shape), _const_spec(d.shape), _const_spec(gw.shape),
                  _const_spec(gb.shape)],
        out_specs=_row_spec(T, D_SSM),
        out_shape=jax.ShapeDtypeStruct((S, D_SSM), BF16),
        scratch_shapes=([pltpu.VMEM((SUBLANES, STATE_LANES), F32),
                         pltpu.VMEM((n_blocks, T, LANES), F32)]
                        + [pltpu.VMEM((STATE_BLOCKS // 2, SUBLANES * S5_PITCH, LANES), F32)]
                        * (T // S5_SUB)),
        compiler_params=_params(),
        name="s5",
    )(u, a2_tile, wb, wc, wd, d, gw, gb)


def _s5_weights(lam_re, lam_im, log_dt, b_re, b_im, c_re, c_im):
    cmul = lambda xr, xi, yr, yi: (xr * yr - xi * yi, xr * yi + xi * yr)
    lr, li = lam_re.astype(F32), lam_im.astype(F32)
    dt = jnp.exp(log_dt.astype(F32))[:, None]
    mag = jnp.exp(lr * dt)
    a_re, a_im = mag * jnp.cos(li * dt), mag * jnp.sin(li * dt)
    a2_re, a2_im = cmul(a_re, a_im, a_re, a_im)
    den = lr * lr + li * li
    coef_re = ((a_re - 1.0) * lr + a_im * li) / den
    coef_im = (a_im * lr - (a_re - 1.0) * li) / den
    bb_re, bb_im = cmul(coef_re[..., None], coef_im[..., None],
                        b_re.astype(F32), b_im.astype(F32))
    abb_re, abb_im = cmul(a_re[..., None], a_im[..., None], bb_re, bb_im)
    cr, ci = c_re.astype(F32), c_im.astype(F32)
    ca_re, ca_im = cmul(cr, ci, a_re[:, None, :], a_im[:, None, :])
    ca2_re, ca2_im = cmul(cr, ci, a2_re[:, None, :], a2_im[:, None, :])

    J, Q, P, Hc = SUBLANES, GROUPS_PER_SUBLANE, SSM_STATE, SSM_GROUP
    a2_tile = jnp.concatenate([a2_re.reshape(J, Q * P), a2_im.reshape(J, Q * P)], axis=1)
    eye = jnp.eye(Q, dtype=F32)
    half = (jnp.arange(J) % 2)[:, None, None, None] == jnp.arange(2)[None, :, None, None]

    def in_weights(w_re, w_im):
        def blocks(w):
            return jnp.einsum('jqpc,qr->jqcrp', w.reshape(J, Q, P, Hc),
                              eye).reshape(J, Q * Hc, Q * P)
        w = jnp.concatenate([blocks(w_re), blocks(w_im)], axis=2)
        return jnp.where(half, w[:, None], 0.0).reshape(J, 2 * Q * Hc, STATE_LANES)

    def out_weights(w_re, w_im):
        def blocks(w):
            return jnp.einsum('jqcp,qr->jqprc', w.reshape(J, Q, Hc, P),
                              eye).reshape(J, Q * P, Q * Hc)
        w = jnp.concatenate([blocks(w_re), -blocks(w_im)], axis=1)
        half_c = jnp.swapaxes(half, 1, 2)
        return jnp.where(half_c, w[:, :, None, :], 0.0).reshape(J, STATE_LANES, 2 * Q * Hc)

    wb = jnp.concatenate([in_weights(abb_re, abb_im), in_weights(bb_re, bb_im)], axis=1)
    wc = jnp.concatenate([out_weights(ca_re, ca_im), out_weights(ca2_re, ca2_im)], axis=2)
    k0 = jnp.einsum('gop,gpc->goc', cr, bb_re) - jnp.einsum('gop,gpc->goc', ci, bb_im)
    k1 = jnp.einsum('gop,gpc->goc', cr, abb_re) - jnp.einsum('gop,gpc->goc', ci, abb_im)
    n_blocks, per = D_SSM // LANES, LANES // Hc
    eye8 = jnp.eye(per, dtype=F32)

    def lane_block(k):
        return jnp.einsum('bgoc,gh->bgcho', k.reshape(n_blocks, per, Hc, Hc),
                          eye8).reshape(n_blocks, LANES, LANES)

    zero = jnp.zeros((n_blocks, LANES, LANES), F32)
    wd = jnp.concatenate([jnp.concatenate([lane_block(k0), lane_block(k1)], axis=2),
                          jnp.concatenate([zero, lane_block(k0)], axis=2)], axis=1)
    return a2_tile, wb.astype(BF16), wc.astype(BF16), wd.astype(BF16)


def _ffn_kernel(x_ref, hm_ref, y_ref, wo_ref, g2_ref, w1_ref, w2_ref, g3_ref, out_ref):
    mixed = (jnp.dot(hm_ref[...], wo_ref[0:D_MLSTM, :], preferred_element_type=F32)
             + jnp.dot(y_ref[...], wo_ref[D_MLSTM:D_MODEL, :], preferred_element_type=F32))
    x1 = x_ref[...] + mixed
    h2 = _rms(x1, g2_ref[...]).astype(BF16)
    mlp = None
    for c in range(D_FF // FF_CHUNK):
        a = jnp.dot(h2, w1_ref[:, c * FF_CHUNK:(c + 1) * FF_CHUNK],
                    preferred_element_type=F32)
        a = jnp.maximum(a, 0.0)
        part = jnp.dot((a * a).astype(BF16), w2_ref[c * FF_CHUNK:(c + 1) * FF_CHUNK, :],
                       preferred_element_type=F32)
        mlp = part if mlp is None else mlp + part
    out_ref[...] = _rms(x1 + mlp, g3_ref[...])


def _ffn_call(x, hm, y, wo, g2, w1, w2, g3):
    S = x.shape[0]
    T = FFN_ROWS
    single = dict(pipeline_mode=pl.Buffered(WEIGHT_BUFFERS))
    return pl.pallas_call(
        _ffn_kernel,
        grid=(S // T,),
        in_specs=[_row_spec(T, D_MODEL), _row_spec(T, D_MLSTM), _row_spec(T, D_SSM),
                  pl.BlockSpec(wo.shape, lambda i: (0, 0), **single),
                  _const_spec(g2.shape),
                  pl.BlockSpec(w1.shape, lambda i: (0, 0), **single),
                  pl.BlockSpec(w2.shape, lambda i: (0, 0), **single),
                  _const_spec(g3.shape)],
        out_specs=_row_spec(T, D_MODEL),
        out_shape=jax.ShapeDtypeStruct((S, D_MODEL), F32),
        compiler_params=_params(),
        name="ffn",
    )(x, hm, y, wo, g2, w1, w2, g3)


def _layer(x, mix_norm_w, w_in, conv_w, conv_b, i_bias, f_bias, mlstm_norm_w,
           lam_re, lam_im, log_dt, b_re, b_im, c_re, c_im, ssm_d,
           glu_w, glu_b, w_out, mlp_norm_w, w_ff1, w_ff2, out_norm_w):
    n_qk, n_vo, n_gate = 2 * D_MLSTM, 2 * D_MLSTM, 2 * N_HEADS
    w_rows = jnp.concatenate([w_in[:, :n_qk], w_in[:, n_qk + n_vo + n_gate:]],
                             axis=1).astype(BF16)
    w_cols = w_in[:, n_qk:n_qk + n_vo].T.astype(BF16)
    wg = w_in[:, n_qk + n_vo:n_qk + n_vo + n_gate].T.astype(BF16)
    qk, u, vt, ot, gr, wo_bf, w1_bf, w2_bf = _proj_call(
        x, mix_norm_w[None, :], w_rows, w_cols, wg, w_out, w_ff1, w_ff2)

    bias_r = jnp.concatenate([i_bias, f_bias]).astype(F32)[:, None]
    nw_cols = jnp.broadcast_to(mlstm_norm_w.astype(F32)[:, None], (D_MLSTM, LANES))
    hm = _mlstm_call(qk, vt, ot, gr, conv_w, conv_b[None, :], bias_r, nw_cols)

    a2_tile, wb, wc, wd = _s5_weights(lam_re, lam_im, log_dt, b_re, b_im, c_re, c_im)
    y = _s5_call(u, a2_tile, wb, wc, wd, ssm_d[None, :], glu_w.astype(BF16),
                 glu_b[None, :])

    return _ffn_call(x, hm, y, wo_bf, mlp_norm_w[None, :], w1_bf, w2_bf,
                     out_norm_w[None, :])


def kernel(x, mix_norm_w, w_in, conv_w, conv_b, i_bias, f_bias, mlstm_norm_w, ssm_lam_re, ssm_lam_im, ssm_log_dt, ssm_b_re, ssm_b_im, ssm_c_re, ssm_c_im, ssm_d, glu_w, glu_b, w_out, mlp_norm_w, w_ff1, w_ff2, final_norm_w):
    assert x.shape[0] == 1 and mix_norm_w.shape[0] == 1
    xs = x[0]
    out = _layer(xs, mix_norm_w[0], w_in[0], conv_w[0], conv_b[0], i_bias[0], f_bias[0],
                 mlstm_norm_w[0], ssm_lam_re[0], ssm_lam_im[0], ssm_log_dt[0],
                 ssm_b_re[0], ssm_b_im[0], ssm_c_re[0], ssm_c_im[0], ssm_d[0],
                 glu_w[0], glu_b[0], w_out[0], mlp_norm_w[0], w_ff1[0], w_ff2[0],
                 final_norm_w)
    return out[None]
```

```python
import math

import jax
import jax.numpy as jnp
from jax import lax
from jax.experimental import pallas as pl
from jax.experimental.pallas import tpu as pltpu

F32 = jnp.float32
BF16 = jnp.bfloat16

D_MODEL = 1024
D_MLSTM = 512
D_QK = 2 * D_MLSTM
N_HEADS = 4
HEAD_DIM = 128
CONV_WIDTH = 4
D_SSM = 512
SSM_GROUP = 16
N_GROUPS = 32
SSM_STATE = 64
D_FF = 4096
EPS = 1e-6

SUBLANES = 8
LANES = 128

PROJ_ROWS = 1024
MLSTM_ROWS = 1024
MLSTM_CHUNK = 128
S5_ROWS = 1024
S5_SUB = 512
S5_PITCH = S5_SUB + 4
FFN_ROWS = 1024
FF_CHUNK = 1024
WEIGHT_BUFFERS = 1
VMEM_LIMIT = 56 * 1024 * 1024

GROUPS_PER_SUBLANE = N_GROUPS // SUBLANES
STATE_HALF = GROUPS_PER_SUBLANE * SSM_STATE
STATE_LANES = 2 * STATE_HALF
STATE_BLOCKS = STATE_LANES // LANES


def _const_spec(shape):
    return pl.BlockSpec(shape, lambda i: (0,) * len(shape))


def _row_spec(rows, cols):
    return pl.BlockSpec((rows, cols), lambda i: (i, 0))


def _lane_spec(rows, cols):
    return pl.BlockSpec((rows, cols), lambda i: (0, i))


def _params():
    return pltpu.CompilerParams(dimension_semantics=("arbitrary",),
                                vmem_limit_bytes=VMEM_LIMIT)


def _rms(x, g):
    r = lax.rsqrt(jnp.mean(x * x, axis=-1, keepdims=True) + EPS)
    return x * r * g


def _sigmoid(x):
    return 0.5 * jnp.tanh(0.5 * x) + 0.5


_NT_DIMS = (((1,), (1,)), ((), ()))


def _proj_kernel(x_ref, g_ref, w_ref, wt_ref, wg_ref, wo_ref, w1_ref, w2_ref,
                 qk_ref, u_ref, vt_ref, ot_ref, gr_ref, wo_bf_ref, w1_bf_ref, w2_bf_ref):
    wo_bf_ref[...] = wo_ref[...].astype(BF16)
    w1_bf_ref[...] = w1_ref[...].astype(BF16)
    w2_bf_ref[...] = w2_ref[...].astype(BF16)

    h = _rms(x_ref[...], g_ref[...]).astype(BF16)
    qk_ref[...] = jnp.dot(h, w_ref[:, 0:D_QK], preferred_element_type=F32)
    u = jnp.dot(h, w_ref[:, D_QK:D_QK + D_SSM], preferred_element_type=F32)
    for b in range(D_SSM // LANES):
        u_ref[b] = u[:, b * LANES:(b + 1) * LANES]
    vt_ref[...] = lax.dot_general(wt_ref[0:D_MLSTM, :], h, _NT_DIMS,
                                  preferred_element_type=F32)
    ot_ref[...] = lax.dot_general(wt_ref[D_MLSTM:2 * D_MLSTM, :], h, _NT_DIMS,
                                  preferred_element_type=F32)
    gr_ref[...] = lax.dot_general(wg_ref[...], h, _NT_DIMS, preferred_element_type=F32)


def _proj_call(x, g, w, wt, wg, wo, w1, w2):
    S = x.shape[0]
    T = PROJ_ROWS
    n = S // T
    wo_spec = _row_spec(wo.shape[0] // n, wo.shape[1])
    w1_spec = pl.BlockSpec((w1.shape[0], w1.shape[1] // n), lambda i: (0, i))
    w2_spec = _row_spec(w2.shape[0] // n, w2.shape[1])
    return pl.pallas_call(
        _proj_kernel,
        grid=(n,),
        in_specs=[_row_spec(T, D_MODEL), _const_spec((1, D_MODEL)),
                  _const_spec(w.shape), _const_spec(wt.shape), _const_spec(wg.shape),
                  wo_spec, w1_spec, w2_spec],
        out_specs=[_row_spec(T, D_QK),
                   pl.BlockSpec((D_SSM // LANES, T, LANES), lambda i: (0, i, 0)),
                   _lane_spec(D_MLSTM, T), _lane_spec(D_MLSTM, T), _lane_spec(SUBLANES, T),
                   wo_spec, w1_spec, w2_spec],
        out_shape=[jax.ShapeDtypeStruct((S, D_QK), F32),
                   jax.ShapeDtypeStruct((D_SSM // LANES, S, LANES), F32),
                   jax.ShapeDtypeStruct((D_MLSTM, S), F32),
                   jax.ShapeDtypeStruct((D_MLSTM, S), F32),
                   jax.ShapeDtypeStruct((SUBLANES, S), F32),
                   jax.ShapeDtypeStruct(wo.shape, BF16),
                   jax.ShapeDtypeStruct(w1.shape, BF16),
                   jax.ShapeDtypeStruct(w2.shape, BF16)],
        compiler_params=_params(),
        name="proj",
    )(x, g, w, wt, wg, wo, w1, w2)


def _log_sigmoid(x):
    return jnp.minimum(x, 0.0) - jnp.log1p(jnp.exp(-jnp.abs(x)))


def _interleave(*stages):
    total = max(n for _, n in stages)
    done = [0] * len(stages)
    for tick in range(1, total + 1):
        for s, (gen, n) in enumerate(stages):
            want = -(-tick * n // total)
            while done[s] < want:
                next(gen, None)
                done[s] += 1
    for gen, _ in stages:
        for _ in gen:
            pass


_LOG2E = math.log2(math.e)
_CUMMAX_LEVELS = LANES.bit_length() - 1
_GATE_PIECES = 3 + _CUMMAX_LEVELS
_QK_BLOCKS = 2 * D_MLSTM // LANES


def _conv_pieces(qk_ref, cw_ref, cb_ref, xpad_ref, qks_ref):
    T = MLSTM_ROWS
    L = MLSTM_CHUNK
    for k in range(_QK_BLOCKS):
        xpad_ref[k, SUBLANES:SUBLANES + T, :] = qk_ref[:, k * LANES:(k + 1) * LANES]
    yield
    for c in range(T // L):
        for k in range(_QK_BLOCKS):
            cols = slice(k * LANES, (k + 1) * LANES)
            for parity in range(2):
                acc = cb_ref[:, cols]
                for j in range(CONV_WIDTH):
                    start = c * L + SUBLANES - (CONV_WIDTH - 1) + parity + j
                    acc = acc + (xpad_ref[k, pl.ds(start, L // 2, stride=2), :]
                                 * cw_ref[j:j + 1, cols])
                act = acc * _sigmoid(acc)
                if k < _QK_BLOCKS // 2:
                    act = act * (HEAD_DIM ** -0.5)
                qks_ref[k, pl.ds(c * L + parity, L // 2, stride=2), :] = act
            yield
    for k in range(_QK_BLOCKS):
        xpad_ref[k, 0:SUBLANES, :] = xpad_ref[k, T:T + SUBLANES, :]


def _gate_pieces(gr_ref, br_ref, m_ref, out):
    T = MLSTM_ROWS
    L = MLSTM_CHUNK
    n_chunks = T // L
    lane = lax.broadcasted_iota(jnp.int32, (SUBLANES, L), 1)
    row_id = lax.broadcasted_iota(jnp.int32, (L, L), 0)
    col_id = lax.broadcasted_iota(jnp.int32, (L, L), 1)
    triu = (row_id <= col_id).astype(BF16)
    g = gr_ref[...] + br_ref[...]
    logf = _log_sigmoid(g)
    yield
    lf = jnp.concatenate([pltpu.roll(logf[:, c * L:(c + 1) * L], N_HEADS, axis=0)
                          for c in range(n_chunks)], axis=0)
    hi = lf.astype(BF16)
    rest = lf - hi.astype(F32)
    mid = rest.astype(BF16)
    lo = (rest - mid.astype(F32)).astype(BF16)
    terms = jnp.dot(jnp.concatenate([hi, mid, lo], axis=0), triu,
                    preferred_element_type=F32)
    n_rows = n_chunks * SUBLANES
    b_all = terms[0:n_rows] + terms[n_rows:2 * n_rows] + terms[2 * n_rows:3 * n_rows]
    bs = [b_all[c * SUBLANES:(c + 1) * SUBLANES] for c in range(n_chunks)]
    rs = [g[:, c * L:(c + 1) * L] - bs[c] for c in range(n_chunks)]
    yield
    m_runs = list(rs)
    for level in range(_CUMMAX_LEVELS):
        d = 1 << level
        m_runs = [jnp.where(lane >= d, jnp.maximum(x, pltpu.roll(x, d, axis=1)), x)
                  for x in m_runs]
        yield
    m = m_ref[...]
    rows = []
    for c in range(n_chunks):
        b, r, m_run = bs[c], rs[c], m_runs[c]
        m_last = jnp.broadcast_to(m_run[:, L - 1:L], (SUBLANES, L))
        b_last = jnp.broadcast_to(b[:, L - 1:L], (SUBLANES, L))
        mu = jnp.maximum(m_run, m)
        mx = jnp.maximum(m, m_last)
        rows.append(dict(
            mu_log2=mu * _LOG2E,
            inter_scale=jnp.exp(m - mu),
            eb=jnp.exp(-b - mu),
            w_new=jnp.exp(r - mx),
            s_old=jnp.exp(m - mx)))
        m = b_last + mx
    m_ref[...] = m
    r_pad = [jnp.zeros((SUBLANES, L), F32)] * (LANES // SUBLANES - n_chunks)
    out["rows"] = rows
    out["r_cols"] = (jnp.concatenate(rs + r_pad, axis=0) * _LOG2E).T


_GATE_ROWS = ("mu_log2", "inter_scale", "eb", "w_new", "s_old")


def _store_gates(gates, grow_ref, rcol_ref):
    for c, chunk in enumerate(gates["rows"]):
        for q, name in enumerate(_GATE_ROWS):
            grow_ref[c * len(_GATE_ROWS) + q] = chunk[name]
    rcol_ref[...] = gates["r_cols"]


def _head_pieces(qks_ref, vt_ref, ot_ref, nw_ref, grow_ref, rcol_ref, c_ref, out_ref):
    L = MLSTM_CHUNK
    D = HEAD_DIM
    row_id = lax.broadcasted_iota(jnp.int32, (L, L), 0)
    col_id = lax.broadcasted_iota(jnp.int32, (L, L), 1)
    causal = row_id <= col_id
    neg_inf = jnp.float32(-jnp.inf)
    for c in range(MLSTM_ROWS // L):
        for h in range(N_HEADS):
            row = lambda name: grow_ref[c * len(_GATE_ROWS) + _GATE_ROWS.index(name),
                                        h:h + 1, :]
            q = qks_ref[h, c * L:(c + 1) * L, :].astype(BF16)
            k = qks_ref[N_HEADS + h, c * L:(c + 1) * L, :].astype(BF16)
            vt = vt_ref[h * D:(h + 1) * D, c * L:(c + 1) * L]
            r_col = rcol_ref[:, c * SUBLANES + h:c * SUBLANES + h + 1]

            dmat = jnp.exp2(jnp.where(causal, r_col - row("mu_log2"), neg_inf))
            s_kq = lax.dot_general(k, q, _NT_DIMS, preferred_element_type=F32) * dmat
            cq = lax.dot_general(c_ref[h].astype(BF16), q, _NT_DIMS,
                                 preferred_element_type=F32)
            isc = row("inter_scale")
            num = isc * cq[0:D, :] + jnp.dot(vt.astype(BF16), s_kq.astype(BF16),
                                             preferred_element_type=F32)
            den = isc * cq[D:D + 1, :] + jnp.sum(s_kq, axis=0, keepdims=True)
            hh = num * (1.0 / jnp.maximum(jnp.abs(den), row("eb")))

            w = row("w_new")
            vw = jnp.concatenate([vt * w, jnp.broadcast_to(w, (SUBLANES, L))],
                                 axis=0).astype(BF16)
            c_loc = jnp.dot(vw, k, preferred_element_type=F32)
            c_ref[h] = row("s_old") * c_ref[h] + c_loc

            mean = jnp.mean(hh, axis=0, keepdims=True)
            cen = hh - mean
            var = jnp.mean(cen * cen, axis=0, keepdims=True)
            hn = cen * lax.rsqrt(var + EPS) * nw_ref[h * D:(h + 1) * D, :]
            gate = _sigmoid(ot_ref[h * D:(h + 1) * D, c * L:(c + 1) * L])
            out_ref[c * L:(c + 1) * L, h * D:(h + 1) * D] = (hn * gate).T.astype(BF16)
            yield


def _mlstm_kernel(qk_ref, vt_ref, ot_ref, gr_ref, gr_next_ref, cw_ref, cb_ref, br_ref, nw_ref,
                  out_ref, xpad_ref, qks_ref, c_ref, m_ref, grow_ref, rcol_ref):
    n_chunks = MLSTM_ROWS // MLSTM_CHUNK

    @pl.when(pl.program_id(0) == 0)
    def _():
        xpad_ref[:, 0:SUBLANES, :] = jnp.zeros((_QK_BLOCKS, SUBLANES, LANES), F32)
        c_ref[...] = jnp.zeros(c_ref.shape, F32)
        m_ref[...] = jnp.zeros(m_ref.shape, F32)
        first = {}
        for _ in _gate_pieces(gr_ref, br_ref, m_ref, first):
            pass
        _store_gates(first, grow_ref, rcol_ref)

    for _ in _conv_pieces(qk_ref, cw_ref, cb_ref, xpad_ref, qks_ref):
        pass
    nxt = {}
    _interleave((_gate_pieces(gr_next_ref, br_ref, m_ref, nxt), _GATE_PIECES),
                (_head_pieces(qks_ref, vt_ref, ot_ref, nw_ref, grow_ref, rcol_ref, c_ref,
                              out_ref), n_chunks * N_HEADS))
    _store_gates(nxt, grow_ref, rcol_ref)


def _mlstm_call(qk, vt, ot, gr, cw, cb, bias_r, nw_cols):
    S = qk.shape[0]
    T = MLSTM_ROWS
    n = S // T
    n_chunks = T // MLSTM_CHUNK
    return pl.pallas_call(
        _mlstm_kernel,
        grid=(n,),
        in_specs=[_row_spec(T, D_QK), _lane_spec(D_MLSTM, T), _lane_spec(D_MLSTM, T),
                  _lane_spec(SUBLANES, T),
                  pl.BlockSpec((SUBLANES, T), lambda i: (0, jnp.minimum(i + 1, n - 1))),
                  _const_spec(cw.shape), _const_spec(cb.shape),
                  _const_spec(bias_r.shape), _const_spec(nw_cols.shape)],
        out_specs=_row_spec(T, D_MLSTM),
        out_shape=jax.ShapeDtypeStruct((S, D_MLSTM), BF16),
        scratch_shapes=[pltpu.VMEM((_QK_BLOCKS, T + SUBLANES, LANES), F32),
                        pltpu.VMEM((_QK_BLOCKS, T, LANES), F32),
                        pltpu.VMEM((N_HEADS, HEAD_DIM + SUBLANES, HEAD_DIM), F32),
                        pltpu.VMEM((SUBLANES, LANES), F32),
                        pltpu.VMEM((n_chunks * len(_GATE_ROWS), SUBLANES, MLSTM_CHUNK), F32),
                        pltpu.VMEM((MLSTM_CHUNK, LANES), F32)],
        compiler_params=_params(),
        name="mlstm",
    )(qk, vt, ot, gr, gr, cw, cb, bias_r, nw_cols)


def _gelu_tanh(x):
    c = math.sqrt(2.0 / math.pi)
    return x * (0.5 * (1.0 + jnp.tanh(c * (x + 0.044715 * (x * x * x)))))


def _s5_kernel(u_ref, a2_ref, wb_ref, wc_ref, wd_ref, d_ref, gw_ref, gb_ref, out_ref,
               x_ref, y_ref, *z_refs):
    T = S5_ROWS

    @pl.when(pl.program_id(0) == 0)
    def _():
        x_ref[...] = jnp.zeros(x_ref.shape, F32)

    R = S5_SUB
    P = R // 2
    n_sub = T // R
    half = STATE_BLOCKS // 2
    n_blocks = D_SSM // LANES
    a_re = [a2_ref[:, c * LANES:(c + 1) * LANES] for c in range(half)]
    a_im = [a2_ref[:, (half + c) * LANES:(half + c + 1) * LANES] for c in range(half)]

    def pair_rows(sb, b):
        even = u_ref[b, pl.ds(sb * R, P, stride=2), :]
        odd = u_ref[b, pl.ds(sb * R + 1, P, stride=2), :]
        return jnp.concatenate([even, odd], axis=1).astype(BF16)

    def expand_piece(sb, j):
        w = jnp.dot(pair_rows(sb, j // 2), wb_ref[j], preferred_element_type=F32)
        for c in range(STATE_BLOCKS):
            z_refs[sb][c // 2, pl.ds(j * S5_PITCH + c % 2, P, stride=2), :] = (
                w[:, c * LANES:(c + 1) * LANES])

    def scan_piece(sb, x, i0, i1):
        z_ref = z_refs[sb]
        for i in range(i0, i1):
            tile = lambda c: (c // 2, pl.ds(2 * i + c % 2, SUBLANES, stride=S5_PITCH),
                              slice(None))
            new = list(x)
            for c in range(half):
                x_re, x_im = x[c], x[half + c]
                new[c] = a_re[c] * x_re - a_im[c] * x_im + z_ref[tile(c)]
                new[half + c] = a_re[c] * x_im + a_im[c] * x_re + z_ref[tile(half + c)]
                z_ref[tile(c)] = x_re
                z_ref[tile(half + c)] = x_im
            x = new
        return x

    def contract_piece(sb, j):
        xs = jnp.concatenate(
            [z_refs[sb][c // 2, pl.ds(j * S5_PITCH + c % 2, P, stride=2), :]
             for c in range(STATE_BLOCKS)], axis=1).astype(BF16)
        return jnp.dot(xs, wc_ref[j], preferred_element_type=F32)

    def finish(sb, parts):
        for b in range(n_blocks):
            yb = (parts[2 * b] + parts[2 * b + 1]
                  + jnp.dot(pair_rows(sb, b), wd_ref[b], preferred_element_type=F32))
            y_ref[b, pl.ds(sb * R, P, stride=2), :] = yb[:, 0:LANES]
            y_ref[b, pl.ds(sb * R + 1, P, stride=2), :] = yb[:, LANES:2 * LANES]
        rows = slice(sb * R, (sb + 1) * R)
        y = jnp.concatenate([y_ref[b, rows, :] for b in range(n_blocks)], axis=1)
        u = jnp.concatenate([u_ref[b, rows, :] for b in range(n_blocks)], axis=1)
        z = _gelu_tanh(y + d_ref[...] * u)
        gate = _sigmoid(jnp.dot(z.astype(BF16), gw_ref[...],
                                preferred_element_type=F32) + gb_ref[...])
        out_ref[rows, :] = (z * gate).astype(BF16)

    x = [x_ref[:, c * LANES:(c + 1) * LANES] for c in range(STATE_BLOCKS)]
    steps = P // SUBLANES
    for j in range(SUBLANES):
        expand_piece(0, j)
    for sb in range(n_sub + 1):
        parts = []
        for j in range(SUBLANES):
            if sb < n_sub:
                x = scan_piece(sb, x, j * steps, (j + 1) * steps)
            if sb + 1 < n_sub:
                expand_piece(sb + 1, j)
            if sb >= 1:
                parts.append(contract_piece(sb - 1, j))
        if sb >= 1:
            finish(sb - 1, parts)
    for c in range(STATE_BLOCKS):
        x_ref[:, c * LANES:(c + 1) * LANES] = x[c]


def _s5_call(u, a2_tile, wb, wc, wd, d, gw, gb):
    S = u.shape[1]
    T = S5_ROWS
    n_blocks = D_SSM // LANES
    return pl.pallas_call(
        _s5_kernel,
        grid=(S // T,),
        in_specs=[pl.BlockSpec((n_blocks, T, LANES), lambda i: (0, i, 0)),
                  _const_spec(a2_tile.shape), _const_spec(wb.shape), _const_spec(wc.shape),
                  _const_spec(wd.shape), _const_spec(d.shape), _const_spec(gw.shape),
                  _const_spec(gb.shape)],
        out_specs=_row_spec(T, D_SSM),
        out_shape=jax.ShapeDtypeStruct((S, D_SSM), BF16),
        scratch_shapes=([pltpu.VMEM((SUBLANES, STATE_LANES), F32),
                         pltpu.VMEM((n_blocks, T, LANES), F32)]
                        + [pltpu.VMEM((STATE_BLOCKS // 2, SUBLANES * S5_PITCH, LANES), F32)]
                        * (T // S5_SUB)),
        compiler_params=_params(),
        name="s5",
    )(u, a2_tile, wb, wc, wd, d, gw, gb)


def _s5_weights(lam_re, lam_im, log_dt, b_re, b_im, c_re, c_im):
    cmul = lambda xr, xi, yr, yi: (xr * yr - xi * yi, xr * yi + xi * yr)
    lr, li = lam_re.astype(F32), lam_im.astype(F32)
    dt = jnp.exp(log_dt.astype(F32))[:, None]
    mag = jnp.exp(lr * dt)
    a_re, a_im = mag * jnp.cos(li * dt), mag * jnp.sin(li * dt)
    a2_re, a2_im = cmul(a_re, a_im, a_re, a_im)
    den = lr * lr + li * li
    coef_re = ((a_re - 1.0) * lr + a_im * li) / den
    coef_im = (a_im * lr - (a_re - 1.0) * li) / den
    bb_re, bb_im = cmul(coef_re[..., None], coef_im[..., None],
                        b_re.astype(F32), b_im.astype(F32))
    abb_re, abb_im = cmul(a_re[..., None], a_im[..., None], bb_re, bb_im)
    cr, ci = c_re.astype(F32), c_im.astype(F32)
    ca_re, ca_im = cmul(cr, ci, a_re[:, None, :], a_im[:, None, :])
    ca2_re, ca2_im = cmul(cr, ci, a2_re[:, None, :], a2_im[:, None, :])

    J, Q, P, Hc = SUBLANES, GROUPS_PER_SUBLANE, SSM_STATE, SSM_GROUP
    a2_tile = jnp.concatenate([a2_re.reshape(J, Q * P), a2_im.reshape(J, Q * P)], axis=1)
    eye = jnp.eye(Q, dtype=F32)
    half = (jnp.arange(J) % 2)[:, None, None, None] == jnp.arange(2)[None, :, None, None]

    def in_weights(w_re, w_im):
        def blocks(w):
            return jnp.einsum('jqpc,qr->jqcrp', w.reshape(J, Q, P, Hc),
                              eye).reshape(J, Q * Hc, Q * P)
        w = jnp.concatenate([blocks(w_re), blocks(w_im)], axis=2)
        return jnp.where(half, w[:, None], 0.0).reshape(J, 2 * Q * Hc, STATE_LANES)

    def out_weights(w_re, w_im):
        def blocks(w):
            return jnp.einsum('jqcp,qr->jqprc', w.reshape(J, Q, Hc, P),
                              eye).reshape(J, Q * P, Q * Hc)
        w = jnp.concatenate([blocks(w_re), -blocks(w_im)], axis=1)
        half_c = jnp.swapaxes(half, 1, 2)
        return jnp.where(half_c, w[:, :, None, :], 0.0).reshape(J, STATE_LANES, 2 * Q * Hc)

    wb = jnp.concatenate([in_weights(abb_re, abb_im), in_weights(bb_re, bb_im)], axis=1)
    wc = jnp.concatenate([out_weights(ca_re, ca_im), out_weights(ca2_re, ca2_im)], axis=2)
    k0 = jnp.einsum('gop,gpc->goc', cr, bb_re) - jnp.einsum('gop,gpc->goc', ci, bb_im)
    k1 = jnp.einsum('gop,gpc->goc', cr, abb_re) - jnp.einsum('gop,gpc->goc', ci, abb_im)
    n_blocks, per = D_SSM // LANES, LANES // Hc
    eye8 = jnp.eye(per, dtype=F32)

    def lane_block(k):
        return jnp.einsum('bgoc,gh->bgcho', k.reshape(n_blocks, per, Hc, Hc),
                          eye8).reshape(n_blocks, LANES, LANES)

    zero = jnp.zeros((n_blocks, LANES, LANES), F32)
    wd = jnp.concatenate([jnp.concatenate([lane_block(k0), lane_block(k1)], axis=2),
                          jnp.concatenate([zero, lane_block(k0)], axis=2)], axis=1)
    return a2_tile, wb.astype(BF16), wc.astype(BF16), wd.astype(BF16)


def _ffn_kernel(x_ref, hm_ref, y_ref, wo_ref, g2_ref, w1_ref, w2_ref, g3_ref, out_ref):
    mixed = (jnp.dot(hm_ref[...], wo_ref[0:D_MLSTM, :], preferred_element_type=F32)
             + jnp.dot(y_ref[...], wo_ref[D_MLSTM:D_MODEL, :], preferred_element_type=F32))
    x1 = x_ref[...] + mixed
    h2 = _rms(x1, g2_ref[...]).astype(BF16)
    mlp = None
    for c in range(D_FF // FF_CHUNK):
        a = jnp.dot(h2, w1_ref[:, c * FF_CHUNK:(c + 1) * FF_CHUNK],
                    preferred_element_type=F32)
        a = jnp.maximum(a, 0.0)
        part = jnp.dot((a * a).astype(BF16), w2_ref[c * FF_CHUNK:(c + 1) * FF_CHUNK, :],
                       preferred_element_type=F32)
        mlp = part if mlp is None else mlp + part
    out_ref[...] = _rms(x1 + mlp, g3_ref[...])


def _ffn_call(x, hm, y, wo, g2, w1, w2, g3):
    S = x.shape[0]
    T = FFN_ROWS
    single = dict(pipeline_mode=pl.Buffered(WEIGHT_BUFFERS))
    return pl.pallas_call(
        _ffn_kernel,
        grid=(S // T,),
        in_specs=[_row_spec(T, D_MODEL), _row_spec(T, D_MLSTM), _row_spec(T, D_SSM),
                  pl.BlockSpec(wo.shape, lambda i: (0, 0), **single),
                  _const_spec(g2.shape),
                  pl.BlockSpec(w1.shape, lambda i: (0, 0), **single),
                  pl.BlockSpec(w2.shape, lambda i: (0, 0), **single),
                  _const_spec(g3.shape)],
        out_specs=_row_spec(T, D_MODEL),
        out_shape=jax.ShapeDtypeStruct((S, D_MODEL), F32),
        compiler_params=_params(),
        name="ffn",
    )(x, hm, y, wo, g2, w1, w2, g3)


def _layer(x, mix_norm_w, w_in, conv_w, conv_b, i_bias, f_bias, mlstm_norm_w,
           lam_re, lam_im, log_dt, b_re, b_im, c_re, c_im, ssm_d,
           glu_w, glu_b, w_out, mlp_norm_w, w_ff1, w_ff2, out_norm_w):
    n_qk, n_vo, n_gate = 2 * D_MLSTM, 2 * D_MLSTM, 2 * N_HEADS
    w_rows = jnp.concatenate([w_in[:, :n_qk], w_in[:, n_qk + n_vo + n_gate:]],
                             axis=1).astype(BF16)
    w_cols = w_in[:, n_qk:n_qk + n_vo].T.astype(BF16)
    wg = w_in[:, n_qk + n_vo:n_qk + n_vo + n_gate].T.astype(BF16)
    qk, u, vt, ot, gr, wo_bf, w1_bf, w2_bf = _proj_call(
        x, mix_norm_w[None, :], w_rows, w_cols, wg, w_out, w_ff1, w_ff2)

    bias_r = jnp.concatenate([i_bias, f_bias]).astype(F32)[:, None]
    nw_cols = jnp.broadcast_to(mlstm_norm_w.astype(F32)[:, None], (D_MLSTM, LANES))
    hm = _mlstm_call(qk, vt, ot, gr, conv_w, conv_b[None, :], bias_r, nw_cols)

    a2_tile, wb, wc, wd = _s5_weights(lam_re, lam_im, log_dt, b_re, b_im, c_re, c_im)
    y = _s5_call(u, a2_tile, wb, wc, wd, ssm_d[None, :], glu_w.astype(BF16),
                 glu_b[None, :])

    return _ffn_call(x, hm, y, wo_bf, mlp_norm_w[None, :], w1_bf, w2_bf,
                     out_norm_w[None, :])


def kernel(x, mix_norm_w, w_in, conv_w, conv_b, i_bias, f_bias, mlstm_norm_w, ssm_lam_re, ssm_lam_im, ssm_log_dt, ssm_b_re, ssm_b_im, ssm_c_re, ssm_c_im, ssm_d, glu_w, glu_b, w_out, mlp_norm_w, w_ff1, w_ff2, final_norm_w):
    assert x.shape[0] == 1 and mix_norm_w.shape[0] == 1
    xs = x[0]
    out = _layer(xs, mix_norm_w[0], w_in[0], conv_w[0], conv_b[0], i_bias[0], f_bias[0],
                 mlstm_norm_w[0], ssm_lam_re[0], ssm_lam_im[0], ssm_log_dt[0],
                 ssm_b_re[0], ssm_b_im[0], ssm_c_re[0], ssm_c_im[0], ssm_d[0],
                 glu_w[0], glu_b[0], w_out[0], mlp_norm_w[0], w_ff1[0], w_ff2[0],
                 final_norm_w)
    return out[None]
```

```python
import math

import jax
import jax.numpy as jnp
from jax import lax
from jax.experimental import pallas as pl
from jax.experimental.pallas import tpu as pltpu

F32 = jnp.float32
BF16 = jnp.bfloat16

D_MODEL = 1024
D_MLSTM = 512
D_QK = 2 * D_MLSTM
N_HEADS = 4
HEAD_DIM = 128
CONV_WIDTH = 4
D_SSM = 512
SSM_GROUP = 16
N_GROUPS = 32
SSM_STATE = 64
D_FF = 4096
EPS = 1e-6

SUBLANES = 8
LANES = 128

PROJ_ROWS = 1024
MLSTM_ROWS = 1024
MLSTM_CHUNK = 128
S5_ROWS = 1024
S5_SUB = 512
S5_PITCH = S5_SUB + 4
FFN_ROWS = 1024
FF_CHUNK = 1024
WEIGHT_BUFFERS = 1
VMEM_LIMIT = 56 * 1024 * 1024

GROUPS_PER_SUBLANE = N_GROUPS // SUBLANES
STATE_HALF = GROUPS_PER_SUBLANE * SSM_STATE
STATE_LANES = 2 * STATE_HALF
STATE_BLOCKS = STATE_LANES // LANES


def _const_spec(shape):
    return pl.BlockSpec(shape, lambda i: (0,) * len(shape))


def _row_spec(rows, cols):
    return pl.BlockSpec((rows, cols), lambda i: (i, 0))


def _lane_spec(rows, cols):
    return pl.BlockSpec((rows, cols), lambda i: (0, i))


def _params():
    return pltpu.CompilerParams(dimension_semantics=("arbitrary",),
                                vmem_limit_bytes=VMEM_LIMIT)


def _rms(x, g):
    r = lax.rsqrt(jnp.mean(x * x, axis=-1, keepdims=True) + EPS)
    return x * r * g


def _sigmoid(x):
    return 0.5 * jnp.tanh(0.5 * x) + 0.5


_NT_DIMS = (((1,), (1,)), ((), ()))


def _proj_kernel(x_ref, g_ref, w_ref, wt_ref, wg_ref, wo_ref, w1_ref, w2_ref,
                 qk_ref, u_ref, vt_ref, ot_ref, gr_ref, wo_bf_ref, w1_bf_ref, w2_bf_ref):
    wo_bf_ref[...] = wo_ref[...].astype(BF16)
    w1_bf_ref[...] = w1_ref[...].astype(BF16)
    w2_bf_ref[...] = w2_ref[...].astype(BF16)

    h = _rms(x_ref[...], g_ref[...]).astype(BF16)
    qk_ref[...] = jnp.dot(h, w_ref[:, 0:D_QK], preferred_element_type=F32)
    u = jnp.dot(h, w_ref[:, D_QK:D_QK + D_SSM], preferred_element_type=F32)
    for b in range(D_SSM // LANES):
        u_ref[b] = u[:, b * LANES:(b + 1) * LANES]
    vt_ref[...] = lax.dot_general(wt_ref[0:D_MLSTM, :], h, _NT_DIMS,
                                  preferred_element_type=F32)
    ot_ref[...] = lax.dot_general(wt_ref[D_MLSTM:2 * D_MLSTM, :], h, _NT_DIMS,
                                  preferred_element_type=F32)
    gr_ref[...] = lax.dot_general(wg_ref[...], h, _NT_DIMS, preferred_element_type=F32)


def _proj_call(x, g, w, wt, wg, wo, w1, w2):
    S = x.shape[0]
    T = PROJ_ROWS
    n = S // T
    wo_spec = _row_spec(wo.shape[0] // n, wo.shape[1])
    w1_spec = pl.BlockSpec((w1.shape[0], w1.shape[1] // n), lambda i: (0, i))
    w2_spec = _row_spec(w2.shape[0] // n, w2.shape[1])
    return pl.pallas_call(
        _proj_kernel,
        grid=(n,),
        in_specs=[_row_spec(T, D_MODEL), _const_spec((1, D_MODEL)),
                  _const_spec(w.shape), _const_spec(wt.shape), _const_spec(wg.shape),
                  wo_spec, w1_spec, w2_spec],
        out_specs=[_row_spec(T, D_QK),
                   pl.BlockSpec((D_SSM // LANES, T, LANES), lambda i: (0, i, 0)),
                   _lane_spec(D_MLSTM, T), _lane_spec(D_MLSTM, T), _lane_spec(SUBLANES, T),
                   wo_spec, w1_spec, w2_spec],
        out_shape=[jax.ShapeDtypeStruct((S, D_QK), F32),
                   jax.ShapeDtypeStruct((D_SSM // LANES, S, LANES), F32),
                   jax.ShapeDtypeStruct((D_MLSTM, S), F32),
                   jax.ShapeDtypeStruct((D_MLSTM, S), F32),
                   jax.ShapeDtypeStruct((SUBLANES, S), F32),
                   jax.ShapeDtypeStruct(wo.shape, BF16),
                   jax.ShapeDtypeStruct(w1.shape, BF16),
                   jax.ShapeDtypeStruct(w2.shape, BF16)],
        compiler_params=_params(),
        name="proj",
    )(x, g, w, wt, wg, wo, w1, w2)


def _log_sigmoid(x):
    return jnp.minimum(x, 0.0) - jnp.log1p(jnp.exp(-jnp.abs(x)))


def _interleave(*stages):
    total = max(n for _, n in stages)
    done = [0] * len(stages)
    for tick in range(1, total + 1):
        for s, (gen, n) in enumerate(stages):
            want = -(-tick * n // total)
            while done[s] < want:
                next(gen, None)
                done[s] += 1
    for gen, _ in stages:
        for _ in gen:
            pass


_LOG2E = math.log2(math.e)
_CUMMAX_LEVELS = LANES.bit_length() - 1
_GATE_PIECES = 3 + _CUMMAX_LEVELS
_QK_BLOCKS = 2 * D_MLSTM // LANES


def _conv_pieces(qk_ref, cw_ref, cb_ref, xpad_ref, qks_ref):
    T = MLSTM_ROWS
    L = MLSTM_CHUNK
    for k in range(_QK_BLOCKS):
        xpad_ref[k, SUBLANES:SUBLANES + T, :] = qk_ref[:, k * LANES:(k + 1) * LANES]
    yield
    for c in range(T // L):
        for k in range(_QK_BLOCKS):
            cols = slice(k * LANES, (k + 1) * LANES)
            for parity in range(2):
                acc = cb_ref[:, cols]
                for j in range(CONV_WIDTH):
                    start = c * L + SUBLANES - (CONV_WIDTH - 1) + parity + j
                    acc = acc + (xpad_ref[k, pl.ds(start, L // 2, stride=2), :]
                                 * cw_ref[j:j + 1, cols])
                act = acc * _sigmoid(acc)
                if k < _QK_BLOCKS // 2:
                    act = act * (HEAD_DIM ** -0.5)
                qks_ref[k, pl.ds(c * L + parity, L // 2, stride=2), :] = act
            yield
    for k in range(_QK_BLOCKS):
        xpad_ref[k, 0:SUBLANES, :] = xpad_ref[k, T:T + SUBLANES, :]


def _gate_pieces(gr_ref, br_ref, m_ref, out):
    T = MLSTM_ROWS
    L = MLSTM_CHUNK
    n_chunks = T // L
    lane = lax.broadcasted_iota(jnp.int32, (SUBLANES, L), 1)
    row_id = lax.broadcasted_iota(jnp.int32, (L, L), 0)
    col_id = lax.broadcasted_iota(jnp.int32, (L, L), 1)
    triu = (row_id <= col_id).astype(BF16)
    g = gr_ref[...] + br_ref[...]
    logf = _log_sigmoid(g)
    yield
    lf = jnp.concatenate([pltpu.roll(logf[:, c * L:(c + 1) * L], N_HEADS, axis=0)
                          for c in range(n_chunks)], axis=0)
    hi = lf.astype(BF16)
    rest = lf - hi.astype(F32)
    mid = rest.astype(BF16)
    lo = (rest - mid.astype(F32)).astype(BF16)
    terms = jnp.dot(jnp.concatenate([hi, mid, lo], axis=0), triu,
                    preferred_element_type=F32)
    n_rows = n_chunks * SUBLANES
    b_all = terms[0:n_rows] + terms[n_rows:2 * n_rows] + terms[2 * n_rows:3 * n_rows]
    bs = [b_all[c * SUBLANES:(c + 1) * SUBLANES] for c in range(n_chunks)]
    rs = [g[:, c * L:(c + 1) * L] - bs[c] for c in range(n_chunks)]
    yield
    m_runs = list(rs)
    for level in range(_CUMMAX_LEVELS):
        d = 1 << level
        m_runs = [jnp.where(lane >= d, jnp.maximum(x, pltpu.roll(x, d, axis=1)), x)
                  for x in m_runs]
        yield
    m = m_ref[...]
    rows = []
    for c in range(n_chunks):
        b, r, m_run = bs[c], rs[c], m_runs[c]
        m_last = jnp.broadcast_to(m_run[:, L - 1:L], (SUBLANES, L))
        b_last = jnp.broadcast_to(b[:, L - 1:L], (SUBLANES, L))
        mu = jnp.maximum(m_run, m)
        mx = jnp.maximum(m, m_last)
        rows.append(dict(
            mu_log2=mu * _LOG2E,
            inter_scale=jnp.exp(m - mu),
            eb=jnp.exp(-b - mu),
            w_new=jnp.exp(r - mx),
            s_old=jnp.exp(m - mx)))
        m = b_last + mx
    m_ref[...] = m
    r_pad = [jnp.zeros((SUBLANES, L), F32)] * (LANES // SUBLANES - n_chunks)
    out["rows"] = rows
    out["r_cols"] = (jnp.concatenate(rs + r_pad, axis=0) * _LOG2E).T


_GATE_ROWS = ("mu_log2", "inter_scale", "eb", "w_new", "s_old")


def _store_gates(gates, grow_ref, rcol_ref):
    for c, chunk in enumerate(gates["rows"]):
        for q, name in enumerate(_GATE_ROWS):
            grow_ref[c * len(_GATE_ROWS) + q] = chunk[name]
    rcol_ref[...] = gates["r_cols"]


def _head_pieces(qks_ref, vt_ref, ot_ref, nw_ref, grow_ref, rcol_ref, c_ref, out_ref):
    L = MLSTM_CHUNK
    D = HEAD_DIM
    row_id = lax.broadcasted_iota(jnp.int32, (L, L), 0)
    col_id = lax.broadcasted_iota(jnp.int32, (L, L), 1)
    causal = row_id <= col_id
    neg_inf = jnp.float32(-jnp.inf)
    for c in range(MLSTM_ROWS // L):
        for h in range(N_HEADS):
            row = lambda name: grow_ref[c * len(_GATE_ROWS) + _GATE_ROWS.index(name),
                                        h:h + 1, :]
            q = qks_ref[h, c * L:(c + 1) * L, :].astype(BF16)
            k = qks_ref[N_HEADS + h, c * L:(c + 1) * L, :].astype(BF16)
            vt = vt_ref[h * D:(h + 1) * D, c * L:(c + 1) * L]
            r_col = rcol_ref[:, c * SUBLANES + h:c * SUBLANES + h + 1]

            dmat = jnp.exp2(jnp.where(causal, r_col - row("mu_log2"), neg_inf))
            s_kq = lax.dot_general(k, q, _NT_DIMS, preferred_element_type=F32) * dmat
            cq = lax.dot_general(c_ref[h].astype(BF16), q, _NT_DIMS,
                                 preferred_element_type=F32)
            isc = row("inter_scale")
            num = isc * cq[0:D, :] + jnp.dot(vt.astype(BF16), s_kq.astype(BF16),
                                             preferred_element_type=F32)
            den = isc * cq[D:D + 1, :] + jnp.sum(s_kq, axis=0, keepdims=True)
            hh = num * (1.0 / jnp.maximum(jnp.abs(den), row("eb")))

            w = row("w_new")
            vw = jnp.concatenate([vt * w, jnp.broadcast_to(w, (SUBLANES, L))],
                                 axis=0).astype(BF16)
            c_loc = jnp.dot(vw, k, preferred_element_type=F32)
            c_ref[h] = row("s_old") * c_ref[h] + c_loc

            mean = jnp.mean(hh, axis=0, keepdims=True)
            cen = hh - mean
            var = jnp.mean(cen * cen, axis=0, keepdims=True)
            hn = cen * lax.rsqrt(var + EPS) * nw_ref[h * D:(h + 1) * D, :]
            gate = _sigmoid(ot_ref[h * D:(h + 1) * D, c * L:(c + 1) * L])
            out_ref[h * D:(h + 1) * D, c * L:(c + 1) * L] = (hn * gate).astype(BF16)
            yield


def _mlstm_kernel(qk_ref, vt_ref, ot_ref, gr_ref, gr_next_ref, cw_ref, cb_ref, br_ref, nw_ref,
                  out_ref, xpad_ref, qks_ref, c_ref, m_ref, grow_ref, rcol_ref):
    n_chunks = MLSTM_ROWS // MLSTM_CHUNK

    @pl.when(pl.program_id(0) == 0)
    def _():
        xpad_ref[:, 0:SUBLANES, :] = jnp.zeros((_QK_BLOCKS, SUBLANES, LANES), F32)
        c_ref[...] = jnp.zeros(c_ref.shape, F32)
        m_ref[...] = jnp.zeros(m_ref.shape, F32)
        first = {}
        for _ in _gate_pieces(gr_ref, br_ref, m_ref, first):
            pass
        _store_gates(first, grow_ref, rcol_ref)

    for _ in _conv_pieces(qk_ref, cw_ref, cb_ref, xpad_ref, qks_ref):
        pass
    nxt = {}
    _interleave((_gate_pieces(gr_next_ref, br_ref, m_ref, nxt), _GATE_PIECES),
                (_head_pieces(qks_ref, vt_ref, ot_ref, nw_ref, grow_ref, rcol_ref, c_ref,
                              out_ref), n_chunks * N_HEADS))
    _store_gates(nxt, grow_ref, rcol_ref)


def _mlstm_call(qk, vt, ot, gr, cw, cb, bias_r, nw_cols):
    S = qk.shape[0]
    T = MLSTM_ROWS
    n = S // T
    n_chunks = T // MLSTM_CHUNK
    return pl.pallas_call(
        _mlstm_kernel,
        grid=(n,),
        in_specs=[_row_spec(T, D_QK), _lane_spec(D_MLSTM, T), _lane_spec(D_MLSTM, T),
                  _lane_spec(SUBLANES, T),
                  pl.BlockSpec((SUBLANES, T), lambda i: (0, jnp.minimum(i + 1, n - 1))),
                  _const_spec(cw.shape), _const_spec(cb.shape),
                  _const_spec(bias_r.shape), _const_spec(nw_cols.shape)],
        out_specs=_lane_spec(D_MLSTM, T),
        out_shape=jax.ShapeDtypeStruct((D_MLSTM, S), BF16),
        scratch_shapes=[pltpu.VMEM((_QK_BLOCKS, T + SUBLANES, LANES), F32),
                        pltpu.VMEM((_QK_BLOCKS, T, LANES), F32),
                        pltpu.VMEM((N_HEADS, HEAD_DIM + SUBLANES, HEAD_DIM), F32),
                        pltpu.VMEM((SUBLANES, LANES), F32),
                        pltpu.VMEM((n_chunks * len(_GATE_ROWS), SUBLANES, MLSTM_CHUNK), F32),
                        pltpu.VMEM((MLSTM_CHUNK, LANES), F32)],
        compiler_params=_params(),
        name="mlstm",
    )(qk, vt, ot, gr, gr, cw, cb, bias_r, nw_cols)


def _gelu_tanh(x):
    c = math.sqrt(2.0 / math.pi)
    return x * (0.5 * (1.0 + jnp.tanh(c * (x + 0.044715 * (x * x * x)))))


def _s5_kernel(u_ref, a2_ref, wb_ref, wc_ref, wd_ref, d_ref, gw_ref, gb_ref, out_ref,
               x_ref, y_ref, *z_refs):
    T = S5_ROWS

    @pl.when(pl.program_id(0) == 0)
    def _():
        x_ref[...] = jnp.zeros(x_ref.shape, F32)

    R = S5_SUB
    P = R // 2
    n_sub = T // R
    half = STATE_BLOCKS // 2
    n_blocks = D_SSM // LANES
    a_re = [a2_ref[:, c * LANES:(c + 1) * LANES] for c in range(half)]
    a_im = [a2_ref[:, (half + c) * LANES:(half + c + 1) * LANES] for c in range(half)]

    def pair_rows(sb, b):
        even = u_ref[b, pl.ds(sb * R, P, stride=2), :]
        odd = u_ref[b, pl.ds(sb * R + 1, P, stride=2), :]
        return jnp.concatenate([even, odd], axis=1).astype(BF16)

    def expand_piece(sb, j):
        w = jnp.dot(pair_rows(sb, j // 2), wb_ref[j], preferred_element_type=F32)
        for c in range(STATE_BLOCKS):
            z_refs[sb][c // 2, pl.ds(j * S5_PITCH + c % 2, P, stride=2), :] = (
                w[:, c * LANES:(c + 1) * LANES])

    def scan_piece(sb, x, i0, i1):
        z_ref = z_refs[sb]
        for i in range(i0, i1):
            tile = lambda c: (c // 2, pl.ds(2 * i + c % 2, SUBLANES, stride=S5_PITCH),
                              slice(None))
            new = list(x)
            for c in range(half):
                x_re, x_im = x[c], x[half + c]
                new[c] = a_re[c] * x_re - a_im[c] * x_im + z_ref[tile(c)]
                new[half + c] = a_re[c] * x_im + a_im[c] * x_re + z_ref[tile(half + c)]
                z_ref[tile(c)] = x_re
                z_ref[tile(half + c)] = x_im
            x = new
        return x

    def contract_piece(sb, j):
        xs = jnp.concatenate(
            [z_refs[sb][c // 2, pl.ds(j * S5_PITCH + c % 2, P, stride=2), :]
             for c in range(STATE_BLOCKS)], axis=1).astype(BF16)
        return jnp.dot(xs, wc_ref[j], preferred_element_type=F32)

    def finish(sb, parts):
        for b in range(n_blocks):
            yb = (parts[2 * b] + parts[2 * b + 1]
                  + jnp.dot(pair_rows(sb, b), wd_ref[b], preferred_element_type=F32))
            y_ref[b, pl.ds(sb * R, P, stride=2), :] = yb[:, 0:LANES]
            y_ref[b, pl.ds(sb * R + 1, P, stride=2), :] = yb[:, LANES:2 * LANES]
        rows = slice(sb * R, (sb + 1) * R)
        y = jnp.concatenate([y_ref[b, rows, :] for b in range(n_blocks)], axis=1)
        u = jnp.concatenate([u_ref[b, rows, :] for b in range(n_blocks)], axis=1)
        z = _gelu_tanh(y + d_ref[...] * u)
        gate = _sigmoid(jnp.dot(z.astype(BF16), gw_ref[...],
                                preferred_element_type=F32) + gb_ref[...])
        out_ref[rows, :] = (z * gate).astype(BF16)

    x = [x_ref[:, c * LANES:(c + 1) * LANES] for c in range(STATE_BLOCKS)]
    steps = P // SUBLANES
    for j in range(SUBLANES):
        expand_piece(0, j)
    for sb in range(n_sub + 1):
        parts = []
        for j in range(SUBLANES):
            if sb < n_sub:
                x = scan_piece(sb, x, j * steps, (j + 1) * steps)
            if sb + 1 < n_sub:
                expand_piece(sb + 1, j)
            if sb >= 1:
                parts.append(contract_piece(sb - 1, j))
        if sb >= 1:
            finish(sb - 1, parts)
    for c in range(STATE_BLOCKS):
        x_ref[:, c * LANES:(c + 1) * LANES] = x[c]


def _s5_call(u, a2_tile, wb, wc, wd, d, gw, gb):
    S = u.shape[1]
    T = S5_ROWS
    n_blocks = D_SSM // LANES
    return pl.pallas_call(
        _s5_kernel,
        grid=(S // T,),
        in_specs=[pl.BlockSpec((n_blocks, T, LANES), lambda i: (0, i, 0)),
                  _const_spec(a2_tile.shape), _const_spec(wb.shape), _const_spec(wc.shape),
                  _const_spec(wd.shape), _const_spec(d.shape), _const_spec(gw.shape),
                  _const_spec(gb.shape)],
        out_specs=_row_spec(T, D_SSM),
        out_shape=jax.ShapeDtypeStruct((S, D_SSM), BF16),
        scratch_shapes=([pltpu.VMEM((SUBLANES, STATE_LANES), F32),
                         pltpu.VMEM((n_blocks, T, LANES), F32)]
                        + [pltpu.VMEM((STATE_BLOCKS // 2, SUBLANES * S5_PITCH, LANES), F32)]
                        * (T // S5_SUB)),
        compiler_params=_params(),
        name="s5",
    )(u, a2_tile, wb, wc, wd, d, gw, gb)


def _s5_weights(lam_re, lam_im, log_dt, b_re, b_im, c_re, c_im):
    cmul = lambda xr, xi, yr, yi: (xr * yr - xi * yi, xr * yi + xi * yr)
    lr, li = lam_re.astype(F32), lam_im.astype(F32)
    dt = jnp.exp(log_dt.astype(F32))[:, None]
    mag = jnp.exp(lr * dt)
    a_re, a_im = mag * jnp.cos(li * dt), mag * jnp.sin(li * dt)
    a2_re, a2_im = cmul(a_re, a_im, a_re, a_im)
    den = lr * lr + li * li
    coef_re = ((a_re - 1.0) * lr + a_im * li) / den
    coef_im = (a_im * lr - (a_re - 1.0) * li) / den
    bb_re, bb_im = cmul(coef_re[..., None], coef_im[..., None],
                        b_re.astype(F32), b_im.astype(F32))
    abb_re, abb_im = cmul(a_re[..., None], a_im[..., None], bb_re, bb_im)
    cr, ci = c_re.astype(F32), c_im.astype(F32)
    ca_re, ca_im = cmul(cr, ci, a_re[:, None, :], a_im[:, None, :])
    ca2_re, ca2_im = cmul(cr, ci, a2_re[:, None, :], a2_im[:, None, :])

    J, Q, P, Hc = SUBLANES, GROUPS_PER_SUBLANE, SSM_STATE, SSM_GROUP
    a2_tile = jnp.concatenate([a2_re.reshape(J, Q * P), a2_im.reshape(J, Q * P)], axis=1)
    eye = jnp.eye(Q, dtype=F32)
    half = (jnp.arange(J) % 2)[:, None, None, None] == jnp.arange(2)[None, :, None, None]

    def in_weights(w_re, w_im):
        def blocks(w):
            return jnp.einsum('jqpc,qr->jqcrp', w.reshape(J, Q, P, Hc),
                              eye).reshape(J, Q * Hc, Q * P)
        w = jnp.concatenate([blocks(w_re), blocks(w_im)], axis=2)
        return jnp.where(half, w[:, None], 0.0).reshape(J, 2 * Q * Hc, STATE_LANES)

    def out_weights(w_re, w_im):
        def blocks(w):
            return jnp.einsum('jqcp,qr->jqprc', w.reshape(J, Q, Hc, P),
                              eye).reshape(J, Q * P, Q * Hc)
        w = jnp.concatenate([blocks(w_re), -blocks(w_im)], axis=1)
        half_c = jnp.swapaxes(half, 1, 2)
        return jnp.where(half_c, w[:, :, None, :], 0.0).reshape(J, STATE_LANES, 2 * Q * Hc)

    wb = jnp.concatenate([in_weights(abb_re, abb_im), in_weights(bb_re, bb_im)], axis=1)
    wc = jnp.concatenate([out_weights(ca_re, ca_im), out_weights(ca2_re, ca2_im)], axis=2)
    k0 = jnp.einsum('gop,gpc->goc', cr, bb_re) - jnp.einsum('gop,gpc->goc', ci, bb_im)
    k1 = jnp.einsum('gop,gpc->goc', cr, abb_re) - jnp.einsum('gop,gpc->goc', ci, abb_im)
    n_blocks, per = D_SSM // LANES, LANES // Hc
    eye8 = jnp.eye(per, dtype=F32)

    def lane_block(k):
        return jnp.einsum('bgoc,gh->bgcho', k.reshape(n_blocks, per, Hc, Hc),
                          eye8).reshape(n_blocks, LANES, LANES)

    zero = jnp.zeros((n_blocks, LANES, LANES), F32)
    wd = jnp.concatenate([jnp.concatenate([lane_block(k0), lane_block(k1)], axis=2),
                          jnp.concatenate([zero, lane_block(k0)], axis=2)], axis=1)
    return a2_tile, wb.astype(BF16), wc.astype(BF16), wd.astype(BF16)


def _ffn_kernel(x_ref, hm_ref, y_ref, wo_ref, g2_ref, w1_ref, w2_ref, g3_ref, out_ref):
    mixed = (lax.dot_general(hm_ref[...], wo_ref[0:D_MLSTM, :], (((0,), (0,)), ((), ())),
                             preferred_element_type=F32)
             + jnp.dot(y_ref[...], wo_ref[D_MLSTM:D_MODEL, :], preferred_element_type=F32))
    x1 = x_ref[...] + mixed
    h2 = _rms(x1, g2_ref[...]).astype(BF16)
    mlp = None
    for c in range(D_FF // FF_CHUNK):
        a = jnp.dot(h2, w1_ref[:, c * FF_CHUNK:(c + 1) * FF_CHUNK],
                    preferred_element_type=F32)
        a = jnp.maximum(a, 0.0)
        part = jnp.dot((a * a).astype(BF16), w2_ref[c * FF_CHUNK:(c + 1) * FF_CHUNK, :],
                       preferred_element_type=F32)
        mlp = part if mlp is None else mlp + part
    out_ref[...] = _rms(x1 + mlp, g3_ref[...])


def _ffn_call(x, hm, y, wo, g2, w1, w2, g3):
    S = x.shape[0]
    T = FFN_ROWS
    single = dict(pipeline_mode=pl.Buffered(WEIGHT_BUFFERS))
    return pl.pallas_call(
        _ffn_kernel,
        grid=(S // T,),
        in_specs=[_row_spec(T, D_MODEL), _lane_spec(D_MLSTM, T), _row_spec(T, D_SSM),
                  pl.BlockSpec(wo.shape, lambda i: (0, 0), **single),
                  _const_spec(g2.shape),
                  pl.BlockSpec(w1.shape, lambda i: (0, 0), **single),
                  pl.BlockSpec(w2.shape, lambda i: (0, 0), **single),
                  _const_spec(g3.shape)],
        out_specs=_row_spec(T, D_MODEL),
        out_shape=jax.ShapeDtypeStruct((S, D_MODEL), F32),
        compiler_params=_params(),
        name="ffn",
    )(x, hm, y, wo, g2, w1, w2, g3)


def _layer(x, mix_norm_w, w_in, conv_w, conv_b, i_bias, f_bias, mlstm_norm_w,
           lam_re, lam_im, log_dt, b_re, b_im, c_re, c_im, ssm_d,
           glu_w, glu_b, w_out, mlp_norm_w, w_ff1, w_ff2, out_norm_w):
    n_qk, n_vo, n_gate = 2 * D_MLSTM, 2 * D_MLSTM, 2 * N_HEADS
    w_rows = jnp.concatenate([w_in[:, :n_qk], w_in[:, n_qk + n_vo + n_gate:]],
                             axis=1).astype(BF16)
    w_cols = w_in[:, n_qk:n_qk + n_vo].T.astype(BF16)
    wg = w_in[:, n_qk + n_vo:n_qk + n_vo + n_gate].T.astype(BF16)
    qk, u, vt, ot, gr, wo_bf, w1_bf, w2_bf = _proj_call(
        x, mix_norm_w[None, :], w_rows, w_cols, wg, w_out, w_ff1, w_ff2)

    bias_r = jnp.concatenate([i_bias, f_bias]).astype(F32)[:, None]
    nw_cols = jnp.broadcast_to(mlstm_norm_w.astype(F32)[:, None], (D_MLSTM, LANES))
    hm = _mlstm_call(qk, vt, ot, gr, conv_w, conv_b[None, :], bias_r, nw_cols)

    a2_tile, wb, wc, wd = _s5_weights(lam_re, lam_im, log_dt, b_re, b_im, c_re, c_im)
    y = _s5_call(u, a2_tile, wb, wc, wd, ssm_d[None, :], glu_w.astype(BF16),
                 glu_b[None, :])

    return _ffn_call(x, hm, y, wo_bf, mlp_norm_w[None, :], w1_bf, w2_bf,
                     out_norm_w[None, :])


def kernel(x, mix_norm_w, w_in, conv_w, conv_b, i_bias, f_bias, mlstm_norm_w, ssm_lam_re, ssm_lam_im, ssm_log_dt, ssm_b_re, ssm_b_im, ssm_c_re, ssm_c_im, ssm_d, glu_w, glu_b, w_out, mlp_norm_w, w_ff1, w_ff2, final_norm_w):
    assert x.shape[0] == 1 and mix_norm_w.shape[0] == 1
    xs = x[0]
    out = _layer(xs, mix_norm_w[0], w_in[0], conv_w[0], conv_b[0], i_bias[0], f_bias[0],
                 mlstm_norm_w[0], ssm_lam_re[0], ssm_lam_im[0], ssm_log_dt[0],
                 ssm_b_re[0], ssm_b_im[0], ssm_c_re[0], ssm_c_im[0], ssm_d[0],
                 glu_w[0], glu_b[0], w_out[0], mlp_norm_w[0], w_ff1[0], w_ff2[0],
                 final_norm_w)
    return out[None]
```

```python
import math

import jax
import jax.numpy as jnp
from jax import lax
from jax.experimental import pallas as pl
from jax.experimental.pallas import tpu as pltpu

F32 = jnp.float32
BF16 = jnp.bfloat16

D_MODEL = 1024
D_MLSTM = 512
D_QK = 2 * D_MLSTM
N_HEADS = 4
HEAD_DIM = 128
CONV_WIDTH = 4
D_SSM = 512
SSM_GROUP = 16
N_GROUPS = 32
SSM_STATE = 64
D_FF = 4096
EPS = 1e-6

SUBLANES = 8
LANES = 128

PROJ_ROWS = 1024
MLSTM_ROWS = 1024
MLSTM_CHUNK = 128
S5_ROWS = 1024
S5_SUB = 512
S5_PITCH = S5_SUB + 4
FFN_ROWS = 1024
FF_CHUNK = 1024
WEIGHT_BUFFERS = 1
VMEM_LIMIT = 56 * 1024 * 1024

GROUPS_PER_SUBLANE = N_GROUPS // SUBLANES
STATE_HALF = GROUPS_PER_SUBLANE * SSM_STATE
STATE_LANES = 2 * STATE_HALF
STATE_BLOCKS = STATE_LANES // LANES


def _const_spec(shape):
    return pl.BlockSpec(shape, lambda i: (0,) * len(shape))


def _row_spec(rows, cols):
    return pl.BlockSpec((rows, cols), lambda i: (i, 0))


def _lane_spec(rows, cols):
    return pl.BlockSpec((rows, cols), lambda i: (0, i))


def _params():
    return pltpu.CompilerParams(dimension_semantics=("arbitrary",),
                                vmem_limit_bytes=VMEM_LIMIT)


def _rms(x, g):
    r = lax.rsqrt(jnp.mean(x * x, axis=-1, keepdims=True) + EPS)
    return x * r * g


def _sigmoid(x):
    return 0.5 * jnp.tanh(0.5 * x) + 0.5


_NT_DIMS = (((1,), (1,)), ((), ()))


def _proj_kernel(x_ref, g_ref, w_ref, wt_ref, wg_ref, wo_ref, w1_ref, w2_ref,
                 qk_ref, u_ref, vt_ref, ot_ref, gr_ref, wo_bf_ref, w1_bf_ref, w2_bf_ref):
    wo_bf_ref[...] = wo_ref[...].astype(BF16)
    w1_bf_ref[...] = w1_ref[...].astype(BF16)
    w2_bf_ref[...] = w2_ref[...].astype(BF16)

    h = _rms(x_ref[...], g_ref[...]).astype(BF16)
    qk_ref[...] = jnp.dot(h, w_ref[:, 0:D_QK], preferred_element_type=F32)
    u = jnp.dot(h, w_ref[:, D_QK:D_QK + D_SSM], preferred_element_type=F32)
    for b in range(D_SSM // LANES):
        u_ref[b] = u[:, b * LANES:(b + 1) * LANES]
    tn_t = (((0,), (1,)), ((), ()))
    vt_ref[...] = lax.dot_general(wt_ref[:, 0:D_MLSTM], h, tn_t,
                                  preferred_element_type=F32)
    ot_ref[...] = lax.dot_general(wt_ref[:, D_MLSTM:2 * D_MLSTM], h, tn_t,
                                  preferred_element_type=F32)
    gr_ref[...] = lax.dot_general(wg_ref[...], h, _NT_DIMS, preferred_element_type=F32)


def _proj_call(x, g, w, wt, wg, wo, w1, w2):
    S = x.shape[0]
    T = PROJ_ROWS
    n = S // T
    wo_spec = _row_spec(wo.shape[0] // n, wo.shape[1])
    w1_spec = pl.BlockSpec((w1.shape[0], w1.shape[1] // n), lambda i: (0, i))
    w2_spec = _row_spec(w2.shape[0] // n, w2.shape[1])
    return pl.pallas_call(
        _proj_kernel,
        grid=(n,),
        in_specs=[_row_spec(T, D_MODEL), _const_spec((1, D_MODEL)),
                  _const_spec(w.shape), _const_spec(wt.shape), _const_spec(wg.shape),
                  wo_spec, w1_spec, w2_spec],
        out_specs=[_row_spec(T, D_QK),
                   pl.BlockSpec((D_SSM // LANES, T, LANES), lambda i: (0, i, 0)),
                   _lane_spec(D_MLSTM, T), _lane_spec(D_MLSTM, T), _lane_spec(SUBLANES, T),
                   wo_spec, w1_spec, w2_spec],
        out_shape=[jax.ShapeDtypeStruct((S, D_QK), F32),
                   jax.ShapeDtypeStruct((D_SSM // LANES, S, LANES), F32),
                   jax.ShapeDtypeStruct((D_MLSTM, S), F32),
                   jax.ShapeDtypeStruct((D_MLSTM, S), F32),
                   jax.ShapeDtypeStruct((SUBLANES, S), F32),
                   jax.ShapeDtypeStruct(wo.shape, BF16),
                   jax.ShapeDtypeStruct(w1.shape, BF16),
                   jax.ShapeDtypeStruct(w2.shape, BF16)],
        compiler_params=_params(),
        name="proj",
    )(x, g, w, wt, wg, wo, w1, w2)


def _log_sigmoid(x):
    return jnp.minimum(x, 0.0) - jnp.log1p(jnp.exp(-jnp.abs(x)))


def _interleave(*stages):
    total = max(n for _, n in stages)
    done = [0] * len(stages)
    for tick in range(1, total + 1):
        for s, (gen, n) in enumerate(stages):
            want = -(-tick * n // total)
            while done[s] < want:
                next(gen, None)
                done[s] += 1
    for gen, _ in stages:
        for _ in gen:
            pass


_LOG2E = math.log2(math.e)
_CUMMAX_LEVELS = LANES.bit_length() - 1
_GATE_PIECES = 3 + _CUMMAX_LEVELS
_QK_BLOCKS = 2 * D_MLSTM // LANES


def _conv_pieces(qk_ref, cw_ref, cb_ref, xpad_ref, qks_ref):
    T = MLSTM_ROWS
    L = MLSTM_CHUNK
    for k in range(_QK_BLOCKS):
        xpad_ref[k, SUBLANES:SUBLANES + T, :] = qk_ref[:, k * LANES:(k + 1) * LANES]
    yield
    for c in range(T // L):
        for k in range(_QK_BLOCKS):
            cols = slice(k * LANES, (k + 1) * LANES)
            for parity in range(2):
                acc = cb_ref[:, cols]
                for j in range(CONV_WIDTH):
                    start = c * L + SUBLANES - (CONV_WIDTH - 1) + parity + j
                    acc = acc + (xpad_ref[k, pl.ds(start, L // 2, stride=2), :]
                                 * cw_ref[j:j + 1, cols])
                act = acc * _sigmoid(acc)
                if k < _QK_BLOCKS // 2:
                    act = act * (HEAD_DIM ** -0.5)
                qks_ref[k, pl.ds(c * L + parity, L // 2, stride=2), :] = act
            yield
    for k in range(_QK_BLOCKS):
        xpad_ref[k, 0:SUBLANES, :] = xpad_ref[k, T:T + SUBLANES, :]


def _gate_pieces(gr_ref, br_ref, m_ref, out):
    T = MLSTM_ROWS
    L = MLSTM_CHUNK
    n_chunks = T // L
    lane = lax.broadcasted_iota(jnp.int32, (SUBLANES, L), 1)
    row_id = lax.broadcasted_iota(jnp.int32, (L, L), 0)
    col_id = lax.broadcasted_iota(jnp.int32, (L, L), 1)
    triu = (row_id <= col_id).astype(BF16)
    g = gr_ref[...] + br_ref[...]
    logf = _log_sigmoid(g)
    yield
    lf = jnp.concatenate([pltpu.roll(logf[:, c * L:(c + 1) * L], N_HEADS, axis=0)
                          for c in range(n_chunks)], axis=0)
    hi = lf.astype(BF16)
    rest = lf - hi.astype(F32)
    mid = rest.astype(BF16)
    lo = (rest - mid.astype(F32)).astype(BF16)
    terms = jnp.dot(jnp.concatenate([hi, mid, lo], axis=0), triu,
                    preferred_element_type=F32)
    n_rows = n_chunks * SUBLANES
    b_all = terms[0:n_rows] + terms[n_rows:2 * n_rows] + terms[2 * n_rows:3 * n_rows]
    bs = [b_all[c * SUBLANES:(c + 1) * SUBLANES] for c in range(n_chunks)]
    rs = [g[:, c * L:(c + 1) * L] - bs[c] for c in range(n_chunks)]
    yield
    m_runs = list(rs)
    for level in range(_CUMMAX_LEVELS):
        d = 1 << level
        m_runs = [jnp.where(lane >= d, jnp.maximum(x, pltpu.roll(x, d, axis=1)), x)
                  for x in m_runs]
        yield
    m = m_ref[...]
    rows = []
    for c in range(n_chunks):
        b, r, m_run = bs[c], rs[c], m_runs[c]
        m_last = jnp.broadcast_to(m_run[:, L - 1:L], (SUBLANES, L))
        b_last = jnp.broadcast_to(b[:, L - 1:L], (SUBLANES, L))
        mu = jnp.maximum(m_run, m)
        mx = jnp.maximum(m, m_last)
        rows.append(dict(
            mu_log2=mu * _LOG2E,
            inter_scale=jnp.exp(m - mu),
            eb=jnp.exp(-b - mu),
            w_new=jnp.exp(r - mx),
            s_old=jnp.exp(m - mx)))
        m = b_last + mx
    m_ref[...] = m
    r_pad = [jnp.zeros((SUBLANES, L), F32)] * (LANES // SUBLANES - n_chunks)
    out["rows"] = rows
    out["r_cols"] = (jnp.concatenate(rs + r_pad, axis=0) * _LOG2E).T


_GATE_ROWS = ("mu_log2", "inter_scale", "eb", "w_new", "s_old")


def _store_gates(gates, grow_ref, rcol_ref):
    for c, chunk in enumerate(gates["rows"]):
        for q, name in enumerate(_GATE_ROWS):
            grow_ref[c * len(_GATE_ROWS) + q] = chunk[name]
    rcol_ref[...] = gates["r_cols"]


def _head_pieces(qks_ref, vt_ref, ot_ref, nw_ref, grow_ref, rcol_ref, c_ref, out_ref):
    L = MLSTM_CHUNK
    D = HEAD_DIM
    row_id = lax.broadcasted_iota(jnp.int32, (L, L), 0)
    col_id = lax.broadcasted_iota(jnp.int32, (L, L), 1)
    causal = row_id <= col_id
    neg_inf = jnp.float32(-jnp.inf)
    for c in range(MLSTM_ROWS // L):
        for h in range(N_HEADS):
            row = lambda name: grow_ref[c * len(_GATE_ROWS) + _GATE_ROWS.index(name),
                                        h:h + 1, :]
            q = qks_ref[h, c * L:(c + 1) * L, :].astype(BF16)
            k = qks_ref[N_HEADS + h, c * L:(c + 1) * L, :].astype(BF16)
            vt = vt_ref[h * D:(h + 1) * D, c * L:(c + 1) * L]
            r_col = rcol_ref[:, c * SUBLANES + h:c * SUBLANES + h + 1]

            dmat = jnp.exp2(jnp.where(causal, r_col - row("mu_log2"), neg_inf))
            s_kq = lax.dot_general(k, q, _NT_DIMS, preferred_element_type=F32) * dmat
            cq = lax.dot_general(c_ref[h].astype(BF16), q, _NT_DIMS,
                                 preferred_element_type=F32)
            isc = row("inter_scale")
            num = isc * cq[0:D, :] + jnp.dot(vt.astype(BF16), s_kq.astype(BF16),
                                             preferred_element_type=F32)
            den = isc * cq[D:D + 1, :] + jnp.sum(s_kq, axis=0, keepdims=True)
            hh = num * (1.0 / jnp.maximum(jnp.abs(den), row("eb")))

            w = row("w_new")
            vw = jnp.concatenate([vt * w, jnp.broadcast_to(w, (SUBLANES, L))],
                                 axis=0).astype(BF16)
            c_loc = jnp.dot(vw, k, preferred_element_type=F32)
            c_ref[h] = row("s_old") * c_ref[h] + c_loc

            mean = jnp.mean(hh, axis=0, keepdims=True)
            cen = hh - mean
            var = jnp.mean(cen * cen, axis=0, keepdims=True)
            hn = cen * lax.rsqrt(var + EPS) * nw_ref[h * D:(h + 1) * D, :]
            gate = _sigmoid(ot_ref[h * D:(h + 1) * D, c * L:(c + 1) * L])
            out_ref[h * D:(h + 1) * D, c * L:(c + 1) * L] = (hn * gate).astype(BF16)
            yield


def _mlstm_kernel(qk_ref, vt_ref, ot_ref, gr_ref, gr_next_ref, cw_ref, cb_ref, br_ref, nw_ref,
                  out_ref, xpad_ref, qks_ref, c_ref, m_ref, grow_ref, rcol_ref):
    n_chunks = MLSTM_ROWS // MLSTM_CHUNK

    @pl.when(pl.program_id(0) == 0)
    def _():
        xpad_ref[:, 0:SUBLANES, :] = jnp.zeros((_QK_BLOCKS, SUBLANES, LANES), F32)
        c_ref[...] = jnp.zeros(c_ref.shape, F32)
        m_ref[...] = jnp.zeros(m_ref.shape, F32)
        first = {}
        for _ in _gate_pieces(gr_ref, br_ref, m_ref, first):
            pass
        _store_gates(first, grow_ref, rcol_ref)

    for _ in _conv_pieces(qk_ref, cw_ref, cb_ref, xpad_ref, qks_ref):
        pass
    nxt = {}
    _interleave((_gate_pieces(gr_next_ref, br_ref, m_ref, nxt), _GATE_PIECES),
                (_head_pieces(qks_ref, vt_ref, ot_ref, nw_ref, grow_ref, rcol_ref, c_ref,
                              out_ref), n_chunks * N_HEADS))
    _store_gates(nxt, grow_ref, rcol_ref)


def _mlstm_call(qk, vt, ot, gr, cw, cb, bias_r, nw_cols):
    S = qk.shape[0]
    T = MLSTM_ROWS
    n = S // T
    n_chunks = T // MLSTM_CHUNK
    return pl.pallas_call(
        _mlstm_kernel,
        grid=(n,),
        in_specs=[_row_spec(T, D_QK), _lane_spec(D_MLSTM, T), _lane_spec(D_MLSTM, T),
                  _lane_spec(SUBLANES, T),
                  pl.BlockSpec((SUBLANES, T), lambda i: (0, jnp.minimum(i + 1, n - 1))),
                  _const_spec(cw.shape), _const_spec(cb.shape),
                  _const_spec(bias_r.shape), _const_spec(nw_cols.shape)],
        out_specs=_lane_spec(D_MLSTM, T),
        out_shape=jax.ShapeDtypeStruct((D_MLSTM, S), BF16),
        scratch_shapes=[pltpu.VMEM((_QK_BLOCKS, T + SUBLANES, LANES), F32),
                        pltpu.VMEM((_QK_BLOCKS, T, LANES), F32),
                        pltpu.VMEM((N_HEADS, HEAD_DIM + SUBLANES, HEAD_DIM), F32),
                        pltpu.VMEM((SUBLANES, LANES), F32),
                        pltpu.VMEM((n_chunks * len(_GATE_ROWS), SUBLANES, MLSTM_CHUNK), F32),
                        pltpu.VMEM((MLSTM_CHUNK, LANES), F32)],
        compiler_params=_params(),
        name="mlstm",
    )(qk, vt, ot, gr, gr, cw, cb, bias_r, nw_cols)


def _gelu_tanh(x):
    c = math.sqrt(2.0 / math.pi)
    return x * (0.5 * (1.0 + jnp.tanh(c * (x + 0.044715 * (x * x * x)))))


def _s5_kernel(u_ref, a2_ref, wb_ref, wc_ref, wd_ref, d_ref, gw_ref, gb_ref, out_ref,
               x_ref, y_ref, *z_refs):
    T = S5_ROWS

    @pl.when(pl.program_id(0) == 0)
    def _():
        x_ref[...] = jnp.zeros(x_ref.shape, F32)

    R = S5_SUB
    P = R // 2
    n_sub = T // R
    half = STATE_BLOCKS // 2
    n_blocks = D_SSM // LANES
    a_re = [a2_ref[:, c * LANES:(c + 1) * LANES] for c in range(half)]
    a_im = [a2_ref[:, (half + c) * LANES:(half + c + 1) * LANES] for c in range(half)]

    def pair_rows(sb, b):
        even = u_ref[b, pl.ds(sb * R, P, stride=2), :]
        odd = u_ref[b, pl.ds(sb * R + 1, P, stride=2), :]
        return jnp.concatenate([even, odd], axis=1).astype(BF16)

    def expand_piece(sb, j):
        w = jnp.dot(pair_rows(sb, j // 2), wb_ref[j], preferred_element_type=F32)
        for c in range(STATE_BLOCKS):
            z_refs[sb][c // 2, pl.ds(j * S5_PITCH + c % 2, P, stride=2), :] = (
                w[:, c * LANES:(c + 1) * LANES])

    def scan_piece(sb, x, i0, i1):
        z_ref = z_refs[sb]
        for i in range(i0, i1):
            tile = lambda c: (c // 2, pl.ds(2 * i + c % 2, SUBLANES, stride=S5_PITCH),
                              slice(None))
            new = list(x)
            for c in range(half):
                x_re, x_im = x[c], x[half + c]
                new[c] = a_re[c] * x_re - a_im[c] * x_im + z_ref[tile(c)]
                new[half + c] = a_re[c] * x_im + a_im[c] * x_re + z_ref[tile(half + c)]
                z_ref[tile(c)] = x_re
                z_ref[tile(half + c)] = x_im
            x = new
        return x

    def contract_piece(sb, j):
        xs = jnp.concatenate(
            [z_refs[sb][c // 2, pl.ds(j * S5_PITCH + c % 2, P, stride=2), :]
             for c in range(STATE_BLOCKS)], axis=1).astype(BF16)
        return jnp.dot(xs, wc_ref[j], preferred_element_type=F32)

    def finish(sb, parts):
        for b in range(n_blocks):
            yb = (parts[2 * b] + parts[2 * b + 1]
                  + jnp.dot(pair_rows(sb, b), wd_ref[b], preferred_element_type=F32))
            y_ref[b, pl.ds(sb * R, P, stride=2), :] = yb[:, 0:LANES]
            y_ref[b, pl.ds(sb * R + 1, P, stride=2), :] = yb[:, LANES:2 * LANES]
        rows = slice(sb * R, (sb + 1) * R)
        y = jnp.concatenate([y_ref[b, rows, :] for b in range(n_blocks)], axis=1)
        u = jnp.concatenate([u_ref[b, rows, :] for b in range(n_blocks)], axis=1)
        z = _gelu_tanh(y + d_ref[...] * u)
        gate = _sigmoid(jnp.dot(z.astype(BF16), gw_ref[...],
                                preferred_element_type=F32) + gb_ref[...])
        out_ref[rows, :] = (z * gate).astype(BF16)

    x = [x_ref[:, c * LANES:(c + 1) * LANES] for c in range(STATE_BLOCKS)]
    steps = P // SUBLANES
    for j in range(SUBLANES):
        expand_piece(0, j)
    for sb in range(n_sub + 1):
        parts = []
        for j in range(SUBLANES):
            if sb < n_sub:
                x = scan_piece(sb, x, j * steps, (j + 1) * steps)
            if sb + 1 < n_sub:
                expand_piece(sb + 1, j)
            if sb >= 1:
                parts.append(contract_piece(sb - 1, j))
        if sb >= 1:
            finish(sb - 1, parts)
    for c in range(STATE_BLOCKS):
        x_ref[:, c * LANES:(c + 1) * LANES] = x[c]


def _s5_call(u, a2_tile, wb, wc, wd, d, gw, gb):
    S = u.shape[1]
    T = S5_ROWS
    n_blocks = D_SSM // LANES
    return pl.pallas_call(
        _s5_kernel,
        grid=(S // T,),
        in_specs=[pl.BlockSpec((n_blocks, T, LANES), lambda i: (0, i, 0)),
                  _const_spec(a2_tile.shape), _const_spec(wb.shape), _const_spec(wc.shape),
                  _const_spec(wd.shape), _const_spec(d.shape), _const_spec(gw.shape),
                  _const_spec(gb.shape)],
        out_specs=_row_spec(T, D_SSM),
        out_shape=jax.ShapeDtypeStruct((S, D_SSM), BF16),
        scratch_shapes=([pltpu.VMEM((SUBLANES, STATE_LANES), F32),
                         pltpu.VMEM((n_blocks, T, LANES), F32)]
                        + [pltpu.VMEM((STATE_BLOCKS // 2, SUBLANES * S5_PITCH, LANES), F32)]
                        * (T // S5_SUB)),
        compiler_params=_params(),
        name="s5",
    )(u, a2_tile, wb, wc, wd, d, gw, gb)


def _s5_weights(lam_re, lam_im, log_dt, b_re, b_im, c_re, c_im):
    cmul = lambda xr, xi, yr, yi: (xr * yr - xi * yi, xr * yi + xi * yr)
    lr, li = lam_re.astype(F32), lam_im.astype(F32)
    dt = jnp.exp(log_dt.astype(F32))[:, None]
    mag = jnp.exp(lr * dt)
    a_re, a_im = mag * jnp.cos(li * dt), mag * jnp.sin(li * dt)
    a2_re, a2_im = cmul(a_re, a_im, a_re, a_im)
    den = lr * lr + li * li
    coef_re = ((a_re - 1.0) * lr + a_im * li) / den
    coef_im = (a_im * lr - (a_re - 1.0) * li) / den
    bb_re, bb_im = cmul(coef_re[..., None], coef_im[..., None],
                        b_re.astype(F32), b_im.astype(F32))
    abb_re, abb_im = cmul(a_re[..., None], a_im[..., None], bb_re, bb_im)
    cr, ci = c_re.astype(F32), c_im.astype(F32)
    ca_re, ca_im = cmul(cr, ci, a_re[:, None, :], a_im[:, None, :])
    ca2_re, ca2_im = cmul(cr, ci, a2_re[:, None, :], a2_im[:, None, :])

    J, Q, P, Hc = SUBLANES, GROUPS_PER_SUBLANE, SSM_STATE, SSM_GROUP
    a2_tile = jnp.concatenate([a2_re.reshape(J, Q * P), a2_im.reshape(J, Q * P)], axis=1)
    eye = jnp.eye(Q, dtype=F32)
    half = (jnp.arange(J) % 2)[:, None, None, None] == jnp.arange(2)[None, :, None, None]

    def in_weights(w_re, w_im):
        def blocks(w):
            return jnp.einsum('jqpc,qr->jqcrp', w.reshape(J, Q, P, Hc),
                              eye).reshape(J, Q * Hc, Q * P)
        w = jnp.concatenate([blocks(w_re), blocks(w_im)], axis=2)
        return jnp.where(half, w[:, None], 0.0).reshape(J, 2 * Q * Hc, STATE_LANES)

    def out_weights(w_re, w_im):
        def blocks(w):
            return jnp.einsum('jqcp,qr->jqprc', w.reshape(J, Q, Hc, P),
                              eye).reshape(J, Q * P, Q * Hc)
        w = jnp.concatenate([blocks(w_re), -blocks(w_im)], axis=1)
        half_c = jnp.swapaxes(half, 1, 2)
        return jnp.where(half_c, w[:, :, None, :], 0.0).reshape(J, STATE_LANES, 2 * Q * Hc)

    wb = jnp.concatenate([in_weights(abb_re, abb_im), in_weights(bb_re, bb_im)], axis=1)
    wc = jnp.concatenate([out_weights(ca_re, ca_im), out_weights(ca2_re, ca2_im)], axis=2)
    k0 = jnp.einsum('gop,gpc->goc', cr, bb_re) - jnp.einsum('gop,gpc->goc', ci, bb_im)
    k1 = jnp.einsum('gop,gpc->goc', cr, abb_re) - jnp.einsum('gop,gpc->goc', ci, abb_im)
    n_blocks, per = D_SSM // LANES, LANES // Hc
    eye8 = jnp.eye(per, dtype=F32)

    def lane_block(k):
        return jnp.einsum('bgoc,gh->bgcho', k.reshape(n_blocks, per, Hc, Hc),
                          eye8).reshape(n_blocks, LANES, LANES)

    zero = jnp.zeros((n_blocks, LANES, LANES), F32)
    wd = jnp.concatenate([jnp.concatenate([lane_block(k0), lane_block(k1)], axis=2),
                          jnp.concatenate([zero, lane_block(k0)], axis=2)], axis=1)
    return a2_tile, wb.astype(BF16), wc.astype(BF16), wd.astype(BF16)


def _ffn_kernel(x_ref, hm_ref, y_ref, wo_ref, g2_ref, w1_ref, w2_ref, g3_ref, out_ref):
    mixed = (lax.dot_general(hm_ref[...], wo_ref[0:D_MLSTM, :], (((0,), (0,)), ((), ())),
                             preferred_element_type=F32)
             + jnp.dot(y_ref[...], wo_ref[D_MLSTM:D_MODEL, :], preferred_element_type=F32))
    x1 = x_ref[...] + mixed
    h2 = _rms(x1, g2_ref[...]).astype(BF16)
    mlp = None
    for c in range(D_FF // FF_CHUNK):
        a = jnp.dot(h2, w1_ref[:, c * FF_CHUNK:(c + 1) * FF_CHUNK],
                    preferred_element_type=F32)
        a = jnp.maximum(a, 0.0)
        part = jnp.dot((a * a).astype(BF16), w2_ref[c * FF_CHUNK:(c + 1) * FF_CHUNK, :],
                       preferred_element_type=F32)
        mlp = part if mlp is None else mlp + part
    out_ref[...] = _rms(x1 + mlp, g3_ref[...])


def _ffn_call(x, hm, y, wo, g2, w1, w2, g3):
    S = x.shape[0]
    T = FFN_ROWS
    single = dict(pipeline_mode=pl.Buffered(WEIGHT_BUFFERS))
    return pl.pallas_call(
        _ffn_kernel,
        grid=(S // T,),
        in_specs=[_row_spec(T, D_MODEL), _lane_spec(D_MLSTM, T), _row_spec(T, D_SSM),
                  pl.BlockSpec(wo.shape, lambda i: (0, 0), **single),
                  _const_spec(g2.shape),
                  pl.BlockSpec(w1.shape, lambda i: (0, 0), **single),
                  pl.BlockSpec(w2.shape, lambda i: (0, 0), **single),
                  _const_spec(g3.shape)],
        out_specs=_row_spec(T, D_MODEL),
        out_shape=jax.ShapeDtypeStruct((S, D_MODEL), F32),
        compiler_params=_params(),
        name="ffn",
    )(x, hm, y, wo, g2, w1, w2, g3)


def _layer(x, mix_norm_w, w_in, conv_w, conv_b, i_bias, f_bias, mlstm_norm_w,
           lam_re, lam_im, log_dt, b_re, b_im, c_re, c_im, ssm_d,
           glu_w, glu_b, w_out, mlp_norm_w, w_ff1, w_ff2, out_norm_w):
    n_qk, n_vo, n_gate = 2 * D_MLSTM, 2 * D_MLSTM, 2 * N_HEADS
    w_rows = jnp.concatenate([w_in[:, :n_qk], w_in[:, n_qk + n_vo + n_gate:]],
                             axis=1).astype(BF16)
    w_cols = w_in[:, n_qk:n_qk + n_vo].astype(BF16)
    wg = w_in[:, n_qk + n_vo:n_qk + n_vo + n_gate].T.astype(BF16)
    qk, u, vt, ot, gr, wo_bf, w1_bf, w2_bf = _proj_call(
        x, mix_norm_w[None, :], w_rows, w_cols, wg, w_out, w_ff1, w_ff2)

    bias_r = jnp.concatenate([i_bias, f_bias]).astype(F32)[:, None]
    nw_cols = jnp.broadcast_to(mlstm_norm_w.astype(F32)[:, None], (D_MLSTM, LANES))
    hm = _mlstm_call(qk, vt, ot, gr, conv_w, conv_b[None, :], bias_r, nw_cols)

    a2_tile, wb, wc, wd = _s5_weights(lam_re, lam_im, log_dt, b_re, b_im, c_re, c_im)
    y = _s5_call(u, a2_tile, wb, wc, wd, ssm_d[None, :], glu_w.astype(BF16),
                 glu_b[None, :])

    return _ffn_call(x, hm, y, wo_bf, mlp_norm_w[None, :], w1_bf, w2_bf,
                     out_norm_w[None, :])


def kernel(x, mix_norm_w, w_in, conv_w, conv_b, i_bias, f_bias, mlstm_norm_w, ssm_lam_re, ssm_lam_im, ssm_log_dt, ssm_b_re, ssm_b_im, ssm_c_re, ssm_c_im, ssm_d, glu_w, glu_b, w_out, mlp_norm_w, w_ff1, w_ff2, final_norm_w):
    assert x.shape[0] == 1 and mix_norm_w.shape[0] == 1
    xs = x[0]
    out = _layer(xs, mix_norm_w[0], w_in[0], conv_w[0], conv_b[0], i_bias[0], f_bias[0],
                 mlstm_norm_w[0], ssm_lam_re[0], ssm_lam_im[0], ssm_log_dt[0],
                 ssm_b_re[0], ssm_b_im[0], ssm_c_re[0], ssm_c_im[0], ssm_d[0],
                 glu_w[0], glu_b[0], w_out[0], mlp_norm_w[0], w_ff1[0], w_ff2[0],
                 final_norm_w)
    return out[None]
```

```python
import math

import jax
import jax.numpy as jnp
from jax import lax
from jax.experimental import pallas as pl
from jax.experimental.pallas import tpu as pltpu

F32 = jnp.float32
BF16 = jnp.bfloat16

D_MODEL = 1024
D_MLSTM = 512
D_QK = 2 * D_MLSTM
N_HEADS = 4
HEAD_DIM = 128
CONV_WIDTH = 4
D_SSM = 512
SSM_GROUP = 16
N_GROUPS = 32
SSM_STATE = 64
D_FF = 4096
EPS = 1e-6

SUBLANES = 8
LANES = 128

PROJ_ROWS = 1024
MLSTM_ROWS = 1024
MLSTM_CHUNK = 128
S5_ROWS = 1024
S5_SUB = 512
S5_STEPS = 4
S5_PITCH = 2 * S5_SUB // S5_STEPS + 4
FFN_ROWS = 1024
FF_CHUNK = 1024
WEIGHT_BUFFERS = 1
VMEM_LIMIT = 56 * 1024 * 1024

GROUPS_PER_SUBLANE = N_GROUPS // SUBLANES
STATE_HALF = GROUPS_PER_SUBLANE * SSM_STATE
STATE_LANES = 2 * STATE_HALF
STATE_BLOCKS = STATE_LANES // LANES


def _const_spec(shape):
    return pl.BlockSpec(shape, lambda i: (0,) * len(shape))


def _row_spec(rows, cols):
    return pl.BlockSpec((rows, cols), lambda i: (i, 0))


def _lane_spec(rows, cols):
    return pl.BlockSpec((rows, cols), lambda i: (0, i))


def _params():
    return pltpu.CompilerParams(dimension_semantics=("arbitrary",),
                                vmem_limit_bytes=VMEM_LIMIT)


def _rms(x, g):
    r = lax.rsqrt(jnp.mean(x * x, axis=-1, keepdims=True) + EPS)
    return x * r * g


def _sigmoid(x):
    return 0.5 * jnp.tanh(0.5 * x) + 0.5


_NT_DIMS = (((1,), (1,)), ((), ()))


def _proj_kernel(x_ref, g_ref, w_ref, wt_ref, wg_ref, wo_ref, w1_ref, w2_ref,
                 qk_ref, u_ref, vt_ref, ot_ref, gr_ref, wo_bf_ref, w1_bf_ref, w2_bf_ref):
    wo_bf_ref[...] = wo_ref[...].astype(BF16)
    w1_bf_ref[...] = w1_ref[...].astype(BF16)
    w2_bf_ref[...] = w2_ref[...].astype(BF16)

    h = _rms(x_ref[...], g_ref[...]).astype(BF16)
    qk_ref[...] = jnp.dot(h, w_ref[:, 0:D_QK], preferred_element_type=F32)
    u = jnp.dot(h, w_ref[:, D_QK:D_QK + D_SSM], preferred_element_type=F32)
    for b in range(D_SSM // LANES):
        u_ref[b] = u[:, b * LANES:(b + 1) * LANES]
    vt_ref[...] = lax.dot_general(wt_ref[0:D_MLSTM, :], h, _NT_DIMS,
                                  preferred_element_type=F32)
    ot_ref[...] = lax.dot_general(wt_ref[D_MLSTM:2 * D_MLSTM, :], h, _NT_DIMS,
                                  preferred_element_type=F32)
    gr_ref[...] = lax.dot_general(wg_ref[...], h, _NT_DIMS, preferred_element_type=F32)


def _proj_call(x, g, w, wt, wg, wo, w1, w2):
    S = x.shape[0]
    T = PROJ_ROWS
    n = S // T
    wo_spec = _row_spec(wo.shape[0] // n, wo.shape[1])
    w1_spec = pl.BlockSpec((w1.shape[0], w1.shape[1] // n), lambda i: (0, i))
    w2_spec = _row_spec(w2.shape[0] // n, w2.shape[1])
    return pl.pallas_call(
        _proj_kernel,
        grid=(n,),
        in_specs=[_row_spec(T, D_MODEL), _const_spec((1, D_MODEL)),
                  _const_spec(w.shape), _const_spec(wt.shape), _const_spec(wg.shape),
                  wo_spec, w1_spec, w2_spec],
        out_specs=[_row_spec(T, D_QK),
                   pl.BlockSpec((D_SSM // LANES, T, LANES), lambda i: (0, i, 0)),
                   _lane_spec(D_MLSTM, T), _lane_spec(D_MLSTM, T), _lane_spec(SUBLANES, T),
                   wo_spec, w1_spec, w2_spec],
        out_shape=[jax.ShapeDtypeStruct((S, D_QK), F32),
                   jax.ShapeDtypeStruct((D_SSM // LANES, S, LANES), F32),
                   jax.ShapeDtypeStruct((D_MLSTM, S), F32),
                   jax.ShapeDtypeStruct((D_MLSTM, S), F32),
                   jax.ShapeDtypeStruct((SUBLANES, S), F32),
                   jax.ShapeDtypeStruct(wo.shape, BF16),
                   jax.ShapeDtypeStruct(w1.shape, BF16),
                   jax.ShapeDtypeStruct(w2.shape, BF16)],
        compiler_params=_params(),
        name="proj",
    )(x, g, w, wt, wg, wo, w1, w2)


def _log_sigmoid(x):
    return jnp.minimum(x, 0.0) - jnp.log1p(jnp.exp(-jnp.abs(x)))


def _interleave(*stages):
    total = max(n for _, n in stages)
    done = [0] * len(stages)
    for tick in range(1, total + 1):
        for s, (gen, n) in enumerate(stages):
            want = -(-tick * n // total)
            while done[s] < want:
                next(gen, None)
                done[s] += 1
    for gen, _ in stages:
        for _ in gen:
            pass


_LOG2E = math.log2(math.e)
_CUMMAX_LEVELS = LANES.bit_length() - 1
_GATE_PIECES = 3 + _CUMMAX_LEVELS
_QK_BLOCKS = 2 * D_MLSTM // LANES


def _conv_pieces(qk_ref, cw_ref, cb_ref, xpad_ref, qks_ref):
    T = MLSTM_ROWS
    L = MLSTM_CHUNK
    for k in range(_QK_BLOCKS):
        xpad_ref[k, SUBLANES:SUBLANES + T, :] = qk_ref[:, k * LANES:(k + 1) * LANES]
    yield
    for c in range(T // L):
        for k in range(_QK_BLOCKS):
            cols = slice(k * LANES, (k + 1) * LANES)
            for parity in range(2):
                acc = cb_ref[:, cols]
                for j in range(CONV_WIDTH):
                    start = c * L + SUBLANES - (CONV_WIDTH - 1) + parity + j
                    acc = acc + (xpad_ref[k, pl.ds(start, L // 2, stride=2), :]
                                 * cw_ref[j:j + 1, cols])
                act = acc * _sigmoid(acc)
                if k < _QK_BLOCKS // 2:
                    act = act * (HEAD_DIM ** -0.5)
                qks_ref[k, pl.ds(c * L + parity, L // 2, stride=2), :] = act
            yield
    for k in range(_QK_BLOCKS):
        xpad_ref[k, 0:SUBLANES, :] = xpad_ref[k, T:T + SUBLANES, :]


def _gate_pieces(gr_ref, br_ref, m_ref, out):
    T = MLSTM_ROWS
    L = MLSTM_CHUNK
    n_chunks = T // L
    lane = lax.broadcasted_iota(jnp.int32, (SUBLANES, L), 1)
    row_id = lax.broadcasted_iota(jnp.int32, (L, L), 0)
    col_id = lax.broadcasted_iota(jnp.int32, (L, L), 1)
    triu = (row_id <= col_id).astype(BF16)
    g = gr_ref[...] + br_ref[...]
    logf = _log_sigmoid(g)
    yield
    lf = jnp.concatenate([pltpu.roll(logf[:, c * L:(c + 1) * L], N_HEADS, axis=0)
                          for c in range(n_chunks)], axis=0)
    hi = lf.astype(BF16)
    rest = lf - hi.astype(F32)
    mid = rest.astype(BF16)
    lo = (rest - mid.astype(F32)).astype(BF16)
    terms = jnp.dot(jnp.concatenate([hi, mid, lo], axis=0), triu,
                    preferred_element_type=F32)
    n_rows = n_chunks * SUBLANES
    b_all = terms[0:n_rows] + terms[n_rows:2 * n_rows] + terms[2 * n_rows:3 * n_rows]
    bs = [b_all[c * SUBLANES:(c + 1) * SUBLANES] for c in range(n_chunks)]
    rs = [g[:, c * L:(c + 1) * L] - bs[c] for c in range(n_chunks)]
    yield
    m_runs = list(rs)
    for level in range(_CUMMAX_LEVELS):
        d = 1 << level
        m_runs = [jnp.where(lane >= d, jnp.maximum(x, pltpu.roll(x, d, axis=1)), x)
                  for x in m_runs]
        yield
    m = m_ref[...]
    rows = []
    for c in range(n_chunks):
        b, r, m_run = bs[c], rs[c], m_runs[c]
        m_last = jnp.broadcast_to(m_run[:, L - 1:L], (SUBLANES, L))
        b_last = jnp.broadcast_to(b[:, L - 1:L], (SUBLANES, L))
        mu = jnp.maximum(m_run, m)
        mx = jnp.maximum(m, m_last)
        rows.append(dict(
            mu_log2=mu * _LOG2E,
            inter_scale=jnp.exp(m - mu),
            eb=jnp.exp(-b - mu),
            w_new=jnp.exp(r - mx),
            s_old=jnp.exp(m - mx)))
        m = b_last + mx
    m_ref[...] = m
    r_pad = [jnp.zeros((SUBLANES, L), F32)] * (LANES // SUBLANES - n_chunks)
    out["rows"] = rows
    out["r_cols"] = (jnp.concatenate(rs + r_pad, axis=0) * _LOG2E).T


_GATE_ROWS = ("mu_log2", "inter_scale", "eb", "w_new", "s_old")


def _store_gates(gates, grow_ref, rcol_ref):
    for c, chunk in enumerate(gates["rows"]):
        for q, name in enumerate(_GATE_ROWS):
            grow_ref[c * len(_GATE_ROWS) + q] = chunk[name]
    rcol_ref[...] = gates["r_cols"]


def _head_pieces(qks_ref, vt_ref, ot_ref, nw_ref, grow_ref, rcol_ref, c_ref, out_ref):
    L = MLSTM_CHUNK
    D = HEAD_DIM
    row_id = lax.broadcasted_iota(jnp.int32, (L, L), 0)
    col_id = lax.broadcasted_iota(jnp.int32, (L, L), 1)
    causal = row_id <= col_id
    neg_inf = jnp.float32(-jnp.inf)
    for c in range(MLSTM_ROWS // L):
        for h in range(N_HEADS):
            row = lambda name: grow_ref[c * len(_GATE_ROWS) + _GATE_ROWS.index(name),
                                        h:h + 1, :]
            q = qks_ref[h, c * L:(c + 1) * L, :].astype(BF16)
            k = qks_ref[N_HEADS + h, c * L:(c + 1) * L, :].astype(BF16)
            vt = vt_ref[h * D:(h + 1) * D, c * L:(c + 1) * L]
            r_col = rcol_ref[:, c * SUBLANES + h:c * SUBLANES + h + 1]

            dmat = jnp.exp2(jnp.where(causal, r_col - row("mu_log2"), neg_inf))
            s_kq = lax.dot_general(k, q, _NT_DIMS, preferred_element_type=F32) * dmat
            cq = lax.dot_general(c_ref[h].astype(BF16), q, _NT_DIMS,
                                 preferred_element_type=F32)
            isc = row("inter_scale")
            num = isc * cq[0:D, :] + jnp.dot(vt.astype(BF16), s_kq.astype(BF16),
                                             preferred_element_type=F32)
            den = isc * cq[D:D + 1, :] + jnp.sum(s_kq, axis=0, keepdims=True)
            hh = num * (1.0 / jnp.maximum(jnp.abs(den), row("eb")))

            w = row("w_new")
            vw = jnp.concatenate([vt * w, jnp.broadcast_to(w, (SUBLANES, L))],
                                 axis=0).astype(BF16)
            c_loc = jnp.dot(vw, k, preferred_element_type=F32)
            c_ref[h] = row("s_old") * c_ref[h] + c_loc

            mean = jnp.mean(hh, axis=0, keepdims=True)
            cen = hh - mean
            var = jnp.mean(cen * cen, axis=0, keepdims=True)
            hn = cen * lax.rsqrt(var + EPS) * nw_ref[h * D:(h + 1) * D, :]
            gate = _sigmoid(ot_ref[h * D:(h + 1) * D, c * L:(c + 1) * L])
            out_ref[h * D:(h + 1) * D, c * L:(c + 1) * L] = (hn * gate).astype(BF16)
            yield


def _mlstm_kernel(qk_ref, vt_ref, ot_ref, gr_ref, gr_next_ref, cw_ref, cb_ref, br_ref, nw_ref,
                  out_ref, xpad_ref, qks_ref, c_ref, m_ref, grow_ref, rcol_ref):
    n_chunks = MLSTM_ROWS // MLSTM_CHUNK

    @pl.when(pl.program_id(0) == 0)
    def _():
        xpad_ref[:, 0:SUBLANES, :] = jnp.zeros((_QK_BLOCKS, SUBLANES, LANES), F32)
        c_ref[...] = jnp.zeros(c_ref.shape, F32)
        m_ref[...] = jnp.zeros(m_ref.shape, F32)
        first = {}
        for _ in _gate_pieces(gr_ref, br_ref, m_ref, first):
            pass
        _store_gates(first, grow_ref, rcol_ref)

    for _ in _conv_pieces(qk_ref, cw_ref, cb_ref, xpad_ref, qks_ref):
        pass
    nxt = {}
    _interleave((_gate_pieces(gr_next_ref, br_ref, m_ref, nxt), _GATE_PIECES),
                (_head_pieces(qks_ref, vt_ref, ot_ref, nw_ref, grow_ref, rcol_ref, c_ref,
                              out_ref), n_chunks * N_HEADS))
    _store_gates(nxt, grow_ref, rcol_ref)


def _mlstm_call(qk, vt, ot, gr, cw, cb, bias_r, nw_cols):
    S = qk.shape[0]
    T = MLSTM_ROWS
    n = S // T
    n_chunks = T // MLSTM_CHUNK
    return pl.pallas_call(
        _mlstm_kernel,
        grid=(n,),
        in_specs=[_row_spec(T, D_QK), _lane_spec(D_MLSTM, T), _lane_spec(D_MLSTM, T),
                  _lane_spec(SUBLANES, T),
                  pl.BlockSpec((SUBLANES, T), lambda i: (0, jnp.minimum(i + 1, n - 1))),
                  _const_spec(cw.shape), _const_spec(cb.shape),
                  _const_spec(bias_r.shape), _const_spec(nw_cols.shape)],
        out_specs=_lane_spec(D_MLSTM, T),
        out_shape=jax.ShapeDtypeStruct((D_MLSTM, S), BF16),
        scratch_shapes=[pltpu.VMEM((_QK_BLOCKS, T + SUBLANES, LANES), F32),
                        pltpu.VMEM((_QK_BLOCKS, T, LANES), F32),
                        pltpu.VMEM((N_HEADS, HEAD_DIM + SUBLANES, HEAD_DIM), F32),
                        pltpu.VMEM((SUBLANES, LANES), F32),
                        pltpu.VMEM((n_chunks * len(_GATE_ROWS), SUBLANES, MLSTM_CHUNK), F32),
                        pltpu.VMEM((MLSTM_CHUNK, LANES), F32)],
        compiler_params=_params(),
        name="mlstm",
    )(qk, vt, ot, gr, gr, cw, cb, bias_r, nw_cols)


def _gelu_tanh(x):
    c = math.sqrt(2.0 / math.pi)
    return x * (0.5 * (1.0 + jnp.tanh(c * (x + 0.044715 * (x * x * x)))))


def _s5_kernel(u_ref, an_ref, wb_ref, wc_ref, wd_ref, d_ref, gw_ref, gb_ref, out_ref,
               x_ref, y_ref, *z_refs):
    T = S5_ROWS
    N = S5_STEPS

    @pl.when(pl.program_id(0) == 0)
    def _():
        x_ref[...] = jnp.zeros(x_ref.shape, F32)

    R = S5_SUB
    P = R // N
    n_sub = T // R
    half = STATE_BLOCKS // 2
    n_blocks = D_SSM // LANES
    a_re = [an_ref[:, c * LANES:(c + 1) * LANES] for c in range(half)]
    a_im = [an_ref[:, (half + c) * LANES:(half + c + 1) * LANES] for c in range(half)]

    def group_rows(sb, b):
        return jnp.concatenate([u_ref[b, pl.ds(sb * R + m, P, stride=N), :]
                                for m in range(N)], axis=1).astype(BF16)

    def expand_piece(sb, j):
        w = jnp.dot(group_rows(sb, j // 2), wb_ref[j], preferred_element_type=F32)
        for c in range(STATE_BLOCKS):
            z_refs[sb][c // 2, pl.ds(j * S5_PITCH + c % 2, P, stride=2), :] = (
                w[:, c * LANES:(c + 1) * LANES])

    def scan_piece(sb, x, i0, i1):
        z_ref = z_refs[sb]
        for i in range(i0, i1):
            tile = lambda c: (c // 2, pl.ds(2 * i + c % 2, SUBLANES, stride=S5_PITCH),
                              slice(None))
            new = list(x)
            for c in range(half):
                x_re, x_im = x[c], x[half + c]
                new[c] = a_re[c] * x_re - a_im[c] * x_im + z_ref[tile(c)]
                new[half + c] = a_re[c] * x_im + a_im[c] * x_re + z_ref[tile(half + c)]
                z_ref[tile(c)] = x_re
                z_ref[tile(half + c)] = x_im
            x = new
        return x

    def contract_piece(sb, j):
        xs = jnp.concatenate(
            [z_refs[sb][c // 2, pl.ds(j * S5_PITCH + c % 2, P, stride=2), :]
             for c in range(STATE_BLOCKS)], axis=1).astype(BF16)
        return jnp.dot(xs, wc_ref[j], preferred_element_type=F32)

    def finish(sb, parts):
        for b in range(n_blocks):
            yb = (parts[2 * b] + parts[2 * b + 1]
                  + jnp.dot(group_rows(sb, b), wd_ref[b], preferred_element_type=F32))
            for m in range(N):
                y_ref[b, pl.ds(sb * R + m, P, stride=N), :] = yb[:, m * LANES:(m + 1) * LANES]
        rows = slice(sb * R, (sb + 1) * R)
        y = jnp.concatenate([y_ref[b, rows, :] for b in range(n_blocks)], axis=1)
        u = jnp.concatenate([u_ref[b, rows, :] for b in range(n_blocks)], axis=1)
        z = _gelu_tanh(y + d_ref[...] * u)
        gate = _sigmoid(jnp.dot(z.astype(BF16), gw_ref[...],
                                preferred_element_type=F32) + gb_ref[...])
        out_ref[rows, :] = (z * gate).astype(BF16)

    x = [x_ref[:, c * LANES:(c + 1) * LANES] for c in range(STATE_BLOCKS)]
    steps = P // SUBLANES
    for j in range(SUBLANES):
        expand_piece(0, j)
    for sb in range(n_sub + 1):
        parts = []
        for j in range(SUBLANES):
            if sb < n_sub:
                x = scan_piece(sb, x, j * steps, (j + 1) * steps)
            if sb + 1 < n_sub:
                expand_piece(sb + 1, j)
            if sb >= 1:
                parts.append(contract_piece(sb - 1, j))
        if sb >= 1:
            finish(sb - 1, parts)
    for c in range(STATE_BLOCKS):
        x_ref[:, c * LANES:(c + 1) * LANES] = x[c]


def _s5_call(u, an_tile, wb, wc, wd, d, gw, gb):
    S = u.shape[1]
    T = S5_ROWS
    n_blocks = D_SSM // LANES
    return pl.pallas_call(
        _s5_kernel,
        grid=(S // T,),
        in_specs=[pl.BlockSpec((n_blocks, T, LANES), lambda i: (0, i, 0)),
                  _const_spec(an_tile.shape), _const_spec(wb.shape), _const_spec(wc.shape),
                  _const_spec(wd.shape), _const_spec(d.shape), _const_spec(gw.shape),
                  _const_spec(gb.shape)],
        out_specs=_row_spec(T, D_SSM),
        out_shape=jax.ShapeDtypeStruct((S, D_SSM), BF16),
        scratch_shapes=([pltpu.VMEM((SUBLANES, STATE_LANES), F32),
                         pltpu.VMEM((n_blocks, T, LANES), F32)]
                        + [pltpu.VMEM((STATE_BLOCKS // 2, SUBLANES * S5_PITCH, LANES), F32)]
                        * (T // S5_SUB)),
        compiler_params=_params(),
        name="s5",
    )(u, an_tile, wb, wc, wd, d, gw, gb)


def _s5_weights(lam_re, lam_im, log_dt, b_re, b_im, c_re, c_im):
    N = S5_STEPS
    cmul = lambda xr, xi, yr, yi: (xr * yr - xi * yi, xr * yi + xi * yr)
    lr, li = lam_re.astype(F32), lam_im.astype(F32)
    dt = jnp.exp(log_dt.astype(F32))[:, None]
    mag = jnp.exp(lr * dt)
    a_re, a_im = mag * jnp.cos(li * dt), mag * jnp.sin(li * dt)
    pw = [(jnp.ones_like(a_re), jnp.zeros_like(a_im))]
    for _ in range(N):
        pw.append(cmul(pw[-1][0], pw[-1][1], a_re, a_im))
    den = lr * lr + li * li
    coef_re = ((a_re - 1.0) * lr + a_im * li) / den
    coef_im = (a_im * lr - (a_re - 1.0) * li) / den
    bb = cmul(coef_re[..., None], coef_im[..., None],
              b_re.astype(F32), b_im.astype(F32))
    anb = [cmul(p[0][..., None], p[1][..., None], bb[0], bb[1]) for p in pw[:N]]
    cr, ci = c_re.astype(F32), c_im.astype(F32)
    can = [cmul(cr, ci, p[0][:, None, :], p[1][:, None, :]) for p in pw[1:]]

    J, Q, P, Hc = SUBLANES, GROUPS_PER_SUBLANE, SSM_STATE, SSM_GROUP
    an_tile = jnp.concatenate([pw[N][0].reshape(J, Q * P), pw[N][1].reshape(J, Q * P)], axis=1)
    eye = jnp.eye(Q, dtype=F32)
    half = (jnp.arange(J) % 2)[:, None, None, None] == jnp.arange(2)[None, :, None, None]

    def in_weights(w):
        def blocks(w):
            return jnp.einsum('jqpc,qr->jqcrp', w.reshape(J, Q, P, Hc),
                              eye).reshape(J, Q * Hc, Q * P)
        w = jnp.concatenate([blocks(w[0]), blocks(w[1])], axis=2)
        return jnp.where(half, w[:, None], 0.0).reshape(J, 2 * Q * Hc, STATE_LANES)

    def out_weights(w):
        def blocks(w):
            return jnp.einsum('jqcp,qr->jqprc', w.reshape(J, Q, Hc, P),
                              eye).reshape(J, Q * P, Q * Hc)
        w = jnp.concatenate([blocks(w[0]), -blocks(w[1])], axis=1)
        half_c = jnp.swapaxes(half, 1, 2)
        return jnp.where(half_c, w[:, :, None, :], 0.0).reshape(J, STATE_LANES, 2 * Q * Hc)

    wb = jnp.concatenate([in_weights(anb[N - 1 - m]) for m in range(N)], axis=1)
    wc = jnp.concatenate([out_weights(can[m]) for m in range(N)], axis=2)
    kn = [jnp.einsum('gop,gpc->goc', cr, w[0]) - jnp.einsum('gop,gpc->goc', ci, w[1])
          for w in anb]
    n_blocks, per = D_SSM // LANES, LANES // Hc
    eye8 = jnp.eye(per, dtype=F32)

    def lane_block(k):
        return jnp.einsum('bgoc,gh->bgcho', k.reshape(n_blocks, per, Hc, Hc),
                          eye8).reshape(n_blocks, LANES, LANES)

    zero = jnp.zeros((n_blocks, LANES, LANES), F32)
    wd = jnp.concatenate(
        [jnp.concatenate([lane_block(kn[m - l]) if m >= l else zero for m in range(N)], axis=2)
         for l in range(N)], axis=1)
    return an_tile, wb.astype(BF16), wc.astype(BF16), wd.astype(BF16)


def _ffn_kernel(x_ref, hm_ref, y_ref, wo_ref, g2_ref, w1_ref, w2_ref, g3_ref, out_ref):
    mixed = (lax.dot_general(hm_ref[...], wo_ref[0:D_MLSTM, :], (((0,), (0,)), ((), ())),
                             preferred_element_type=F32)
             + jnp.dot(y_ref[...], wo_ref[D_MLSTM:D_MODEL, :], preferred_element_type=F32))
    x1 = x_ref[...] + mixed
    h2 = _rms(x1, g2_ref[...]).astype(BF16)
    mlp = None
    for c in range(D_FF // FF_CHUNK):
        a = jnp.dot(h2, w1_ref[:, c * FF_CHUNK:(c + 1) * FF_CHUNK],
                    preferred_element_type=F32)
        a = jnp.maximum(a, 0.0)
        part = jnp.dot((a * a).astype(BF16), w2_ref[c * FF_CHUNK:(c + 1) * FF_CHUNK, :],
                       preferred_element_type=F32)
        mlp = part if mlp is None else mlp + part
    out_ref[...] = _rms(x1 + mlp, g3_ref[...])


def _ffn_call(x, hm, y, wo, g2, w1, w2, g3):
    S = x.shape[0]
    T = FFN_ROWS
    single = dict(pipeline_mode=pl.Buffered(WEIGHT_BUFFERS))
    return pl.pallas_call(
        _ffn_kernel,
        grid=(S // T,),
        in_specs=[_row_spec(T, D_MODEL), _lane_spec(D_MLSTM, T), _row_spec(T, D_SSM),
                  pl.BlockSpec(wo.shape, lambda i: (0, 0), **single),
                  _const_spec(g2.shape),
                  pl.BlockSpec(w1.shape, lambda i: (0, 0), **single),
                  pl.BlockSpec(w2.shape, lambda i: (0, 0), **single),
                  _const_spec(g3.shape)],
        out_specs=_row_spec(T, D_MODEL),
        out_shape=jax.ShapeDtypeStruct((S, D_MODEL), F32),
        compiler_params=_params(),
        name="ffn",
    )(x, hm, y, wo, g2, w1, w2, g3)


def _layer(x, mix_norm_w, w_in, conv_w, conv_b, i_bias, f_bias, mlstm_norm_w,
           lam_re, lam_im, log_dt, b_re, b_im, c_re, c_im, ssm_d,
           glu_w, glu_b, w_out, mlp_norm_w, w_ff1, w_ff2, out_norm_w):
    assert x.shape[1] == D_MODEL
    assert x.shape[0] % max(PROJ_ROWS, MLSTM_ROWS, S5_ROWS, FFN_ROWS) == 0
    n_qk, n_vo, n_gate = 2 * D_MLSTM, 2 * D_MLSTM, 2 * N_HEADS
    w_rows = jnp.concatenate([w_in[:, :n_qk], w_in[:, n_qk + n_vo + n_gate:]],
                             axis=1).astype(BF16)
    w_cols = w_in[:, n_qk:n_qk + n_vo].T.astype(BF16)
    wg = w_in[:, n_qk + n_vo:n_qk + n_vo + n_gate].T.astype(BF16)
    qk, u, vt, ot, gr, wo_bf, w1_bf, w2_bf = _proj_call(
        x, mix_norm_w[None, :], w_rows, w_cols, wg, w_out, w_ff1, w_ff2)

    bias_r = jnp.concatenate([i_bias, f_bias]).astype(F32)[:, None]
    nw_cols = jnp.broadcast_to(mlstm_norm_w.astype(F32)[:, None], (D_MLSTM, LANES))
    hm = _mlstm_call(qk, vt, ot, gr, conv_w, conv_b[None, :], bias_r, nw_cols)

    an_tile, wb, wc, wd = _s5_weights(lam_re, lam_im, log_dt, b_re, b_im, c_re, c_im)
    y = _s5_call(u, an_tile, wb, wc, wd, ssm_d[None, :], glu_w.astype(BF16),
                 glu_b[None, :])

    return _ffn_call(x, hm, y, wo_bf, mlp_norm_w[None, :], w1_bf, w2_bf,
                     out_norm_w[None, :])


def kernel(x, mix_norm_w, w_in, conv_w, conv_b, i_bias, f_bias, mlstm_norm_w, ssm_lam_re, ssm_lam_im, ssm_log_dt, ssm_b_re, ssm_b_im, ssm_c_re, ssm_c_im, ssm_d, glu_w, glu_b, w_out, mlp_norm_w, w_ff1, w_ff2, final_norm_w):
    assert x.shape[0] == 1 and mix_norm_w.shape[0] == 1
    xs = x[0]
    out = _layer(xs, mix_norm_w[0], w_in[0], conv_w[0], conv_b[0], i_bias[0], f_bias[0],
                 mlstm_norm_w[0], ssm_lam_re[0], ssm_lam_im[0], ssm_log_dt[0],
                 ssm_b_re[0], ssm_b_im[0], ssm_c_re[0], ssm_c_im[0], ssm_d[0],
                 glu_w[0], glu_b[0], w_out[0], mlp_norm_w[0], w_ff1[0], w_ff2[0],
                 final_norm_w)
    return out[None]
```

```python
import math

import jax
import jax.numpy as jnp
from jax import lax
from jax.experimental import pallas as pl
from jax.experimental.pallas import tpu as pltpu

F32 = jnp.float32
BF16 = jnp.bfloat16

D_MODEL = 1024
D_MLSTM = 512
D_QK = 2 * D_MLSTM
N_HEADS = 4
HEAD_DIM = 128
CONV_WIDTH = 4
D_SSM = 512
SSM_GROUP = 16
N_GROUPS = 32
SSM_STATE = 64
D_FF = 4096
EPS = 1e-6

SUBLANES = 8
LANES = 128

PROJ_ROWS = 1024
MLSTM_ROWS = 1024
MLSTM_CHUNK = 128
S5_ROWS = 1024
S5_SUB = 512
S5_STEPS = 2
S5_PITCH = 2 * S5_SUB // S5_STEPS + 4
FFN_ROWS = 1024
FF_CHUNK = 1024
WEIGHT_BUFFERS = 1
VMEM_LIMIT = 56 * 1024 * 1024

GROUPS_PER_SUBLANE = N_GROUPS // SUBLANES
STATE_HALF = GROUPS_PER_SUBLANE * SSM_STATE
STATE_LANES = 2 * STATE_HALF
STATE_BLOCKS = STATE_LANES // LANES


def _const_spec(shape):
    return pl.BlockSpec(shape, lambda i: (0,) * len(shape))


def _row_spec(rows, cols):
    return pl.BlockSpec((rows, cols), lambda i: (i, 0))


def _lane_spec(rows, cols):
    return pl.BlockSpec((rows, cols), lambda i: (0, i))


def _params():
    return pltpu.CompilerParams(dimension_semantics=("arbitrary",),
                                vmem_limit_bytes=VMEM_LIMIT)


def _rms(x, g):
    r = lax.rsqrt(jnp.mean(x * x, axis=-1, keepdims=True) + EPS)
    return x * r * g


def _sigmoid(x):
    return 0.5 * jnp.tanh(0.5 * x) + 0.5


_NT_DIMS = (((1,), (1,)), ((), ()))


def _proj_kernel(x_ref, g_ref, w_ref, wt_ref, wg_ref, wo_ref, w1_ref, w2_ref,
                 qk_ref, u_ref, vt_ref, ot_ref, gr_ref, wo_bf_ref, w1_bf_ref, w2_bf_ref):
    wo_bf_ref[...] = wo_ref[...].astype(BF16)
    w1_bf_ref[...] = w1_ref[...].astype(BF16)
    w2_bf_ref[...] = w2_ref[...].astype(BF16)

    h = _rms(x_ref[...], g_ref[...]).astype(BF16)
    qk_ref[...] = jnp.dot(h, w_ref[:, 0:D_QK], preferred_element_type=F32)
    u = jnp.dot(h, w_ref[:, D_QK:D_QK + D_SSM], preferred_element_type=F32)
    for b in range(D_SSM // LANES):
        u_ref[b] = u[:, b * LANES:(b + 1) * LANES]
    vt_ref[...] = lax.dot_general(wt_ref[0:D_MLSTM, :], h, _NT_DIMS,
                                  preferred_element_type=F32)
    ot_ref[...] = lax.dot_general(wt_ref[D_MLSTM:2 * D_MLSTM, :], h, _NT_DIMS,
                                  preferred_element_type=F32)
    gr_ref[...] = lax.dot_general(wg_ref[...], h, _NT_DIMS, preferred_element_type=F32)


def _proj_call(x, g, w, wt, wg, wo, w1, w2):
    S = x.shape[0]
    T = PROJ_ROWS
    n = S // T
    wo_spec = _row_spec(wo.shape[0] // n, wo.shape[1])
    w1_spec = pl.BlockSpec((w1.shape[0], w1.shape[1] // n), lambda i: (0, i))
    w2_spec = _row_spec(w2.shape[0] // n, w2.shape[1])
    return pl.pallas_call(
        _proj_kernel,
        grid=(n,),
        in_specs=[_row_spec(T, D_MODEL), _const_spec((1, D_MODEL)),
                  _const_spec(w.shape), _const_spec(wt.shape), _const_spec(wg.shape),
                  wo_spec, w1_spec, w2_spec],
        out_specs=[_row_spec(T, D_QK),
                   pl.BlockSpec((D_SSM // LANES, T, LANES), lambda i: (0, i, 0)),
                   _lane_spec(D_MLSTM, T), _lane_spec(D_MLSTM, T), _lane_spec(SUBLANES, T),
                   wo_spec, w1_spec, w2_spec],
        out_shape=[jax.ShapeDtypeStruct((S, D_QK), F32),
                   jax.ShapeDtypeStruct((D_SSM // LANES, S, LANES), F32),
                   jax.ShapeDtypeStruct((D_MLSTM, S), F32),
                   jax.ShapeDtypeStruct((D_MLSTM, S), F32),
                   jax.ShapeDtypeStruct((SUBLANES, S), F32),
                   jax.ShapeDtypeStruct(wo.shape, BF16),
                   jax.ShapeDtypeStruct(w1.shape, BF16),
                   jax.ShapeDtypeStruct(w2.shape, BF16)],
        compiler_params=_params(),
        name="proj",
    )(x, g, w, wt, wg, wo, w1, w2)


def _log_sigmoid(x):
    return jnp.minimum(x, 0.0) - jnp.log1p(jnp.exp(-jnp.abs(x)))


def _interleave(*stages):
    total = max(n for _, n in stages)
    done = [0] * len(stages)
    for tick in range(1, total + 1):
        for s, (gen, n) in enumerate(stages):
            want = -(-tick * n // total)
            while done[s] < want:
                next(gen, None)
                done[s] += 1
    for gen, _ in stages:
        for _ in gen:
            pass


_LOG2E = math.log2(math.e)
_CUMMAX_LEVELS = LANES.bit_length() - 1
_GATE_PIECES = 3 + _CUMMAX_LEVELS
_QK_BLOCKS = 2 * D_MLSTM // LANES


def _conv_pieces(qk_ref, cw_ref, cb_ref, xpad_ref, qks_ref):
    T = MLSTM_ROWS
    L = MLSTM_CHUNK
    for k in range(_QK_BLOCKS):
        xpad_ref[k, SUBLANES:SUBLANES + T, :] = qk_ref[:, k * LANES:(k + 1) * LANES]
    yield
    for c in range(T // L):
        for k in range(_QK_BLOCKS):
            cols = slice(k * LANES, (k + 1) * LANES)
            for parity in range(2):
                acc = cb_ref[:, cols]
                for j in range(CONV_WIDTH):
                    start = c * L + SUBLANES - (CONV_WIDTH - 1) + parity + j
                    acc = acc + (xpad_ref[k, pl.ds(start, L // 2, stride=2), :]
                                 * cw_ref[j:j + 1, cols])
                act = acc * _sigmoid(acc)
                if k < _QK_BLOCKS // 2:
                    act = act * (HEAD_DIM ** -0.5)
                qks_ref[k, pl.ds(c * L + parity, L // 2, stride=2), :] = act
            yield
    for k in range(_QK_BLOCKS):
        xpad_ref[k, 0:SUBLANES, :] = xpad_ref[k, T:T + SUBLANES, :]


def _gate_pieces(gr_ref, br_ref, m_ref, out):
    T = MLSTM_ROWS
    L = MLSTM_CHUNK
    n_chunks = T // L
    lane = lax.broadcasted_iota(jnp.int32, (SUBLANES, L), 1)
    row_id = lax.broadcasted_iota(jnp.int32, (L, L), 0)
    col_id = lax.broadcasted_iota(jnp.int32, (L, L), 1)
    triu = (row_id <= col_id).astype(BF16)
    g = gr_ref[...] + br_ref[...]
    logf = _log_sigmoid(g)
    yield
    lf = jnp.concatenate([pltpu.roll(logf[:, c * L:(c + 1) * L], N_HEADS, axis=0)
                          for c in range(n_chunks)], axis=0)
    hi = lf.astype(BF16)
    rest = lf - hi.astype(F32)
    mid = rest.astype(BF16)
    lo = (rest - mid.astype(F32)).astype(BF16)
    terms = jnp.dot(jnp.concatenate([hi, mid, lo], axis=0), triu,
                    preferred_element_type=F32)
    n_rows = n_chunks * SUBLANES
    b_all = terms[0:n_rows] + terms[n_rows:2 * n_rows] + terms[2 * n_rows:3 * n_rows]
    bs = [b_all[c * SUBLANES:(c + 1) * SUBLANES] for c in range(n_chunks)]
    rs = [g[:, c * L:(c + 1) * L] - bs[c] for c in range(n_chunks)]
    yield
    m_runs = list(rs)
    for level in range(_CUMMAX_LEVELS):
        d = 1 << level
        m_runs = [jnp.where(lane >= d, jnp.maximum(x, pltpu.roll(x, d, axis=1)), x)
                  for x in m_runs]
        yield
    m = m_ref[...]
    rows = []
    for c in range(n_chunks):
        b, r, m_run = bs[c], rs[c], m_runs[c]
        m_last = jnp.broadcast_to(m_run[:, L - 1:L], (SUBLANES, L))
        b_last = jnp.broadcast_to(b[:, L - 1:L], (SUBLANES, L))
        mu = jnp.maximum(m_run, m)
        mx = jnp.maximum(m, m_last)
        rows.append(dict(
            mu_log2=mu * _LOG2E,
            inter_scale=jnp.exp(m - mu),
            eb=jnp.exp(-b - mu),
            w_new=jnp.exp(r - mx),
            s_old=jnp.exp(m - mx)))
        m = b_last + mx
    m_ref[...] = m
    r_pad = [jnp.zeros((SUBLANES, L), F32)] * (LANES // SUBLANES - n_chunks)
    out["rows"] = rows
    out["r_cols"] = (jnp.concatenate(rs + r_pad, axis=0) * _LOG2E).T


_GATE_ROWS = ("mu_log2", "inter_scale", "eb", "w_new", "s_old")


def _store_gates(gates, grow_ref, rcol_ref):
    for c, chunk in enumerate(gates["rows"]):
        for q, name in enumerate(_GATE_ROWS):
            grow_ref[c * len(_GATE_ROWS) + q] = chunk[name]
    rcol_ref[...] = gates["r_cols"]


def _head_pieces(qks_ref, vt_ref, ot_ref, nw_ref, grow_ref, rcol_ref, c_ref, out_ref):
    L = MLSTM_CHUNK
    D = HEAD_DIM
    row_id = lax.broadcasted_iota(jnp.int32, (L, L), 0)
    col_id = lax.broadcasted_iota(jnp.int32, (L, L), 1)
    causal = row_id <= col_id
    neg_inf = jnp.float32(-jnp.inf)
    for c in range(MLSTM_ROWS // L):
        for h in range(N_HEADS):
            row = lambda name: grow_ref[c * len(_GATE_ROWS) + _GATE_ROWS.index(name),
                                        h:h + 1, :]
            q = qks_ref[h, c * L:(c + 1) * L, :].astype(BF16)
            k = qks_ref[N_HEADS + h, c * L:(c + 1) * L, :].astype(BF16)
            vt = vt_ref[h * D:(h + 1) * D, c * L:(c + 1) * L]
            r_col = rcol_ref[:, c * SUBLANES + h:c * SUBLANES + h + 1]

            dmat = jnp.exp2(jnp.where(causal, r_col - row("mu_log2"), neg_inf))
            s_kq = lax.dot_general(k, q, _NT_DIMS, preferred_element_type=F32) * dmat
            cq = lax.dot_general(c_ref[h].astype(BF16), q, _NT_DIMS,
                                 preferred_element_type=F32)
            isc = row("inter_scale")
            num = isc * cq[0:D, :] + jnp.dot(vt.astype(BF16), s_kq.astype(BF16),
                                             preferred_element_type=F32)
            den = isc * cq[D:D + 1, :] + jnp.sum(s_kq, axis=0, keepdims=True)
            hh = num * (1.0 / jnp.maximum(jnp.abs(den), row("eb")))

            w = row("w_new")
            vw = jnp.concatenate([vt * w, jnp.broadcast_to(w, (SUBLANES, L))],
                                 axis=0).astype(BF16)
            c_loc = jnp.dot(vw, k, preferred_element_type=F32)
            c_ref[h] = row("s_old") * c_ref[h] + c_loc

            mean = jnp.mean(hh, axis=0, keepdims=True)
            cen = hh - mean
            var = jnp.mean(cen * cen, axis=0, keepdims=True)
            hn = cen * lax.rsqrt(var + EPS) * nw_ref[h * D:(h + 1) * D, :]
            gate = _sigmoid(ot_ref[h * D:(h + 1) * D, c * L:(c + 1) * L])
            out_ref[h * D:(h + 1) * D, c * L:(c + 1) * L] = (hn * gate).astype(BF16)
            yield


def _mlstm_kernel(qk_ref, vt_ref, ot_ref, gr_ref, gr_next_ref, cw_ref, cb_ref, br_ref, nw_ref,
                  out_ref, xpad_ref, qks_ref, c_ref, m_ref, grow_ref, rcol_ref):
    n_chunks = MLSTM_ROWS // MLSTM_CHUNK

    @pl.when(pl.program_id(0) == 0)
    def _():
        xpad_ref[:, 0:SUBLANES, :] = jnp.zeros((_QK_BLOCKS, SUBLANES, LANES), F32)
        c_ref[...] = jnp.zeros(c_ref.shape, F32)
        m_ref[...] = jnp.zeros(m_ref.shape, F32)
        first = {}
        for _ in _gate_pieces(gr_ref, br_ref, m_ref, first):
            pass
        _store_gates(first, grow_ref, rcol_ref)

    for _ in _conv_pieces(qk_ref, cw_ref, cb_ref, xpad_ref, qks_ref):
        pass
    nxt = {}
    _interleave((_gate_pieces(gr_next_ref, br_ref, m_ref, nxt), _GATE_PIECES),
                (_head_pieces(qks_ref, vt_ref, ot_ref, nw_ref, grow_ref, rcol_ref, c_ref,
                              out_ref), n_chunks * N_HEADS))
    _store_gates(nxt, grow_ref, rcol_ref)


def _mlstm_call(qk, vt, ot, gr, cw, cb, bias_r, nw_cols):
    S = qk.shape[0]
    T = MLSTM_ROWS
    n = S // T
    n_chunks = T // MLSTM_CHUNK
    return pl.pallas_call(
        _mlstm_kernel,
        grid=(n,),
        in_specs=[_row_spec(T, D_QK), _lane_spec(D_MLSTM, T), _lane_spec(D_MLSTM, T),
                  _lane_spec(SUBLANES, T),
                  pl.BlockSpec((SUBLANES, T), lambda i: (0, jnp.minimum(i + 1, n - 1))),
                  _const_spec(cw.shape), _const_spec(cb.shape),
                  _const_spec(bias_r.shape), _const_spec(nw_cols.shape)],
        out_specs=_lane_spec(D_MLSTM, T),
        out_shape=jax.ShapeDtypeStruct((D_MLSTM, S), BF16),
        scratch_shapes=[pltpu.VMEM((_QK_BLOCKS, T + SUBLANES, LANES), F32),
                        pltpu.VMEM((_QK_BLOCKS, T, LANES), F32),
                        pltpu.VMEM((N_HEADS, HEAD_DIM + SUBLANES, HEAD_DIM), F32),
                        pltpu.VMEM((SUBLANES, LANES), F32),
                        pltpu.VMEM((n_chunks * len(_GATE_ROWS), SUBLANES, MLSTM_CHUNK), F32),
                        pltpu.VMEM((MLSTM_CHUNK, LANES), F32)],
        compiler_params=_params(),
        name="mlstm",
    )(qk, vt, ot, gr, gr, cw, cb, bias_r, nw_cols)


def _gelu_tanh(x):
    c = math.sqrt(2.0 / math.pi)
    return x * (0.5 * (1.0 + jnp.tanh(c * (x + 0.044715 * (x * x * x)))))


def _s5_kernel(u_ref, an_ref, wb_ref, wc_ref, wd_ref, d_ref, gw_ref, gb_ref, out_ref,
               x_ref, y_ref, *z_refs):
    T = S5_ROWS
    N = S5_STEPS

    @pl.when(pl.program_id(0) == 0)
    def _():
        x_ref[...] = jnp.zeros(x_ref.shape, F32)

    R = S5_SUB
    P = R // N
    n_sub = T // R
    half = STATE_BLOCKS // 2
    n_blocks = D_SSM // LANES
    a_re = [an_ref[:, c * LANES:(c + 1) * LANES] for c in range(half)]
    a_im = [an_ref[:, (half + c) * LANES:(half + c + 1) * LANES] for c in range(half)]

    def group_rows(sb, b):
        return jnp.concatenate([u_ref[b, pl.ds(sb * R + m, P, stride=N), :]
                                for m in range(N)], axis=1).astype(BF16)

    def expand_piece(sb, j):
        w = jnp.dot(group_rows(sb, j // 2), wb_ref[j], preferred_element_type=F32)
        for c in range(STATE_BLOCKS):
            z_refs[sb][c // 2, pl.ds(j * S5_PITCH + c % 2, P, stride=2), :] = (
                w[:, c * LANES:(c + 1) * LANES])

    def scan_piece(sb, x, i0, i1):
        z_ref = z_refs[sb]
        for i in range(i0, i1):
            tile = lambda c: (c // 2, pl.ds(2 * i + c % 2, SUBLANES, stride=S5_PITCH),
                              slice(None))
            new = list(x)
            for c in range(half):
                x_re, x_im = x[c], x[half + c]
                new[c] = a_re[c] * x_re - a_im[c] * x_im + z_ref[tile(c)]
                new[half + c] = a_re[c] * x_im + a_im[c] * x_re + z_ref[tile(half + c)]
                z_ref[tile(c)] = x_re
                z_ref[tile(half + c)] = x_im
            x = new
        return x

    def contract_piece(sb, j):
        xs = jnp.concatenate(
            [z_refs[sb][c // 2, pl.ds(j * S5_PITCH + c % 2, P, stride=2), :]
             for c in range(STATE_BLOCKS)], axis=1).astype(BF16)
        return jnp.dot(xs, wc_ref[j], preferred_element_type=F32)

    def finish(sb, parts):
        for b in range(n_blocks):
            yb = (parts[2 * b] + parts[2 * b + 1]
                  + jnp.dot(group_rows(sb, b), wd_ref[b], preferred_element_type=F32))
            for m in range(N):
                y_ref[b, pl.ds(sb * R + m, P, stride=N), :] = yb[:, m * LANES:(m + 1) * LANES]
        rows = slice(sb * R, (sb + 1) * R)
        y = jnp.concatenate([y_ref[b, rows, :] for b in range(n_blocks)], axis=1)
        u = jnp.concatenate([u_ref[b, rows, :] for b in range(n_blocks)], axis=1)
        z = _gelu_tanh(y + d_ref[...] * u)
        gate = _sigmoid(jnp.dot(z.astype(BF16), gw_ref[...],
                                preferred_element_type=F32) + gb_ref[...])
        out_ref[rows, :] = (z * gate).astype(BF16)

    x = [x_ref[:, c * LANES:(c + 1) * LANES] for c in range(STATE_BLOCKS)]
    steps = P // SUBLANES
    for j in range(SUBLANES):
        expand_piece(0, j)
    for sb in range(n_sub + 1):
        parts = []
        for j in range(SUBLANES):
            if sb < n_sub:
                x = scan_piece(sb, x, j * steps, (j + 1) * steps)
            if sb + 1 < n_sub:
                expand_piece(sb + 1, j)
            if sb >= 1:
                parts.append(contract_piece(sb - 1, j))
        if sb >= 1:
            finish(sb - 1, parts)
    for c in range(STATE_BLOCKS):
        x_ref[:, c * LANES:(c + 1) * LANES] = x[c]


def _s5_call(u, an_tile, wb, wc, wd, d, gw, gb):
    S = u.shape[1]
    T = S5_ROWS
    n_blocks = D_SSM // LANES
    return pl.pallas_call(
        _s5_kernel,
        grid=(S // T,),
        in_specs=[pl.BlockSpec((n_blocks, T, LANES), lambda i: (0, i, 0)),
                  _const_spec(an_tile.shape), _const_spec(wb.shape), _const_spec(wc.shape),
                  _const_spec(wd.shape), _const_spec(d.shape), _const_spec(gw.shape),
                  _const_spec(gb.shape)],
        out_specs=_row_spec(T, D_SSM),
        out_shape=jax.ShapeDtypeStruct((S, D_SSM), BF16),
        scratch_shapes=([pltpu.VMEM((SUBLANES, STATE_LANES), F32),
                         pltpu.VMEM((n_blocks, T, LANES), F32)]
                        + [pltpu.VMEM((STATE_BLOCKS // 2, SUBLANES * S5_PITCH, LANES), F32)]
                        * (T // S5_SUB)),
        compiler_params=_params(),
        name="s5",
    )(u, an_tile, wb, wc, wd, d, gw, gb)


def _s5_weights(lam_re, lam_im, log_dt, b_re, b_im, c_re, c_im):
    N = S5_STEPS
    cmul = lambda xr, xi, yr, yi: (xr * yr - xi * yi, xr * yi + xi * yr)
    lr, li = lam_re.astype(F32), lam_im.astype(F32)
    dt = jnp.exp(log_dt.astype(F32))[:, None]
    mag = jnp.exp(lr * dt)
    a_re, a_im = mag * jnp.cos(li * dt), mag * jnp.sin(li * dt)
    pw = [(jnp.ones_like(a_re), jnp.zeros_like(a_im))]
    for _ in range(N):
        pw.append(cmul(pw[-1][0], pw[-1][1], a_re, a_im))
    den = lr * lr + li * li
    coef_re = ((a_re - 1.0) * lr + a_im * li) / den
    coef_im = (a_im * lr - (a_re - 1.0) * li) / den
    bb = cmul(coef_re[..., None], coef_im[..., None],
              b_re.astype(F32), b_im.astype(F32))
    anb = [cmul(p[0][..., None], p[1][..., None], bb[0], bb[1]) for p in pw[:N]]
    cr, ci = c_re.astype(F32), c_im.astype(F32)
    can = [cmul(cr, ci, p[0][:, None, :], p[1][:, None, :]) for p in pw[1:]]

    J, Q, P, Hc = SUBLANES, GROUPS_PER_SUBLANE, SSM_STATE, SSM_GROUP
    an_tile = jnp.concatenate([pw[N][0].reshape(J, Q * P), pw[N][1].reshape(J, Q * P)], axis=1)
    eye = jnp.eye(Q, dtype=F32)
    half = (jnp.arange(J) % 2)[:, None, None, None] == jnp.arange(2)[None, :, None, None]

    def in_weights(w):
        def blocks(w):
            return jnp.einsum('jqpc,qr->jqcrp', w.reshape(J, Q, P, Hc),
                              eye).reshape(J, Q * Hc, Q * P)
        w = jnp.concatenate([blocks(w[0]), blocks(w[1])], axis=2)
        return jnp.where(half, w[:, None], 0.0).reshape(J, 2 * Q * Hc, STATE_LANES)

    def out_weights(w):
        def blocks(w):
            return jnp.einsum('jqcp,qr->jqprc', w.reshape(J, Q, Hc, P),
                              eye).reshape(J, Q * P, Q * Hc)
        w = jnp.concatenate([blocks(w[0]), -blocks(w[1])], axis=1)
        half_c = jnp.swapaxes(half, 1, 2)
        return jnp.where(half_c, w[:, :, None, :], 0.0).reshape(J, STATE_LANES, 2 * Q * Hc)

    wb = jnp.concatenate([in_weights(anb[N - 1 - m]) for m in range(N)], axis=1)
    wc = jnp.concatenate([out_weights(can[m]) for m in range(N)], axis=2)
    kn = [jnp.einsum('gop,gpc->goc', cr, w[0]) - jnp.einsum('gop,gpc->goc', ci, w[1])
          for w in anb]
    n_blocks, per = D_SSM // LANES, LANES // Hc
    eye8 = jnp.eye(per, dtype=F32)

    def lane_block(k):
        return jnp.einsum('bgoc,gh->bgcho', k.reshape(n_blocks, per, Hc, Hc),
                          eye8).reshape(n_blocks, LANES, LANES)

    zero = jnp.zeros((n_blocks, LANES, LANES), F32)
    wd = jnp.concatenate(
        [jnp.concatenate([lane_block(kn[m - l]) if m >= l else zero for m in range(N)], axis=2)
         for l in range(N)], axis=1)
    return an_tile, wb.astype(BF16), wc.astype(BF16), wd.astype(BF16)


def _ffn_kernel(x_ref, hm_ref, y_ref, wo_ref, g2_ref, w1_ref, w2_ref, g3_ref, out_ref):
    mixed = (lax.dot_general(hm_ref[...], wo_ref[0:D_MLSTM, :], (((0,), (0,)), ((), ())),
                             preferred_element_type=F32)
             + jnp.dot(y_ref[...], wo_ref[D_MLSTM:D_MODEL, :], preferred_element_type=F32))
    x1 = x_ref[...] + mixed
    h2 = _rms(x1, g2_ref[...]).astype(BF16)
    mlp = None
    for c in range(D_FF // FF_CHUNK):
        a = jnp.dot(h2, w1_ref[:, c * FF_CHUNK:(c + 1) * FF_CHUNK],
                    preferred_element_type=F32)
        a = jnp.maximum(a, 0.0)
        part = jnp.dot((a * a).astype(BF16), w2_ref[c * FF_CHUNK:(c + 1) * FF_CHUNK, :],
                       preferred_element_type=F32)
        mlp = part if mlp is None else mlp + part
    out_ref[...] = _rms(x1 + mlp, g3_ref[...])


def _ffn_call(x, hm, y, wo, g2, w1, w2, g3):
    S = x.shape[0]
    T = FFN_ROWS
    single = dict(pipeline_mode=pl.Buffered(WEIGHT_BUFFERS))
    return pl.pallas_call(
        _ffn_kernel,
        grid=(S // T,),
        in_specs=[_row_spec(T, D_MODEL), _lane_spec(D_MLSTM, T), _row_spec(T, D_SSM),
                  pl.BlockSpec(wo.shape, lambda i: (0, 0), **single),
                  _const_spec(g2.shape),
                  pl.BlockSpec(w1.shape, lambda i: (0, 0), **single),
                  pl.BlockSpec(w2.shape, lambda i: (0, 0), **single),
                  _const_spec(g3.shape)],
        out_specs=_row_spec(T, D_MODEL),
        out_shape=jax.ShapeDtypeStruct((S, D_MODEL), F32),
        compiler_params=_params(),
        name="ffn",
    )(x, hm, y, wo, g2, w1, w2, g3)


def _layer(x, mix_norm_w, w_in, conv_w, conv_b, i_bias, f_bias, mlstm_norm_w,
           lam_re, lam_im, log_dt, b_re, b_im, c_re, c_im, ssm_d,
           glu_w, glu_b, w_out, mlp_norm_w, w_ff1, w_ff2, out_norm_w):
    assert x.shape[1] == D_MODEL
    assert x.shape[0] % max(PROJ_ROWS, MLSTM_ROWS, S5_ROWS, FFN_ROWS) == 0
    n_qk, n_vo, n_gate = 2 * D_MLSTM, 2 * D_MLSTM, 2 * N_HEADS
    w_rows = jnp.concatenate([w_in[:, :n_qk], w_in[:, n_qk + n_vo + n_gate:]],
                             axis=1).astype(BF16)
    w_cols = w_in[:, n_qk:n_qk + n_vo].T.astype(BF16)
    wg = w_in[:, n_qk + n_vo:n_qk + n_vo + n_gate].T.astype(BF16)
    qk, u, vt, ot, gr, wo_bf, w1_bf, w2_bf = _proj_call(
        x, mix_norm_w[None, :], w_rows, w_cols, wg, w_out, w_ff1, w_ff2)

    bias_r = jnp.concatenate([i_bias, f_bias]).astype(F32)[:, None]
    nw_cols = jnp.broadcast_to(mlstm_norm_w.astype(F32)[:, None], (D_MLSTM, LANES))
    hm = _mlstm_call(qk, vt, ot, gr, conv_w, conv_b[None, :], bias_r, nw_cols)

    an_tile, wb, wc, wd = _s5_weights(lam_re, lam_im, log_dt, b_re, b_im, c_re, c_im)
    y = _s5_call(u, an_tile, wb, wc, wd, ssm_d[None, :], glu_w.astype(BF16),
                 glu_b[None, :])

    return _ffn_call(x, hm, y, wo_bf, mlp_norm_w[None, :], w1_bf, w2_bf,
                     out_norm_w[None, :])


def kernel(x, mix_norm_w, w_in, conv_w, conv_b, i_bias, f_bias, mlstm_norm_w, ssm_lam_re, ssm_lam_im, ssm_log_dt, ssm_b_re, ssm_b_im, ssm_c_re, ssm_c_im, ssm_d, glu_w, glu_b, w_out, mlp_norm_w, w_ff1, w_ff2, final_norm_w):
    assert x.shape[0] == 1 and mix_norm_w.shape[0] == 1
    xs = x[0]
    out = _layer(xs, mix_norm_w[0], w_in[0], conv_w[0], conv_b[0], i_bias[0], f_bias[0],
                 mlstm_norm_w[0], ssm_lam_re[0], ssm_lam_im[0], ssm_log_dt[0],
                 ssm_b_re[0], ssm_b_im[0], ssm_c_re[0], ssm_c_im[0], ssm_d[0],
                 glu_w[0], glu_b[0], w_out[0], mlp_norm_w[0], w_ff1[0], w_ff2[0],
                 final_norm_w)
    return out[None]
```

```python
import math

import jax
import jax.numpy as jnp
from jax import lax
from jax.experimental import pallas as pl
from jax.experimental.pallas import tpu as pltpu

F32 = jnp.float32
BF16 = jnp.bfloat16

D_MODEL = 1024
D_MLSTM = 512
D_QK = 2 * D_MLSTM
N_HEADS = 4
HEAD_DIM = 128
CONV_WIDTH = 4
D_SSM = 512
SSM_GROUP = 16
N_GROUPS = 32
SSM_STATE = 64
D_FF = 4096
EPS = 1e-6

SUBLANES = 8
LANES = 128

PROJ_ROWS = 1024
MLSTM_ROWS = 1024
MLSTM_CHUNK = 128
S5_ROWS = 1024
S5_SUB = 512
S5_STEPS = 2
S5_PITCH = 2 * S5_SUB // S5_STEPS + 4
FFN_ROWS = 1024
FF_CHUNK = 1024
WEIGHT_BUFFERS = 1
VMEM_LIMIT = 56 * 1024 * 1024

GROUPS_PER_SUBLANE = N_GROUPS // SUBLANES
STATE_HALF = GROUPS_PER_SUBLANE * SSM_STATE
STATE_LANES = 2 * STATE_HALF
STATE_BLOCKS = STATE_LANES // LANES


def _const_spec(shape):
    return pl.BlockSpec(shape, lambda i: (0,) * len(shape))


def _row_spec(rows, cols):
    return pl.BlockSpec((rows, cols), lambda i: (i, 0))


def _lane_spec(rows, cols):
    return pl.BlockSpec((rows, cols), lambda i: (0, i))


def _params():
    return pltpu.CompilerParams(dimension_semantics=("arbitrary",),
                                vmem_limit_bytes=VMEM_LIMIT)


def _rms(x, g):
    r = lax.rsqrt(jnp.mean(x * x, axis=-1, keepdims=True) + EPS)
    return x * r * g


def _sigmoid(x):
    return 0.5 * jnp.tanh(0.5 * x) + 0.5


_NT_DIMS = (((1,), (1,)), ((), ()))


def _proj_kernel(x_ref, g_ref, w_ref, wt_ref, wg_ref, wo_ref, w1_ref, w2_ref,
                 qk_ref, u_ref, vt_ref, ot_ref, gr_ref, wo_bf_ref, w1_bf_ref, w2_bf_ref):
    wo_bf_ref[...] = wo_ref[...].astype(BF16)
    w1_bf_ref[...] = w1_ref[...].astype(BF16)
    w2_bf_ref[...] = w2_ref[...].astype(BF16)

    h = _rms(x_ref[...], g_ref[...]).astype(BF16)
    qk_ref[...] = jnp.dot(h, w_ref[:, 0:D_QK], preferred_element_type=F32)
    u = jnp.dot(h, w_ref[:, D_QK:D_QK + D_SSM], preferred_element_type=F32)
    for b in range(D_SSM // LANES):
        u_ref[b] = u[:, b * LANES:(b + 1) * LANES]
    vt_ref[...] = lax.dot_general(wt_ref[0:D_MLSTM, :], h, _NT_DIMS,
                                  preferred_element_type=F32)
    ot_ref[...] = lax.dot_general(wt_ref[D_MLSTM:2 * D_MLSTM, :], h, _NT_DIMS,
                                  preferred_element_type=F32)
    gr_ref[...] = lax.dot_general(wg_ref[...], h, _NT_DIMS, preferred_element_type=F32)


def _proj_call(x, g, w, wt, wg, wo, w1, w2):
    S = x.shape[0]
    T = PROJ_ROWS
    n = S // T
    wo_spec = _row_spec(wo.shape[0] // n, wo.shape[1])
    w1_spec = pl.BlockSpec((w1.shape[0], w1.shape[1] // n), lambda i: (0, i))
    w2_spec = _row_spec(w2.shape[0] // n, w2.shape[1])
    return pl.pallas_call(
        _proj_kernel,
        grid=(n,),
        in_specs=[_row_spec(T, D_MODEL), _const_spec((1, D_MODEL)),
                  _const_spec(w.shape), _const_spec(wt.shape), _const_spec(wg.shape),
                  wo_spec, w1_spec, w2_spec],
        out_specs=[_row_spec(T, D_QK),
                   pl.BlockSpec((D_SSM // LANES, T, LANES), lambda i: (0, i, 0)),
                   _lane_spec(D_MLSTM, T), _lane_spec(D_MLSTM, T), _lane_spec(SUBLANES, T),
                   wo_spec, w1_spec, w2_spec],
        out_shape=[jax.ShapeDtypeStruct((S, D_QK), F32),
                   jax.ShapeDtypeStruct((D_SSM // LANES, S, LANES), F32),
                   jax.ShapeDtypeStruct((D_MLSTM, S), F32),
                   jax.ShapeDtypeStruct((D_MLSTM, S), F32),
                   jax.ShapeDtypeStruct((SUBLANES, S), F32),
                   jax.ShapeDtypeStruct(wo.shape, BF16),
                   jax.ShapeDtypeStruct(w1.shape, BF16),
                   jax.ShapeDtypeStruct(w2.shape, BF16)],
        compiler_params=_params(),
        name="proj",
    )(x, g, w, wt, wg, wo, w1, w2)


def _log_sigmoid(x):
    return jnp.minimum(x, 0.0) - jnp.log1p(jnp.exp(-jnp.abs(x)))


def _interleave(*stages):
    total = max(n for _, n in stages)
    done = [0] * len(stages)
    for tick in range(1, total + 1):
        for s, (gen, n) in enumerate(stages):
            want = -(-tick * n // total)
            while done[s] < want:
                next(gen, None)
                done[s] += 1
    for gen, _ in stages:
        for _ in gen:
            pass


_LOG2E = math.log2(math.e)
_CUMMAX_LEVELS = LANES.bit_length() - 1
_GATE_PIECES = 3 + _CUMMAX_LEVELS
_QK_BLOCKS = 2 * D_MLSTM // LANES


def _conv_pieces(qk_ref, cw_ref, cb_ref, xpad_ref, qks_ref):
    T = MLSTM_ROWS
    L = MLSTM_CHUNK
    for k in range(_QK_BLOCKS):
        xpad_ref[k, SUBLANES:SUBLANES + T, :] = qk_ref[:, k * LANES:(k + 1) * LANES]
    yield
    for c in range(T // L):
        for k in range(_QK_BLOCKS):
            cols = slice(k * LANES, (k + 1) * LANES)
            for parity in range(2):
                acc = cb_ref[:, cols]
                for j in range(CONV_WIDTH):
                    start = c * L + SUBLANES - (CONV_WIDTH - 1) + parity + j
                    acc = acc + (xpad_ref[k, pl.ds(start, L // 2, stride=2), :]
                                 * cw_ref[j:j + 1, cols])
                act = acc * _sigmoid(acc)
                if k < _QK_BLOCKS // 2:
                    act = act * (HEAD_DIM ** -0.5)
                qks_ref[k, pl.ds(c * L + parity, L // 2, stride=2), :] = act
            yield
    for k in range(_QK_BLOCKS):
        xpad_ref[k, 0:SUBLANES, :] = xpad_ref[k, T:T + SUBLANES, :]


def _gate_pieces(gr_ref, br_ref, m_ref, out):
    T = MLSTM_ROWS
    L = MLSTM_CHUNK
    n_chunks = T // L
    lane = lax.broadcasted_iota(jnp.int32, (SUBLANES, L), 1)
    row_id = lax.broadcasted_iota(jnp.int32, (L, L), 0)
    col_id = lax.broadcasted_iota(jnp.int32, (L, L), 1)
    triu = (row_id <= col_id).astype(BF16)
    g = gr_ref[...] + br_ref[...]
    logf = _log_sigmoid(g)
    yield
    lf = jnp.concatenate([pltpu.roll(logf[:, c * L:(c + 1) * L], N_HEADS, axis=0)
                          for c in range(n_chunks)], axis=0)
    hi = lf.astype(BF16)
    rest = lf - hi.astype(F32)
    mid = rest.astype(BF16)
    lo = (rest - mid.astype(F32)).astype(BF16)
    terms = jnp.dot(jnp.concatenate([hi, mid, lo], axis=0), triu,
                    preferred_element_type=F32)
    n_rows = n_chunks * SUBLANES
    b_all = terms[0:n_rows] + terms[n_rows:2 * n_rows] + terms[2 * n_rows:3 * n_rows]
    bs = [b_all[c * SUBLANES:(c + 1) * SUBLANES] for c in range(n_chunks)]
    rs = [g[:, c * L:(c + 1) * L] - bs[c] for c in range(n_chunks)]
    yield
    m_runs = list(rs)
    for level in range(_CUMMAX_LEVELS):
        d = 1 << level
        m_runs = [jnp.where(lane >= d, jnp.maximum(x, pltpu.roll(x, d, axis=1)), x)
                  for x in m_runs]
        yield
    m = m_ref[...]
    rows = []
    for c in range(n_chunks):
        b, r, m_run = bs[c], rs[c], m_runs[c]
        m_last = jnp.broadcast_to(m_run[:, L - 1:L], (SUBLANES, L))
        b_last = jnp.broadcast_to(b[:, L - 1:L], (SUBLANES, L))
        mu = jnp.maximum(m_run, m)
        mx = jnp.maximum(m, m_last)
        rows.append(dict(
            mu_log2=mu * _LOG2E,
            inter_scale=jnp.exp(m - mu),
            eb=jnp.exp(-b - mu),
            w_new=jnp.exp(r - mx),
            s_old=jnp.exp(m - mx)))
        m = b_last + mx
    m_ref[...] = m
    r_pad = [jnp.zeros((SUBLANES, L), F32)] * (LANES // SUBLANES - n_chunks)
    out["rows"] = rows
    out["r_cols"] = (jnp.concatenate(rs + r_pad, axis=0) * _LOG2E).T


_GATE_ROWS = ("mu_log2", "inter_scale", "eb", "w_new", "s_old")


def _store_gates(gates, grow_ref, rcol_ref):
    for c, chunk in enumerate(gates["rows"]):
        for q, name in enumerate(_GATE_ROWS):
            grow_ref[c * len(_GATE_ROWS) + q] = chunk[name]
    rcol_ref[...] = gates["r_cols"]


def _head_pieces(qks_ref, vt_ref, ot_ref, nw_ref, grow_ref, rcol_ref, c_ref, out_ref):
    L = MLSTM_CHUNK
    D = HEAD_DIM
    row_id = lax.broadcasted_iota(jnp.int32, (L, L), 0)
    col_id = lax.broadcasted_iota(jnp.int32, (L, L), 1)
    causal = row_id <= col_id
    neg_inf = jnp.float32(-jnp.inf)
    for c in range(MLSTM_ROWS // L):
        for h in range(N_HEADS):
            row = lambda name: grow_ref[c * len(_GATE_ROWS) + _GATE_ROWS.index(name),
                                        h:h + 1, :]
            q = qks_ref[h, c * L:(c + 1) * L, :].astype(BF16)
            k = qks_ref[N_HEADS + h, c * L:(c + 1) * L, :].astype(BF16)
            vt = vt_ref[h * D:(h + 1) * D, c * L:(c + 1) * L]
            r_col = rcol_ref[:, c * SUBLANES + h:c * SUBLANES + h + 1]

            dmat = jnp.exp2(jnp.where(causal, r_col - row("mu_log2"), neg_inf))
            s_kq = lax.dot_general(k, q, _NT_DIMS, preferred_element_type=F32) * dmat
            cq = lax.dot_general(c_ref[h].astype(BF16), q, _NT_DIMS,
                                 preferred_element_type=F32)
            isc = row("inter_scale")
            num = isc * cq[0:D, :] + jnp.dot(vt.astype(BF16), s_kq.astype(BF16),
                                             preferred_element_type=F32)
            den = isc * cq[D:D + 1, :] + jnp.sum(s_kq, axis=0, keepdims=True)
            hh = num * (1.0 / jnp.maximum(jnp.abs(den), row("eb")))

            w = row("w_new")
            vw = jnp.concatenate([vt * w, jnp.broadcast_to(w, (SUBLANES, L))],
                                 axis=0).astype(BF16)
            c_loc = jnp.dot(vw, k, preferred_element_type=F32)
            c_ref[h] = row("s_old") * c_ref[h] + c_loc

            mean = jnp.mean(hh, axis=0, keepdims=True)
            cen = hh - mean
            var = jnp.mean(cen * cen, axis=0, keepdims=True)
            hn = cen * lax.rsqrt(var + EPS) * nw_ref[h * D:(h + 1) * D, :]
            gate = _sigmoid(ot_ref[h * D:(h + 1) * D, c * L:(c + 1) * L])
            out_ref[h * D:(h + 1) * D, c * L:(c + 1) * L] = (hn * gate).astype(BF16)
            yield


def _mlstm_kernel(qk_ref, vt_ref, ot_ref, gr_ref, gr_next_ref, cw_ref, cb_ref, br_ref, nw_ref,
                  out_ref, xpad_ref, qks_ref, c_ref, m_ref, grow_ref, rcol_ref):
    n_chunks = MLSTM_ROWS // MLSTM_CHUNK

    @pl.when(pl.program_id(0) == 0)
    def _():
        xpad_ref[:, 0:SUBLANES, :] = jnp.zeros((_QK_BLOCKS, SUBLANES, LANES), F32)
        c_ref[...] = jnp.zeros(c_ref.shape, F32)
        m_ref[...] = jnp.zeros(m_ref.shape, F32)
        first = {}
        for _ in _gate_pieces(gr_ref, br_ref, m_ref, first):
            pass
        _store_gates(first, grow_ref, rcol_ref)

    for _ in _conv_pieces(qk_ref, cw_ref, cb_ref, xpad_ref, qks_ref):
        pass
    nxt = {}
    _interleave((_gate_pieces(gr_next_ref, br_ref, m_ref, nxt), _GATE_PIECES),
                (_head_pieces(qks_ref, vt_ref, ot_ref, nw_ref, grow_ref, rcol_ref, c_ref,
                              out_ref), n_chunks * N_HEADS))
    _store_gates(nxt, grow_ref, rcol_ref)


def _mlstm_call(qk, vt, ot, gr, cw, cb, bias_r, nw_cols):
    S = qk.shape[0]
    T = MLSTM_ROWS
    n = S // T
    n_chunks = T // MLSTM_CHUNK
    return pl.pallas_call(
        _mlstm_kernel,
        grid=(n,),
        in_specs=[_row_spec(T, D_QK), _lane_spec(D_MLSTM, T), _lane_spec(D_MLSTM, T),
                  _lane_spec(SUBLANES, T),
                  pl.BlockSpec((SUBLANES, T), lambda i: (0, jnp.minimum(i + 1, n - 1))),
                  _const_spec(cw.shape), _const_spec(cb.shape),
                  _const_spec(bias_r.shape), _const_spec(nw_cols.shape)],
        out_specs=_lane_spec(D_MLSTM, T),
        out_shape=jax.ShapeDtypeStruct((D_MLSTM, S), BF16),
        scratch_shapes=[pltpu.VMEM((_QK_BLOCKS, T + SUBLANES, LANES), F32),
                        pltpu.VMEM((_QK_BLOCKS, T, LANES), F32),
                        pltpu.VMEM((N_HEADS, HEAD_DIM + SUBLANES, HEAD_DIM), F32),
                        pltpu.VMEM((SUBLANES, LANES), F32),
                        pltpu.VMEM((n_chunks * len(_GATE_ROWS), SUBLANES, MLSTM_CHUNK), F32),
                        pltpu.VMEM((MLSTM_CHUNK, LANES), F32)],
        compiler_params=_params(),
        name="mlstm",
    )(qk, vt, ot, gr, gr, cw, cb, bias_r, nw_cols)


def _gelu_tanh(x):
    c = math.sqrt(2.0 / math.pi)
    return x * (0.5 * (1.0 + jnp.tanh(c * (x + 0.044715 * (x * x * x)))))


def _s5_kernel(u_ref, an_ref, wb_ref, wc_ref, wd_ref, d_ref, gw_ref, gb_ref, out_ref,
               x_ref, y_ref, *z_refs):
    T = S5_ROWS
    N = S5_STEPS

    @pl.when(pl.program_id(0) == 0)
    def _():
        x_ref[...] = jnp.zeros(x_ref.shape, F32)

    R = S5_SUB
    P = R // N
    n_sub = T // R
    half = STATE_BLOCKS // 2
    n_blocks = D_SSM // LANES
    a_re = [an_ref[:, c * LANES:(c + 1) * LANES] for c in range(half)]
    a_im = [an_ref[:, (half + c) * LANES:(half + c + 1) * LANES] for c in range(half)]

    def group_rows(sb, b):
        return jnp.concatenate([u_ref[b, pl.ds(sb * R + m, P, stride=N), :]
                                for m in range(N)], axis=1).astype(BF16)

    def expand_piece(sb, j):
        w = jnp.dot(group_rows(sb, j // 2), wb_ref[j], preferred_element_type=F32)
        for c in range(STATE_BLOCKS):
            z_refs[sb][c // 2, pl.ds(j * S5_PITCH + c % 2, P, stride=2), :] = (
                w[:, c * LANES:(c + 1) * LANES])

    def scan_piece(sb, x, i0, i1):
        z_ref = z_refs[sb]
        for i in range(i0, i1):
            tile = lambda c: (c // 2, pl.ds(2 * i + c % 2, SUBLANES, stride=S5_PITCH),
                              slice(None))
            new = list(x)
            for c in range(half):
                x_re, x_im = x[c], x[half + c]
                new[c] = a_re[c] * x_re - a_im[c] * x_im + z_ref[tile(c)]
                new[half + c] = a_re[c] * x_im + a_im[c] * x_re + z_ref[tile(half + c)]
                z_ref[tile(c)] = x_re
                z_ref[tile(half + c)] = x_im
            x = new
        return x

    def contract_piece(sb, j):
        xs = jnp.concatenate(
            [z_refs[sb][c // 2, pl.ds(j * S5_PITCH + c % 2, P, stride=2), :]
             for c in range(STATE_BLOCKS)], axis=1).astype(BF16)
        return jnp.dot(xs, wc_ref[j], preferred_element_type=F32)

    def finish(sb, parts):
        for b in range(n_blocks):
            yb = (parts[2 * b] + parts[2 * b + 1]
                  + jnp.dot(group_rows(sb, b), wd_ref[b], preferred_element_type=F32))
            for m in range(N):
                y_ref[b, pl.ds(sb * R + m, P, stride=N), :] = yb[:, m * LANES:(m + 1) * LANES]
        rows = slice(sb * R, (sb + 1) * R)
        y = jnp.concatenate([y_ref[b, rows, :] for b in range(n_blocks)], axis=1)
        u = jnp.concatenate([u_ref[b, rows, :] for b in range(n_blocks)], axis=1)
        z = _gelu_tanh(y + d_ref[...] * u)
        gate = _sigmoid(jnp.dot(z.astype(BF16), gw_ref[...],
                                preferred_element_type=F32) + gb_ref[...])
        out_ref[rows, :] = (z * gate).astype(BF16)

    x = [x_ref[:, c * LANES:(c + 1) * LANES] for c in range(STATE_BLOCKS)]
    steps = P // SUBLANES
    for j in range(SUBLANES):
        expand_piece(0, j)
    for sb in range(n_sub + 1):
        parts = []
        for j in range(SUBLANES):
            if sb < n_sub:
                x = scan_piece(sb, x, j * steps, (j + 1) * steps)
            if sb + 1 < n_sub:
                expand_piece(sb + 1, j)
            if sb >= 1:
                parts.append(contract_piece(sb - 1, j))
        if sb >= 1:
            finish(sb - 1, parts)
    for c in range(STATE_BLOCKS):
        x_ref[:, c * LANES:(c + 1) * LANES] = x[c]


def _s5_call(u, an_tile, wb, wc, wd, d, gw, gb):
    S = u.shape[1]
    T = S5_ROWS
    n_blocks = D_SSM // LANES
    return pl.pallas_call(
        _s5_kernel,
        grid=(S // T,),
        in_specs=[pl.BlockSpec((n_blocks, T, LANES), lambda i: (0, i, 0)),
                  _const_spec(an_tile.shape), _const_spec(wb.shape), _const_spec(wc.shape),
                  _const_spec(wd.shape), _const_spec(d.shape), _const_spec(gw.shape),
                  _const_spec(gb.shape)],
        out_specs=_row_spec(T, D_SSM),
        out_shape=jax.ShapeDtypeStruct((S, D_SSM), BF16),
        scratch_shapes=([pltpu.VMEM((SUBLANES, STATE_LANES), F32),
                         pltpu.VMEM((n_blocks, T, LANES), F32)]
                        + [pltpu.VMEM((STATE_BLOCKS // 2, SUBLANES * S5_PITCH, LANES), F32)]
                        * (T // S5_SUB)),
        compiler_params=_params(),
        name="s5",
    )(u, an_tile, wb, wc, wd, d, gw, gb)


def _s5_weights(lam_re, lam_im, log_dt, b_re, b_im, c_re, c_im):
    N = S5_STEPS
    cmul = lambda xr, xi, yr, yi: (xr * yr - xi * yi, xr * yi + xi * yr)
    lr, li = lam_re.astype(F32), lam_im.astype(F32)
    dt = jnp.exp(log_dt.astype(F32))[:, None]
    mag = jnp.exp(lr * dt)
    a_re, a_im = mag * jnp.cos(li * dt), mag * jnp.sin(li * dt)
    pw = [(jnp.ones_like(a_re), jnp.zeros_like(a_im))]
    for _ in range(N):
        pw.append(cmul(pw[-1][0], pw[-1][1], a_re, a_im))
    den = lr * lr + li * li
    coef_re = ((a_re - 1.0) * lr + a_im * li) / den
    coef_im = (a_im * lr - (a_re - 1.0) * li) / den
    bb = cmul(coef_re[..., None], coef_im[..., None],
              b_re.astype(F32), b_im.astype(F32))
    anb = [cmul(p[0][..., None], p[1][..., None], bb[0], bb[1]) for p in pw[:N]]
    cr, ci = c_re.astype(F32), c_im.astype(F32)
    can = [cmul(cr, ci, p[0][:, None, :], p[1][:, None, :]) for p in pw[1:]]

    J, Q, P, Hc = SUBLANES, GROUPS_PER_SUBLANE, SSM_STATE, SSM_GROUP
    an_tile = jnp.concatenate([pw[N][0].reshape(J, Q * P), pw[N][1].reshape(J, Q * P)], axis=1)
    eye = jnp.eye(Q, dtype=jnp.bool_)[None, :, None, :, None]
    half = (jnp.arange(J) % 2)[:, None, None, None] == jnp.arange(2)[None, :, None, None]

    def in_weights(w):
        def blocks(w):
            w = jnp.swapaxes(w.reshape(J, Q, P, Hc), 2, 3)
            return jnp.where(eye, w[:, :, :, None, :], 0.0).reshape(J, Q * Hc, Q * P)
        w = jnp.concatenate([blocks(w[0]), blocks(w[1])], axis=2)
        return jnp.where(half, w[:, None], 0.0).reshape(J, 2 * Q * Hc, STATE_LANES)

    def out_weights(w):
        def blocks(w):
            w = jnp.swapaxes(w.reshape(J, Q, Hc, P), 2, 3)
            return jnp.where(eye, w[:, :, :, None, :], 0.0).reshape(J, Q * P, Q * Hc)
        w = jnp.concatenate([blocks(w[0]), -blocks(w[1])], axis=1)
        half_c = jnp.swapaxes(half, 1, 2)
        return jnp.where(half_c, w[:, :, None, :], 0.0).reshape(J, STATE_LANES, 2 * Q * Hc)

    wb = jnp.concatenate([in_weights(anb[N - 1 - m]) for m in range(N)], axis=1)
    wc = jnp.concatenate([out_weights(can[m]) for m in range(N)], axis=2)
    kn = [jnp.einsum('gop,gpc->goc', cr, w[0]) - jnp.einsum('gop,gpc->goc', ci, w[1])
          for w in anb]
    n_blocks, per = D_SSM // LANES, LANES // Hc
    eye8 = jnp.eye(per, dtype=jnp.bool_)[None, :, None, :, None]

    def lane_block(k):
        k = jnp.swapaxes(k.reshape(n_blocks, per, Hc, Hc), 2, 3)
        return jnp.where(eye8, k[:, :, :, None, :], 0.0).reshape(n_blocks, LANES, LANES)

    zero = jnp.zeros((n_blocks, LANES, LANES), F32)
    wd = jnp.concatenate(
        [jnp.concatenate([lane_block(kn[m - l]) if m >= l else zero for m in range(N)], axis=2)
         for l in range(N)], axis=1)
    return an_tile, wb.astype(BF16), wc.astype(BF16), wd.astype(BF16)


def _ffn_kernel(x_ref, hm_ref, y_ref, wo_ref, g2_ref, w1_ref, w2_ref, g3_ref, out_ref):
    mixed = (lax.dot_general(hm_ref[...], wo_ref[0:D_MLSTM, :], (((0,), (0,)), ((), ())),
                             preferred_element_type=F32)
             + jnp.dot(y_ref[...], wo_ref[D_MLSTM:D_MODEL, :], preferred_element_type=F32))
    x1 = x_ref[...] + mixed
    h2 = _rms(x1, g2_ref[...]).astype(BF16)
    mlp = None
    for c in range(D_FF // FF_CHUNK):
        a = jnp.dot(h2, w1_ref[:, c * FF_CHUNK:(c + 1) * FF_CHUNK],
                    preferred_element_type=F32)
        a = jnp.maximum(a, 0.0)
        part = jnp.dot((a * a).astype(BF16), w2_ref[c * FF_CHUNK:(c + 1) * FF_CHUNK, :],
                       preferred_element_type=F32)
        mlp = part if mlp is None else mlp + part
    out_ref[...] = _rms(x1 + mlp, g3_ref[...])


def _ffn_call(x, hm, y, wo, g2, w1, w2, g3):
    S = x.shape[0]
    T = FFN_ROWS
    single = dict(pipeline_mode=pl.Buffered(WEIGHT_BUFFERS))
    return pl.pallas_call(
        _ffn_kernel,
        grid=(S // T,),
        in_specs=[_row_spec(T, D_MODEL), _lane_spec(D_MLSTM, T), _row_spec(T, D_SSM),
                  pl.BlockSpec(wo.shape, lambda i: (0, 0), **single),
                  _const_spec(g2.shape),
                  pl.BlockSpec(w1.shape, lambda i: (0, 0), **single),
                  pl.BlockSpec(w2.shape, lambda i: (0, 0), **single),
                  _const_spec(g3.shape)],
        out_specs=_row_spec(T, D_MODEL),
        out_shape=jax.ShapeDtypeStruct((S, D_MODEL), F32),
        compiler_params=_params(),
        name="ffn",
    )(x, hm, y, wo, g2, w1, w2, g3)


def _layer(x, mix_norm_w, w_in, conv_w, conv_b, i_bias, f_bias, mlstm_norm_w,
           lam_re, lam_im, log_dt, b_re, b_im, c_re, c_im, ssm_d,
           glu_w, glu_b, w_out, mlp_norm_w, w_ff1, w_ff2, out_norm_w):
    assert x.shape[1] == D_MODEL
    assert x.shape[0] % max(PROJ_ROWS, MLSTM_ROWS, S5_ROWS, FFN_ROWS) == 0
    n_qk, n_vo, n_gate = 2 * D_MLSTM, 2 * D_MLSTM, 2 * N_HEADS
    w_rows = jnp.concatenate([w_in[:, :n_qk], w_in[:, n_qk + n_vo + n_gate:]],
                             axis=1).astype(BF16)
    w_cols = w_in[:, n_qk:n_qk + n_vo].T.astype(BF16)
    wg = w_in[:, n_qk + n_vo:n_qk + n_vo + n_gate].T.astype(BF16)
    qk, u, vt, ot, gr, wo_bf, w1_bf, w2_bf = _proj_call(
        x, mix_norm_w[None, :], w_rows, w_cols, wg, w_out, w_ff1, w_ff2)

    bias_r = jnp.concatenate([i_bias, f_bias]).astype(F32)[:, None]
    nw_cols = jnp.broadcast_to(mlstm_norm_w.astype(F32)[:, None], (D_MLSTM, LANES))
    hm = _mlstm_call(qk, vt, ot, gr, conv_w, conv_b[None, :], bias_r, nw_cols)

    an_tile, wb, wc, wd = _s5_weights(lam_re, lam_im, log_dt, b_re, b_im, c_re, c_im)
    y = _s5_call(u, an_tile, wb, wc, wd, ssm_d[None, :], glu_w.astype(BF16),
                 glu_b[None, :])

    return _ffn_call(x, hm, y, wo_bf, mlp_norm_w[None, :], w1_bf, w2_bf,
                     out_norm_w[None, :])


def kernel(x, mix_norm_w, w_in, conv_w, conv_b, i_bias, f_bias, mlstm_norm_w, ssm_lam_re, ssm_lam_im, ssm_log_dt, ssm_b_re, ssm_b_im, ssm_c_re, ssm_c_im, ssm_d, glu_w, glu_b, w_out, mlp_norm_w, w_ff1, w_ff2, final_norm_w):
    assert x.shape[0] == 1 and mix_norm_w.shape[0] == 1
    xs = x[0]
    out = _layer(xs, mix_norm_w[0], w_in[0], conv_w[0], conv_b[0], i_bias[0], f_bias[0],
                 mlstm_norm_w[0], ssm_lam_re[0], ssm_lam_im[0], ssm_log_dt[0],
                 ssm_b_re[0], ssm_b_im[0], ssm_c_re[0], ssm_c_im[0], ssm_d[0],
                 glu_w[0], glu_b[0], w_out[0], mlp_norm_w[0], w_ff1[0], w_ff2[0],
                 final_norm_w)
    return out[None]
```

```python
import math

import jax
import jax.numpy as jnp
from jax import lax
from jax.experimental import pallas as pl
from jax.experimental.pallas import tpu as pltpu

F32 = jnp.float32
BF16 = jnp.bfloat16

D_MODEL = 1024
D_MLSTM = 512
D_QK = 2 * D_MLSTM
N_HEADS = 4
HEAD_DIM = 128
CONV_WIDTH = 4
D_SSM = 512
SSM_GROUP = 16
N_GROUPS = 32
SSM_STATE = 64
D_FF = 4096
EPS = 1e-6

SUBLANES = 8
LANES = 128

PROJ_ROWS = 1024
MLSTM_ROWS = 1024
MLSTM_CHUNK = 128
S5_ROWS = 1024
S5_SUB = 512
S5_STEPS = 2
S5_PITCH = 2 * S5_SUB // S5_STEPS + 4
FFN_ROWS = 1024
FF_CHUNK = 1024
WEIGHT_BUFFERS = 1
VMEM_LIMIT = 56 * 1024 * 1024

GROUPS_PER_SUBLANE = N_GROUPS // SUBLANES
STATE_HALF = GROUPS_PER_SUBLANE * SSM_STATE
STATE_LANES = 2 * STATE_HALF
STATE_BLOCKS = STATE_LANES // LANES


def _const_spec(shape):
    return pl.BlockSpec(shape, lambda i: (0,) * len(shape))


def _row_spec(rows, cols):
    return pl.BlockSpec((rows, cols), lambda i: (i, 0))


def _lane_spec(rows, cols):
    return pl.BlockSpec((rows, cols), lambda i: (0, i))


def _params():
    return pltpu.CompilerParams(dimension_semantics=("arbitrary",),
                                vmem_limit_bytes=VMEM_LIMIT)


def _rms(x, g):
    r = lax.rsqrt(jnp.mean(x * x, axis=-1, keepdims=True) + EPS)
    return x * r * g


def _sigmoid(x):
    return 0.5 * jnp.tanh(0.5 * x) + 0.5


_NT_DIMS = (((1,), (1,)), ((), ()))


def _proj_kernel(x_ref, g_ref, w_ref, wt_ref, wg_ref, wo_ref, w1_ref, w2_ref,
                 qk_ref, u_ref, vt_ref, ot_ref, gr_ref, wo_bf_ref, w1_bf_ref, w2_bf_ref):
    wo_bf_ref[...] = wo_ref[...].astype(BF16)
    w1_bf_ref[...] = w1_ref[...].astype(BF16)
    w2_bf_ref[...] = w2_ref[...].astype(BF16)

    h = _rms(x_ref[...], g_ref[...]).astype(BF16)
    qk_ref[...] = jnp.dot(h, w_ref[:, 0:D_QK], preferred_element_type=F32)
    u = jnp.dot(h, w_ref[:, D_QK:D_QK + D_SSM], preferred_element_type=F32)
    for b in range(D_SSM // LANES):
        u_ref[b] = u[:, b * LANES:(b + 1) * LANES]
    vt_ref[...] = lax.dot_general(wt_ref[0:D_MLSTM, :], h, _NT_DIMS,
                                  preferred_element_type=F32)
    ot_ref[...] = lax.dot_general(wt_ref[D_MLSTM:2 * D_MLSTM, :], h, _NT_DIMS,
                                  preferred_element_type=F32)
    gr_ref[...] = lax.dot_general(wg_ref[...], h, _NT_DIMS, preferred_element_type=F32)


def _proj_call(x, g, w, wt, wg, wo, w1, w2):
    S = x.shape[0]
    T = PROJ_ROWS
    n = S // T
    wo_spec = _row_spec(wo.shape[0] // n, wo.shape[1])
    w1_spec = pl.BlockSpec((w1.shape[0], w1.shape[1] // n), lambda i: (0, i))
    w2_spec = _row_spec(w2.shape[0] // n, w2.shape[1])
    return pl.pallas_call(
        _proj_kernel,
        grid=(n,),
        in_specs=[_row_spec(T, D_MODEL), _const_spec((1, D_MODEL)),
                  _const_spec(w.shape), _const_spec(wt.shape), _const_spec(wg.shape),
                  wo_spec, w1_spec, w2_spec],
        out_specs=[_row_spec(T, D_QK),
                   pl.BlockSpec((D_SSM // LANES, T, LANES), lambda i: (0, i, 0)),
                   _lane_spec(D_MLSTM, T), _lane_spec(D_MLSTM, T), _lane_spec(SUBLANES, T),
                   wo_spec, w1_spec, w2_spec],
        out_shape=[jax.ShapeDtypeStruct((S, D_QK), F32),
                   jax.ShapeDtypeStruct((D_SSM // LANES, S, LANES), F32),
                   jax.ShapeDtypeStruct((D_MLSTM, S), F32),
                   jax.ShapeDtypeStruct((D_MLSTM, S), F32),
                   jax.ShapeDtypeStruct((SUBLANES, S), F32),
                   jax.ShapeDtypeStruct(wo.shape, BF16),
                   jax.ShapeDtypeStruct(w1.shape, BF16),
                   jax.ShapeDtypeStruct(w2.shape, BF16)],
        compiler_params=_params(),
        name="proj",
    )(x, g, w, wt, wg, wo, w1, w2)


def _log_sigmoid(x):
    return jnp.minimum(x, 0.0) - jnp.log1p(jnp.exp(-jnp.abs(x)))


def _interleave(*stages):
    total = max(n for _, n in stages)
    done = [0] * len(stages)
    for tick in range(1, total + 1):
        for s, (gen, n) in enumerate(stages):
            want = -(-tick * n // total)
            while done[s] < want:
                next(gen, None)
                done[s] += 1
    for gen, _ in stages:
        for _ in gen:
            pass


_LOG2E = math.log2(math.e)
_CUMMAX_LEVELS = LANES.bit_length() - 1
_GATE_PIECES = 3 + _CUMMAX_LEVELS
_QK_BLOCKS = 2 * D_MLSTM // LANES


def _conv_pieces(qk_ref, cw_ref, cb_ref, xpad_ref, qks_ref):
    T = MLSTM_ROWS
    L = MLSTM_CHUNK
    for k in range(_QK_BLOCKS):
        xpad_ref[k, SUBLANES:SUBLANES + T, :] = qk_ref[:, k * LANES:(k + 1) * LANES]
    yield
    for c in range(T // L):
        for k in range(_QK_BLOCKS):
            cols = slice(k * LANES, (k + 1) * LANES)
            for parity in range(2):
                acc = cb_ref[:, cols]
                for j in range(CONV_WIDTH):
                    start = c * L + SUBLANES - (CONV_WIDTH - 1) + parity + j
                    acc = acc + (xpad_ref[k, pl.ds(start, L // 2, stride=2), :]
                                 * cw_ref[j:j + 1, cols])
                act = acc * _sigmoid(acc)
                if k < _QK_BLOCKS // 2:
                    act = act * (HEAD_DIM ** -0.5)
                qks_ref[k, pl.ds(c * L + parity, L // 2, stride=2), :] = act
            yield
    for k in range(_QK_BLOCKS):
        xpad_ref[k, 0:SUBLANES, :] = xpad_ref[k, T:T + SUBLANES, :]


def _gate_pieces(gr_ref, br_ref, m_ref, out):
    T = MLSTM_ROWS
    L = MLSTM_CHUNK
    n_chunks = T // L
    lane = lax.broadcasted_iota(jnp.int32, (SUBLANES, L), 1)
    row_id = lax.broadcasted_iota(jnp.int32, (L, L), 0)
    col_id = lax.broadcasted_iota(jnp.int32, (L, L), 1)
    triu = (row_id <= col_id).astype(BF16)
    g = gr_ref[...] + br_ref[...]
    logf = _log_sigmoid(g)
    yield
    lf = jnp.concatenate([pltpu.roll(logf[:, c * L:(c + 1) * L], N_HEADS, axis=0)
                          for c in range(n_chunks)], axis=0)
    hi = lf.astype(BF16)
    rest = lf - hi.astype(F32)
    mid = rest.astype(BF16)
    lo = (rest - mid.astype(F32)).astype(BF16)
    terms = jnp.dot(jnp.concatenate([hi, mid, lo], axis=0), triu,
                    preferred_element_type=F32)
    n_rows = n_chunks * SUBLANES
    b_all = terms[0:n_rows] + terms[n_rows:2 * n_rows] + terms[2 * n_rows:3 * n_rows]
    bs = [b_all[c * SUBLANES:(c + 1) * SUBLANES] for c in range(n_chunks)]
    rs = [g[:, c * L:(c + 1) * L] - bs[c] for c in range(n_chunks)]
    yield
    m_runs = list(rs)
    for level in range(_CUMMAX_LEVELS):
        d = 1 << level
        m_runs = [jnp.where(lane >= d, jnp.maximum(x, pltpu.roll(x, d, axis=1)), x)
                  for x in m_runs]
        yield
    m = m_ref[...]
    rows = []
    for c in range(n_chunks):
        b, r, m_run = bs[c], rs[c], m_runs[c]
        m_last = jnp.broadcast_to(m_run[:, L - 1:L], (SUBLANES, L))
        b_last = jnp.broadcast_to(b[:, L - 1:L], (SUBLANES, L))
        mu = jnp.maximum(m_run, m)
        mx = jnp.maximum(m, m_last)
        rows.append(dict(
            mu_log2=mu * _LOG2E,
            inter_scale=jnp.exp(m - mu),
            eb=jnp.exp(-b - mu),
            w_new=jnp.exp(r - mx),
            s_old=jnp.exp(m - mx)))
        m = b_last + mx
    m_ref[...] = m
    r_pad = [jnp.zeros((SUBLANES, L), F32)] * (LANES // SUBLANES - n_chunks)
    out["rows"] = rows
    out["r_cols"] = (jnp.concatenate(rs + r_pad, axis=0) * _LOG2E).T


_GATE_ROWS = ("mu_log2", "inter_scale", "eb", "w_new", "s_old")


def _store_gates(gates, grow_ref, rcol_ref):
    for c, chunk in enumerate(gates["rows"]):
        for q, name in enumerate(_GATE_ROWS):
            grow_ref[c * len(_GATE_ROWS) + q] = chunk[name]
    rcol_ref[...] = gates["r_cols"]


def _head_pieces(qks_ref, vt_ref, ot_ref, nw_ref, grow_ref, rcol_ref, c_ref, out_ref):
    L = MLSTM_CHUNK
    D = HEAD_DIM
    row_id = lax.broadcasted_iota(jnp.int32, (L, L), 0)
    col_id = lax.broadcasted_iota(jnp.int32, (L, L), 1)
    causal = row_id <= col_id
    neg_inf = jnp.float32(-jnp.inf)
    for c in range(MLSTM_ROWS // L):
        for h in range(N_HEADS):
            row = lambda name: grow_ref[c * len(_GATE_ROWS) + _GATE_ROWS.index(name),
                                        h:h + 1, :]
            q = qks_ref[h, c * L:(c + 1) * L, :].astype(BF16)
            k = qks_ref[N_HEADS + h, c * L:(c + 1) * L, :].astype(BF16)
            vt = vt_ref[h * D:(h + 1) * D, c * L:(c + 1) * L]
            r_col = rcol_ref[:, c * SUBLANES + h:c * SUBLANES + h + 1]

            dmat = jnp.exp2(jnp.where(causal, r_col - row("mu_log2"), neg_inf))
            s_kq = lax.dot_general(k, q, _NT_DIMS, preferred_element_type=F32) * dmat
            cq = lax.dot_general(c_ref[h].astype(BF16), q, _NT_DIMS,
                                 preferred_element_type=F32)
            isc = row("inter_scale")
            num = isc * cq[0:D, :] + jnp.dot(vt.astype(BF16), s_kq.astype(BF16),
                                             preferred_element_type=F32)
            den = isc * cq[D:D + 1, :] + jnp.sum(s_kq, axis=0, keepdims=True)
            hh = num * (1.0 / jnp.maximum(jnp.abs(den), row("eb")))

            w = row("w_new")
            vw = jnp.concatenate([vt * w, jnp.broadcast_to(w, (SUBLANES, L))],
                                 axis=0).astype(BF16)
            c_loc = jnp.dot(vw, k, preferred_element_type=F32)
            c_ref[h] = row("s_old") * c_ref[h] + c_loc

            mean = jnp.mean(hh, axis=0, keepdims=True)
            cen = hh - mean
            var = jnp.mean(cen * cen, axis=0, keepdims=True)
            hn = cen * lax.rsqrt(var + EPS) * nw_ref[h * D:(h + 1) * D, :]
            gate = _sigmoid(ot_ref[h * D:(h + 1) * D, c * L:(c + 1) * L])
            out_ref[h * D:(h + 1) * D, c * L:(c + 1) * L] = (hn * gate).astype(BF16)
            yield


def _mlstm_kernel(qk_ref, vt_ref, ot_ref, gr_ref, gr_next_ref, cw_ref, cb_ref, br_ref, nw_ref,
                  out_ref, xpad_ref, qks_ref, c_ref, m_ref, grow_ref, rcol_ref):
    n_chunks = MLSTM_ROWS // MLSTM_CHUNK

    @pl.when(pl.program_id(0) == 0)
    def _():
        xpad_ref[:, 0:SUBLANES, :] = jnp.zeros((_QK_BLOCKS, SUBLANES, LANES), F32)
        c_ref[...] = jnp.zeros(c_ref.shape, F32)
        m_ref[...] = jnp.zeros(m_ref.shape, F32)
        first = {}
        for _ in _gate_pieces(gr_ref, br_ref, m_ref, first):
            pass
        _store_gates(first, grow_ref, rcol_ref)

    for _ in _conv_pieces(qk_ref, cw_ref, cb_ref, xpad_ref, qks_ref):
        pass
    nxt = {}
    _interleave((_gate_pieces(gr_next_ref, br_ref, m_ref, nxt), _GATE_PIECES),
                (_head_pieces(qks_ref, vt_ref, ot_ref, nw_ref, grow_ref, rcol_ref, c_ref,
                              out_ref), n_chunks * N_HEADS))
    _store_gates(nxt, grow_ref, rcol_ref)


def _mlstm_call(qk, vt, ot, gr, cw, cb, bias_r, nw_cols):
    S = qk.shape[0]
    T = MLSTM_ROWS
    n = S // T
    n_chunks = T // MLSTM_CHUNK
    return pl.pallas_call(
        _mlstm_kernel,
        grid=(n,),
        in_specs=[_row_spec(T, D_QK), _lane_spec(D_MLSTM, T), _lane_spec(D_MLSTM, T),
                  _lane_spec(SUBLANES, T),
                  pl.BlockSpec((SUBLANES, T), lambda i: (0, jnp.minimum(i + 1, n - 1))),
                  _const_spec(cw.shape), _const_spec(cb.shape),
                  _const_spec(bias_r.shape), _const_spec(nw_cols.shape)],
        out_specs=_lane_spec(D_MLSTM, T),
        out_shape=jax.ShapeDtypeStruct((D_MLSTM, S), BF16),
        scratch_shapes=[pltpu.VMEM((_QK_BLOCKS, T + SUBLANES, LANES), F32),
                        pltpu.VMEM((_QK_BLOCKS, T, LANES), F32),
                        pltpu.VMEM((N_HEADS, HEAD_DIM + SUBLANES, HEAD_DIM), F32),
                        pltpu.VMEM((SUBLANES, LANES), F32),
                        pltpu.VMEM((n_chunks * len(_GATE_ROWS), SUBLANES, MLSTM_CHUNK), F32),
                        pltpu.VMEM((MLSTM_CHUNK, LANES), F32)],
        compiler_params=_params(),
        name="mlstm",
    )(qk, vt, ot, gr, gr, cw, cb, bias_r, nw_cols)


def _gelu_tanh(x):
    c = math.sqrt(2.0 / math.pi)
    return x * (0.5 * (1.0 + jnp.tanh(c * (x + 0.044715 * (x * x * x)))))


def _s5_kernel(u_ref, an_ref, wb_ref, wc_ref, wd_ref, d_ref, gw_ref, gb_ref, out_ref,
               x_ref, y_ref, *z_refs):
    T = S5_ROWS
    N = S5_STEPS

    @pl.when(pl.program_id(0) == 0)
    def _():
        x_ref[...] = jnp.zeros(x_ref.shape, F32)

    R = S5_SUB
    P = R // N
    n_sub = T // R
    half = STATE_BLOCKS // 2
    n_blocks = D_SSM // LANES
    a_re = [an_ref[:, c * LANES:(c + 1) * LANES] for c in range(half)]
    a_im = [an_ref[:, (half + c) * LANES:(half + c + 1) * LANES] for c in range(half)]

    def group_rows(sb, b):
        return jnp.concatenate([u_ref[b, pl.ds(sb * R + m, P, stride=N), :]
                                for m in range(N)], axis=1).astype(BF16)

    def expand_piece(sb, j):
        w = jnp.dot(group_rows(sb, j // 2), wb_ref[j], preferred_element_type=F32)
        for c in range(STATE_BLOCKS):
            z_refs[sb][c // 2, pl.ds(j * S5_PITCH + c % 2, P, stride=2), :] = (
                w[:, c * LANES:(c + 1) * LANES])

    def scan_piece(sb, x, i0, i1):
        z_ref = z_refs[sb]
        for i in range(i0, i1):
            tile = lambda c: (c // 2, pl.ds(2 * i + c % 2, SUBLANES, stride=S5_PITCH),
                              slice(None))
            new = list(x)
            for c in range(half):
                x_re, x_im = x[c], x[half + c]
                new[c] = a_re[c] * x_re - a_im[c] * x_im + z_ref[tile(c)]
                new[half + c] = a_re[c] * x_im + a_im[c] * x_re + z_ref[tile(half + c)]
                z_ref[tile(c)] = x_re
                z_ref[tile(half + c)] = x_im
            x = new
        return x

    def contract_piece(sb, j):
        xs = jnp.concatenate(
            [z_refs[sb][c // 2, pl.ds(j * S5_PITCH + c % 2, P, stride=2), :]
             for c in range(STATE_BLOCKS)], axis=1).astype(BF16)
        return jnp.dot(xs, wc_ref[j], preferred_element_type=F32)

    def finish(sb, parts):
        for b in range(n_blocks):
            yb = (parts[2 * b] + parts[2 * b + 1]
                  + jnp.dot(group_rows(sb, b), wd_ref[b], preferred_element_type=F32))
            for m in range(N):
                y_ref[b, pl.ds(sb * R + m, P, stride=N), :] = yb[:, m * LANES:(m + 1) * LANES]
        rows = slice(sb * R, (sb + 1) * R)
        y = jnp.concatenate([y_ref[b, rows, :] for b in range(n_blocks)], axis=1)
        u = jnp.concatenate([u_ref[b, rows, :] for b in range(n_blocks)], axis=1)
        z = _gelu_tanh(y + d_ref[...] * u)
        gate = _sigmoid(jnp.dot(z.astype(BF16), gw_ref[...],
                                preferred_element_type=F32) + gb_ref[...])
        out_ref[rows, :] = (z * gate).astype(BF16)

    x = [x_ref[:, c * LANES:(c + 1) * LANES] for c in range(STATE_BLOCKS)]
    steps = P // SUBLANES
    for j in range(SUBLANES):
        expand_piece(0, j)
    for sb in range(n_sub + 1):
        parts = []
        for j in range(SUBLANES):
            if sb < n_sub:
                x = scan_piece(sb, x, j * steps, (j + 1) * steps)
            if sb + 1 < n_sub:
                expand_piece(sb + 1, j)
            if sb >= 1:
                parts.append(contract_piece(sb - 1, j))
        if sb >= 1:
            finish(sb - 1, parts)
    for c in range(STATE_BLOCKS):
        x_ref[:, c * LANES:(c + 1) * LANES] = x[c]


def _s5_call(u, an_tile, wb, wc, wd, d, gw, gb):
    S = u.shape[1]
    T = S5_ROWS
    n_blocks = D_SSM // LANES
    return pl.pallas_call(
        _s5_kernel,
        grid=(S // T,),
        in_specs=[pl.BlockSpec((n_blocks, T, LANES), lambda i: (0, i, 0)),
                  _const_spec(an_tile.shape), _const_spec(wb.shape), _const_spec(wc.shape),
                  _const_spec(wd.shape), _const_spec(d.shape), _const_spec(gw.shape),
                  _const_spec(gb.shape)],
        out_specs=_row_spec(T, D_SSM),
        out_shape=jax.ShapeDtypeStruct((S, D_SSM), BF16),
        scratch_shapes=([pltpu.VMEM((SUBLANES, STATE_LANES), F32),
                         pltpu.VMEM((n_blocks, T, LANES), F32)]
                        + [pltpu.VMEM((STATE_BLOCKS // 2, SUBLANES * S5_PITCH, LANES), F32)]
                        * (T // S5_SUB)),
        compiler_params=_params(),
        name="s5",
    )(u, an_tile, wb, wc, wd, d, gw, gb)


def _s5_weights(lam_re, lam_im, log_dt, b_re, b_im, c_re, c_im):
    N = S5_STEPS
    cmul = lambda xr, xi, yr, yi: (xr * yr - xi * yi, xr * yi + xi * yr)
    lr, li = lam_re.astype(F32), lam_im.astype(F32)
    dt = jnp.exp(log_dt.astype(F32))[:, None]
    mag = jnp.exp(lr * dt)
    a_re, a_im = mag * jnp.cos(li * dt), mag * jnp.sin(li * dt)
    pw = [(jnp.ones_like(a_re), jnp.zeros_like(a_im))]
    for _ in range(N):
        pw.append(cmul(pw[-1][0], pw[-1][1], a_re, a_im))
    den = lr * lr + li * li
    coef_re = ((a_re - 1.0) * lr + a_im * li) / den
    coef_im = (a_im * lr - (a_re - 1.0) * li) / den
    bb = cmul(coef_re[..., None], coef_im[..., None],
              b_re.astype(F32), b_im.astype(F32))
    anb = [cmul(p[0][..., None], p[1][..., None], bb[0], bb[1]) for p in pw[:N]]
    cr, ci = c_re.astype(F32), c_im.astype(F32)
    can = [cmul(cr, ci, p[0][:, None, :], p[1][:, None, :]) for p in pw[1:]]

    J, Q, P, Hc = SUBLANES, GROUPS_PER_SUBLANE, SSM_STATE, SSM_GROUP
    an_tile = jnp.concatenate([pw[N][0].reshape(J, Q * P), pw[N][1].reshape(J, Q * P)], axis=1)
    j_id, h_id, q_id = jnp.arange(J), jnp.arange(2), jnp.arange(Q)
    own_in = ((j_id[:, None] % 2 == h_id[None, :])[:, None, :, None, None, None, None, None]
              & (q_id[:, None] == q_id[None, :])[None, None, None, :, None, None, :, None])
    own_out = ((j_id[:, None] % 2 == h_id[None, :])[:, None, None, None, None, :, None, None]
               & (q_id[:, None] == q_id[None, :])[None, None, :, None, None, None, :, None])

    wb = jnp.stack([jnp.stack(anb[N - 1 - m]) for m in range(N)]).reshape(N, 2, J, Q, P, Hc)
    wb = jnp.transpose(wb, (2, 0, 3, 5, 1, 4))
    wb = jnp.where(own_in, wb[:, :, None, :, :, :, None, :], 0.0)
    wb = wb.reshape(J, N * 2 * Q * Hc, STATE_LANES)
    wc = jnp.stack([jnp.stack([w[0], -w[1]]) for w in can]).reshape(N, 2, J, Q, Hc, P)
    wc = jnp.transpose(wc, (2, 1, 3, 5, 0, 4))
    wc = jnp.where(own_out, wc[:, :, :, :, :, None, None, :], 0.0)
    wc = wc.reshape(J, STATE_LANES, N * 2 * Q * Hc)
    kn = [jnp.einsum('gop,gpc->goc', cr, w[0]) - jnp.einsum('gop,gpc->goc', ci, w[1])
          for w in anb]
    n_blocks, per = D_SSM // LANES, LANES // Hc
    n_id, g_id = jnp.arange(N), jnp.arange(per)
    own_direct = ((n_id[:, None] <= n_id[None, :])[None, :, None, None, :, None, None]
                  & (g_id[:, None] == g_id[None, :])[None, None, :, None, None, :, None])
    wd = jnp.stack([jnp.stack([kn[max(m - l, 0)] for m in range(N)]) for l in range(N)])
    wd = wd.reshape(N, N, n_blocks, per, Hc, Hc)
    wd = jnp.transpose(wd, (2, 0, 3, 5, 1, 4))
    wd = jnp.where(own_direct, wd[:, :, :, :, :, None, :], 0.0)
    wd = wd.reshape(n_blocks, N * LANES, N * LANES)
    return an_tile, wb.astype(BF16), wc.astype(BF16), wd.astype(BF16)


def _ffn_kernel(x_ref, hm_ref, y_ref, wo_ref, g2_ref, w1_ref, w2_ref, g3_ref, out_ref):
    mixed = (lax.dot_general(hm_ref[...], wo_ref[0:D_MLSTM, :], (((0,), (0,)), ((), ())),
                             preferred_element_type=F32)
             + jnp.dot(y_ref[...], wo_ref[D_MLSTM:D_MODEL, :], preferred_element_type=F32))
    x1 = x_ref[...] + mixed
    h2 = _rms(x1, g2_ref[...]).astype(BF16)
    mlp = None
    for c in range(D_FF // FF_CHUNK):
        a = jnp.dot(h2, w1_ref[:, c * FF_CHUNK:(c + 1) * FF_CHUNK],
                    preferred_element_type=F32)
        a = jnp.maximum(a, 0.0)
        part = jnp.dot((a * a).astype(BF16), w2_ref[c * FF_CHUNK:(c + 1) * FF_CHUNK, :],
                       preferred_element_type=F32)
        mlp = part if mlp is None else mlp + part
    out_ref[...] = _rms(x1 + mlp, g3_ref[...])


def _ffn_call(x, hm, y, wo, g2, w1, w2, g3):
    S = x.shape[0]
    T = FFN_ROWS
    single = dict(pipeline_mode=pl.Buffered(WEIGHT_BUFFERS))
    return pl.pallas_call(
        _ffn_kernel,
        grid=(S // T,),
        in_specs=[_row_spec(T, D_MODEL), _lane_spec(D_MLSTM, T), _row_spec(T, D_SSM),
                  pl.BlockSpec(wo.shape, lambda i: (0, 0), **single),
                  _const_spec(g2.shape),
                  pl.BlockSpec(w1.shape, lambda i: (0, 0), **single),
                  pl.BlockSpec(w2.shape, lambda i: (0, 0), **single),
                  _const_spec(g3.shape)],
        out_specs=_row_spec(T, D_MODEL),
        out_shape=jax.ShapeDtypeStruct((S, D_MODEL), F32),
        compiler_params=_params(),
        name="ffn",
    )(x, hm, y, wo, g2, w1, w2, g3)


def _layer(x, mix_norm_w, w_in, conv_w, conv_b, i_bias, f_bias, mlstm_norm_w,
           lam_re, lam_im, log_dt, b_re, b_im, c_re, c_im, ssm_d,
           glu_w, glu_b, w_out, mlp_norm_w, w_ff1, w_ff2, out_norm_w):
    assert x.shape[1] == D_MODEL
    assert x.shape[0] % max(PROJ_ROWS, MLSTM_ROWS, S5_ROWS, FFN_ROWS) == 0
    n_qk, n_vo, n_gate = 2 * D_MLSTM, 2 * D_MLSTM, 2 * N_HEADS
    w_rows = jnp.concatenate([w_in[:, :n_qk], w_in[:, n_qk + n_vo + n_gate:]],
                             axis=1).astype(BF16)
    w_cols = w_in[:, n_qk:n_qk + n_vo].T.astype(BF16)
    wg = w_in[:, n_qk + n_vo:n_qk + n_vo + n_gate].T.astype(BF16)
    qk, u, vt, ot, gr, wo_bf, w1_bf, w2_bf = _proj_call(
        x, mix_norm_w[None, :], w_rows, w_cols, wg, w_out, w_ff1, w_ff2)

    bias_r = jnp.concatenate([i_bias, f_bias]).astype(F32)[:, None]
    nw_cols = jnp.broadcast_to(mlstm_norm_w.astype(F32)[:, None], (D_MLSTM, LANES))
    hm = _mlstm_call(qk, vt, ot, gr, conv_w, conv_b[None, :], bias_r, nw_cols)

    an_tile, wb, wc, wd = _s5_weights(lam_re, lam_im, log_dt, b_re, b_im, c_re, c_im)
    y = _s5_call(u, an_tile, wb, wc, wd, ssm_d[None, :], glu_w.astype(BF16),
                 glu_b[None, :])

    return _ffn_call(x, hm, y, wo_bf, mlp_norm_w[None, :], w1_bf, w2_bf,
                     out_norm_w[None, :])


def kernel(x, mix_norm_w, w_in, conv_w, conv_b, i_bias, f_bias, mlstm_norm_w, ssm_lam_re, ssm_lam_im, ssm_log_dt, ssm_b_re, ssm_b_im, ssm_c_re, ssm_c_im, ssm_d, glu_w, glu_b, w_out, mlp_norm_w, w_ff1, w_ff2, final_norm_w):
    assert x.shape[0] == 1 and mix_norm_w.shape[0] == 1
    xs = x[0]
    out = _layer(xs, mix_norm_w[0], w_in[0], conv_w[0], conv_b[0], i_bias[0], f_bias[0],
                 mlstm_norm_w[0], ssm_lam_re[0], ssm_lam_im[0], ssm_log_dt[0],
                 ssm_b_re[0], ssm_b_im[0], ssm_c_re[0], ssm_c_im[0], ssm_d[0],
                 glu_w[0], glu_b[0], w_out[0], mlp_norm_w[0], w_ff1[0], w_ff2[0],
                 final_norm_w)
    return out[None]
```

```python
import math

import jax
import jax.numpy as jnp
from jax import lax
from jax.experimental import pallas as pl
from jax.experimental.pallas import tpu as pltpu

F32 = jnp.float32
BF16 = jnp.bfloat16

D_MODEL = 1024
D_MLSTM = 512
D_QK = 2 * D_MLSTM
N_HEADS = 4
HEAD_DIM = 128
CONV_WIDTH = 4
D_SSM = 512
SSM_GROUP = 16
N_GROUPS = 32
SSM_STATE = 64
D_FF = 4096
EPS = 1e-6

SUBLANES = 8
LANES = 128

PROJ_ROWS = 1024
MLSTM_ROWS = 1024
MLSTM_CHUNK = 128
S5_ROWS = 1024
S5_SUB = 512
S5_PITCH = S5_SUB + 4
FFN_ROWS = 1024
FF_CHUNK = 1024
WEIGHT_BUFFERS = 1
VMEM_LIMIT = 56 * 1024 * 1024

GROUPS_PER_SUBLANE = N_GROUPS // SUBLANES
STATE_HALF = GROUPS_PER_SUBLANE * SSM_STATE
STATE_LANES = 2 * STATE_HALF
STATE_BLOCKS = STATE_LANES // LANES


def _const_spec(shape):
    return pl.BlockSpec(shape, lambda i: (0,) * len(shape))


def _row_spec(rows, cols):
    return pl.BlockSpec((rows, cols), lambda i: (i, 0))


def _lane_spec(rows, cols):
    return pl.BlockSpec((rows, cols), lambda i: (0, i))


def _params():
    return pltpu.CompilerParams(dimension_semantics=("arbitrary",),
                                vmem_limit_bytes=VMEM_LIMIT)


def _rms(x, g):
    r = lax.rsqrt(jnp.mean(x * x, axis=-1, keepdims=True) + EPS)
    return x * r * g


def _sigmoid(x):
    return 0.5 * jnp.tanh(0.5 * x) + 0.5


_NT_DIMS = (((1,), (1,)), ((), ()))


_U_COL = 2 * D_QK + 2 * N_HEADS


def _proj_kernel(x_ref, g_ref, w_ref, wg_ref, wo_ref, w1_ref, w2_ref,
                 qk_ref, u_ref, vt_ref, ot_ref, gr_ref, wo_bf_ref, w1_bf_ref, w2_bf_ref,
                 wqk_ref, wvo_ref, wu_ref):
    @pl.when(pl.program_id(0) == 0)
    def _():
        wqk_ref[...] = w_ref[:, 0:D_QK].astype(BF16)
        wvo_ref[...] = w_ref[:, D_QK:2 * D_QK].astype(BF16)
        wu_ref[...] = w_ref[:, _U_COL:_U_COL + D_SSM].astype(BF16)

    wo_bf_ref[...] = wo_ref[...].astype(BF16)
    w1_bf_ref[...] = w1_ref[...].astype(BF16)
    w2_bf_ref[...] = w2_ref[...].astype(BF16)

    h = _rms(x_ref[...], g_ref[...]).astype(BF16)
    qk_ref[...] = jnp.dot(h, wqk_ref[...], preferred_element_type=F32)
    u = jnp.dot(h, wu_ref[...], preferred_element_type=F32)
    for b in range(D_SSM // LANES):
        u_ref[b] = u[:, b * LANES:(b + 1) * LANES]
    tn_t = (((0,), (1,)), ((), ()))
    vt_ref[...] = lax.dot_general(wvo_ref[:, 0:D_MLSTM], h, tn_t,
                                  preferred_element_type=F32)
    ot_ref[...] = lax.dot_general(wvo_ref[:, D_MLSTM:2 * D_MLSTM], h, tn_t,
                                  preferred_element_type=F32)
    gr_ref[...] = lax.dot_general(wg_ref[...], h, _NT_DIMS, preferred_element_type=F32)


def _proj_call(x, g, w, wg, wo, w1, w2):
    S = x.shape[0]
    T = PROJ_ROWS
    n = S // T
    wo_spec = _row_spec(wo.shape[0] // n, wo.shape[1])
    w1_spec = pl.BlockSpec((w1.shape[0], w1.shape[1] // n), lambda i: (0, i))
    w2_spec = _row_spec(w2.shape[0] // n, w2.shape[1])
    return pl.pallas_call(
        _proj_kernel,
        grid=(n,),
        in_specs=[_row_spec(T, D_MODEL), _const_spec((1, D_MODEL)),
                  pl.BlockSpec(w.shape, lambda i: (0, 0),
                               pipeline_mode=pl.Buffered(WEIGHT_BUFFERS)),
                  _const_spec(wg.shape),
                  wo_spec, w1_spec, w2_spec],
        scratch_shapes=[pltpu.VMEM((D_MODEL, D_QK), BF16),
                        pltpu.VMEM((D_MODEL, 2 * D_MLSTM), BF16),
                        pltpu.VMEM((D_MODEL, D_SSM), BF16)],
        out_specs=[_row_spec(T, D_QK),
                   pl.BlockSpec((D_SSM // LANES, T, LANES), lambda i: (0, i, 0)),
                   _lane_spec(D_MLSTM, T), _lane_spec(D_MLSTM, T), _lane_spec(SUBLANES, T),
                   wo_spec, w1_spec, w2_spec],
        out_shape=[jax.ShapeDtypeStruct((S, D_QK), F32),
                   jax.ShapeDtypeStruct((D_SSM // LANES, S, LANES), F32),
                   jax.ShapeDtypeStruct((D_MLSTM, S), F32),
                   jax.ShapeDtypeStruct((D_MLSTM, S), F32),
                   jax.ShapeDtypeStruct((SUBLANES, S), F32),
                   jax.ShapeDtypeStruct(wo.shape, BF16),
                   jax.ShapeDtypeStruct(w1.shape, BF16),
                   jax.ShapeDtypeStruct(w2.shape, BF16)],
        compiler_params=_params(),
        name="proj",
    )(x, g, w, wg, wo, w1, w2)


def _log_sigmoid(x):
    return jnp.minimum(x, 0.0) - jnp.log1p(jnp.exp(-jnp.abs(x)))


def _interleave(*stages):
    total = max(n for _, n in stages)
    done = [0] * len(stages)
    for tick in range(1, total + 1):
        for s, (gen, n) in enumerate(stages):
            want = -(-tick * n // total)
            while done[s] < want:
                next(gen, None)
                done[s] += 1
    for gen, _ in stages:
        for _ in gen:
            pass


_LOG2E = math.log2(math.e)
_CUMMAX_LEVELS = LANES.bit_length() - 1
_GATE_PIECES = 3 + _CUMMAX_LEVELS
_QK_BLOCKS = 2 * D_MLSTM // LANES


def _conv_pieces(qk_ref, cw_ref, cb_ref, xpad_ref, qks_ref):
    T = MLSTM_ROWS
    L = MLSTM_CHUNK
    for k in range(_QK_BLOCKS):
        xpad_ref[k, SUBLANES:SUBLANES + T, :] = qk_ref[:, k * LANES:(k + 1) * LANES]
    yield
    for c in range(T // L):
        for k in range(_QK_BLOCKS):
            cols = slice(k * LANES, (k + 1) * LANES)
            for parity in range(2):
                acc = cb_ref[:, cols]
                for j in range(CONV_WIDTH):
                    start = c * L + SUBLANES - (CONV_WIDTH - 1) + parity + j
                    acc = acc + (xpad_ref[k, pl.ds(start, L // 2, stride=2), :]
                                 * cw_ref[j:j + 1, cols])
                act = acc * _sigmoid(acc)
                if k < _QK_BLOCKS // 2:
                    act = act * (HEAD_DIM ** -0.5)
                qks_ref[k, pl.ds(c * L + parity, L // 2, stride=2), :] = act
            yield
    for k in range(_QK_BLOCKS):
        xpad_ref[k, 0:SUBLANES, :] = xpad_ref[k, T:T + SUBLANES, :]


def _gate_pieces(gr_ref, br_ref, m_ref, out):
    T = MLSTM_ROWS
    L = MLSTM_CHUNK
    n_chunks = T // L
    lane = lax.broadcasted_iota(jnp.int32, (SUBLANES, L), 1)
    row_id = lax.broadcasted_iota(jnp.int32, (L, L), 0)
    col_id = lax.broadcasted_iota(jnp.int32, (L, L), 1)
    triu = (row_id <= col_id).astype(BF16)
    g = gr_ref[...] + br_ref[...]
    logf = _log_sigmoid(g)
    yield
    lf = jnp.concatenate([pltpu.roll(logf[:, c * L:(c + 1) * L], N_HEADS, axis=0)
                          for c in range(n_chunks)], axis=0)
    hi = lf.astype(BF16)
    rest = lf - hi.astype(F32)
    mid = rest.astype(BF16)
    lo = (rest - mid.astype(F32)).astype(BF16)
    terms = jnp.dot(jnp.concatenate([hi, mid, lo], axis=0), triu,
                    preferred_element_type=F32)
    n_rows = n_chunks * SUBLANES
    b_all = terms[0:n_rows] + terms[n_rows:2 * n_rows] + terms[2 * n_rows:3 * n_rows]
    bs = [b_all[c * SUBLANES:(c + 1) * SUBLANES] for c in range(n_chunks)]
    rs = [g[:, c * L:(c + 1) * L] - bs[c] for c in range(n_chunks)]
    yield
    m_runs = list(rs)
    for level in range(_CUMMAX_LEVELS):
        d = 1 << level
        m_runs = [jnp.where(lane >= d, jnp.maximum(x, pltpu.roll(x, d, axis=1)), x)
                  for x in m_runs]
        yield
    m = m_ref[...]
    rows = []
    for c in range(n_chunks):
        b, r, m_run = bs[c], rs[c], m_runs[c]
        m_last = jnp.broadcast_to(m_run[:, L - 1:L], (SUBLANES, L))
        b_last = jnp.broadcast_to(b[:, L - 1:L], (SUBLANES, L))
        mu = jnp.maximum(m_run, m)
        mx = jnp.maximum(m, m_last)
        rows.append(dict(
            mu_log2=mu * _LOG2E,
            inter_scale=jnp.exp(m - mu),
            eb=jnp.exp(-b - mu),
            w_new=jnp.exp(r - mx),
            s_old=jnp.exp(m - mx)))
        m = b_last + mx
    m_ref[...] = m
    r_pad = [jnp.zeros((SUBLANES, L), F32)] * (LANES // SUBLANES - n_chunks)
    out["rows"] = rows
    out["r_cols"] = (jnp.concatenate(rs + r_pad, axis=0) * _LOG2E).T


_GATE_ROWS = ("mu_log2", "inter_scale", "eb", "w_new", "s_old")


def _store_gates(gates, grow_ref, rcol_ref):
    for c, chunk in enumerate(gates["rows"]):
        for q, name in enumerate(_GATE_ROWS):
            grow_ref[c * len(_GATE_ROWS) + q] = chunk[name]
    rcol_ref[...] = gates["r_cols"]


def _head_pieces(qks_ref, vt_ref, ot_ref, nw_ref, grow_ref, rcol_ref, c_ref, out_ref):
    L = MLSTM_CHUNK
    D = HEAD_DIM
    row_id = lax.broadcasted_iota(jnp.int32, (L, L), 0)
    col_id = lax.broadcasted_iota(jnp.int32, (L, L), 1)
    causal = row_id <= col_id
    neg_inf = jnp.float32(-jnp.inf)
    for c in range(MLSTM_ROWS // L):
        for h in range(N_HEADS):
            row = lambda name: grow_ref[c * len(_GATE_ROWS) + _GATE_ROWS.index(name),
                                        h:h + 1, :]
            q = qks_ref[h, c * L:(c + 1) * L, :].astype(BF16)
            k = qks_ref[N_HEADS + h, c * L:(c + 1) * L, :].astype(BF16)
            vt = vt_ref[h * D:(h + 1) * D, c * L:(c + 1) * L]
            r_col = rcol_ref[:, c * SUBLANES + h:c * SUBLANES + h + 1]

            dmat = jnp.exp2(jnp.where(causal, r_col - row("mu_log2"), neg_inf))
            s_kq = lax.dot_general(k, q, _NT_DIMS, preferred_element_type=F32) * dmat
            cq = lax.dot_general(c_ref[h].astype(BF16), q, _NT_DIMS,
                                 preferred_element_type=F32)
            isc = row("inter_scale")
            num = isc * cq[0:D, :] + jnp.dot(vt.astype(BF16), s_kq.astype(BF16),
                                             preferred_element_type=F32)
            den = isc * cq[D:D + 1, :] + jnp.sum(s_kq, axis=0, keepdims=True)
            hh = num * (1.0 / jnp.maximum(jnp.abs(den), row("eb")))

            w = row("w_new")
            vw = jnp.concatenate([vt * w, jnp.broadcast_to(w, (SUBLANES, L))],
                                 axis=0).astype(BF16)
            c_loc = jnp.dot(vw, k, preferred_element_type=F32)
            c_ref[h] = row("s_old") * c_ref[h] + c_loc

            mean = jnp.mean(hh, axis=0, keepdims=True)
            cen = hh - mean
            var = jnp.mean(cen * cen, axis=0, keepdims=True)
            hn = cen * lax.rsqrt(var + EPS) * nw_ref[h * D:(h + 1) * D, :]
            gate = _sigmoid(ot_ref[h * D:(h + 1) * D, c * L:(c + 1) * L])
            out_ref[h * D:(h + 1) * D, c * L:(c + 1) * L] = (hn * gate).astype(BF16)
            yield


def _mlstm_kernel(qk_ref, vt_ref, ot_ref, gr_ref, gr_next_ref, cw_ref, cb_ref, br_ref, nw_ref,
                  out_ref, xpad_ref, qks_ref, c_ref, m_ref, grow_ref, rcol_ref):
    n_chunks = MLSTM_ROWS // MLSTM_CHUNK

    @pl.when(pl.program_id(0) == 0)
    def _():
        xpad_ref[:, 0:SUBLANES, :] = jnp.zeros((_QK_BLOCKS, SUBLANES, LANES), F32)
        c_ref[...] = jnp.zeros(c_ref.shape, F32)
        m_ref[...] = jnp.zeros(m_ref.shape, F32)
        first = {}
        for _ in _gate_pieces(gr_ref, br_ref, m_ref, first):
            pass
        _store_gates(first, grow_ref, rcol_ref)

    for _ in _conv_pieces(qk_ref, cw_ref, cb_ref, xpad_ref, qks_ref):
        pass
    nxt = {}
    _interleave((_gate_pieces(gr_next_ref, br_ref, m_ref, nxt), _GATE_PIECES),
                (_head_pieces(qks_ref, vt_ref, ot_ref, nw_ref, grow_ref, rcol_ref, c_ref,
                              out_ref), n_chunks * N_HEADS))
    _store_gates(nxt, grow_ref, rcol_ref)


def _mlstm_call(qk, vt, ot, gr, cw, cb, bias_r, nw_cols):
    S = qk.shape[0]
    T = MLSTM_ROWS
    n = S // T
    n_chunks = T // MLSTM_CHUNK
    return pl.pallas_call(
        _mlstm_kernel,
        grid=(n,),
        in_specs=[_row_spec(T, D_QK), _lane_spec(D_MLSTM, T), _lane_spec(D_MLSTM, T),
                  _lane_spec(SUBLANES, T),
                  pl.BlockSpec((SUBLANES, T), lambda i: (0, jnp.minimum(i + 1, n - 1))),
                  _const_spec(cw.shape), _const_spec(cb.shape),
                  _const_spec(bias_r.shape), _const_spec(nw_cols.shape)],
        out_specs=_lane_spec(D_MLSTM, T),
        out_shape=jax.ShapeDtypeStruct((D_MLSTM, S), BF16),
        scratch_shapes=[pltpu.VMEM((_QK_BLOCKS, T + SUBLANES, LANES), F32),
                        pltpu.VMEM((_QK_BLOCKS, T, LANES), F32),
                        pltpu.VMEM((N_HEADS, HEAD_DIM + SUBLANES, HEAD_DIM), F32),
                        pltpu.VMEM((SUBLANES, LANES), F32),
                        pltpu.VMEM((n_chunks * len(_GATE_ROWS), SUBLANES, MLSTM_CHUNK), F32),
                        pltpu.VMEM((MLSTM_CHUNK, LANES), F32)],
        compiler_params=_params(),
        name="mlstm",
    )(qk, vt, ot, gr, gr, cw, cb, bias_r, nw_cols)


def _gelu_tanh(x):
    c = math.sqrt(2.0 / math.pi)
    return x * (0.5 * (1.0 + jnp.tanh(c * (x + 0.044715 * (x * x * x)))))


def _s5_kernel(u_ref, a2_ref, wb_ref, wc_ref, wd_ref, d_ref, gw_ref, gb_ref, out_ref,
               x_ref, y_ref, *z_refs):
    T = S5_ROWS

    @pl.when(pl.program_id(0) == 0)
    def _():
        x_ref[...] = jnp.zeros(x_ref.shape, F32)

    R = S5_SUB
    P = R // 2
    n_sub = T // R
    half = STATE_BLOCKS // 2
    n_blocks = D_SSM // LANES
    a_re = [a2_ref[:, c * LANES:(c + 1) * LANES] for c in range(half)]
    a_im = [a2_ref[:, (half + c) * LANES:(half + c + 1) * LANES] for c in range(half)]

    def pair_rows(sb, b):
        even = u_ref[b, pl.ds(sb * R, P, stride=2), :]
        odd = u_ref[b, pl.ds(sb * R + 1, P, stride=2), :]
        return jnp.concatenate([even, odd], axis=1).astype(BF16)

    def expand_piece(sb, j):
        w = jnp.dot(pair_rows(sb, j // 2), wb_ref[j], preferred_element_type=F32)
        for c in range(STATE_BLOCKS):
            z_refs[sb][c // 2, pl.ds(j * S5_PITCH + c % 2, P, stride=2), :] = (
                w[:, c * LANES:(c + 1) * LANES])

    def scan_piece(sb, x, i0, i1):
        z_ref = z_refs[sb]
        for i in range(i0, i1):
            tile = lambda c: (c // 2, pl.ds(2 * i + c % 2, SUBLANES, stride=S5_PITCH),
                              slice(None))
            new = list(x)
            for c in range(half):
                x_re, x_im = x[c], x[half + c]
                new[c] = a_re[c] * x_re - a_im[c] * x_im + z_ref[tile(c)]
                new[half + c] = a_re[c] * x_im + a_im[c] * x_re + z_ref[tile(half + c)]
                z_ref[tile(c)] = x_re
                z_ref[tile(half + c)] = x_im
            x = new
        return x

    def contract_piece(sb, j):
        xs = jnp.concatenate(
            [z_refs[sb][c // 2, pl.ds(j * S5_PITCH + c % 2, P, stride=2), :]
             for c in range(STATE_BLOCKS)], axis=1).astype(BF16)
        return jnp.dot(xs, wc_ref[j], preferred_element_type=F32)

    def finish(sb, parts):
        for b in range(n_blocks):
            yb = (parts[2 * b] + parts[2 * b + 1]
                  + jnp.dot(pair_rows(sb, b), wd_ref[b], preferred_element_type=F32))
            y_ref[b, pl.ds(sb * R, P, stride=2), :] = yb[:, 0:LANES]
            y_ref[b, pl.ds(sb * R + 1, P, stride=2), :] = yb[:, LANES:2 * LANES]
        rows = slice(sb * R, (sb + 1) * R)
        y = jnp.concatenate([y_ref[b, rows, :] for b in range(n_blocks)], axis=1)
        u = jnp.concatenate([u_ref[b, rows, :] for b in range(n_blocks)], axis=1)
        z = _gelu_tanh(y + d_ref[...] * u)
        gate = _sigmoid(jnp.dot(z.astype(BF16), gw_ref[...],
                                preferred_element_type=F32) + gb_ref[...])
        out_ref[rows, :] = (z * gate).astype(BF16)

    x = [x_ref[:, c * LANES:(c + 1) * LANES] for c in range(STATE_BLOCKS)]
    steps = P // SUBLANES
    for j in range(SUBLANES):
        expand_piece(0, j)
    for sb in range(n_sub + 1):
        parts = []
        for j in range(SUBLANES):
            if sb < n_sub:
                x = scan_piece(sb, x, j * steps, (j + 1) * steps)
            if sb + 1 < n_sub:
                expand_piece(sb + 1, j)
            if sb >= 1:
                parts.append(contract_piece(sb - 1, j))
        if sb >= 1:
            finish(sb - 1, parts)
    for c in range(STATE_BLOCKS):
        x_ref[:, c * LANES:(c + 1) * LANES] = x[c]


def _s5_call(u, a2_tile, wb, wc, wd, d, gw, gb):
    S = u.shape[1]
    T = S5_ROWS
    n_blocks = D_SSM // LANES
    return pl.pallas_call(
        _s5_kernel,
        grid=(S // T,),
        in_specs=[pl.BlockSpec((n_blocks, T, LANES), lambda i: (0, i, 0)),
                  _const_spec(a2_tile.shape), _const_spec(wb.shape), _const_spec(wc.shape),
                  _const_spec(wd.shape), _const_spec(d.shape), _const_spec(gw.shape),
                  _const_spec(gb.shape)],
        out_specs=_row_spec(T, D_SSM),
        out_shape=jax.ShapeDtypeStruct((S, D_SSM), BF16),
        scratch_shapes=([pltpu.VMEM((SUBLANES, STATE_LANES), F32),
                         pltpu.VMEM((n_blocks, T, LANES), F32)]
                        + [pltpu.VMEM((STATE_BLOCKS // 2, SUBLANES * S5_PITCH, LANES), F32)]
                        * (T // S5_SUB)),
        compiler_params=_params(),
        name="s5",
    )(u, a2_tile, wb, wc, wd, d, gw, gb)


def _s5_weights(lam_re, lam_im, log_dt, b_re, b_im, c_re, c_im):
    cmul = lambda xr, xi, yr, yi: (xr * yr - xi * yi, xr * yi + xi * yr)
    lr, li = lam_re.astype(F32), lam_im.astype(F32)
    dt = jnp.exp(log_dt.astype(F32))[:, None]
    mag = jnp.exp(lr * dt)
    a_re, a_im = mag * jnp.cos(li * dt), mag * jnp.sin(li * dt)
    a2_re, a2_im = cmul(a_re, a_im, a_re, a_im)
    den = lr * lr + li * li
    coef_re = ((a_re - 1.0) * lr + a_im * li) / den
    coef_im = (a_im * lr - (a_re - 1.0) * li) / den
    bb_re, bb_im = cmul(coef_re[..., None], coef_im[..., None],
                        b_re.astype(F32), b_im.astype(F32))
    abb_re, abb_im = cmul(a_re[..., None], a_im[..., None], bb_re, bb_im)
    cr, ci = c_re.astype(F32), c_im.astype(F32)
    ca_re, ca_im = cmul(cr, ci, a_re[:, None, :], a_im[:, None, :])
    ca2_re, ca2_im = cmul(cr, ci, a2_re[:, None, :], a2_im[:, None, :])

    J, Q, P, Hc = SUBLANES, GROUPS_PER_SUBLANE, SSM_STATE, SSM_GROUP
    a2_tile = jnp.concatenate([a2_re.reshape(J, Q * P), a2_im.reshape(J, Q * P)], axis=1)
    eye = jnp.eye(Q, dtype=F32)
    half = (jnp.arange(J) % 2)[:, None, None, None] == jnp.arange(2)[None, :, None, None]

    def in_weights(w_re, w_im):
        def blocks(w):
            return jnp.einsum('jqpc,qr->jqcrp', w.reshape(J, Q, P, Hc),
                              eye).reshape(J, Q * Hc, Q * P)
        w = jnp.concatenate([blocks(w_re), blocks(w_im)], axis=2)
        return jnp.where(half, w[:, None], 0.0).reshape(J, 2 * Q * Hc, STATE_LANES)

    def out_weights(w_re, w_im):
        def blocks(w):
            return jnp.einsum('jqcp,qr->jqprc', w.reshape(J, Q, Hc, P),
                              eye).reshape(J, Q * P, Q * Hc)
        w = jnp.concatenate([blocks(w_re), -blocks(w_im)], axis=1)
        half_c = jnp.swapaxes(half, 1, 2)
        return jnp.where(half_c, w[:, :, None, :], 0.0).reshape(J, STATE_LANES, 2 * Q * Hc)

    wb = jnp.concatenate([in_weights(abb_re, abb_im), in_weights(bb_re, bb_im)], axis=1)
    wc = jnp.concatenate([out_weights(ca_re, ca_im), out_weights(ca2_re, ca2_im)], axis=2)
    k0 = jnp.einsum('gop,gpc->goc', cr, bb_re) - jnp.einsum('gop,gpc->goc', ci, bb_im)
    k1 = jnp.einsum('gop,gpc->goc', cr, abb_re) - jnp.einsum('gop,gpc->goc', ci, abb_im)
    n_blocks, per = D_SSM // LANES, LANES // Hc
    eye8 = jnp.eye(per, dtype=F32)

    def lane_block(k):
        return jnp.einsum('bgoc,gh->bgcho', k.reshape(n_blocks, per, Hc, Hc),
                          eye8).reshape(n_blocks, LANES, LANES)

    zero = jnp.zeros((n_blocks, LANES, LANES), F32)
    wd = jnp.concatenate([jnp.concatenate([lane_block(k0), lane_block(k1)], axis=2),
                          jnp.concatenate([zero, lane_block(k0)], axis=2)], axis=1)
    return a2_tile, wb.astype(BF16), wc.astype(BF16), wd.astype(BF16)


def _ffn_kernel(x_ref, hm_ref, y_ref, wo_ref, g2_ref, w1_ref, w2_ref, g3_ref, out_ref):
    mixed = (lax.dot_general(hm_ref[...], wo_ref[0:D_MLSTM, :], (((0,), (0,)), ((), ())),
                             preferred_element_type=F32)
             + jnp.dot(y_ref[...], wo_ref[D_MLSTM:D_MODEL, :], preferred_element_type=F32))
    x1 = x_ref[...] + mixed
    h2 = _rms(x1, g2_ref[...]).astype(BF16)
    mlp = None
    for c in range(D_FF // FF_CHUNK):
        a = jnp.dot(h2, w1_ref[:, c * FF_CHUNK:(c + 1) * FF_CHUNK],
                    preferred_element_type=F32)
        a = jnp.maximum(a, 0.0)
        part = jnp.dot((a * a).astype(BF16), w2_ref[c * FF_CHUNK:(c + 1) * FF_CHUNK, :],
                       preferred_element_type=F32)
        mlp = part if mlp is None else mlp + part
    out_ref[...] = _rms(x1 + mlp, g3_ref[...])


def _ffn_call(x, hm, y, wo, g2, w1, w2, g3):
    S = x.shape[0]
    T = FFN_ROWS
    single = dict(pipeline_mode=pl.Buffered(WEIGHT_BUFFERS))
    return pl.pallas_call(
        _ffn_kernel,
        grid=(S // T,),
        in_specs=[_row_spec(T, D_MODEL), _lane_spec(D_MLSTM, T), _row_spec(T, D_SSM),
                  pl.BlockSpec(wo.shape, lambda i: (0, 0), **single),
                  _const_spec(g2.shape),
                  pl.BlockSpec(w1.shape, lambda i: (0, 0), **single),
                  pl.BlockSpec(w2.shape, lambda i: (0, 0), **single),
                  _const_spec(g3.shape)],
        out_specs=_row_spec(T, D_MODEL),
        out_shape=jax.ShapeDtypeStruct((S, D_MODEL), F32),
        compiler_params=_params(),
        name="ffn",
    )(x, hm, y, wo, g2, w1, w2, g3)


def _layer(x, mix_norm_w, w_in, conv_w, conv_b, i_bias, f_bias, mlstm_norm_w,
           lam_re, lam_im, log_dt, b_re, b_im, c_re, c_im, ssm_d,
           glu_w, glu_b, w_out, mlp_norm_w, w_ff1, w_ff2, out_norm_w):
    n_qk, n_vo, n_gate = 2 * D_MLSTM, 2 * D_MLSTM, 2 * N_HEADS
    assert w_in.shape == (D_MODEL, n_qk + n_vo + n_gate + D_SSM) and w_in.dtype == F32
    wg = w_in[:, n_qk + n_vo:n_qk + n_vo + n_gate].T.astype(BF16)
    qk, u, vt, ot, gr, wo_bf, w1_bf, w2_bf = _proj_call(
        x, mix_norm_w[None, :], w_in, wg, w_out, w_ff1, w_ff2)

    bias_r = jnp.concatenate([i_bias, f_bias]).astype(F32)[:, None]
    nw_cols = jnp.broadcast_to(mlstm_norm_w.astype(F32)[:, None], (D_MLSTM, LANES))
    hm = _mlstm_call(qk, vt, ot, gr, conv_w, conv_b[None, :], bias_r, nw_cols)

    a2_tile, wb, wc, wd = _s5_weights(lam_re, lam_im, log_dt, b_re, b_im, c_re, c_im)
    y = _s5_call(u, a2_tile, wb, wc, wd, ssm_d[None, :], glu_w.astype(BF16),
                 glu_b[None, :])

    return _ffn_call(x, hm, y, wo_bf, mlp_norm_w[None, :], w1_bf, w2_bf,
                     out_norm_w[None, :])


def kernel(x, mix_norm_w, w_in, conv_w, conv_b, i_bias, f_bias, mlstm_norm_w, ssm_lam_re, ssm_lam_im, ssm_log_dt, ssm_b_re, ssm_b_im, ssm_c_re, ssm_c_im, ssm_d, glu_w, glu_b, w_out, mlp_norm_w, w_ff1, w_ff2, final_norm_w):
    assert x.shape[0] == 1 and mix_norm_w.shape[0] == 1
    xs = x[0]
    out = _layer(xs, mix_norm_w[0], w_in[0], conv_w[0], conv_b[0], i_bias[0], f_bias[0],
                 mlstm_norm_w[0], ssm_lam_re[0], ssm_lam_im[0], ssm_log_dt[0],
                 ssm_b_re[0], ssm_b_im[0], ssm_c_re[0], ssm_c_im[0], ssm_d[0],
                 glu_w[0], glu_b[0], w_out[0], mlp_norm_w[0], w_ff1[0], w_ff2[0],
                 final_norm_w)
    return out[None]
```

```python
import math

import jax
import jax.numpy as jnp
from jax import lax
from jax.experimental import pallas as pl
from jax.experimental.pallas import tpu as pltpu

F32 = jnp.float32
BF16 = jnp.bfloat16

D_MODEL = 1024
D_MLSTM = 512
D_QK = 2 * D_MLSTM
N_HEADS = 4
HEAD_DIM = 128
CONV_WIDTH = 4
D_SSM = 512
SSM_GROUP = 16
N_GROUPS = 32
SSM_STATE = 64
D_FF = 4096
EPS = 1e-6

SUBLANES = 8
LANES = 128

PROJ_ROWS = 1024
MLSTM_ROWS = 1024
MLSTM_CHUNK = 128
S5_ROWS = 1024
S5_SUB = 512
S5_PITCH = S5_SUB + 4
FFN_ROWS = 1024
FF_CHUNK = 1024
WEIGHT_BUFFERS = 1
VMEM_LIMIT = 56 * 1024 * 1024

GROUPS_PER_SUBLANE = N_GROUPS // SUBLANES
STATE_HALF = GROUPS_PER_SUBLANE * SSM_STATE
STATE_LANES = 2 * STATE_HALF
STATE_BLOCKS = STATE_LANES // LANES


def _const_spec(shape):
    return pl.BlockSpec(shape, lambda i: (0,) * len(shape))


def _row_spec(rows, cols):
    return pl.BlockSpec((rows, cols), lambda i: (i, 0))


def _lane_spec(rows, cols):
    return pl.BlockSpec((rows, cols), lambda i: (0, i))


def _params():
    return pltpu.CompilerParams(dimension_semantics=("arbitrary",),
                                vmem_limit_bytes=VMEM_LIMIT)


def _rms(x, g):
    r = lax.rsqrt(jnp.mean(x * x, axis=-1, keepdims=True) + EPS)
    return x * r * g


def _sigmoid(x):
    return 0.5 * jnp.tanh(0.5 * x) + 0.5


_NT_DIMS = (((1,), (1,)), ((), ()))


X_SLOTS = 3


def _proj_kernel(x_hbm, g_ref, w_ref, wt_ref, wg_ref, wo_ref, w1_ref, w2_ref,
                 qk_ref, u_ref, vt_ref, ot_ref, gr_ref, wo_bf_ref, w1_bf_ref, w2_bf_ref,
                 xbuf_ref, xsem):
    T = PROJ_ROWS
    step = pl.program_id(0)
    n_steps = pl.num_programs(0)

    def x_copy(tile):
        slot = lax.rem(tile, X_SLOTS)
        return pltpu.make_async_copy(x_hbm.at[pl.ds(tile * T, T), :], xbuf_ref.at[slot],
                                     xsem.at[slot])

    @pl.when(step == 0)
    def _():
        x_copy(0).start()
        x_copy(1).start()

    @pl.when(step + 2 < n_steps)
    def _():
        x_copy(step + 2).start()

    wo_bf_ref[...] = wo_ref[...].astype(BF16)
    w1_bf_ref[...] = w1_ref[...].astype(BF16)
    w2_bf_ref[...] = w2_ref[...].astype(BF16)

    x_copy(step).wait()
    x = xbuf_ref[lax.rem(step, X_SLOTS)]
    h = _rms(x, g_ref[...]).astype(BF16)
    qk_ref[...] = jnp.dot(h, w_ref[:, 0:D_QK], preferred_element_type=F32)
    u = jnp.dot(h, w_ref[:, D_QK:D_QK + D_SSM], preferred_element_type=F32)
    for b in range(D_SSM // LANES):
        u_ref[b] = u[:, b * LANES:(b + 1) * LANES]
    vt_ref[...] = lax.dot_general(wt_ref[0:D_MLSTM, :], h, _NT_DIMS,
                                  preferred_element_type=F32)
    ot_ref[...] = lax.dot_general(wt_ref[D_MLSTM:2 * D_MLSTM, :], h, _NT_DIMS,
                                  preferred_element_type=F32)
    gr_ref[...] = lax.dot_general(wg_ref[...], h, _NT_DIMS, preferred_element_type=F32)


def _proj_call(x, g, w, wt, wg, wo, w1, w2):
    S = x.shape[0]
    T = PROJ_ROWS
    n = S // T
    assert S % T == 0 and n >= 2
    wo_spec = _row_spec(wo.shape[0] // n, wo.shape[1])
    w1_spec = pl.BlockSpec((w1.shape[0], w1.shape[1] // n), lambda i: (0, i))
    w2_spec = _row_spec(w2.shape[0] // n, w2.shape[1])
    return pl.pallas_call(
        _proj_kernel,
        grid=(n,),
        in_specs=[pl.BlockSpec(memory_space=pl.ANY), _const_spec((1, D_MODEL)),
                  _const_spec(w.shape), _const_spec(wt.shape), _const_spec(wg.shape),
                  wo_spec, w1_spec, w2_spec],
        scratch_shapes=[pltpu.VMEM((X_SLOTS, T, D_MODEL), F32),
                        pltpu.SemaphoreType.DMA((X_SLOTS,))],
        out_specs=[_row_spec(T, D_QK),
                   pl.BlockSpec((D_SSM // LANES, T, LANES), lambda i: (0, i, 0)),
                   _lane_spec(D_MLSTM, T), _lane_spec(D_MLSTM, T), _lane_spec(SUBLANES, T),
                   wo_spec, w1_spec, w2_spec],
        out_shape=[jax.ShapeDtypeStruct((S, D_QK), F32),
                   jax.ShapeDtypeStruct((D_SSM // LANES, S, LANES), F32),
                   jax.ShapeDtypeStruct((D_MLSTM, S), F32),
                   jax.ShapeDtypeStruct((D_MLSTM, S), F32),
                   jax.ShapeDtypeStruct((SUBLANES, S), F32),
                   jax.ShapeDtypeStruct(wo.shape, BF16),
                   jax.ShapeDtypeStruct(w1.shape, BF16),
                   jax.ShapeDtypeStruct(w2.shape, BF16)],
        compiler_params=_params(),
        name="proj",
    )(x, g, w, wt, wg, wo, w1, w2)


def _log_sigmoid(x):
    return jnp.minimum(x, 0.0) - jnp.log1p(jnp.exp(-jnp.abs(x)))


def _interleave(*stages):
    total = max(n for _, n in stages)
    done = [0] * len(stages)
    for tick in range(1, total + 1):
        for s, (gen, n) in enumerate(stages):
            want = -(-tick * n // total)
            while done[s] < want:
                next(gen, None)
                done[s] += 1
    for gen, _ in stages:
        for _ in gen:
            pass


_LOG2E = math.log2(math.e)
_CUMMAX_LEVELS = LANES.bit_length() - 1
_GATE_PIECES = 3 + _CUMMAX_LEVELS
_QK_BLOCKS = 2 * D_MLSTM // LANES


def _conv_pieces(qk_ref, cw_ref, cb_ref, xpad_ref, qks_ref):
    T = MLSTM_ROWS
    L = MLSTM_CHUNK
    for k in range(_QK_BLOCKS):
        xpad_ref[k, SUBLANES:SUBLANES + T, :] = qk_ref[:, k * LANES:(k + 1) * LANES]
    yield
    for c in range(T // L):
        for k in range(_QK_BLOCKS):
            cols = slice(k * LANES, (k + 1) * LANES)
            for parity in range(2):
                acc = cb_ref[:, cols]
                for j in range(CONV_WIDTH):
                    start = c * L + SUBLANES - (CONV_WIDTH - 1) + parity + j
                    acc = acc + (xpad_ref[k, pl.ds(start, L // 2, stride=2), :]
                                 * cw_ref[j:j + 1, cols])
                act = acc * _sigmoid(acc)
                if k < _QK_BLOCKS // 2:
                    act = act * (HEAD_DIM ** -0.5)
                qks_ref[k, pl.ds(c * L + parity, L // 2, stride=2), :] = act
            yield
    for k in range(_QK_BLOCKS):
        xpad_ref[k, 0:SUBLANES, :] = xpad_ref[k, T:T + SUBLANES, :]


def _gate_pieces(gr_ref, br_ref, m_ref, out):
    T = MLSTM_ROWS
    L = MLSTM_CHUNK
    n_chunks = T // L
    lane = lax.broadcasted_iota(jnp.int32, (SUBLANES, L), 1)
    row_id = lax.broadcasted_iota(jnp.int32, (L, L), 0)
    col_id = lax.broadcasted_iota(jnp.int32, (L, L), 1)
    triu = (row_id <= col_id).astype(BF16)
    g = gr_ref[...] + br_ref[...]
    logf = _log_sigmoid(g)
    yield
    lf = jnp.concatenate([pltpu.roll(logf[:, c * L:(c + 1) * L], N_HEADS, axis=0)
                          for c in range(n_chunks)], axis=0)
    hi = lf.astype(BF16)
    rest = lf - hi.astype(F32)
    mid = rest.astype(BF16)
    lo = (rest - mid.astype(F32)).astype(BF16)
    terms = jnp.dot(jnp.concatenate([hi, mid, lo], axis=0), triu,
                    preferred_element_type=F32)
    n_rows = n_chunks * SUBLANES
    b_all = terms[0:n_rows] + terms[n_rows:2 * n_rows] + terms[2 * n_rows:3 * n_rows]
    bs = [b_all[c * SUBLANES:(c + 1) * SUBLANES] for c in range(n_chunks)]
    rs = [g[:, c * L:(c + 1) * L] - bs[c] for c in range(n_chunks)]
    yield
    m_runs = list(rs)
    for level in range(_CUMMAX_LEVELS):
        d = 1 << level
        m_runs = [jnp.where(lane >= d, jnp.maximum(x, pltpu.roll(x, d, axis=1)), x)
                  for x in m_runs]
        yield
    m = m_ref[...]
    rows = []
    for c in range(n_chunks):
        b, r, m_run = bs[c], rs[c], m_runs[c]
        m_last = jnp.broadcast_to(m_run[:, L - 1:L], (SUBLANES, L))
        b_last = jnp.broadcast_to(b[:, L - 1:L], (SUBLANES, L))
        mu = jnp.maximum(m_run, m)
        mx = jnp.maximum(m, m_last)
        rows.append(dict(
            mu_log2=mu * _LOG2E,
            inter_scale=jnp.exp(m - mu),
            eb=jnp.exp(-b - mu),
            w_new=jnp.exp(r - mx),
            s_old=jnp.exp(m - mx)))
        m = b_last + mx
    m_ref[...] = m
    r_pad = [jnp.zeros((SUBLANES, L), F32)] * (LANES // SUBLANES - n_chunks)
    out["rows"] = rows
    out["r_cols"] = (jnp.concatenate(rs + r_pad, axis=0) * _LOG2E).T


_GATE_ROWS = ("mu_log2", "inter_scale", "eb", "w_new", "s_old")


def _store_gates(gates, grow_ref, rcol_ref):
    for c, chunk in enumerate(gates["rows"]):
        for q, name in enumerate(_GATE_ROWS):
            grow_ref[c * len(_GATE_ROWS) + q] = chunk[name]
    rcol_ref[...] = gates["r_cols"]


def _head_pieces(qks_ref, vt_ref, ot_ref, nw_ref, grow_ref, rcol_ref, c_ref, out_ref):
    L = MLSTM_CHUNK
    D = HEAD_DIM
    row_id = lax.broadcasted_iota(jnp.int32, (L, L), 0)
    col_id = lax.broadcasted_iota(jnp.int32, (L, L), 1)
    causal = row_id <= col_id
    neg_inf = jnp.float32(-jnp.inf)
    for c in range(MLSTM_ROWS // L):
        for h in range(N_HEADS):
            row = lambda name: grow_ref[c * len(_GATE_ROWS) + _GATE_ROWS.index(name),
                                        h:h + 1, :]
            q = qks_ref[h, c * L:(c + 1) * L, :].astype(BF16)
            k = qks_ref[N_HEADS + h, c * L:(c + 1) * L, :].astype(BF16)
            vt = vt_ref[h * D:(h + 1) * D, c * L:(c + 1) * L]
            r_col = rcol_ref[:, c * SUBLANES + h:c * SUBLANES + h + 1]

            dmat = jnp.exp2(jnp.where(causal, r_col - row("mu_log2"), neg_inf))
            s_kq = lax.dot_general(k, q, _NT_DIMS, preferred_element_type=F32) * dmat
            cq = lax.dot_general(c_ref[h].astype(BF16), q, _NT_DIMS,
                                 preferred_element_type=F32)
            isc = row("inter_scale")
            num = isc * cq[0:D, :] + jnp.dot(vt.astype(BF16), s_kq.astype(BF16),
                                             preferred_element_type=F32)
            den = isc * cq[D:D + 1, :] + jnp.sum(s_kq, axis=0, keepdims=True)
            hh = num * (1.0 / jnp.maximum(jnp.abs(den), row("eb")))

            w = row("w_new")
            vw = jnp.concatenate([vt * w, jnp.broadcast_to(w, (SUBLANES, L))],
                                 axis=0).astype(BF16)
            c_loc = jnp.dot(vw, k, preferred_element_type=F32)
            c_ref[h] = row("s_old") * c_ref[h] + c_loc

            mean = jnp.mean(hh, axis=0, keepdims=True)
            cen = hh - mean
            var = jnp.mean(cen * cen, axis=0, keepdims=True)
            hn = cen * lax.rsqrt(var + EPS) * nw_ref[h * D:(h + 1) * D, :]
            gate = _sigmoid(ot_ref[h * D:(h + 1) * D, c * L:(c + 1) * L])
            out_ref[h * D:(h + 1) * D, c * L:(c + 1) * L] = (hn * gate).astype(BF16)
            yield


def _mlstm_kernel(qk_ref, vt_ref, ot_ref, gr_ref, gr_next_ref, cw_ref, cb_ref, br_ref, nw_ref,
                  out_ref, xpad_ref, qks_ref, c_ref, m_ref, grow_ref, rcol_ref):
    n_chunks = MLSTM_ROWS // MLSTM_CHUNK

    @pl.when(pl.program_id(0) == 0)
    def _():
        xpad_ref[:, 0:SUBLANES, :] = jnp.zeros((_QK_BLOCKS, SUBLANES, LANES), F32)
        c_ref[...] = jnp.zeros(c_ref.shape, F32)
        m_ref[...] = jnp.zeros(m_ref.shape, F32)
        first = {}
        for _ in _gate_pieces(gr_ref, br_ref, m_ref, first):
            pass
        _store_gates(first, grow_ref, rcol_ref)

    for _ in _conv_pieces(qk_ref, cw_ref, cb_ref, xpad_ref, qks_ref):
        pass
    nxt = {}
    _interleave((_gate_pieces(gr_next_ref, br_ref, m_ref, nxt), _GATE_PIECES),
                (_head_pieces(qks_ref, vt_ref, ot_ref, nw_ref, grow_ref, rcol_ref, c_ref,
                              out_ref), n_chunks * N_HEADS))
    _store_gates(nxt, grow_ref, rcol_ref)


def _mlstm_call(qk, vt, ot, gr, cw, cb, bias_r, nw_cols):
    S = qk.shape[0]
    T = MLSTM_ROWS
    n = S // T
    n_chunks = T // MLSTM_CHUNK
    return pl.pallas_call(
        _mlstm_kernel,
        grid=(n,),
        in_specs=[_row_spec(T, D_QK), _lane_spec(D_MLSTM, T), _lane_spec(D_MLSTM, T),
                  _lane_spec(SUBLANES, T),
                  pl.BlockSpec((SUBLANES, T), lambda i: (0, jnp.minimum(i + 1, n - 1))),
                  _const_spec(cw.shape), _const_spec(cb.shape),
                  _const_spec(bias_r.shape), _const_spec(nw_cols.shape)],
        out_specs=_lane_spec(D_MLSTM, T),
        out_shape=jax.ShapeDtypeStruct((D_MLSTM, S), BF16),
        scratch_shapes=[pltpu.VMEM((_QK_BLOCKS, T + SUBLANES, LANES), F32),
                        pltpu.VMEM((_QK_BLOCKS, T, LANES), F32),
                        pltpu.VMEM((N_HEADS, HEAD_DIM + SUBLANES, HEAD_DIM), F32),
                        pltpu.VMEM((SUBLANES, LANES), F32),
                        pltpu.VMEM((n_chunks * len(_GATE_ROWS), SUBLANES, MLSTM_CHUNK), F32),
                        pltpu.VMEM((MLSTM_CHUNK, LANES), F32)],
        compiler_params=_params(),
        name="mlstm",
    )(qk, vt, ot, gr, gr, cw, cb, bias_r, nw_cols)


def _gelu_tanh(x):
    c = math.sqrt(2.0 / math.pi)
    return x * (0.5 * (1.0 + jnp.tanh(c * (x + 0.044715 * (x * x * x)))))


def _s5_kernel(u_ref, a2_ref, wb_ref, wc_ref, wd_ref, d_ref, gw_ref, gb_ref, out_ref,
               x_ref, y_ref, *z_refs):
    T = S5_ROWS

    @pl.when(pl.program_id(0) == 0)
    def _():
        x_ref[...] = jnp.zeros(x_ref.shape, F32)

    R = S5_SUB
    P = R // 2
    n_sub = T // R
    half = STATE_BLOCKS // 2
    n_blocks = D_SSM // LANES
    a_re = [a2_ref[:, c * LANES:(c + 1) * LANES] for c in range(half)]
    a_im = [a2_ref[:, (half + c) * LANES:(half + c + 1) * LANES] for c in range(half)]

    def pair_rows(sb, b):
        even = u_ref[b, pl.ds(sb * R, P, stride=2), :]
        odd = u_ref[b, pl.ds(sb * R + 1, P, stride=2), :]
        return jnp.concatenate([even, odd], axis=1).astype(BF16)

    def expand_piece(sb, j):
        w = jnp.dot(pair_rows(sb, j // 2), wb_ref[j], preferred_element_type=F32)
        for c in range(STATE_BLOCKS):
            z_refs[sb][c // 2, pl.ds(j * S5_PITCH + c % 2, P, stride=2), :] = (
                w[:, c * LANES:(c + 1) * LANES])

    def scan_piece(sb, x, i0, i1):
        z_ref = z_refs[sb]
        for i in range(i0, i1):
            tile = lambda c: (c // 2, pl.ds(2 * i + c % 2, SUBLANES, stride=S5_PITCH),
                              slice(None))
            new = list(x)
            for c in range(half):
                x_re, x_im = x[c], x[half + c]
                new[c] = a_re[c] * x_re - a_im[c] * x_im + z_ref[tile(c)]
                new[half + c] = a_re[c] * x_im + a_im[c] * x_re + z_ref[tile(half + c)]
                z_ref[tile(c)] = x_re
                z_ref[tile(half + c)] = x_im
            x = new
        return x

    def contract_piece(sb, j):
        xs = jnp.concatenate(
            [z_refs[sb][c // 2, pl.ds(j * S5_PITCH + c % 2, P, stride=2), :]
             for c in range(STATE_BLOCKS)], axis=1).astype(BF16)
        return jnp.dot(xs, wc_ref[j], preferred_element_type=F32)

    def finish(sb, parts):
        for b in range(n_blocks):
            yb = (parts[2 * b] + parts[2 * b + 1]
                  + jnp.dot(pair_rows(sb, b), wd_ref[b], preferred_element_type=F32))
            y_ref[b, pl.ds(sb * R, P, stride=2), :] = yb[:, 0:LANES]
            y_ref[b, pl.ds(sb * R + 1, P, stride=2), :] = yb[:, LANES:2 * LANES]
        rows = slice(sb * R, (sb + 1) * R)
        y = jnp.concatenate([y_ref[b, rows, :] for b in range(n_blocks)], axis=1)
        u = jnp.concatenate([u_ref[b, rows, :] for b in range(n_blocks)], axis=1)
        z = _gelu_tanh(y + d_ref[...] * u)
        gate = _sigmoid(jnp.dot(z.astype(BF16), gw_ref[...],
                                preferred_element_type=F32) + gb_ref[...])
        out_ref[rows, :] = (z * gate).astype(BF16)

    x = [x_ref[:, c * LANES:(c + 1) * LANES] for c in range(STATE_BLOCKS)]
    steps = P // SUBLANES
    for j in range(SUBLANES):
        expand_piece(0, j)
    for sb in range(n_sub + 1):
        parts = []
        for j in range(SUBLANES):
            if sb < n_sub:
                x = scan_piece(sb, x, j * steps, (j + 1) * steps)
            if sb + 1 < n_sub:
                expand_piece(sb + 1, j)
            if sb >= 1:
                parts.append(contract_piece(sb - 1, j))
        if sb >= 1:
            finish(sb - 1, parts)
    for c in range(STATE_BLOCKS):
        x_ref[:, c * LANES:(c + 1) * LANES] = x[c]


def _s5_call(u, a2_tile, wb, wc, wd, d, gw, gb):
    S = u.shape[1]
    T = S5_ROWS
    n_blocks = D_SSM // LANES
    return pl.pallas_call(
        _s5_kernel,
        grid=(S // T,),
        in_specs=[pl.BlockSpec((n_blocks, T, LANES), lambda i: (0, i, 0)),
                  _const_spec(a2_tile.shape), _const_spec(wb.shape), _const_spec(wc.shape),
                  _const_spec(wd.shape), _const_spec(d.shape), _const_spec(gw.shape),
                  _const_spec(gb.shape)],
        out_specs=_row_spec(T, D_SSM),
        out_shape=jax.ShapeDtypeStruct((S, D_SSM), BF16),
        scratch_shapes=([pltpu.VMEM((SUBLANES, STATE_LANES), F32),
                         pltpu.VMEM((n_blocks, T, LANES), F32)]
                        + [pltpu.VMEM((STATE_BLOCKS // 2, SUBLANES * S5_PITCH, LANES), F32)]
                        * (T // S5_SUB)),
        compiler_params=_params(),
        name="s5",
    )(u, a2_tile, wb, wc, wd, d, gw, gb)


def _s5_weights(lam_re, lam_im, log_dt, b_re, b_im, c_re, c_im):
    cmul = lambda xr, xi, yr, yi: (xr * yr - xi * yi, xr * yi + xi * yr)
    lr, li = lam_re.astype(F32), lam_im.astype(F32)
    dt = jnp.exp(log_dt.astype(F32))[:, None]
    mag = jnp.exp(lr * dt)
    a_re, a_im = mag * jnp.cos(li * dt), mag * jnp.sin(li * dt)
    a2_re, a2_im = cmul(a_re, a_im, a_re, a_im)
    den = lr * lr + li * li
    coef_re = ((a_re - 1.0) * lr + a_im * li) / den
    coef_im = (a_im * lr - (a_re - 1.0) * li) / den
    bb_re, bb_im = cmul(coef_re[..., None], coef_im[..., None],
                        b_re.astype(F32), b_im.astype(F32))
    abb_re, abb_im = cmul(a_re[..., None], a_im[..., None], bb_re, bb_im)
    cr, ci = c_re.astype(F32), c_im.astype(F32)
    ca_re, ca_im = cmul(cr, ci, a_re[:, None, :], a_im[:, None, :])
    ca2_re, ca2_im = cmul(cr, ci, a2_re[:, None, :], a2_im[:, None, :])

    J, Q, P, Hc = SUBLANES, GROUPS_PER_SUBLANE, SSM_STATE, SSM_GROUP
    a2_tile = jnp.concatenate([a2_re.reshape(J, Q * P), a2_im.reshape(J, Q * P)], axis=1)
    eye = jnp.eye(Q, dtype=F32)
    half = (jnp.arange(J) % 2)[:, None, None, None] == jnp.arange(2)[None, :, None, None]

    def in_weights(w_re, w_im):
        def blocks(w):
            return jnp.einsum('jqpc,qr->jqcrp', w.reshape(J, Q, P, Hc),
                              eye).reshape(J, Q * Hc, Q * P)
        w = jnp.concatenate([blocks(w_re), blocks(w_im)], axis=2)
        return jnp.where(half, w[:, None], 0.0).reshape(J, 2 * Q * Hc, STATE_LANES)

    def out_weights(w_re, w_im):
        def blocks(w):
            return jnp.einsum('jqcp,qr->jqprc', w.reshape(J, Q, Hc, P),
                              eye).reshape(J, Q * P, Q * Hc)
        w = jnp.concatenate([blocks(w_re), -blocks(w_im)], axis=1)
        half_c = jnp.swapaxes(half, 1, 2)
        return jnp.where(half_c, w[:, :, None, :], 0.0).reshape(J, STATE_LANES, 2 * Q * Hc)

    wb = jnp.concatenate([in_weights(abb_re, abb_im), in_weights(bb_re, bb_im)], axis=1)
    wc = jnp.concatenate([out_weights(ca_re, ca_im), out_weights(ca2_re, ca2_im)], axis=2)
    k0 = jnp.einsum('gop,gpc->goc', cr, bb_re) - jnp.einsum('gop,gpc->goc', ci, bb_im)
    k1 = jnp.einsum('gop,gpc->goc', cr, abb_re) - jnp.einsum('gop,gpc->goc', ci, abb_im)
    n_blocks, per = D_SSM // LANES, LANES // Hc
    eye8 = jnp.eye(per, dtype=F32)

    def lane_block(k):
        return jnp.einsum('bgoc,gh->bgcho', k.reshape(n_blocks, per, Hc, Hc),
                          eye8).reshape(n_blocks, LANES, LANES)

    zero = jnp.zeros((n_blocks, LANES, LANES), F32)
    wd = jnp.concatenate([jnp.concatenate([lane_block(k0), lane_block(k1)], axis=2),
                          jnp.concatenate([zero, lane_block(k0)], axis=2)], axis=1)
    return a2_tile, wb.astype(BF16), wc.astype(BF16), wd.astype(BF16)


def _ffn_kernel(x_ref, hm_ref, y_ref, wo_ref, g2_ref, w1_ref, w2_ref, g3_ref, out_ref):
    mixed = (lax.dot_general(hm_ref[...], wo_ref[0:D_MLSTM, :], (((0,), (0,)), ((), ())),
                             preferred_element_type=F32)
             + jnp.dot(y_ref[...], wo_ref[D_MLSTM:D_MODEL, :], preferred_element_type=F32))
    x1 = x_ref[...] + mixed
    h2 = _rms(x1, g2_ref[...]).astype(BF16)
    mlp = None
    for c in range(D_FF // FF_CHUNK):
        a = jnp.dot(h2, w1_ref[:, c * FF_CHUNK:(c + 1) * FF_CHUNK],
                    preferred_element_type=F32)
        a = jnp.maximum(a, 0.0)
        part = jnp.dot((a * a).astype(BF16), w2_ref[c * FF_CHUNK:(c + 1) * FF_CHUNK, :],
                       preferred_element_type=F32)
        mlp = part if mlp is None else mlp + part
    out_ref[...] = _rms(x1 + mlp, g3_ref[...])


def _ffn_call(x, hm, y, wo, g2, w1, w2, g3):
    S = x.shape[0]
    T = FFN_ROWS
    single = dict(pipeline_mode=pl.Buffered(WEIGHT_BUFFERS))
    return pl.pallas_call(
        _ffn_kernel,
        grid=(S // T,),
        in_specs=[_row_spec(T, D_MODEL), _lane_spec(D_MLSTM, T), _row_spec(T, D_SSM),
                  pl.BlockSpec(wo.shape, lambda i: (0, 0), **single),
                  _const_spec(g2.shape),
                  pl.BlockSpec(w1.shape, lambda i: (0, 0), **single),
                  pl.BlockSpec(w2.shape, lambda i: (0, 0), **single),
                  _const_spec(g3.shape)],
        out_specs=_row_spec(T, D_MODEL),
        out_shape=jax.ShapeDtypeStruct((S, D_MODEL), F32),
        compiler_params=_params(),
        name="ffn",
    )(x, hm, y, wo, g2, w1, w2, g3)


def _layer(x, mix_norm_w, w_in, conv_w, conv_b, i_bias, f_bias, mlstm_norm_w,
           lam_re, lam_im, log_dt, b_re, b_im, c_re, c_im, ssm_d,
           glu_w, glu_b, w_out, mlp_norm_w, w_ff1, w_ff2, out_norm_w):
    n_qk, n_vo, n_gate = 2 * D_MLSTM, 2 * D_MLSTM, 2 * N_HEADS
    w_rows = jnp.concatenate([w_in[:, :n_qk], w_in[:, n_qk + n_vo + n_gate:]],
                             axis=1).astype(BF16)
    w_cols = w_in[:, n_qk:n_qk + n_vo].T.astype(BF16)
    wg = w_in[:, n_qk + n_vo:n_qk + n_vo + n_gate].T.astype(BF16)
    qk, u, vt, ot, gr, wo_bf, w1_bf, w2_bf = _proj_call(
        x, mix_norm_w[None, :], w_rows, w_cols, wg, w_out, w_ff1, w_ff2)

    bias_r = jnp.concatenate([i_bias, f_bias]).astype(F32)[:, None]
    nw_cols = jnp.broadcast_to(mlstm_norm_w.astype(F32)[:, None], (D_MLSTM, LANES))
    hm = _mlstm_call(qk, vt, ot, gr, conv_w, conv_b[None, :], bias_r, nw_cols)

    a2_tile, wb, wc, wd = _s5_weights(lam_re, lam_im, log_dt, b_re, b_im, c_re, c_im)
    y = _s5_call(u, a2_tile, wb, wc, wd, ssm_d[None, :], glu_w.astype(BF16),
                 glu_b[None, :])

    return _ffn_call(x, hm, y, wo_bf, mlp_norm_w[None, :], w1_bf, w2_bf,
                     out_norm_w[None, :])


def kernel(x, mix_norm_w, w_in, conv_w, conv_b, i_bias, f_bias, mlstm_norm_w, ssm_lam_re, ssm_lam_im, ssm_log_dt, ssm_b_re, ssm_b_im, ssm_c_re, ssm_c_im, ssm_d, glu_w, glu_b, w_out, mlp_norm_w, w_ff1, w_ff2, final_norm_w):
    assert x.shape[0] == 1 and mix_norm_w.shape[0] == 1
    xs = x[0]
    out = _layer(xs, mix_norm_w[0], w_in[0], conv_w[0], conv_b[0], i_bias[0], f_bias[0],
                 mlstm_norm_w[0], ssm_lam_re[0], ssm_lam_im[0], ssm_log_dt[0],
                 ssm_b_re[0], ssm_b_im[0], ssm_c_re[0], ssm_c_im[0], ssm_d[0],
                 glu_w[0], glu_b[0], w_out[0], mlp_norm_w[0], w_ff1[0], w_ff2[0],
                 final_norm_w)
    return out[None]
```

```python
import math

import jax
import jax.numpy as jnp
from jax import lax
from jax.experimental import pallas as pl
from jax.experimental.pallas import tpu as pltpu

F32 = jnp.float32
BF16 = jnp.bfloat16

D_MODEL = 1024
D_MLSTM = 512
D_QK = 2 * D_MLSTM
N_HEADS = 4
HEAD_DIM = 128
CONV_WIDTH = 4
D_SSM = 512
SSM_GROUP = 16
N_GROUPS = 32
SSM_STATE = 64
D_FF = 4096
EPS = 1e-6

SUBLANES = 8
LANES = 128

PROJ_ROWS = 1024
MLSTM_ROWS = 1024
MLSTM_CHUNK = 128
S5_ROWS = 1024
S5_SUB = 512
S5_PITCH = S5_SUB + 4
FFN_ROWS = 1024
FF_CHUNK = 1024
WEIGHT_BUFFERS = 1
VMEM_LIMIT = 56 * 1024 * 1024

GROUPS_PER_SUBLANE = N_GROUPS // SUBLANES
STATE_HALF = GROUPS_PER_SUBLANE * SSM_STATE
STATE_LANES = 2 * STATE_HALF
STATE_BLOCKS = STATE_LANES // LANES


def _const_spec(shape):
    return pl.BlockSpec(shape, lambda i: (0,) * len(shape))


def _row_spec(rows, cols):
    return pl.BlockSpec((rows, cols), lambda i: (i, 0))


def _lane_spec(rows, cols):
    return pl.BlockSpec((rows, cols), lambda i: (0, i))


def _params():
    return pltpu.CompilerParams(dimension_semantics=("arbitrary",),
                                vmem_limit_bytes=VMEM_LIMIT)


def _rms(x, g):
    r = lax.rsqrt(jnp.mean(x * x, axis=-1, keepdims=True) + EPS)
    return x * r * g


def _sigmoid(x):
    return 0.5 * jnp.tanh(0.5 * x) + 0.5


_NT_DIMS = (((1,), (1,)), ((), ()))


def _proj_kernel(x_ref, g_ref, w_ref, wt_ref, wg_ref, wo_ref, w1_ref, w2_ref,
                 qk_ref, u_ref, vt_ref, ot_ref, gr_ref, wo_bf_ref, w1_bf_ref, w2_bf_ref):
    wo_bf_ref[...] = wo_ref[...].astype(BF16)
    w1_bf_ref[...] = w1_ref[...].astype(BF16)
    w2_bf_ref[...] = w2_ref[...].astype(BF16)

    h = _rms(x_ref[...], g_ref[...]).astype(BF16)
    qk_ref[...] = jnp.dot(h, w_ref[:, 0:D_QK], preferred_element_type=F32)
    u = jnp.dot(h, w_ref[:, D_QK:D_QK + D_SSM], preferred_element_type=F32)
    for b in range(D_SSM // LANES):
        u_ref[b] = u[:, b * LANES:(b + 1) * LANES]
    vt_ref[...] = lax.dot_general(wt_ref[0:D_MLSTM, :], h, _NT_DIMS,
                                  preferred_element_type=F32)
    ot_ref[...] = lax.dot_general(wt_ref[D_MLSTM:2 * D_MLSTM, :], h, _NT_DIMS,
                                  preferred_element_type=F32)
    gr_ref[...] = lax.dot_general(wg_ref[...], h, _NT_DIMS, preferred_element_type=F32)


def _proj_call(x, g, w, wt, wg, wo, w1, w2):
    S = x.shape[0]
    T = PROJ_ROWS
    n = S // T
    wo_spec = _row_spec(wo.shape[0] // n, wo.shape[1])
    w1_spec = pl.BlockSpec((w1.shape[0], w1.shape[1] // n), lambda i: (0, i))
    w2_spec = _row_spec(w2.shape[0] // n, w2.shape[1])
    return pl.pallas_call(
        _proj_kernel,
        grid=(n,),
        in_specs=[_row_spec(T, D_MODEL), _const_spec((1, D_MODEL)),
                  _const_spec(w.shape), _const_spec(wt.shape), _const_spec(wg.shape),
                  wo_spec, w1_spec, w2_spec],
        out_specs=[_row_spec(T, D_QK),
                   pl.BlockSpec((D_SSM // LANES, T, LANES), lambda i: (0, i, 0)),
                   _lane_spec(D_MLSTM, T), _lane_spec(D_MLSTM, T), _lane_spec(SUBLANES, T),
                   wo_spec, w1_spec, w2_spec],
        out_shape=[jax.ShapeDtypeStruct((S, D_QK), F32),
                   jax.ShapeDtypeStruct((D_SSM // LANES, S, LANES), F32),
                   jax.ShapeDtypeStruct((D_MLSTM, S), F32),
                   jax.ShapeDtypeStruct((D_MLSTM, S), F32),
                   jax.ShapeDtypeStruct((SUBLANES, S), F32),
                   jax.ShapeDtypeStruct(wo.shape, BF16),
                   jax.ShapeDtypeStruct(w1.shape, BF16),
                   jax.ShapeDtypeStruct(w2.shape, BF16)],
        compiler_params=_params(),
        name="proj",
    )(x, g, w, wt, wg, wo, w1, w2)


def _log_sigmoid(x):
    return jnp.minimum(x, 0.0) - jnp.log1p(jnp.exp(-jnp.abs(x)))


def _interleave(*stages):
    total = max(n for _, n in stages)
    done = [0] * len(stages)
    for tick in range(1, total + 1):
        for s, (gen, n) in enumerate(stages):
            want = -(-tick * n // total)
            while done[s] < want:
                next(gen, None)
                done[s] += 1
    for gen, _ in stages:
        for _ in gen:
            pass


_LOG2E = math.log2(math.e)
_CUMMAX_LEVELS = LANES.bit_length() - 1
_GATE_PIECES = 3 + _CUMMAX_LEVELS
_QK_BLOCKS = 2 * D_MLSTM // LANES


def _conv_pieces(qk_ref, cw_ref, cb_ref, xpad_ref, qks_ref):
    T = MLSTM_ROWS
    L = MLSTM_CHUNK
    for k in range(_QK_BLOCKS):
        xpad_ref[k, SUBLANES:SUBLANES + T, :] = qk_ref[:, k * LANES:(k + 1) * LANES]
    yield
    for c in range(T // L):
        for k in range(_QK_BLOCKS):
            cols = slice(k * LANES, (k + 1) * LANES)
            for parity in range(2):
                acc = cb_ref[:, cols]
                for j in range(CONV_WIDTH):
                    start = c * L + SUBLANES - (CONV_WIDTH - 1) + parity + j
                    acc = acc + (xpad_ref[k, pl.ds(start, L // 2, stride=2), :]
                                 * cw_ref[j:j + 1, cols])
                act = acc * _sigmoid(acc)
                if k < _QK_BLOCKS // 2:
                    act = act * (HEAD_DIM ** -0.5)
                qks_ref[k, pl.ds(c * L + parity, L // 2, stride=2), :] = act
            yield
    for k in range(_QK_BLOCKS):
        xpad_ref[k, 0:SUBLANES, :] = xpad_ref[k, T:T + SUBLANES, :]


def _gate_pieces(gr_ref, br_ref, m_ref, out):
    T = MLSTM_ROWS
    L = MLSTM_CHUNK
    n_chunks = T // L
    lane = lax.broadcasted_iota(jnp.int32, (SUBLANES, L), 1)
    row_id = lax.broadcasted_iota(jnp.int32, (L, L), 0)
    col_id = lax.broadcasted_iota(jnp.int32, (L, L), 1)
    triu = (row_id <= col_id).astype(BF16)
    g = gr_ref[...] + br_ref[...]
    logf = _log_sigmoid(g)
    yield
    lf = jnp.concatenate([pltpu.roll(logf[:, c * L:(c + 1) * L], N_HEADS, axis=0)
                          for c in range(n_chunks)], axis=0)
    hi = lf.astype(BF16)
    rest = lf - hi.astype(F32)
    mid = rest.astype(BF16)
    lo = (rest - mid.astype(F32)).astype(BF16)
    terms = jnp.dot(jnp.concatenate([hi, mid, lo], axis=0), triu,
                    preferred_element_type=F32)
    n_rows = n_chunks * SUBLANES
    b_all = terms[0:n_rows] + terms[n_rows:2 * n_rows] + terms[2 * n_rows:3 * n_rows]
    bs = [b_all[c * SUBLANES:(c + 1) * SUBLANES] for c in range(n_chunks)]
    rs = [g[:, c * L:(c + 1) * L] - bs[c] for c in range(n_chunks)]
    yield
    m_runs = list(rs)
    for level in range(_CUMMAX_LEVELS):
        d = 1 << level
        m_runs = [jnp.where(lane >= d, jnp.maximum(x, pltpu.roll(x, d, axis=1)), x)
                  for x in m_runs]
        yield
    m = m_ref[...]
    rows = []
    for c in range(n_chunks):
        b, r, m_run = bs[c], rs[c], m_runs[c]
        m_last = jnp.broadcast_to(m_run[:, L - 1:L], (SUBLANES, L))
        b_last = jnp.broadcast_to(b[:, L - 1:L], (SUBLANES, L))
        mu = jnp.maximum(m_run, m)
        mx = jnp.maximum(m, m_last)
        rows.append(dict(
            mu_log2=mu * _LOG2E,
            inter_scale=jnp.exp(m - mu),
            eb=jnp.exp(-b - mu),
            w_new=jnp.exp(r - mx),
            s_old=jnp.exp(m - mx)))
        m = b_last + mx
    m_ref[...] = m
    r_pad = [jnp.zeros((SUBLANES, L), F32)] * (LANES // SUBLANES - n_chunks)
    out["rows"] = rows
    out["r_cols"] = (jnp.concatenate(rs + r_pad, axis=0) * _LOG2E).T


_GATE_ROWS = ("mu_log2", "inter_scale", "eb", "w_new", "s_old")


def _store_gates(gates, grow_ref, rcol_ref):
    for c, chunk in enumerate(gates["rows"]):
        for q, name in enumerate(_GATE_ROWS):
            grow_ref[c * len(_GATE_ROWS) + q] = chunk[name]
    rcol_ref[...] = gates["r_cols"]


def _head_pieces(qks_ref, vt_ref, ot_ref, nw_ref, grow_ref, rcol_ref, c_ref, out_ref):
    L = MLSTM_CHUNK
    D = HEAD_DIM
    row_id = lax.broadcasted_iota(jnp.int32, (L, L), 0)
    col_id = lax.broadcasted_iota(jnp.int32, (L, L), 1)
    causal = row_id <= col_id
    neg_inf = jnp.float32(-jnp.inf)
    for c in range(MLSTM_ROWS // L):
        for h in range(N_HEADS):
            row = lambda name: grow_ref[c * len(_GATE_ROWS) + _GATE_ROWS.index(name),
                                        h:h + 1, :]
            q = qks_ref[h, c * L:(c + 1) * L, :].astype(BF16)
            k = qks_ref[N_HEADS + h, c * L:(c + 1) * L, :].astype(BF16)
            vt = vt_ref[h * D:(h + 1) * D, c * L:(c + 1) * L]
            r_col = rcol_ref[:, c * SUBLANES + h:c * SUBLANES + h + 1]

            dmat = jnp.exp2(jnp.where(causal, r_col - row("mu_log2"), neg_inf))
            s_kq = lax.dot_general(k, q, _NT_DIMS, preferred_element_type=F32) * dmat
            cq = lax.dot_general(c_ref[h].astype(BF16), q, _NT_DIMS,
                                 preferred_element_type=F32)
            isc = row("inter_scale")
            num = isc * cq[0:D, :] + jnp.dot(vt.astype(BF16), s_kq.astype(BF16),
                                             preferred_element_type=F32)
            den = isc * cq[D:D + 1, :] + jnp.sum(s_kq, axis=0, keepdims=True)
            hh = num * (1.0 / jnp.maximum(jnp.abs(den), row("eb")))

            w = row("w_new")
            vw = jnp.concatenate([vt * w, jnp.broadcast_to(w, (SUBLANES, L))],
                                 axis=0).astype(BF16)
            c_loc = jnp.dot(vw, k, preferred_element_type=F32)
            c_ref[h] = row("s_old") * c_ref[h] + c_loc

            mean = jnp.mean(hh, axis=0, keepdims=True)
            cen = hh - mean
            var = jnp.mean(cen * cen, axis=0, keepdims=True)
            hn = cen * lax.rsqrt(var + EPS) * nw_ref[h * D:(h + 1) * D, :]
            gate = _sigmoid(ot_ref[h * D:(h + 1) * D, c * L:(c + 1) * L])
            out_ref[h * D:(h + 1) * D, c * L:(c + 1) * L] = (hn * gate).astype(BF16)
            yield


def _mlstm_kernel(qk_ref, vt_ref, ot_ref, gr_ref, gr_next_ref, cw_ref, cb_ref, br_ref, nw_ref,
                  out_ref, xpad_ref, qks_ref, c_ref, grow_ref, rcol_ref, m_ref):
    n_chunks = MLSTM_ROWS // MLSTM_CHUNK

    @pl.when(pl.program_id(0) == 0)
    def _():
        xpad_ref[:, 0:SUBLANES, :] = jnp.zeros((_QK_BLOCKS, SUBLANES, LANES), F32)
        c_ref[...] = jnp.zeros(c_ref.shape, F32)
        m_ref[...] = jnp.zeros(m_ref.shape, F32)
        first = {}
        for _ in _gate_pieces(gr_ref, br_ref, m_ref, first):
            pass
        _store_gates(first, grow_ref, rcol_ref)

    for _ in _conv_pieces(qk_ref, cw_ref, cb_ref, xpad_ref, qks_ref):
        pass
    nxt = {}
    _interleave((_gate_pieces(gr_next_ref, br_ref, m_ref, nxt), _GATE_PIECES),
                (_head_pieces(qks_ref, vt_ref, ot_ref, nw_ref, grow_ref, rcol_ref, c_ref,
                              out_ref), n_chunks * N_HEADS))
    _store_gates(nxt, grow_ref, rcol_ref)


def _mlstm_call(qk, vt, ot, gr, cw, cb, bias_r, nw_cols):
    S = qk.shape[0]
    T = MLSTM_ROWS
    n = S // T
    n_chunks = T // MLSTM_CHUNK
    return pl.pallas_call(
        _mlstm_kernel,
        grid=(n,),
        in_specs=[_row_spec(T, D_QK), _lane_spec(D_MLSTM, T), _lane_spec(D_MLSTM, T),
                  _lane_spec(SUBLANES, T),
                  pl.BlockSpec((SUBLANES, T), lambda i: (0, jnp.minimum(i + 1, n - 1))),
                  _const_spec(cw.shape), _const_spec(cb.shape),
                  _const_spec(bias_r.shape), _const_spec(nw_cols.shape)],
        out_specs=_lane_spec(D_MLSTM, T),
        out_shape=jax.ShapeDtypeStruct((D_MLSTM, S), BF16),
        scratch_shapes=[pltpu.VMEM((_QK_BLOCKS, T + SUBLANES, LANES), F32),
                        pltpu.VMEM((_QK_BLOCKS, T, LANES), F32),
                        pltpu.VMEM((N_HEADS, HEAD_DIM + SUBLANES, HEAD_DIM), F32),
                        pltpu.VMEM((n_chunks * len(_GATE_ROWS), SUBLANES, MLSTM_CHUNK), F32),
                        pltpu.VMEM((MLSTM_CHUNK, LANES), F32),
                        pltpu.VMEM((SUBLANES, LANES), F32)],
        compiler_params=_params(),
        name="mlstm",
    )(qk, vt, ot, gr, gr, cw, cb, bias_r, nw_cols)


def _gelu_tanh(x):
    c = math.sqrt(2.0 / math.pi)
    return x * (0.5 * (1.0 + jnp.tanh(c * (x + 0.044715 * (x * x * x)))))


def _s5_kernel(u_ref, a2_ref, wb_ref, wc_ref, wd_ref, d_ref, gw_ref, gb_ref, out_ref,
               x_ref, y_ref, *z_refs):
    T = S5_ROWS

    @pl.when(pl.program_id(0) == 0)
    def _():
        x_ref[...] = jnp.zeros(x_ref.shape, F32)

    R = S5_SUB
    P = R // 2
    n_sub = T // R
    half = STATE_BLOCKS // 2
    n_blocks = D_SSM // LANES
    a_re = [a2_ref[:, c * LANES:(c + 1) * LANES] for c in range(half)]
    a_im = [a2_ref[:, (half + c) * LANES:(half + c + 1) * LANES] for c in range(half)]

    def pair_rows(sb, b):
        even = u_ref[b, pl.ds(sb * R, P, stride=2), :]
        odd = u_ref[b, pl.ds(sb * R + 1, P, stride=2), :]
        return jnp.concatenate([even, odd], axis=1).astype(BF16)

    def expand_piece(sb, j):
        w = jnp.dot(pair_rows(sb, j // 2), wb_ref[j], preferred_element_type=F32)
        for c in range(STATE_BLOCKS):
            z_refs[sb][c // 2, pl.ds(j * S5_PITCH + c % 2, P, stride=2), :] = (
                w[:, c * LANES:(c + 1) * LANES])

    def scan_piece(sb, x, i0, i1):
        z_ref = z_refs[sb]
        for i in range(i0, i1):
            tile = lambda c: (c // 2, pl.ds(2 * i + c % 2, SUBLANES, stride=S5_PITCH),
                              slice(None))
            new = list(x)
            for c in range(half):
                x_re, x_im = x[c], x[half + c]
                new[c] = a_re[c] * x_re - a_im[c] * x_im + z_ref[tile(c)]
                new[half + c] = a_re[c] * x_im + a_im[c] * x_re + z_ref[tile(half + c)]
                z_ref[tile(c)] = x_re
                z_ref[tile(half + c)] = x_im
            x = new
        return x

    def contract_piece(sb, j):
        xs = jnp.concatenate(
            [z_refs[sb][c // 2, pl.ds(j * S5_PITCH + c % 2, P, stride=2), :]
             for c in range(STATE_BLOCKS)], axis=1).astype(BF16)
        return jnp.dot(xs, wc_ref[j], preferred_element_type=F32)

    def finish(sb, parts):
        for b in range(n_blocks):
            yb = (parts[2 * b] + parts[2 * b + 1]
                  + jnp.dot(pair_rows(sb, b), wd_ref[b], preferred_element_type=F32))
            y_ref[b, pl.ds(sb * R, P, stride=2), :] = yb[:, 0:LANES]
            y_ref[b, pl.ds(sb * R + 1, P, stride=2), :] = yb[:, LANES:2 * LANES]
        rows = slice(sb * R, (sb + 1) * R)
        y = jnp.concatenate([y_ref[b, rows, :] for b in range(n_blocks)], axis=1)
        u = jnp.concatenate([u_ref[b, rows, :] for b in range(n_blocks)], axis=1)
        z = _gelu_tanh(y + d_ref[...] * u)
        gate = _sigmoid(jnp.dot(z.astype(BF16), gw_ref[...],
                                preferred_element_type=F32) + gb_ref[...])
        out_ref[rows, :] = (z * gate).astype(BF16)

    x = [x_ref[:, c * LANES:(c + 1) * LANES] for c in range(STATE_BLOCKS)]
    steps = P // SUBLANES
    for j in range(SUBLANES):
        expand_piece(0, j)
    for sb in range(n_sub + 1):
        parts = []
        for j in range(SUBLANES):
            if sb < n_sub:
                x = scan_piece(sb, x, j * steps, (j + 1) * steps)
            if sb + 1 < n_sub:
                expand_piece(sb + 1, j)
            if sb >= 1:
                parts.append(contract_piece(sb - 1, j))
        if sb >= 1:
            finish(sb - 1, parts)
    for c in range(STATE_BLOCKS):
        x_ref[:, c * LANES:(c + 1) * LANES] = x[c]


def _s5_call(u, a2_tile, wb, wc, wd, d, gw, gb):
    S = u.shape[1]
    T = S5_ROWS
    n_blocks = D_SSM // LANES
    return pl.pallas_call(
        _s5_kernel,
        grid=(S // T,),
        in_specs=[pl.BlockSpec((n_blocks, T, LANES), lambda i: (0, i, 0)),
                  _const_spec(a2_tile.shape), _const_spec(wb.shape), _const_spec(wc.shape),
                  _const_spec(wd.shape), _const_spec(d.shape), _const_spec(gw.shape),
                  _const_spec(gb.shape)],
        out_specs=_row_spec(T, D_SSM),
        out_shape=jax.ShapeDtypeStruct((S, D_SSM), BF16),
        scratch_shapes=([pltpu.VMEM((SUBLANES, STATE_LANES), F32),
                         pltpu.VMEM((n_blocks, T, LANES), F32)]
                        + [pltpu.VMEM((STATE_BLOCKS // 2, SUBLANES * S5_PITCH, LANES), F32)]
                        * (T // S5_SUB)),
        compiler_params=_params(),
        name="s5",
    )(u, a2_tile, wb, wc, wd, d, gw, gb)


def _s5_weights(lam_re, lam_im, log_dt, b_re, b_im, c_re, c_im):
    cmul = lambda xr, xi, yr, yi: (xr * yr - xi * yi, xr * yi + xi * yr)
    lr, li = lam_re.astype(F32), lam_im.astype(F32)
    dt = jnp.exp(log_dt.astype(F32))[:, None]
    mag = jnp.exp(lr * dt)
    a_re, a_im = mag * jnp.cos(li * dt), mag * jnp.sin(li * dt)
    a2_re, a2_im = cmul(a_re, a_im, a_re, a_im)
    den = lr * lr + li * li
    coef_re = ((a_re - 1.0) * lr + a_im * li) / den
    coef_im = (a_im * lr - (a_re - 1.0) * li) / den
    bb_re, bb_im = cmul(coef_re[..., None], coef_im[..., None],
                        b_re.astype(F32), b_im.astype(F32))
    abb_re, abb_im = cmul(a_re[..., None], a_im[..., None], bb_re, bb_im)
    cr, ci = c_re.astype(F32), c_im.astype(F32)
    ca_re, ca_im = cmul(cr, ci, a_re[:, None, :], a_im[:, None, :])
    ca2_re, ca2_im = cmul(cr, ci, a2_re[:, None, :], a2_im[:, None, :])

    J, Q, P, Hc = SUBLANES, GROUPS_PER_SUBLANE, SSM_STATE, SSM_GROUP
    a2_tile = jnp.concatenate([a2_re.reshape(J, Q * P), a2_im.reshape(J, Q * P)], axis=1)
    eye = jnp.eye(Q, dtype=F32)
    half = (jnp.arange(J) % 2)[:, None, None, None] == jnp.arange(2)[None, :, None, None]

    def in_weights(w_re, w_im):
        def blocks(w):
            return jnp.einsum('jqpc,qr->jqcrp', w.reshape(J, Q, P, Hc),
                              eye).reshape(J, Q * Hc, Q * P)
        w = jnp.concatenate([blocks(w_re), blocks(w_im)], axis=2)
        return jnp.where(half, w[:, None], 0.0).reshape(J, 2 * Q * Hc, STATE_LANES)

    def out_weights(w_re, w_im):
        def blocks(w):
            return jnp.einsum('jqcp,qr->jqprc', w.reshape(J, Q, Hc, P),
                              eye).reshape(J, Q * P, Q * Hc)
        w = jnp.concatenate([blocks(w_re), -blocks(w_im)], axis=1)
        half_c = jnp.swapaxes(half, 1, 2)
        return jnp.where(half_c, w[:, :, None, :], 0.0).reshape(J, STATE_LANES, 2 * Q * Hc)

    wb = jnp.concatenate([in_weights(abb_re, abb_im), in_weights(bb_re, bb_im)], axis=1)
    wc = jnp.concatenate([out_weights(ca_re, ca_im), out_weights(ca2_re, ca2_im)], axis=2)
    k0 = jnp.einsum('gop,gpc->goc', cr, bb_re) - jnp.einsum('gop,gpc->goc', ci, bb_im)
    k1 = jnp.einsum('gop,gpc->goc', cr, abb_re) - jnp.einsum('gop,gpc->goc', ci, abb_im)
    n_blocks, per = D_SSM // LANES, LANES // Hc
    eye8 = jnp.eye(per, dtype=F32)

    def lane_block(k):
        return jnp.einsum('bgoc,gh->bgcho', k.reshape(n_blocks, per, Hc, Hc),
                          eye8).reshape(n_blocks, LANES, LANES)

    zero = jnp.zeros((n_blocks, LANES, LANES), F32)
    wd = jnp.concatenate([jnp.concatenate([lane_block(k0), lane_block(k1)], axis=2),
                          jnp.concatenate([zero, lane_block(k0)], axis=2)], axis=1)
    return a2_tile, wb.astype(BF16), wc.astype(BF16), wd.astype(BF16)


def _ffn_kernel(x_ref, hm_ref, y_ref, wo_ref, g2_ref, w1_ref, w2_ref, g3_ref, out_ref):
    mixed = (lax.dot_general(hm_ref[...], wo_ref[0:D_MLSTM, :], (((0,), (0,)), ((), ())),
                             preferred_element_type=F32)
             + jnp.dot(y_ref[...], wo_ref[D_MLSTM:D_MODEL, :], preferred_element_type=F32))
    x1 = x_ref[...] + mixed
    h2 = _rms(x1, g2_ref[...]).astype(BF16)
    mlp = None
    for c in range(D_FF // FF_CHUNK):
        a = jnp.dot(h2, w1_ref[:, c * FF_CHUNK:(c + 1) * FF_CHUNK],
                    preferred_element_type=F32)
        a = jnp.maximum(a, 0.0)
        part = jnp.dot((a * a).astype(BF16), w2_ref[c * FF_CHUNK:(c + 1) * FF_CHUNK, :],
                       preferred_element_type=F32)
        mlp = part if mlp is None else mlp + part
    out_ref[...] = _rms(x1 + mlp, g3_ref[...])


def _ffn_call(x, hm, y, wo, g2, w1, w2, g3):
    S = x.shape[0]
    T = FFN_ROWS
    single = dict(pipeline_mode=pl.Buffered(WEIGHT_BUFFERS))
    return pl.pallas_call(
        _ffn_kernel,
        grid=(S // T,),
        in_specs=[_row_spec(T, D_MODEL), _lane_spec(D_MLSTM, T), _row_spec(T, D_SSM),
                  pl.BlockSpec(wo.shape, lambda i: (0, 0), **single),
                  _const_spec(g2.shape),
                  pl.BlockSpec(w1.shape, lambda i: (0, 0), **single),
                  pl.BlockSpec(w2.shape, lambda i: (0, 0), **single),
                  _const_spec(g3.shape)],
        out_specs=_row_spec(T, D_MODEL),
        out_shape=jax.ShapeDtypeStruct((S, D_MODEL), F32),
        compiler_params=_params(),
        name="ffn",
    )(x, hm, y, wo, g2, w1, w2, g3)


def _layer(x, mix_norm_w, w_in, conv_w, conv_b, i_bias, f_bias, mlstm_norm_w,
           lam_re, lam_im, log_dt, b_re, b_im, c_re, c_im, ssm_d,
           glu_w, glu_b, w_out, mlp_norm_w, w_ff1, w_ff2, out_norm_w):
    n_qk, n_vo, n_gate = 2 * D_MLSTM, 2 * D_MLSTM, 2 * N_HEADS
    w_rows = jnp.concatenate([w_in[:, :n_qk], w_in[:, n_qk + n_vo + n_gate:]],
                             axis=1).astype(BF16)
    w_cols = w_in[:, n_qk:n_qk + n_vo].T.astype(BF16)
    wg = w_in[:, n_qk + n_vo:n_qk + n_vo + n_gate].T.astype(BF16)
    qk, u, vt, ot, gr, wo_bf, w1_bf, w2_bf = _proj_call(
        x, mix_norm_w[None, :], w_rows, w_cols, wg, w_out, w_ff1, w_ff2)

    bias_r = jnp.concatenate([i_bias, f_bias]).astype(F32)[:, None]
    nw_cols = jnp.broadcast_to(mlstm_norm_w.astype(F32)[:, None], (D_MLSTM, LANES))
    hm = _mlstm_call(qk, vt, ot, gr, conv_w, conv_b[None, :], bias_r, nw_cols)

    a2_tile, wb, wc, wd = _s5_weights(lam_re, lam_im, log_dt, b_re, b_im, c_re, c_im)
    y = _s5_call(u, a2_tile, wb, wc, wd, ssm_d[None, :], glu_w.astype(BF16),
                 glu_b[None, :])

    return _ffn_call(x, hm, y, wo_bf, mlp_norm_w[None, :], w1_bf, w2_bf,
                     out_norm_w[None, :])


def kernel(x, mix_norm_w, w_in, conv_w, conv_b, i_bias, f_bias, mlstm_norm_w, ssm_lam_re, ssm_lam_im, ssm_log_dt, ssm_b_re, ssm_b_im, ssm_c_re, ssm_c_im, ssm_d, glu_w, glu_b, w_out, mlp_norm_w, w_ff1, w_ff2, final_norm_w):
    assert x.shape[0] == 1 and mix_norm_w.shape[0] == 1
    xs = x[0]
    out = _layer(xs, mix_norm_w[0], w_in[0], conv_w[0], conv_b[0], i_bias[0], f_bias[0],
                 mlstm_norm_w[0], ssm_lam_re[0], ssm_lam_im[0], ssm_log_dt[0],
                 ssm_b_re[0], ssm_b_im[0], ssm_c_re[0], ssm_c_im[0], ssm_d[0],
                 glu_w[0], glu_b[0], w_out[0], mlp_norm_w[0], w_ff1[0], w_ff2[0],
                 final_norm_w)
    return out[None]
```
